```python
import jax, jax.numpy as jnp
from jax import lax
import numpy as np

D_MODEL = 1024
BATCH = 8
SEQ = 4096
DEPTH = 4

GRID_W = 64
CTX_LEN = 256
D_MIX = D_MODEL
GROUP_W = D_MIX // 4
D_FF = 4 * D_MODEL
N_MOD = 6
EPS = 1e-6
ROPE_BASE = 10000.0
Q_BLOCK = 128
CONV_WIDTH = 31
MLA_HEADS = 4
MLA_NOPE = 64
MLA_ROPE = 32
MLA_V = 64
MLA_Q_RANK = 256
MLA_KV_RANK = 128
GQA_HEADS = 4
GQA_KV_HEADS = 2
GQA_HEAD_DIM = 64
RET_HEADS = 4
RET_QK = 32
RET_V = 64
RET_CHUNK = 128

IN_SIZES = (2 * GROUP_W,
            MLA_Q_RANK, MLA_KV_RANK, MLA_ROPE,
            GQA_HEADS * GQA_HEAD_DIM, GQA_KV_HEADS * GQA_HEAD_DIM, GQA_KV_HEADS * GQA_HEAD_DIM,
            RET_HEADS * RET_QK, RET_HEADS * RET_QK, RET_HEADS * RET_V,
            GROUP_W, GROUP_W)
D_IN = sum(IN_SIZES)

kernel_name = 'hybrid_parallel_groups_flow_block'


def rms_norm(x, gain=None):
    xf = x.astype(jnp.float32)
    y = xf * lax.rsqrt(jnp.mean(xf * xf, axis=-1, keepdims=True) + EPS)
    if gain is not None:
        y = y * gain.astype(jnp.float32)
    return y.astype(x.dtype)


def layer_norm(x, gain, bias):
    xf = x.astype(jnp.float32)
    mu = jnp.mean(xf, axis=-1, keepdims=True)
    var = jnp.mean(jnp.square(xf - mu), axis=-1, keepdims=True)
    y = (xf - mu) * lax.rsqrt(var + EPS) * gain.astype(jnp.float32) + bias.astype(jnp.float32)
    return y.astype(x.dtype)


def split_cols(u):
    parts, off = [], 0
    for s in IN_SIZES:
        parts.append(u[..., off:off + s])
        off += s
    return tuple(parts)


def heads(t, n, d):
    b, l, _ = t.shape
    return t.reshape(b, l, n, d).transpose(0, 2, 1, 3)


def merge_heads(o):
    b, hk, g, l, d = o.shape
    return o.reshape(b, hk * g, l, d).transpose(0, 2, 1, 3).reshape(b, l, hk * g * d)


def flip_seq(t):
    return jnp.flip(t, axis=2)


def rope_1d(x, pos):
    half = x.shape[-1] // 2
    freqs = ROPE_BASE ** (-jnp.arange(half, dtype=jnp.float32) / half)
    ang = pos.astype(jnp.float32)[:, None] * freqs[None, :]
    cos, sin = jnp.cos(ang), jnp.sin(ang)
    xf = x.astype(jnp.float32)
    x1, x2 = xf[..., :half], xf[..., half:]
    return jnp.concatenate([x1 * cos - x2 * sin, x1 * sin + x2 * cos], axis=-1).astype(x.dtype)


def rope_2d(x, row, col):
    half = x.shape[-1] // 2
    return jnp.concatenate([rope_1d(x[..., :half], row), rope_1d(x[..., half:], col)], axis=-1)


def grouped_attention(q, k, v):
    b, hk, g, lq, dq = q.shape
    nb = lq // Q_BLOCK
    scale = dq ** -0.5
    qb = jnp.moveaxis(q.reshape(b, hk, g, nb, Q_BLOCK, dq), 3, 0)

    def one_block(qblk):
        s = jnp.einsum('bkgqd,bksd->bkgqs', qblk, k, preferred_element_type=jnp.float32) * scale
        p = jax.nn.softmax(s, axis=-1)
        return jnp.einsum('bkgqs,bksd->bkgqd', p.astype(v.dtype), v)

    o = lax.map(one_block, qb)
    return jnp.moveaxis(o, 0, 3).reshape(b, hk, g, lq, v.shape[-1])


def conformer_conv(u, w_dw, b_dw, ln_g, ln_b, w_pw):
    y = u[..., :GROUP_W] * jax.nn.sigmoid(u[..., GROUP_W:])
    pad = CONV_WIDTH // 2
    y = lax.conv_general_dilated(y, w_dw[:, None, :], window_strides=(1,), padding=((pad, pad),),
                                 dimension_numbers=('NWC', 'WIO', 'NWC'),
                                 feature_group_count=GROUP_W) + b_dw
    y = jax.nn.silu(layer_norm(y, ln_g, ln_b))
    return y @ w_pw


def mla_q(u, g, w_uq, pos):
    q = heads(rms_norm(u, g) @ w_uq, MLA_HEADS, MLA_NOPE + MLA_ROPE)
    if pos is not None:
        q = jnp.concatenate([q[..., :MLA_NOPE], rope_2d(q[..., MLA_NOPE:], *pos)], axis=-1)
    return q[:, :, None]


def mla_kv(u_kv, u_kpe, g, w_ukv, pos):
    kv = heads(rms_norm(u_kv, g) @ w_ukv, MLA_HEADS, MLA_NOPE + MLA_V)
    k_pe = u_kpe
    if pos is not None:
        k_pe = rope_2d(k_pe, *pos)
    b, h, l, _ = kv.shape
    k = jnp.concatenate([kv[..., :MLA_NOPE], jnp.broadcast_to(k_pe[:, None], (b, h, l, MLA_ROPE))], axis=-1)
    return k, kv[..., MLA_NOPE:]


def gqa_q(u, g, pos):
    q = rms_norm(heads(u, GQA_HEADS, GQA_HEAD_DIM), g)
    if pos is not None:
        q = rope_2d(q, *pos)
    b, h, l, d = q.shape
    return q.reshape(b, GQA_KV_HEADS, h // GQA_KV_HEADS, l, d)


def gqa_kv(uk, uv, g, pos):
    k = rms_norm(heads(uk, GQA_KV_HEADS, GQA_HEAD_DIM), g)
    if pos is not None:
        k = rope_2d(k, *pos)
    return k, heads(uv, GQA_KV_HEADS, GQA_HEAD_DIM)


def ret_qkv(uq, uk, uv):
    return (heads(uq, RET_HEADS, RET_QK), heads(uk, RET_HEADS, RET_QK) * (RET_QK ** -0.5),
            heads(uv, RET_HEADS, RET_V))


def retention_chunked(q, k, v, log_gamma, s0, with_output):
    b, h, l, dk = q.shape
    dv = v.shape[-1]
    n = l // RET_CHUNK
    qc = q.astype(jnp.float32).reshape(b, h, n, RET_CHUNK, dk)
    kc = k.astype(jnp.float32).reshape(b, h, n, RET_CHUNK, dk)
    vc = v.astype(jnp.float32).reshape(b, h, n, RET_CHUNK, dv)
    idx = jnp.arange(RET_CHUNK, dtype=jnp.float32)
    lg = log_gamma[:, None]
    k_decay = jnp.exp(lg * (RET_CHUNK - 1.0 - idx))
    chunk_kv = jnp.einsum('bhnjd,bhnje->bhnde', kc * k_decay[:, None, :, None], vc)
    chunk_decay = jnp.exp(log_gamma * RET_CHUNK)[:, None, None]

    def step(s, kv_j):
        return s * chunk_decay + kv_j, s

    s_final, s_prev = lax.scan(step, s0, jnp.moveaxis(chunk_kv, 2, 0))
    if not with_output:
        return None, s_final
    s_prev = jnp.moveaxis(s_prev, 0, 2)
    diff = idx[:, None] - idx[None, :]
    decay = jnp.where(diff >= 0, jnp.exp(lg[:, :, None] * jnp.maximum(diff, 0.0)), 0.0)
    scores = jnp.einsum('bhnid,bhnjd->bhnij', qc, kc) * decay[:, None]
    inner = jnp.einsum('bhnij,bhnje->bhnie', scores, vc)
    q_decay = jnp.exp(lg * (idx + 1.0))
    cross = jnp.einsum('bhnid,bhnde->bhnie', qc * q_decay[:, None, :, None], s_prev)
    return (inner + cross).reshape(b, h, l, dv), s_final


def ret_readout(o, gain, gate):
    b, h, l, d = o.shape
    y = rms_norm(o, gain[:, None, :]).transpose(0, 2, 1, 3).reshape(b, l, h * d)
    return y.astype(gate.dtype) * jax.nn.silu(gate)


def adaln(cond, w, b):
    m = (jax.nn.silu(cond) @ w + b).reshape(cond.shape[0], N_MOD, D_MODEL)
    return tuple(m[:, i, None, :] for i in range(N_MOD))


def modulate(x, shift, scale):
    return rms_norm(x) * (1.0 + scale) + shift


def sq_relu_mlp(h, w1, w2):
    return jnp.square(jax.nn.relu(h @ w1)) @ w2


def mixing_sublayer(h_ctx, h_lat, pos, ctx_out, w_in, w_out, conv_dw, conv_b, conv_ln_g, conv_ln_b,
                    conv_pw, mla_q_g, mla_kv_g, mla_uq, mla_ukv, gqa_q_g, gqa_k_g, ret_decay, ret_norm_g):
    (a_c, cq_c, ckv_c, kpe_c, q_c, k_c, v_c, rq_c, rk_c, rv_c, gf_c, gb_c) = split_cols(h_ctx @ w_in)
    (a_l, cq_l, ckv_l, kpe_l, q_l, k_l, v_l, rq_l, rk_l, rv_l, gf_l, gb_l) = split_cols(h_lat @ w_in)
    conv_args = (conv_dw, conv_b, conv_ln_g, conv_ln_b, conv_pw)
    ya_l = conformer_conv(a_l, *conv_args)
    mk_c, mv_c = mla_kv(ckv_c, kpe_c, mla_kv_g, mla_ukv, None)
    mk_l, mv_l = mla_kv(ckv_l, kpe_l, mla_kv_g, mla_ukv, pos)
    yb_l = merge_heads(grouped_attention(mla_q(cq_l, mla_q_g, mla_uq, pos),
                                         jnp.concatenate([mk_c, mk_l], axis=2),
                                         jnp.concatenate([mv_c, mv_l], axis=2)))
    gk_c, gv_c = gqa_kv(k_c, v_c, gqa_k_g, None)
    gk_l, gv_l = gqa_kv(k_l, v_l, gqa_k_g, pos)
    yc_l = merge_heads(grouped_attention(gqa_q(q_l, gqa_q_g, pos),
                                         jnp.concatenate([gk_c, gk_l], axis=2),
                                         jnp.concatenate([gv_c, gv_l], axis=2)))
    log_g = jax.nn.log_sigmoid(ret_decay.astype(jnp.float32))
    tq_c, tk_c, tv_c = ret_qkv(rq_c, rk_c, rv_c)
    tq_l, tk_l, tv_l = ret_qkv(rq_l, rk_l, rv_l)
    s0 = jnp.zeros((h_ctx.shape[0], RET_HEADS, RET_QK, RET_V), jnp.float32)
    oc_f, s_f = retention_chunked(tq_c, tk_c, tv_c, log_g[0], s0, ctx_out)
    oc_b, s_b = retention_chunked(flip_seq(tq_c), flip_seq(tk_c), flip_seq(tv_c), log_g[1], s0, ctx_out)
    ol_f, _ = retention_chunked(tq_l, tk_l, tv_l, log_g[0], s_f, True)
    ol_b, _ = retention_chunked(flip_seq(tq_l), flip_seq(tk_l), flip_seq(tv_l), log_g[1], s_b, True)
    yd_l = ret_readout(ol_f, ret_norm_g[0], gf_l) + ret_readout(flip_seq(ol_b), ret_norm_g[1], gb_l)
    y_lat = jnp.concatenate([ya_l, yb_l, yc_l, yd_l], axis=-1) @ w_out
    if not ctx_out:
        return None, y_lat
    ya_c = conformer_conv(a_c, *conv_args)
    yb_c = merge_heads(grouped_attention(mla_q(cq_c, mla_q_g, mla_uq, None), mk_c, mv_c))
    yc_c = merge_heads(grouped_attention(gqa_q(q_c, gqa_q_g, None), gk_c, gv_c))
    yd_c = ret_readout(oc_f, ret_norm_g[0], gf_c) + ret_readout(flip_seq(oc_b), ret_norm_g[1], gb_c)
    y_ctx = jnp.concatenate([ya_c, yb_c, yc_c, yd_c], axis=-1) @ w_out
    return y_ctx, y_lat


def setup_inputs(seed: int = 0) -> dict:
    key = jax.random.key(seed)
    ks = jax.random.split(key, 24)

    def nrm(k, shape, scale):
        return jax.random.normal(k, shape, jnp.float32) * scale

    a = 5.0 + np.arange(RET_HEADS, dtype=np.float32)
    decay_logit = jnp.asarray(np.log(2.0 ** a - 1.0).astype(np.float32))
    return {
        'x': nrm(ks[0], (BATCH, SEQ, D_MODEL), 1.0),
        'c': nrm(ks[1], (BATCH, D_MODEL), 1.0),
        'ctx': nrm(ks[2], (BATCH, CTX_LEN, D_MODEL), 1.0),
        'c_ctx': nrm(ks[3], (D_MODEL,), 1.0),
        'w_mod': nrm(ks[4], (DEPTH, D_MODEL, N_MOD * D_MODEL), 0.5 * D_MODEL ** -0.5),
        'b_mod': nrm(ks[5], (DEPTH, N_MOD * D_MODEL), 0.02),
        'w_in': nrm(ks[6], (DEPTH, D_MODEL, D_IN), D_MODEL ** -0.5),
        'w_out': nrm(ks[7], (DEPTH, D_MIX, D_MODEL), D_MIX ** -0.5),
        'conv_dw': nrm(ks[8], (DEPTH, CONV_WIDTH, GROUP_W), CONV_WIDTH ** -0.5),
        'conv_b': nrm(ks[9], (DEPTH, GROUP_W), 0.02),
        'conv_ln_g': 1.0 + nrm(ks[10], (DEPTH, GROUP_W), 0.02),
        'conv_ln_b': nrm(ks[11], (DEPTH, GROUP_W), 0.02),
        'conv_pw': nrm(ks[12], (DEPTH, GROUP_W, GROUP_W), GROUP_W ** -0.5),
        'mla_q_g': 1.0 + nrm(ks[13], (DEPTH, MLA_Q_RANK), 0.02),
        'mla_kv_g': 1.0 + nrm(ks[14], (DEPTH, MLA_KV_RANK), 0.02),
        'mla_uq': nrm(ks[15], (DEPTH, MLA_Q_RANK, MLA_HEADS * (MLA_NOPE + MLA_ROPE)), MLA_Q_RANK ** -0.5),
        'mla_ukv': nrm(ks[16], (DEPTH, MLA_KV_RANK, MLA_HEADS * (MLA_NOPE + MLA_V)), MLA_KV_RANK ** -0.5),
        'gqa_q_g': 1.0 + nrm(ks[17], (DEPTH, GQA_HEAD_DIM), 0.02),
        'gqa_k_g': 1.0 + nrm(ks[18], (DEPTH, GQA_HEAD_DIM), 0.02),
        'ret_decay': decay_logit[None, None, :] + nrm(ks[19], (DEPTH, 2, RET_HEADS), 0.1),
        'ret_norm_g': 1.0 + nrm(ks[20], (DEPTH, 2, RET_HEADS, RET_V), 0.02),
        'mlp_w1': nrm(ks[21], (DEPTH, D_MODEL, D_FF), D_MODEL ** -0.5),
        'mlp_w2': nrm(ks[22], (DEPTH, D_FF, D_MODEL), D_FF ** -0.5),
        'final_g': 1.0 + nrm(ks[23], (D_MODEL,), 0.02),
    }


def reference(x, c, ctx, c_ctx, w_mod, b_mod, w_in, w_out, conv_dw, conv_b, conv_ln_g, conv_ln_b,
              conv_pw, mla_q_g, mla_kv_g, mla_uq, mla_ukv, gqa_q_g, gqa_k_g, ret_decay, ret_norm_g,
              mlp_w1, mlp_w2, final_g):
    n_lat = x.shape[1]
    n_rows = n_lat // GRID_W
    row = jnp.repeat(jnp.arange(n_rows, dtype=jnp.int32), GRID_W)
    col = jnp.tile(jnp.arange(GRID_W, dtype=jnp.int32), n_rows)
    pos = (row, col)
    cx = ctx
    for layer in range(DEPTH):
        last = layer == DEPTH - 1
        sh1, sc1, g1, sh2, sc2, g2 = adaln(c, w_mod[layer], b_mod[layer])
        csh1, csc1, cg1, csh2, csc2, cg2 = adaln(c_ctx[None, :], w_mod[layer], b_mod[layer])
        y_ctx, y_lat = mixing_sublayer(
            modulate(cx, csh1, csc1), modulate(x, sh1, sc1), pos, not last,
            w_in[layer], w_out[layer], conv_dw[layer], conv_b[layer], conv_ln_g[layer], conv_ln_b[layer],
            conv_pw[layer], mla_q_g[layer], mla_kv_g[layer], mla_uq[layer], mla_ukv[layer],
            gqa_q_g[layer], gqa_k_g[layer], ret_decay[layer], ret_norm_g[layer])
        x = x + g1 * y_lat
        x = x + g2 * sq_relu_mlp(modulate(x, sh2, sc2), mlp_w1[layer], mlp_w2[layer])
        if not last:
            cx = cx + cg1 * y_ctx
            cx = cx + cg2 * sq_relu_mlp(modulate(cx, csh2, csc2), mlp_w1[layer], mlp_w2[layer])
    return rms_norm(x, final_g)
```

```python
import functools

import numpy as np
import jax
import jax.numpy as jnp
from jax import lax
from jax.experimental import pallas as pl
from jax.experimental.pallas import tpu as pltpu

F32 = jnp.float32
BF16 = jnp.bfloat16

GRID_W = 64
N_MOD = 6
EPS = 1e-6
ROPE_BASE = 10000.0
GROUP_W = 256
CONV_WIDTH = 31
CONV_PAD = 16
MLA_HEADS = 4
MLA_NOPE = 64
MLA_ROPE = 32
MLA_V = 64
MLA_Q_RANK = 256
MLA_KV_RANK = 128
GQA_HEADS = 4
GQA_KV_HEADS = 2
GQA_HEAD_DIM = 64
RET_HEADS = 4
RET_QK = 32
RET_V = 64
RET_CHUNK = 128
LANES = 128
TOKEN_TILE = 256
VMEM_LIMIT = 56 * 1024 * 1024

OFF_A, OFF_CQ, OFF_CKV, OFF_KPE, OFF_GQ, OFF_GK, OFF_GV = 0, 512, 768, 896, 1024, 1280, 1408
OFF_RQ, OFF_RK, OFF_RV, OFF_GF, OFF_GB, N_IN = 1536, 1664, 1792, 2048, 2304, 2560


def _cparams(n_axes):
    return pltpu.CompilerParams(dimension_semantics=("arbitrary",) * n_axes,
                                vmem_limit_bytes=VMEM_LIMIT)


def _rms(x):
    return x * lax.rsqrt(jnp.mean(x * x, axis=-1, keepdims=True) + EPS)


def _sigmoid(x):
    return 1.0 / (1.0 + jnp.exp(-x))


def _silu(x):
    return x * _sigmoid(x)


def _dot(a, b):
    return jnp.dot(a, b, preferred_element_type=F32)


def _dot_nt(a, b):
    return lax.dot_general(a, b, (((1,), (1,)), ((), ())), preferred_element_type=F32)


def _dot_tn(a, b):
    return lax.dot_general(a, b, (((0,), (0,)), ((), ())), preferred_element_type=F32)


def _rope(x, cos, sin_up, sin_dn, shift):
    n = x.shape[-1]
    return x * cos + pltpu.roll(x, n - shift, 1) * sin_up + pltpu.roll(x, shift, 1) * sin_dn


def _adaln_kernel(cond_ref, w_ref, b_ref, o_ref):
    cond = cond_ref[...]
    o_ref[...] = _dot(_silu(cond).astype(BF16), w_ref[...].astype(BF16)) + b_ref[...]


def _adaln(cond, w_mod, b_mod):
    depth, d, n = w_mod.shape
    r = cond.shape[0]
    tn = 1536
    return pl.pallas_call(
        _adaln_kernel,
        grid=(depth, n // tn),
        in_specs=[pl.BlockSpec((r, d), lambda l, j: (0, 0)),
                  pl.BlockSpec((None, d, tn), lambda l, j: (l, 0, j)),
                  pl.BlockSpec((None, 1, tn), lambda l, j: (l, 0, j))],
        out_specs=pl.BlockSpec((None, r, tn), lambda l, j: (l, 0, j)),
        out_shape=jax.ShapeDtypeStruct((depth, r, n), F32),
        compiler_params=_cparams(2),
        name="adaln",
    )(cond, w_mod, b_mod.reshape(depth, 1, n))


def _in_proj_kernel(x_ref, mod_ref, w_in_ref, w_uq_ref, w_ukv_ref, g_mq_ref, g_mkv_ref, g_gq_ref, g_gk_ref,
                    tg_ref, tm_ref,
                    yglu_ref, mq_ref, mk_ref, mv_ref, gq_ref, gk_ref, gv_ref,
                    rq_ref, rk_ref, rv_ref, gf_ref, gb_ref):
    x = x_ref[...]
    h = (_rms(x) * (1.0 + mod_ref[1:2, :]) + mod_ref[0:1, :]).astype(BF16)

    def proj(off, width):
        return _dot(h, w_in_ref[:, off:off + width])

    lane = lax.broadcasted_iota(jnp.int32, (1, LANES), 1)
    lo = lane < (LANES // 2)

    a = proj(OFF_A, 2 * GROUP_W)
    yglu_ref[...] = a[:, :GROUP_W] * _sigmoid(a[:, GROUP_W:])

    cq = (_rms(proj(OFF_CQ, MLA_Q_RANK)) * g_mq_ref[...]).astype(BF16)
    q = _dot(cq, w_uq_ref[...])
    m_cos, m_up, m_dn = tm_ref[0], tm_ref[1], tm_ref[2]
    q_scale = (MLA_NOPE + MLA_ROPE) ** -0.5
    for hd in range(MLA_HEADS):
        qh = _rope(q[:, hd * LANES:(hd + 1) * LANES], m_cos, m_up, m_dn, MLA_ROPE // 4)
        mq_ref[:, hd * LANES:(hd + 1) * LANES] = (qh * q_scale).astype(BF16)
    ckv = (_rms(proj(OFF_CKV, MLA_KV_RANK)) * g_mkv_ref[...]).astype(BF16)
    kv = _dot(ckv, w_ukv_ref[...])
    kpe = _rope(proj(OFF_KPE, LANES), m_cos, m_up, m_dn, MLA_ROPE // 4)
    for hd in range(MLA_HEADS):
        mk_ref[:, hd * LANES:(hd + 1) * LANES] = (kv[:, hd * LANES:(hd + 1) * LANES] + kpe).astype(BF16)
    mv_ref[...] = kv[:, MLA_HEADS * LANES:].astype(BF16)

    g_cos, g_up, g_dn = tg_ref[0], tg_ref[1], tg_ref[2]

    def pair_norm_rope(xp, gain):
        x2 = xp * xp
        s_lo = jnp.sum(jnp.where(lo, x2, 0.0), axis=-1, keepdims=True)
        s_hi = jnp.sum(jnp.where(lo, 0.0, x2), axis=-1, keepdims=True)
        r = jnp.where(lo, lax.rsqrt(s_lo * (1.0 / GQA_HEAD_DIM) + EPS), lax.rsqrt(s_hi * (1.0 / GQA_HEAD_DIM) + EPS))
        return _rope(xp * r * gain, g_cos, g_up, g_dn, GQA_HEAD_DIM // 4)

    gq = proj(OFF_GQ, GQA_HEADS * GQA_HEAD_DIM)
    for pr in range(2):
        qp = pair_norm_rope(gq[:, pr * LANES:(pr + 1) * LANES], g_gq_ref[...])
        gq_ref[:, pr * LANES:(pr + 1) * LANES] = (qp * GQA_HEAD_DIM ** -0.5).astype(BF16)
    kp = pair_norm_rope(proj(OFF_GK, LANES), g_gk_ref[...])
    kp_sw = pltpu.roll(kp, LANES // 2, 1)
    gk_ref[:, 0 * LANES:1 * LANES] = jnp.where(lo, kp, 0.0).astype(BF16)
    gk_ref[:, 1 * LANES:2 * LANES] = jnp.where(lo, 0.0, kp_sw).astype(BF16)
    gk_ref[:, 2 * LANES:3 * LANES] = jnp.where(lo, kp_sw, 0.0).astype(BF16)
    gk_ref[:, 3 * LANES:4 * LANES] = jnp.where(lo, 0.0, kp).astype(BF16)
    vp = proj(OFF_GV, LANES)
    vp_sw = pltpu.roll(vp, LANES // 2, 1)
    gv_ref[:, 0 * LANES:1 * LANES] = jnp.where(lo, vp, vp_sw).astype(BF16)
    gv_ref[:, 1 * LANES:2 * LANES] = jnp.where(lo, vp_sw, vp).astype(BF16)

    rq_ref[...] = proj(OFF_RQ, LANES)
    rk_ref[...] = proj(OFF_RK, LANES) * RET_QK ** -0.5
    rv_ref[...] = proj(OFF_RV, RET_HEADS * RET_V)
    gf_ref[...] = _silu(proj(OFF_GF, GROUP_W))
    gb_ref[...] = _silu(proj(OFF_GB, GROUP_W))


def _in_proj(xs, mods, w_in, w_uq, w_ukv, g_mq, g_mkv, g_gq, g_gk, tab_gqa, tab_mla):
    b, t, d = xs.shape
    tm = TOKEN_TILE
    tok = lambda w: pl.BlockSpec((None, tm, w), lambda i, j: (i, j, 0))
    full = lambda a: pl.BlockSpec(a.shape, lambda i, j: (0,) * a.ndim)
    widths_dtypes = [(GROUP_W, F32), (512, BF16), (512, BF16), (256, BF16), (256, BF16), (512, BF16), (256, BF16),
                     (LANES, F32), (LANES, F32), (256, F32), (GROUP_W, F32), (GROUP_W, F32)]
    return pl.pallas_call(
        _in_proj_kernel,
        grid=(b, t // tm),
        in_specs=[tok(d),
                  pl.BlockSpec((None, None, N_MOD, d), lambda i, j: (i, jnp.minimum(j, 1), 0, 0)),
                  full(w_in), full(w_uq), full(w_ukv), full(g_mq), full(g_mkv), full(g_gq), full(g_gk),
                  pl.BlockSpec((3, tm, LANES), lambda i, j: (0, j, 0)),
                  pl.BlockSpec((3, tm, LANES), lambda i, j: (0, j, 0))],
        out_specs=[tok(w) for w, _ in widths_dtypes],
        out_shape=[jax.ShapeDtypeStruct((b, t, w), dt) for w, dt in widths_dtypes],
        compiler_params=_cparams(2),
        name="in_proj",
    )(xs, mods, w_in, w_uq, w_ukv, g_mq, g_mkv, g_gq, g_gk, tab_gqa, tab_mla)


def _conv_kernel(y_ref, dw_ref, b_ref, lg_ref, lb_ref, pw_ref, o_ref, pad_ref, *, n_ctx, rows):
    t = y_ref.shape[0]
    zeros = jnp.zeros((CONV_PAD, GROUP_W), F32)
    bias = b_ref[...]

    def segment(start, length):
        pad_ref[0:CONV_PAD, :] = zeros
        pad_ref[CONV_PAD:CONV_PAD + length, :] = y_ref[start:start + length, :]
        pad_ref[CONV_PAD + length:2 * CONV_PAD + length, :] = zeros

        def chunk(c, carry):
            base = pl.multiple_of(c * rows, rows)
            acc = jnp.zeros((rows, GROUP_W), F32) + bias
            win = pad_ref[pl.ds(base, rows + 2 * CONV_PAD), :]
            for k in range(CONV_WIDTH):
                off = CONV_PAD - CONV_WIDTH // 2 + k
                acc = acc + win[off:off + rows, :] * dw_ref[k:k + 1, :]
            mu = jnp.mean(acc, axis=-1, keepdims=True)
            cen = acc - mu
            var = jnp.mean(cen * cen, axis=-1, keepdims=True)
            z = _silu(cen * lax.rsqrt(var + EPS) * lg_ref[...] + lb_ref[...])
            o_ref[pl.ds(start + base, rows), :] = _dot(z.astype(BF16), pw_ref[...]).astype(BF16)
            return carry

        lax.fori_loop(0, length // rows, chunk, 0)

    segment(0, n_ctx)
    segment(n_ctx, t - n_ctx)


def _conv(yglu, dw, bias, ln_g, ln_b, pw, n_ctx):
    b, t, w = yglu.shape
    full = lambda a: pl.BlockSpec(a.shape, lambda i: (0,) * a.ndim)
    return pl.pallas_call(
        functools.partial(_conv_kernel, n_ctx=n_ctx, rows=64),
        grid=(b,),
        in_specs=[pl.BlockSpec((None, t, w), lambda i: (i, 0, 0)),
                  full(dw), full(bias), full(ln_g), full(ln_b), full(pw)],
        out_specs=pl.BlockSpec((None, t, w), lambda i: (i, 0, 0)),
        out_shape=jax.ShapeDtypeStruct((b, t, w), BF16),
        scratch_shapes=[pltpu.VMEM((t - n_ctx + 2 * CONV_PAD, w), F32)],
        compiler_params=_cparams(1),
        name="conv",
    )(yglu, dw, bias, ln_g, ln_b, pw)


def _attn_kernel(q_ref, k_ref, v_ref, o_ref, *, q_blocks, n_ctx):
    lane = lax.broadcasted_iota(jnp.int32, (1, LANES), 1)
    lo = lane < (LANES // 2)

    def body(n_keys):
        for pr in range(2):
            halves = []
            for half in range(2):
                hd = 2 * pr + half
                qb = q_blocks[hd]
                q = q_ref[:, qb * LANES:(qb + 1) * LANES]
                k = k_ref[0:n_keys, hd * LANES:(hd + 1) * LANES]
                s = _dot_nt(q, k)
                p = jnp.exp(s - jnp.max(s, axis=-1, keepdims=True))
                l = jnp.sum(p, axis=-1, keepdims=True)
                o = _dot(p.astype(BF16), v_ref[0:n_keys, pr * LANES:(pr + 1) * LANES])
                halves.append(o / l)
            o_ref[:, pr * LANES:(pr + 1) * LANES] = jnp.where(lo, halves[0], halves[1]).astype(BF16)

    is_ctx = pl.program_id(1) == 0

    @pl.when(is_ctx)
    def _():
        body(n_ctx)

    @pl.when(jnp.logical_not(is_ctx))
    def _():
        body(k_ref.shape[0])


def _attention(q, k, v, q_blocks, n_ctx, name):
    b, t, qw = q.shape
    tq = TOKEN_TILE
    return pl.pallas_call(
        functools.partial(_attn_kernel, q_blocks=q_blocks, n_ctx=n_ctx),
        grid=(b, t // tq),
        in_specs=[pl.BlockSpec((None, tq, qw), lambda i, j: (i, j, 0)),
                  pl.BlockSpec((None, t, k.shape[2]), lambda i, j: (i, 0, 0)),
                  pl.BlockSpec((None, t, v.shape[2]), lambda i, j: (i, 0, 0))],
        out_specs=pl.BlockSpec((None, tq, 256), lambda i, j: (i, j, 0)),
        out_shape=jax.ShapeDtypeStruct((b, t, 256), BF16),
        compiler_params=_cparams(2),
        name=name,
    )(q, k, v)


def _log_sigmoid(x):
    return jnp.minimum(x, 0.0) - jnp.log(1.0 + jnp.exp(-jnp.abs(x)))


def _ret_kernel(q_ref, k_ref, v_ref, gf_ref, gb_ref, dq_ref, dv_ref, dc_ref, ng_ref, o_ref, acc_ref, *, n_ctx):
    c = RET_CHUNK
    t = q_ref.shape[0]
    n_ctx_chunks = n_ctx // c
    n_chunks = t // c
    qk_w = RET_HEADS * RET_QK
    v_w = RET_HEADS * RET_V

    row_i = lax.broadcasted_iota(jnp.int32, (c, 1), 0).astype(F32)
    qk_head = lax.broadcasted_iota(jnp.int32, (1, qk_w), 1) // RET_QK
    v_head = lax.broadcasted_iota(jnp.int32, (1, v_w), 1) // RET_V
    qk_masks = [(qk_head == hd).astype(F32) for hd in range(RET_HEADS)]
    v_masks = [(v_head == hd).astype(F32) for hd in range(RET_HEADS)]
    state_rows = lax.broadcasted_iota(jnp.int32, (qk_w, 1), 0) // RET_QK
    bd_mask = (state_rows == v_head).astype(F32)
    ci = lax.broadcasted_iota(jnp.int32, (c, RET_HEADS * c), 0).astype(F32)
    cj = (lax.broadcasted_iota(jnp.int32, (c, RET_HEADS * c), 1) % c).astype(F32)

    def direction(d, order_segments, gate_ref, first):
        lg_q = _log_sigmoid(dq_ref[d:d + 1, :])
        lg_v = _log_sigmoid(dv_ref[d:d + 1, :])
        lg_c = _log_sigmoid(dc_ref[d:d + 1, :])
        gain = ng_ref[d:d + 1, :]
        if d == 0:
            diff = ci - cj
            q_dec = jnp.exp(lg_q * (row_i + 1.0))
            k_dec = jnp.exp(lg_q * (c - 1.0 - row_i))
        else:
            diff = cj - ci
            q_dec = jnp.exp(lg_q * (c - row_i))
            k_dec = jnp.exp(lg_q * row_i)
        decay = jnp.where(diff >= 0.0, jnp.exp(lg_c * jnp.maximum(diff, 0.0)), 0.0)
        chunk_dec = jnp.exp(lg_v * float(c))

        def step(r0, state):
            q = q_ref[pl.ds(r0, c), :]
            k = k_ref[pl.ds(r0, c), :]
            v = v_ref[pl.ds(r0, c), :]
            k_bd = jnp.concatenate([k * m for m in qk_masks], axis=0).astype(BF16)
            scores = _dot_nt(q.astype(BF16), k_bd) * decay
            v_bd = jnp.concatenate([v * m for m in v_masks], axis=0).astype(BF16)
            inner = _dot(scores.astype(BF16), v_bd)
            cross = _dot((q * q_dec).astype(BF16), state.astype(BF16))
            o = inner + cross
            kv = _dot_tn((k * k_dec).astype(BF16), v.astype(BF16))
            new_state = state * chunk_dec + kv * bd_mask
            o2 = o * o
            r = jnp.zeros_like(o)
            for m in v_masks:
                ss = jnp.sum(o2 * m, axis=-1, keepdims=True)
                r = r + m * lax.rsqrt(ss * (1.0 / RET_V) + EPS)
            y = o * r * gain * gate_ref[pl.ds(r0, c), :]
            if first:
                acc_ref[pl.ds(r0, c), :] = y
            else:
                o_ref[pl.ds(r0, c), :] = (acc_ref[pl.ds(r0, c), :] + y).astype(BF16)
            return new_state

        state = jnp.zeros((qk_w, v_w), F32)
        for lo_chunk, n, reverse in order_segments:
            def seg_step(i, st, lo_chunk=lo_chunk, n=n, reverse=reverse):
                ch = lo_chunk + (n - 1 - i if reverse else i)
                return step(pl.multiple_of(ch * c, c), st)
            state = lax.fori_loop(0, n, seg_step, state)

    n_lat_chunks = n_chunks - n_ctx_chunks
    direction(0, [(0, n_chunks, False)], gf_ref, True)
    direction(1, [(0, n_ctx_chunks, True), (n_ctx_chunks, n_lat_chunks, True)], gb_ref, False)


def _retention(rq, rk, rv, gf, gb, dec_q, dec_v, dec_c, norm_g, n_ctx):
    b, t, _ = rq.shape
    seq = lambda w: pl.BlockSpec((None, t, w), lambda i: (i, 0, 0))
    full = lambda a: pl.BlockSpec(a.shape, lambda i: (0,) * a.ndim)
    return pl.pallas_call(
        functools.partial(_ret_kernel, n_ctx=n_ctx),
        grid=(b,),
        in_specs=[seq(128), seq(128), seq(256), seq(256), seq(256),
                  full(dec_q), full(dec_v), full(dec_c), full(norm_g)],
        out_specs=seq(256),
        out_shape=jax.ShapeDtypeStruct((b, t, 256), BF16),
        scratch_shapes=[pltpu.VMEM((t, 256), F32)],
        compiler_params=_cparams(1),
        name="retention",
    )(rq, rk, rv, gf, gb, dec_q, dec_v, dec_c, norm_g)


def _out_mlp_kernel(x_ref, ya_ref, yb_ref, yc_ref, yd_ref, mod_ref, wo_ref, w1_ref, w2_ref, o_ref, *, ff_chunk):
    x = x_ref[...]
    gw = GROUP_W
    y = (_dot(ya_ref[...], wo_ref[0 * gw:1 * gw, :]) + _dot(yb_ref[...], wo_ref[1 * gw:2 * gw, :])
         + _dot(yc_ref[...], wo_ref[2 * gw:3 * gw, :]) + _dot(yd_ref[...], wo_ref[3 * gw:4 * gw, :]))
    x1 = x + mod_ref[2:3, :] * y
    h = (_rms(x1) * (1.0 + mod_ref[4:5, :]) + mod_ref[3:4, :]).astype(BF16)
    acc = jnp.zeros_like(x1)
    for j in range(w1_ref.shape[1] // ff_chunk):
        a = jnp.maximum(_dot(h, w1_ref[:, j * ff_chunk:(j + 1) * ff_chunk]), 0.0)
        acc = acc + _dot((a * a).astype(BF16), w2_ref[j * ff_chunk:(j + 1) * ff_chunk, :])
    o_ref[...] = x1 + mod_ref[5:6, :] * acc


def _out_mlp(xs, ya, yb, yc, yd, mods, w_out, w1, w2):
    b, t, d = xs.shape
    tm = TOKEN_TILE
    tok = lambda w: pl.BlockSpec((None, tm, w), lambda i, j: (i, j, 0))
    full = lambda a: pl.BlockSpec(a.shape, lambda i, j: (0,) * a.ndim)
    return pl.pallas_call(
        functools.partial(_out_mlp_kernel, ff_chunk=1024),
        grid=(b, t // tm),
        in_specs=[tok(d), tok(256), tok(256), tok(256), tok(256),
                  pl.BlockSpec((None, None, N_MOD, d), lambda i, j: (i, jnp.minimum(j, 1), 0, 0)),
                  full(w_out), full(w1), full(w2)],
        out_specs=tok(d),
        out_shape=jax.ShapeDtypeStruct((b, t, d), F32),
        input_output_aliases={0: 0},
        compiler_params=_cparams(2),
        name="out_mlp",
    )(xs, ya, yb, yc, yd, mods, w_out, w1, w2)


def _final_kernel(x_ref, g_ref, o_ref):
    o_ref[...] = _rms(x_ref[...]) * g_ref[...]


def _final_norm(xs, gain, n_ctx):
    b, t, d = xs.shape
    tm = TOKEN_TILE
    skip = n_ctx // tm
    return pl.pallas_call(
        _final_kernel,
        grid=(b, (t - n_ctx) // tm),
        in_specs=[pl.BlockSpec((None, tm, d), lambda i, j: (i, j + skip, 0)),
                  pl.BlockSpec((1, d), lambda i, j: (0, 0))],
        out_specs=pl.BlockSpec((None, tm, d), lambda i, j: (i, j, 0)),
        out_shape=jax.ShapeDtypeStruct((b, t - n_ctx, d), F32),
        compiler_params=_cparams(2),
        name="final_norm",
    )(xs, gain)


def _rope_tables(n_ctx, n_lat):
    t = n_ctx + n_lat
    n = np.arange(n_lat)
    row, col = (n // GRID_W).astype(np.float64), (n % GRID_W).astype(np.float64)

    def block(rot_start, rot_dim):
        cos = np.ones((t, LANES)); up = np.zeros((t, LANES)); dn = np.zeros((t, LANES))
        half = rot_dim // 2
        q = half // 2
        freqs = ROPE_BASE ** (-np.arange(q, dtype=np.float64) / q)
        for axis, pos in enumerate((row, col)):
            ang = pos[:, None] * freqs[None, :]
            base = rot_start + axis * half
            cos[n_ctx:, base:base + q] = np.cos(ang)
            cos[n_ctx:, base + q:base + 2 * q] = np.cos(ang)
            up[n_ctx:, base:base + q] = -np.sin(ang)
            dn[n_ctx:, base + q:base + 2 * q] = np.sin(ang)
        return cos, up, dn

    def merge(blocks):
        cos = np.ones((t, LANES)); up = np.zeros((t, LANES)); dn = np.zeros((t, LANES))
        for c_, u_, d_ in blocks:
            cos = np.where(c_ != 1.0, c_, cos); up = up + u_; dn = dn + d_
        return np.stack([cos, up, dn]).astype(np.float32)

    gqa = merge([block(0, GQA_HEAD_DIM), block(GQA_HEAD_DIM, GQA_HEAD_DIM)])
    mla = merge([block(MLA_NOPE, MLA_ROPE)])
    return jnp.asarray(gqa), jnp.asarray(mla)


def _layout_w_in(w_in):
    d = w_in.shape[0]
    kpe_src = OFF_KPE
    w = w_in.astype(BF16)
    out = jnp.concatenate([w[:, :kpe_src], jnp.zeros((d, MLA_NOPE), BF16), w[:, kpe_src:kpe_src + MLA_ROPE],
                           jnp.zeros((d, LANES - MLA_NOPE - MLA_ROPE), BF16), w[:, kpe_src + MLA_ROPE:]], axis=1)
    assert out.shape[1] == N_IN
    return out


def _layout_w_uq(w_uq):
    r = w_uq.shape[0]
    w = w_uq.reshape(r, MLA_HEADS, MLA_NOPE + MLA_ROPE)
    w = jnp.pad(w, ((0, 0), (0, 0), (0, LANES - MLA_NOPE - MLA_ROPE)))
    return w.reshape(r, MLA_HEADS * LANES).astype(BF16)


def _layout_w_ukv(w_ukv):
    r = w_ukv.shape[0]
    w = w_ukv.reshape(r, MLA_HEADS, MLA_NOPE + MLA_V)
    k = jnp.pad(w[:, :, :MLA_NOPE], ((0, 0), (0, 0), (0, LANES - MLA_NOPE))).reshape(r, MLA_HEADS * LANES)
    v = w[:, :, MLA_NOPE:].reshape(r, MLA_HEADS * MLA_V)
    return jnp.concatenate([k, v], axis=1).astype(BF16)


def kernel(x, c, ctx, c_ctx, w_mod, b_mod, w_in, w_out, conv_dw, conv_b, conv_ln_g, conv_ln_b, conv_pw, mla_q_g, mla_kv_g, mla_uq, mla_ukv, gqa_q_g, gqa_k_g, ret_decay, ret_norm_g, mlp_w1, mlp_w2, final_g):
    b, n_lat, d = x.shape
    n_ctx = ctx.shape[1]
    depth = w_mod.shape[0]
    assert n_ctx % TOKEN_TILE == 0 and n_lat % TOKEN_TILE == 0 and TOKEN_TILE % RET_CHUNK == 0
    assert n_ctx // TOKEN_TILE == 1, "the modulation row select assumes one context tile"

    xs = jnp.concatenate([ctx, x], axis=1)
    rows = ((b + 1 + 7) // 8) * 8
    cond = jnp.zeros((rows, d), F32).at[:b].set(c).at[b].set(c_ctx)
    mod_all = _adaln(cond, w_mod, b_mod).reshape(depth, rows, N_MOD, d)
    tab_gqa, tab_mla = _rope_tables(n_ctx, n_lat)

    for layer in range(depth):
        m = mod_all[layer]
        mods = jnp.stack([jnp.broadcast_to(m[b], (b, N_MOD, d)), m[:b]], axis=1)
        (yglu, mq, mk, mv, gq, gk, gv, rq, rk, rv, gf, gb) = _in_proj(
            xs, mods, _layout_w_in(w_in[layer]), _layout_w_uq(mla_uq[layer]), _layout_w_ukv(mla_ukv[layer]),
            mla_q_g[layer][None, :], mla_kv_g[layer][None, :],
            jnp.tile(gqa_q_g[layer], 2)[None, :], jnp.tile(gqa_k_g[layer], 2)[None, :], tab_gqa, tab_mla)
        ya = _conv(yglu, conv_dw[layer], conv_b[layer][None, :], conv_ln_g[layer][None, :],
                   conv_ln_b[layer][None, :], conv_pw[layer].astype(BF16), n_ctx)
        yb = _attention(mq, mk, mv, (0, 1, 2, 3), n_ctx, "mla_attn")
        yc = _attention(gq, gk, gv, (0, 0, 1, 1), n_ctx, "gqa_attn")
        dec = ret_decay[layer].astype(F32)
        yd = _retention(rq, rk, rv, gf, gb,
                        jnp.repeat(dec, RET_QK, axis=1), jnp.repeat(dec, RET_V, axis=1),
                        jnp.repeat(dec, RET_CHUNK, axis=1),
                        ret_norm_g[layer].reshape(2, RET_HEADS * RET_V), n_ctx)
        xs = _out_mlp(xs, ya, yb, yc, yd, mods, w_out[layer].astype(BF16),
                      mlp_w1[layer].astype(BF16), mlp_w2[layer].astype(BF16))
    return _final_norm(xs, final_g[None, :], n_ctx)
```

```python
import functools

import numpy as np
import jax
import jax.numpy as jnp
from jax import lax
from jax.experimental import pallas as pl
from jax.experimental.pallas import tpu as pltpu

F32 = jnp.float32
BF16 = jnp.bfloat16

GRID_W = 64
N_MOD = 6
EPS = 1e-6
ROPE_BASE = 10000.0
GROUP_W = 256
CONV_WIDTH = 31
CONV_PAD = 16
MLA_HEADS = 4
MLA_NOPE = 64
MLA_ROPE = 32
MLA_V = 64
MLA_Q_RANK = 256
MLA_KV_RANK = 128
GQA_HEADS = 4
GQA_KV_HEADS = 2
GQA_HEAD_DIM = 64
RET_HEADS = 4
RET_QK = 32
RET_V = 64
RET_CHUNK = 128
LANES = 128
LOG2_E = 1.4426950408889634
TOKEN_TILE = 256
VMEM_LIMIT = 56 * 1024 * 1024

OFF_A, OFF_CQ, OFF_CKV, OFF_KPE, OFF_GQ, OFF_GK, OFF_GV = 0, 512, 768, 896, 1024, 1280, 1408
OFF_RQ, OFF_RK, OFF_RV, OFF_GF, OFF_GB, N_IN = 1536, 1664, 1792, 2048, 2304, 2560


def _cparams(n_axes):
    return pltpu.CompilerParams(dimension_semantics=("arbitrary",) * n_axes,
                                vmem_limit_bytes=VMEM_LIMIT)


def _rms(x):
    return x * lax.rsqrt(jnp.mean(x * x, axis=-1, keepdims=True) + EPS)


def _sigmoid(x):
    return 1.0 / (1.0 + jnp.exp(-x))


def _silu(x):
    return x * _sigmoid(x)


def _dot(a, b):
    return jnp.dot(a, b, preferred_element_type=F32)


def _dot_nt(a, b):
    return lax.dot_general(a, b, (((1,), (1,)), ((), ())), preferred_element_type=F32)


def _dot_tn(a, b):
    return lax.dot_general(a, b, (((0,), (0,)), ((), ())), preferred_element_type=F32)


def _rope(x, cos, sin_up, sin_dn, shift):
    n = x.shape[-1]
    return x * cos + pltpu.roll(x, n - shift, 1) * sin_up + pltpu.roll(x, shift, 1) * sin_dn


def _adaln_kernel(cond_ref, w_ref, b_ref, o_ref):
    cond = cond_ref[...]
    o_ref[...] = _dot(_silu(cond).astype(BF16), w_ref[...].astype(BF16)) + b_ref[...]


def _adaln(cond, w_mod, b_mod):
    depth, d, n = w_mod.shape
    r = cond.shape[0]
    tn = 1536
    return pl.pallas_call(
        _adaln_kernel,
        grid=(depth, n // tn),
        in_specs=[pl.BlockSpec((r, d), lambda l, j: (0, 0)),
                  pl.BlockSpec((None, d, tn), lambda l, j: (l, 0, j)),
                  pl.BlockSpec((None, 1, tn), lambda l, j: (l, 0, j))],
        out_specs=pl.BlockSpec((None, r, tn), lambda l, j: (l, 0, j)),
        out_shape=jax.ShapeDtypeStruct((depth, r, n), F32),
        compiler_params=_cparams(2),
        name="adaln",
    )(cond, w_mod, b_mod.reshape(depth, 1, n))


def _in_proj_kernel(x_ref, mod_ref, w_in_ref, w_uq_ref, w_ukv_ref, g_mq_ref, g_mkv_ref, g_gq_ref, g_gk_ref,
                    tg_ref, tm_ref,
                    yglu_ref, mq_ref, mk_ref, mv_ref, gq_ref, gk_ref, gv_ref,
                    rq_ref, rk_ref, rv_ref, gf_ref, gb_ref):
    x = x_ref[...]
    h = (_rms(x) * (1.0 + mod_ref[1:2, :]) + mod_ref[0:1, :]).astype(BF16)

    def proj(off, width):
        return _dot(h, w_in_ref[:, off:off + width])

    lane = lax.broadcasted_iota(jnp.int32, (1, LANES), 1)
    lo = lane < (LANES // 2)

    a = proj(OFF_A, 2 * GROUP_W)
    yglu_ref[...] = a[:, :GROUP_W] * _sigmoid(a[:, GROUP_W:])

    cq = (_rms(proj(OFF_CQ, MLA_Q_RANK)) * g_mq_ref[...]).astype(BF16)
    q = _dot(cq, w_uq_ref[...])
    m_cos, m_up, m_dn = tm_ref[0], tm_ref[1], tm_ref[2]
    q_scale = (MLA_NOPE + MLA_ROPE) ** -0.5 * LOG2_E
    for hd in range(MLA_HEADS):
        qh = _rope(q[:, hd * LANES:(hd + 1) * LANES], m_cos, m_up, m_dn, MLA_ROPE // 4)
        mq_ref[:, hd * LANES:(hd + 1) * LANES] = (qh * q_scale).astype(BF16)
    ckv = (_rms(proj(OFF_CKV, MLA_KV_RANK)) * g_mkv_ref[...]).astype(BF16)
    kv = _dot(ckv, w_ukv_ref[...])
    kpe = _rope(proj(OFF_KPE, LANES), m_cos, m_up, m_dn, MLA_ROPE // 4)
    for hd in range(MLA_HEADS):
        mk_ref[:, hd * LANES:(hd + 1) * LANES] = (kv[:, hd * LANES:(hd + 1) * LANES] + kpe).astype(BF16)
    for hd in range(MLA_HEADS):
        vh = kv[:, (MLA_HEADS + hd) * LANES:(MLA_HEADS + hd + 1) * LANES]
        mv_ref[:, hd * LANES:(hd + 1) * LANES] = jnp.where(lo, vh, 1.0).astype(BF16)

    g_cos, g_up, g_dn = tg_ref[0], tg_ref[1], tg_ref[2]

    def pair_norm_rope(xp, gain):
        x2 = xp * xp
        s_lo = jnp.sum(jnp.where(lo, x2, 0.0), axis=-1, keepdims=True)
        s_hi = jnp.sum(jnp.where(lo, 0.0, x2), axis=-1, keepdims=True)
        r = jnp.where(lo, lax.rsqrt(s_lo * (1.0 / GQA_HEAD_DIM) + EPS), lax.rsqrt(s_hi * (1.0 / GQA_HEAD_DIM) + EPS))
        return _rope(xp * r * gain, g_cos, g_up, g_dn, GQA_HEAD_DIM // 4)

    gq = proj(OFF_GQ, GQA_HEADS * GQA_HEAD_DIM)
    for pr in range(2):
        qp = pair_norm_rope(gq[:, pr * LANES:(pr + 1) * LANES], g_gq_ref[...])
        gq_ref[:, pr * LANES:(pr + 1) * LANES] = (qp * (GQA_HEAD_DIM ** -0.5 * LOG2_E)).astype(BF16)
    kp = pair_norm_rope(proj(OFF_GK, LANES), g_gk_ref[...])
    kp_sw = pltpu.roll(kp, LANES // 2, 1)
    gk_ref[:, 0 * LANES:1 * LANES] = jnp.where(lo, kp, 0.0).astype(BF16)
    gk_ref[:, 1 * LANES:2 * LANES] = jnp.where(lo, 0.0, kp_sw).astype(BF16)
    gk_ref[:, 2 * LANES:3 * LANES] = jnp.where(lo, kp_sw, 0.0).astype(BF16)
    gk_ref[:, 3 * LANES:4 * LANES] = jnp.where(lo, 0.0, kp).astype(BF16)
    vp = proj(OFF_GV, LANES)
    vp_sw = pltpu.roll(vp, LANES // 2, 1)
    gv_ref[:, 0 * LANES:1 * LANES] = jnp.where(lo, vp, 1.0).astype(BF16)
    gv_ref[:, 1 * LANES:2 * LANES] = jnp.where(lo, vp_sw, 1.0).astype(BF16)

    rq_ref[...] = proj(OFF_RQ, LANES)
    rk_ref[...] = proj(OFF_RK, LANES) * RET_QK ** -0.5
    rv_ref[...] = proj(OFF_RV, RET_HEADS * RET_V)
    gf_ref[...] = _silu(proj(OFF_GF, GROUP_W))
    gb_ref[...] = _silu(proj(OFF_GB, GROUP_W))


def _in_proj(xs, mods, w_in, w_uq, w_ukv, g_mq, g_mkv, g_gq, g_gk, tab_gqa, tab_mla):
    b, t, d = xs.shape
    tm = TOKEN_TILE
    tok = lambda w: pl.BlockSpec((None, tm, w), lambda i, j: (i, j, 0))
    full = lambda a: pl.BlockSpec(a.shape, lambda i, j: (0,) * a.ndim)
    widths_dtypes = [(GROUP_W, F32), (512, BF16), (512, BF16), (512, BF16), (256, BF16), (512, BF16), (256, BF16),
                     (LANES, F32), (LANES, F32), (256, F32), (GROUP_W, F32), (GROUP_W, F32)]
    return pl.pallas_call(
        _in_proj_kernel,
        grid=(b, t // tm),
        in_specs=[tok(d),
                  pl.BlockSpec((None, None, N_MOD, d), lambda i, j: (i, jnp.minimum(j, 1), 0, 0)),
                  full(w_in), full(w_uq), full(w_ukv), full(g_mq), full(g_mkv), full(g_gq), full(g_gk),
                  pl.BlockSpec((3, tm, LANES), lambda i, j: (0, j, 0)),
                  pl.BlockSpec((3, tm, LANES), lambda i, j: (0, j, 0))],
        out_specs=[tok(w) for w, _ in widths_dtypes],
        out_shape=[jax.ShapeDtypeStruct((b, t, w), dt) for w, dt in widths_dtypes],
        compiler_params=_cparams(2),
        name="in_proj",
    )(xs, mods, w_in, w_uq, w_ukv, g_mq, g_mkv, g_gq, g_gk, tab_gqa, tab_mla)


def _conv_kernel(y_ref, dw_ref, b_ref, lg_ref, lb_ref, pw_ref, o_ref, pad_ref, *, n_ctx, rows):
    t = y_ref.shape[0]
    zeros = jnp.zeros((CONV_PAD, GROUP_W), F32)
    bias = b_ref[...]

    def segment(start, length):
        pad_ref[0:CONV_PAD, :] = zeros
        pad_ref[CONV_PAD:CONV_PAD + length, :] = y_ref[start:start + length, :]
        pad_ref[CONV_PAD + length:2 * CONV_PAD + length, :] = zeros

        def chunk(c, carry):
            base = pl.multiple_of(c * rows, rows)
            acc = jnp.zeros((rows, GROUP_W), F32) + bias
            win = pad_ref[pl.ds(base, rows + 2 * CONV_PAD), :]
            for k in range(CONV_WIDTH):
                off = CONV_PAD - CONV_WIDTH // 2 + k
                acc = acc + win[off:off + rows, :] * dw_ref[k:k + 1, :]
            mu = jnp.mean(acc, axis=-1, keepdims=True)
            cen = acc - mu
            var = jnp.mean(cen * cen, axis=-1, keepdims=True)
            z = _silu(cen * lax.rsqrt(var + EPS) * lg_ref[...] + lb_ref[...])
            o_ref[pl.ds(start + base, rows), :] = _dot(z.astype(BF16), pw_ref[...]).astype(BF16)
            return carry

        lax.fori_loop(0, length // rows, chunk, 0)

    segment(0, n_ctx)
    segment(n_ctx, t - n_ctx)


def _conv(yglu, dw, bias, ln_g, ln_b, pw, n_ctx):
    b, t, w = yglu.shape
    full = lambda a: pl.BlockSpec(a.shape, lambda i: (0,) * a.ndim)
    return pl.pallas_call(
        functools.partial(_conv_kernel, n_ctx=n_ctx, rows=64),
        grid=(b,),
        in_specs=[pl.BlockSpec((None, t, w), lambda i: (i, 0, 0)),
                  full(dw), full(bias), full(ln_g), full(ln_b), full(pw)],
        out_specs=pl.BlockSpec((None, t, w), lambda i: (i, 0, 0)),
        out_shape=jax.ShapeDtypeStruct((b, t, w), BF16),
        scratch_shapes=[pltpu.VMEM((t - n_ctx + 2 * CONV_PAD, w), F32)],
        compiler_params=_cparams(1),
        name="conv",
    )(yglu, dw, bias, ln_g, ln_b, pw)


def _attn_kernel(q_ref, qn_ref, k_ref, v_ref, o_ref, s_ref, m_ref, mb_ref, acc_ref, *, q_blocks, v_blocks, n_ctx, tk):
    tq = q_ref.shape[0]
    n_keys = k_ref.shape[0]
    half_lanes = LANES // 2
    lo = lax.broadcasted_iota(jnp.int32, (1, LANES), 1) < half_lanes

    def q_of(hd, ref=None):
        ref = q_ref if ref is None else ref
        return ref[:, q_blocks[hd] * LANES:(q_blocks[hd] + 1) * LANES]

    def normalized_pair(acc_even, acc_odd):
        return jnp.where(lo, acc_even / pltpu.roll(acc_even, half_lanes, 1),
                         pltpu.roll(acc_odd, half_lanes, 1) / acc_odd)

    def ctx_tile():
        for pr in range(2):
            accs = []
            for half in range(2):
                hd = 2 * pr + half
                s = _dot_nt(q_of(hd), k_ref[0:n_ctx, hd * LANES:(hd + 1) * LANES])
                p = jnp.exp2(s - jnp.max(s, axis=-1, keepdims=True))
                accs.append(_dot(p.astype(BF16), v_ref[0:n_ctx, v_blocks[hd] * LANES:(v_blocks[hd] + 1) * LANES]))
            o_ref[:, pr * LANES:(pr + 1) * LANES] = normalized_pair(*accs).astype(BF16)

    def scores_chunk(hd, kc, ref=None):
        slot = hd % 2
        s = _dot_nt(q_of(hd, ref), k_ref[kc * tk:(kc + 1) * tk, hd * LANES:(hd + 1) * LANES])
        s_ref[slot, :, kc * tk:(kc + 1) * tk] = s
        mx = s[:, 0:LANES]
        for i in range(1, tk // LANES):
            mx = jnp.maximum(mx, s[:, i * LANES:(i + 1) * LANES])
        m_ref[slot] = mx if kc == 0 else jnp.maximum(m_ref[slot], mx)

    def scores_finish(hd):
        slot = hd % 2
        mb_ref[slot] = jnp.broadcast_to(jnp.max(m_ref[slot], axis=-1, keepdims=True), (tq, LANES))

    def values_chunk(hd, kc):
        slot = hd % 2
        mb = mb_ref[slot]
        p = jnp.concatenate(
            [jnp.exp2(s_ref[slot, :, kc * tk + i * LANES:kc * tk + (i + 1) * LANES] - mb).astype(BF16)
             for i in range(tk // LANES)], axis=1)
        pv = _dot(p, v_ref[kc * tk:(kc + 1) * tk, v_blocks[hd] * LANES:(v_blocks[hd] + 1) * LANES])
        acc_ref[slot] = pv if kc == 0 else acc_ref[slot] + pv

    def latent_tile():
        n_chunks = n_keys // tk

        @pl.when(pl.program_id(1) == n_ctx // tq)
        def _():
            for kc in range(n_chunks):
                scores_chunk(0, kc)
            scores_finish(0)

        for hd in range(4):
            nxt = (hd + 1) % 4
            for kc in range(n_chunks):
                values_chunk(hd, kc)
                scores_chunk(nxt, kc, q_ref if hd < 3 else qn_ref)
            scores_finish(nxt)
            if hd % 2 == 1:
                pr = hd // 2
                o_ref[:, pr * LANES:(pr + 1) * LANES] = normalized_pair(acc_ref[0], acc_ref[1]).astype(BF16)

    is_ctx = pl.program_id(1) == 0
    pl.when(is_ctx)(ctx_tile)
    pl.when(jnp.logical_not(is_ctx))(latent_tile)


def _attention(q, k, v, q_blocks, v_blocks, n_ctx, name):
    b, t, qw = q.shape
    tq = TOKEN_TILE
    return pl.pallas_call(
        functools.partial(_attn_kernel, q_blocks=q_blocks, v_blocks=v_blocks, n_ctx=n_ctx, tk=256),
        grid=(b, t // tq),
        in_specs=[pl.BlockSpec((None, tq, qw), lambda i, j: (i, j, 0)),
                  pl.BlockSpec((None, tq, qw), lambda i, j: (i, jnp.minimum(j + 1, t // tq - 1), 0)),
                  pl.BlockSpec((None, t, k.shape[2]), lambda i, j: (i, 0, 0)),
                  pl.BlockSpec((None, t, v.shape[2]), lambda i, j: (i, 0, 0))],
        out_specs=pl.BlockSpec((None, tq, 256), lambda i, j: (i, j, 0)),
        out_shape=jax.ShapeDtypeStruct((b, t, 256), BF16),
        scratch_shapes=[pltpu.VMEM((2, tq, t), F32), pltpu.VMEM((2, tq, LANES), F32),
                        pltpu.VMEM((2, tq, LANES), F32), pltpu.VMEM((2, tq, LANES), F32)],
        compiler_params=_cparams(2),
        name=name,
    )(q, q, k, v)


def _log_sigmoid(x):
    return jnp.minimum(x, 0.0) - jnp.log(1.0 + jnp.exp(-jnp.abs(x)))


def _ret_kernel(q_ref, k_ref, v_ref, gf_ref, gb_ref, dq_ref, dv_ref, dc_ref, ng_ref, o_ref, acc_ref, *, n_ctx):
    c = RET_CHUNK
    t = q_ref.shape[0]
    n_ctx_chunks = n_ctx // c
    n_chunks = t // c
    qk_w = RET_HEADS * RET_QK
    v_w = RET_HEADS * RET_V

    row_i = lax.broadcasted_iota(jnp.int32, (c, 1), 0).astype(F32)
    qk_head = lax.broadcasted_iota(jnp.int32, (1, qk_w), 1) // RET_QK
    v_head = lax.broadcasted_iota(jnp.int32, (1, v_w), 1) // RET_V
    qk_masks = [(qk_head == hd).astype(F32) for hd in range(RET_HEADS)]
    v_masks = [(v_head == hd).astype(F32) for hd in range(RET_HEADS)]
    state_rows = lax.broadcasted_iota(jnp.int32, (qk_w, 1), 0) // RET_QK
    bd_mask = (state_rows == v_head).astype(F32)
    ci = lax.broadcasted_iota(jnp.int32, (c, RET_HEADS * c), 0).astype(F32)
    cj = (lax.broadcasted_iota(jnp.int32, (c, RET_HEADS * c), 1) % c).astype(F32)

    def direction(d, order_segments, gate_ref, first):
        lg_q = _log_sigmoid(dq_ref[d:d + 1, :])
        lg_v = _log_sigmoid(dv_ref[d:d + 1, :])
        lg_c = _log_sigmoid(dc_ref[d:d + 1, :])
        gain = ng_ref[d:d + 1, :]
        if d == 0:
            diff = ci - cj
            q_dec = jnp.exp(lg_q * (row_i + 1.0))
            k_dec = jnp.exp(lg_q * (c - 1.0 - row_i))
        else:
            diff = cj - ci
            q_dec = jnp.exp(lg_q * (c - row_i))
            k_dec = jnp.exp(lg_q * row_i)
        decay = jnp.where(diff >= 0.0, jnp.exp(lg_c * jnp.maximum(diff, 0.0)), 0.0)
        chunk_dec = jnp.exp(lg_v * float(c))

        def step(r0, state):
            q = q_ref[pl.ds(r0, c), :]
            k = k_ref[pl.ds(r0, c), :]
            v = v_ref[pl.ds(r0, c), :]
            k_bd = jnp.concatenate([k * m for m in qk_masks], axis=0).astype(BF16)
            scores = _dot_nt(q.astype(BF16), k_bd) * decay
            v_bd = jnp.concatenate([v * m for m in v_masks], axis=0).astype(BF16)
            inner = _dot(scores.astype(BF16), v_bd)
            cross = _dot((q * q_dec).astype(BF16), state.astype(BF16))
            o = inner + cross
            kv = _dot_tn((k * k_dec).astype(BF16), v.astype(BF16))
            new_state = state * chunk_dec + kv * bd_mask
            o2 = o * o
            r = jnp.zeros_like(o)
            for m in v_masks:
                ss = jnp.sum(o2 * m, axis=-1, keepdims=True)
                r = r + m * lax.rsqrt(ss * (1.0 / RET_V) + EPS)
            y = o * r * gain * gate_ref[pl.ds(r0, c), :]
            if first:
                acc_ref[pl.ds(r0, c), :] = y
            else:
                o_ref[pl.ds(r0, c), :] = (acc_ref[pl.ds(r0, c), :] + y).astype(BF16)
            return new_state

        state = jnp.zeros((qk_w, v_w), F32)
        for lo_chunk, n, reverse in order_segments:
            def seg_step(i, st, lo_chunk=lo_chunk, n=n, reverse=reverse):
                ch = lo_chunk + (n - 1 - i if reverse else i)
                return step(pl.multiple_of(ch * c, c), st)
            state = lax.fori_loop(0, n, seg_step, state)

    n_lat_chunks = n_chunks - n_ctx_chunks
    direction(0, [(0, n_chunks, False)], gf_ref, True)
    direction(1, [(0, n_ctx_chunks, True), (n_ctx_chunks, n_lat_chunks, True)], gb_ref, False)


def _retention(rq, rk, rv, gf, gb, dec_q, dec_v, dec_c, norm_g, n_ctx):
    b, t, _ = rq.shape
    seq = lambda w: pl.BlockSpec((None, t, w), lambda i: (i, 0, 0))
    full = lambda a: pl.BlockSpec(a.shape, lambda i: (0,) * a.ndim)
    return pl.pallas_call(
        functools.partial(_ret_kernel, n_ctx=n_ctx),
        grid=(b,),
        in_specs=[seq(128), seq(128), seq(256), seq(256), seq(256),
                  full(dec_q), full(dec_v), full(dec_c), full(norm_g)],
        out_specs=seq(256),
        out_shape=jax.ShapeDtypeStruct((b, t, 256), BF16),
        scratch_shapes=[pltpu.VMEM((t, 256), F32)],
        compiler_params=_cparams(1),
        name="retention",
    )(rq, rk, rv, gf, gb, dec_q, dec_v, dec_c, norm_g)


def _out_mlp_kernel(x_ref, ya_ref, yb_ref, yc_ref, yd_ref, mod_ref, wo_ref, w1_ref, w2_ref, o_ref, *, ff_chunk):
    x = x_ref[...]
    gw = GROUP_W
    y = (_dot(ya_ref[...], wo_ref[0 * gw:1 * gw, :]) + _dot(yb_ref[...], wo_ref[1 * gw:2 * gw, :])
         + _dot(yc_ref[...], wo_ref[2 * gw:3 * gw, :]) + _dot(yd_ref[...], wo_ref[3 * gw:4 * gw, :]))
    x1 = x + mod_ref[2:3, :] * y
    h = (_rms(x1) * (1.0 + mod_ref[4:5, :]) + mod_ref[3:4, :]).astype(BF16)
    acc = jnp.zeros_like(x1)
    for j in range(w1_ref.shape[1] // ff_chunk):
        a = jnp.maximum(_dot(h, w1_ref[:, j * ff_chunk:(j + 1) * ff_chunk]), 0.0)
        acc = acc + _dot((a * a).astype(BF16), w2_ref[j * ff_chunk:(j + 1) * ff_chunk, :])
    o_ref[...] = x1 + mod_ref[5:6, :] * acc


def _out_mlp(xs, ya, yb, yc, yd, mods, w_out, w1, w2):
    b, t, d = xs.shape
    tm = TOKEN_TILE
    tok = lambda w: pl.BlockSpec((None, tm, w), lambda i, j: (i, j, 0))
    full = lambda a: pl.BlockSpec(a.shape, lambda i, j: (0,) * a.ndim)
    return pl.pallas_call(
        functools.partial(_out_mlp_kernel, ff_chunk=1024),
        grid=(b, t // tm),
        in_specs=[tok(d), tok(256), tok(256), tok(256), tok(256),
                  pl.BlockSpec((None, None, N_MOD, d), lambda i, j: (i, jnp.minimum(j, 1), 0, 0)),
                  full(w_out), full(w1), full(w2)],
        out_specs=tok(d),
        out_shape=jax.ShapeDtypeStruct((b, t, d), F32),
        input_output_aliases={0: 0},
        compiler_params=_cparams(2),
        name="out_mlp",
    )(xs, ya, yb, yc, yd, mods, w_out, w1, w2)


def _final_kernel(x_ref, g_ref, o_ref):
    o_ref[...] = _rms(x_ref[...]) * g_ref[...]


def _final_norm(xs, gain, n_ctx):
    b, t, d = xs.shape
    tm = TOKEN_TILE
    skip = n_ctx // tm
    return pl.pallas_call(
        _final_kernel,
        grid=(b, (t - n_ctx) // tm),
        in_specs=[pl.BlockSpec((None, tm, d), lambda i, j: (i, j + skip, 0)),
                  pl.BlockSpec((1, d), lambda i, j: (0, 0))],
        out_specs=pl.BlockSpec((None, tm, d), lambda i, j: (i, j, 0)),
        out_shape=jax.ShapeDtypeStruct((b, t - n_ctx, d), F32),
        compiler_params=_cparams(2),
        name="final_norm",
    )(xs, gain)


def _rope_tables(n_ctx, n_lat):
    t = n_ctx + n_lat
    n = np.arange(n_lat)
    row, col = (n // GRID_W).astype(np.float64), (n % GRID_W).astype(np.float64)

    def block(rot_start, rot_dim):
        cos = np.ones((t, LANES)); up = np.zeros((t, LANES)); dn = np.zeros((t, LANES))
        half = rot_dim // 2
        q = half // 2
        freqs = ROPE_BASE ** (-np.arange(q, dtype=np.float64) / q)
        for axis, pos in enumerate((row, col)):
            ang = pos[:, None] * freqs[None, :]
            base = rot_start + axis * half
            cos[n_ctx:, base:base + q] = np.cos(ang)
            cos[n_ctx:, base + q:base + 2 * q] = np.cos(ang)
            up[n_ctx:, base:base + q] = -np.sin(ang)
            dn[n_ctx:, base + q:base + 2 * q] = np.sin(ang)
        return cos, up, dn

    def merge(blocks):
        cos = np.ones((t, LANES)); up = np.zeros((t, LANES)); dn = np.zeros((t, LANES))
        for c_, u_, d_ in blocks:
            cos = np.where(c_ != 1.0, c_, cos); up = up + u_; dn = dn + d_
        return np.stack([cos, up, dn]).astype(np.float32)

    gqa = merge([block(0, GQA_HEAD_DIM), block(GQA_HEAD_DIM, GQA_HEAD_DIM)])
    mla = merge([block(MLA_NOPE, MLA_ROPE)])
    return jnp.asarray(gqa), jnp.asarray(mla)


def _layout_w_in(w_in):
    d = w_in.shape[0]
    kpe_src = OFF_KPE
    w = w_in.astype(BF16)
    out = jnp.concatenate([w[:, :kpe_src], jnp.zeros((d, MLA_NOPE), BF16), w[:, kpe_src:kpe_src + MLA_ROPE],
                           jnp.zeros((d, LANES - MLA_NOPE - MLA_ROPE), BF16), w[:, kpe_src + MLA_ROPE:]], axis=1)
    assert out.shape[1] == N_IN
    return out


def _layout_w_uq(w_uq):
    r = w_uq.shape[0]
    w = w_uq.reshape(r, MLA_HEADS, MLA_NOPE + MLA_ROPE)
    w = jnp.pad(w, ((0, 0), (0, 0), (0, LANES - MLA_NOPE - MLA_ROPE)))
    return w.reshape(r, MLA_HEADS * LANES).astype(BF16)


def _layout_w_ukv(w_ukv):
    r = w_ukv.shape[0]
    w = w_ukv.reshape(r, MLA_HEADS, MLA_NOPE + MLA_V)
    k = jnp.pad(w[:, :, :MLA_NOPE], ((0, 0), (0, 0), (0, LANES - MLA_NOPE))).reshape(r, MLA_HEADS * LANES)
    v = jnp.pad(w[:, :, MLA_NOPE:], ((0, 0), (0, 0), (0, LANES - MLA_V))).reshape(r, MLA_HEADS * LANES)
    return jnp.concatenate([k, v], axis=1).astype(BF16)


def kernel(x, c, ctx, c_ctx, w_mod, b_mod, w_in, w_out, conv_dw, conv_b, conv_ln_g, conv_ln_b, conv_pw, mla_q_g, mla_kv_g, mla_uq, mla_ukv, gqa_q_g, gqa_k_g, ret_decay, ret_norm_g, mlp_w1, mlp_w2, final_g):
    b, n_lat, d = x.shape
    n_ctx = ctx.shape[1]
    depth = w_mod.shape[0]
    assert n_ctx % TOKEN_TILE == 0 and n_lat % TOKEN_TILE == 0 and TOKEN_TILE % RET_CHUNK == 0
    assert n_ctx // TOKEN_TILE == 1, "the modulation row select assumes one context tile"

    xs = jnp.concatenate([ctx, x], axis=1)
    rows = ((b + 1 + 7) // 8) * 8
    cond = jnp.zeros((rows, d), F32).at[:b].set(c).at[b].set(c_ctx)
    mod_all = _adaln(cond, w_mod, b_mod).reshape(depth, rows, N_MOD, d)
    tab_gqa, tab_mla = _rope_tables(n_ctx, n_lat)

    for layer in range(depth):
        m = mod_all[layer]
        mods = jnp.stack([jnp.broadcast_to(m[b], (b, N_MOD, d)), m[:b]], axis=1)
        (yglu, mq, mk, mv, gq, gk, gv, rq, rk, rv, gf, gb) = _in_proj(
            xs, mods, _layout_w_in(w_in[layer]), _layout_w_uq(mla_uq[layer]), _layout_w_ukv(mla_ukv[layer]),
            mla_q_g[layer][None, :], mla_kv_g[layer][None, :],
            jnp.tile(gqa_q_g[layer], 2)[None, :], jnp.tile(gqa_k_g[layer], 2)[None, :], tab_gqa, tab_mla)
        ya = _conv(yglu, conv_dw[layer], conv_b[layer][None, :], conv_ln_g[layer][None, :],
                   conv_ln_b[layer][None, :], conv_pw[layer].astype(BF16), n_ctx)
        yb = _attention(mq, mk, mv, (0, 1, 2, 3), (0, 1, 2, 3), n_ctx, "mla_attn")
        yc = _attention(gq, gk, gv, (0, 0, 1, 1), (0, 0, 1, 1), n_ctx, "gqa_attn")
        dec = ret_decay[layer].astype(F32)
        yd = _retention(rq, rk, rv, gf, gb,
                        jnp.repeat(dec, RET_QK, axis=1), jnp.repeat(dec, RET_V, axis=1),
                        jnp.repeat(dec, RET_CHUNK, axis=1),
                        ret_norm_g[layer].reshape(2, RET_HEADS * RET_V), n_ctx)
        xs = _out_mlp(xs, ya, yb, yc, yd, mods, w_out[layer].astype(BF16),
                      mlp_w1[layer].astype(BF16), mlp_w2[layer].astype(BF16))
    return _final_norm(xs, final_g[None, :], n_ctx)
```

```python
import functools

import numpy as np
import jax
import jax.numpy as jnp
from jax import lax
from jax.experimental import pallas as pl
from jax.experimental.pallas import tpu as pltpu

F32 = jnp.float32
BF16 = jnp.bfloat16

GRID_W = 64
N_MOD = 6
EPS = 1e-6
ROPE_BASE = 10000.0
GROUP_W = 256
CONV_WIDTH = 31
CONV_PAD = 16
MLA_HEADS = 4
MLA_NOPE = 64
MLA_ROPE = 32
MLA_V = 64
MLA_Q_RANK = 256
MLA_KV_RANK = 128
GQA_HEADS = 4
GQA_KV_HEADS = 2
GQA_HEAD_DIM = 64
RET_HEADS = 4
RET_QK = 32
RET_V = 64
RET_CHUNK = 128
LANES = 128
SUBLANES = 8
LOG2_E = 1.4426950408889634
TOKEN_TILE = 256
VMEM_LIMIT = 56 * 1024 * 1024

OFF_A, OFF_CQ, OFF_CKV, OFF_KPE, OFF_GQ, OFF_GK, OFF_GV = 0, 512, 768, 896, 1024, 1280, 1408
OFF_RQ, OFF_RK, OFF_RV, OFF_GF, OFF_GB, N_IN = 1536, 1664, 1792, 2048, 2304, 2560


def _cparams(n_axes):
    return pltpu.CompilerParams(dimension_semantics=("arbitrary",) * n_axes,
                                vmem_limit_bytes=VMEM_LIMIT)


def _rms(x):
    return x * lax.rsqrt(jnp.mean(x * x, axis=-1, keepdims=True) + EPS)


def _sigmoid(x):
    return 1.0 / (1.0 + jnp.exp(-x))


def _silu(x):
    return x * _sigmoid(x)


def _dot(a, b):
    return jnp.dot(a, b, preferred_element_type=F32)


def _dot_nt(a, b):
    return lax.dot_general(a, b, (((1,), (1,)), ((), ())), preferred_element_type=F32)


def _dot_tn(a, b):
    return lax.dot_general(a, b, (((0,), (0,)), ((), ())), preferred_element_type=F32)


def _rope(x, cos, sin_up, sin_dn, shift):
    n = x.shape[-1]
    return x * cos + pltpu.roll(x, n - shift, 1) * sin_up + pltpu.roll(x, shift, 1) * sin_dn


def _adaln_kernel(cond_ref, w_ref, b_ref, o_ref):
    cond = cond_ref[...]
    o_ref[...] = _dot(_silu(cond).astype(BF16), w_ref[...].astype(BF16)) + b_ref[...]


def _adaln(cond, w_mod, b_mod):
    depth, d, n = w_mod.shape
    r = cond.shape[0]
    tn = 1536
    return pl.pallas_call(
        _adaln_kernel,
        grid=(depth, n // tn),
        in_specs=[pl.BlockSpec((r, d), lambda l, j: (0, 0)),
                  pl.BlockSpec((None, d, tn), lambda l, j: (l, 0, j)),
                  pl.BlockSpec((None, 1, tn), lambda l, j: (l, 0, j))],
        out_specs=pl.BlockSpec((None, r, tn), lambda l, j: (l, 0, j)),
        out_shape=jax.ShapeDtypeStruct((depth, r, n), F32),
        compiler_params=_cparams(2),
        name="adaln",
    )(cond, w_mod, b_mod.reshape(depth, 1, n))


def _in_proj_kernel(x_ref, mod_ref, w_in_ref, w_uq_ref, w_ukv_ref, g_mq_ref, g_mkv_ref, g_gq_ref, g_gk_ref,
                    tg_ref, tm_ref,
                    yglu_ref, mq_ref, mk_ref, mv_ref, gq_ref, gk_ref, gv_ref,
                    rq_ref, rk_ref, rv_ref, gf_ref, gb_ref):
    x = x_ref[...]
    h = (_rms(x) * (1.0 + mod_ref[1:2, :]) + mod_ref[0:1, :]).astype(BF16)

    def proj(off, width):
        return _dot(h, w_in_ref[:, off:off + width])

    lane = lax.broadcasted_iota(jnp.int32, (1, LANES), 1)
    lo = lane < (LANES // 2)

    a = proj(OFF_A, 2 * GROUP_W)
    yglu_ref[...] = a[:, :GROUP_W] * _sigmoid(a[:, GROUP_W:])

    cq = (_rms(proj(OFF_CQ, MLA_Q_RANK)) * g_mq_ref[...]).astype(BF16)
    q = _dot(cq, w_uq_ref[...])
    m_cos, m_up, m_dn = tm_ref[0], tm_ref[1], tm_ref[2]
    q_scale = (MLA_NOPE + MLA_ROPE) ** -0.5 * LOG2_E
    for hd in range(MLA_HEADS):
        qh = _rope(q[:, hd * LANES:(hd + 1) * LANES], m_cos, m_up, m_dn, MLA_ROPE // 4)
        mq_ref[:, hd * LANES:(hd + 1) * LANES] = (qh * q_scale).astype(BF16)
    ckv_kpe = proj(OFF_CKV, 2 * LANES)
    ckv = (_rms(ckv_kpe[:, :LANES]) * g_mkv_ref[...]).astype(BF16)
    kv = _dot(ckv, w_ukv_ref[...])
    kpe = _rope(ckv_kpe[:, LANES:], m_cos, m_up, m_dn, MLA_ROPE // 4)
    for hd in range(MLA_HEADS):
        mk_ref[:, hd * LANES:(hd + 1) * LANES] = (kv[:, hd * LANES:(hd + 1) * LANES] + kpe).astype(BF16)
    for hd in range(MLA_HEADS):
        vh = kv[:, (MLA_HEADS + hd) * LANES:(MLA_HEADS + hd + 1) * LANES]
        mv_ref[:, hd * LANES:(hd + 1) * LANES] = jnp.where(lo, vh, 1.0).astype(BF16)

    g_cos, g_up, g_dn = tg_ref[0], tg_ref[1], tg_ref[2]

    def pair_norm_rope(xp, gain):
        x2 = xp * xp
        s_lo = jnp.sum(jnp.where(lo, x2, 0.0), axis=-1, keepdims=True)
        s_hi = jnp.sum(jnp.where(lo, 0.0, x2), axis=-1, keepdims=True)
        r = jnp.where(lo, lax.rsqrt(s_lo * (1.0 / GQA_HEAD_DIM) + EPS), lax.rsqrt(s_hi * (1.0 / GQA_HEAD_DIM) + EPS))
        return _rope(xp * r * gain, g_cos, g_up, g_dn, GQA_HEAD_DIM // 4)

    gq = proj(OFF_GQ, GQA_HEADS * GQA_HEAD_DIM)
    for pr in range(2):
        qp = pair_norm_rope(gq[:, pr * LANES:(pr + 1) * LANES], g_gq_ref[...])
        gq_ref[:, pr * LANES:(pr + 1) * LANES] = (qp * (GQA_HEAD_DIM ** -0.5 * LOG2_E)).astype(BF16)
    gkv = proj(OFF_GK, 2 * LANES)
    kp = pair_norm_rope(gkv[:, :LANES], g_gk_ref[...])
    kp_sw = pltpu.roll(kp, LANES // 2, 1)
    gk_ref[:, 0 * LANES:1 * LANES] = jnp.where(lo, kp, 0.0).astype(BF16)
    gk_ref[:, 1 * LANES:2 * LANES] = jnp.where(lo, 0.0, kp_sw).astype(BF16)
    gk_ref[:, 2 * LANES:3 * LANES] = jnp.where(lo, kp_sw, 0.0).astype(BF16)
    gk_ref[:, 3 * LANES:4 * LANES] = jnp.where(lo, 0.0, kp).astype(BF16)
    vp = gkv[:, LANES:]
    vp_sw = pltpu.roll(vp, LANES // 2, 1)
    gv_ref[:, 0 * LANES:1 * LANES] = jnp.where(lo, vp, 1.0).astype(BF16)
    gv_ref[:, 1 * LANES:2 * LANES] = jnp.where(lo, vp_sw, 1.0).astype(BF16)

    rqk = proj(OFF_RQ, 2 * LANES)
    rq_ref[...] = rqk[:, :LANES]
    rk_ref[...] = rqk[:, LANES:] * RET_QK ** -0.5
    rv_ref[...] = proj(OFF_RV, RET_HEADS * RET_V)
    gf_ref[...] = _silu(proj(OFF_GF, GROUP_W))
    gb_ref[...] = _silu(proj(OFF_GB, GROUP_W))


def _in_proj(xs, mods, w_in, w_uq, w_ukv, g_mq, g_mkv, g_gq, g_gk, tab_gqa, tab_mla):
    b, t, d = xs.shape
    tm = TOKEN_TILE
    tok = lambda w: pl.BlockSpec((None, tm, w), lambda i, j: (i, j, 0))
    full = lambda a: pl.BlockSpec(a.shape, lambda i, j: (0,) * a.ndim)
    widths_dtypes = [(GROUP_W, F32), (512, BF16), (512, BF16), (512, BF16), (256, BF16), (512, BF16), (256, BF16),
                     (LANES, F32), (LANES, F32), (256, F32), (GROUP_W, F32), (GROUP_W, F32)]
    return pl.pallas_call(
        _in_proj_kernel,
        grid=(b, t // tm),
        in_specs=[tok(d),
                  pl.BlockSpec((None, None, N_MOD, d), lambda i, j: (i, jnp.minimum(j, 1), 0, 0)),
                  full(w_in), full(w_uq), full(w_ukv), full(g_mq), full(g_mkv), full(g_gq), full(g_gk),
                  pl.BlockSpec((3, tm, LANES), lambda i, j: (0, j, 0)),
                  pl.BlockSpec((3, tm, LANES), lambda i, j: (0, j, 0))],
        out_specs=[tok(w) for w, _ in widths_dtypes],
        out_shape=[jax.ShapeDtypeStruct((b, t, w), dt) for w, dt in widths_dtypes],
        compiler_params=_cparams(2),
        name="in_proj",
    )(xs, mods, w_in, w_uq, w_ukv, g_mq, g_mkv, g_gq, g_gk, tab_gqa, tab_mla)


def _conv_kernel(y_ref, dw_ref, b_ref, lg_ref, lb_ref, pw_ref, o_ref, sh_ref, *, n_ctx, rows, norm_rows):
    t = y_ref.shape[0]
    zeros = jnp.zeros((CONV_PAD, GROUP_W), F32)
    bias = b_ref[...]
    first_off = CONV_PAD - CONV_WIDTH // 2
    tail = ((first_off + CONV_WIDTH - 1) // SUBLANES) * SUBLANES

    def segment(start, length):
        sh_ref[0, 0:CONV_PAD, :] = zeros
        sh_ref[0, CONV_PAD:CONV_PAD + length, :] = y_ref[start:start + length, :]
        sh_ref[0, CONV_PAD + length:2 * CONV_PAD + length, :] = zeros

        def shift_block(c, carry):
            base = pl.multiple_of(c * rows, rows)
            win = sh_ref[0, pl.ds(base, rows + SUBLANES), :]
            for r in range(1, SUBLANES):
                sh_ref[r, pl.ds(base, rows), :] = win[r:r + rows, :]
            return carry

        lax.fori_loop(0, length // rows, shift_block, 0)
        win = sh_ref[0, length:length + tail + SUBLANES, :]
        for r in range(1, SUBLANES):
            sh_ref[r, length:length + tail, :] = win[r:r + tail, :]

        def taps(c, carry):
            base = pl.multiple_of(c * rows, rows)
            acc = jnp.zeros((rows, GROUP_W), F32) + bias
            for k in range(CONV_WIDTH):
                off = first_off + k
                slab = sh_ref[off % SUBLANES, pl.ds(base + (off // SUBLANES) * SUBLANES, rows), :]
                acc = acc + slab * dw_ref[k:k + 1, :]
            sh_ref[0, pl.ds(base, rows), :] = acc
            return carry

        lax.fori_loop(0, length // rows, taps, 0)

        def finish(c, carry):
            base = pl.multiple_of(c * norm_rows, norm_rows)
            acc = sh_ref[0, pl.ds(base, norm_rows), :]
            mu = jnp.mean(acc, axis=-1, keepdims=True)
            cen = acc - mu
            var = jnp.mean(cen * cen, axis=-1, keepdims=True)
            z = _silu(cen * lax.rsqrt(var + EPS) * lg_ref[...] + lb_ref[...])
            o_ref[pl.ds(start + base, norm_rows), :] = _dot(z.astype(BF16), pw_ref[...]).astype(BF16)
            return carry

        n_norm = length // norm_rows
        lax.fori_loop(0, n_norm, finish, 0, unroll=min(n_norm, 4))

    segment(0, n_ctx)
    segment(n_ctx, t - n_ctx)


def _conv(yglu, dw, bias, ln_g, ln_b, pw, n_ctx):
    b, t, w = yglu.shape
    full = lambda a: pl.BlockSpec(a.shape, lambda i: (0,) * a.ndim)
    return pl.pallas_call(
        functools.partial(_conv_kernel, n_ctx=n_ctx, rows=64, norm_rows=256),
        grid=(b,),
        in_specs=[pl.BlockSpec((None, t, w), lambda i: (i, 0, 0)),
                  full(dw), full(bias), full(ln_g), full(ln_b), full(pw)],
        out_specs=pl.BlockSpec((None, t, w), lambda i: (i, 0, 0)),
        out_shape=jax.ShapeDtypeStruct((b, t, w), BF16),
        scratch_shapes=[pltpu.VMEM((SUBLANES, t - n_ctx + 2 * CONV_PAD, w), F32)],
        compiler_params=_cparams(1),
        name="conv",
    )(yglu, dw, bias, ln_g, ln_b, pw)


def _attn_kernel(q_ref, qn_ref, k_ref, v_ref, o_ref, s_ref, m_ref, mb_ref, acc_ref, *, q_blocks, v_blocks, n_ctx, tk):
    tq = q_ref.shape[0]
    n_keys = k_ref.shape[0]
    half_lanes = LANES // 2
    lo = lax.broadcasted_iota(jnp.int32, (1, LANES), 1) < half_lanes

    def q_of(hd, ref=None):
        ref = q_ref if ref is None else ref
        return ref[:, q_blocks[hd] * LANES:(q_blocks[hd] + 1) * LANES]

    def normalized_pair(acc_even, acc_odd):
        return jnp.where(lo, acc_even / pltpu.roll(acc_even, half_lanes, 1),
                         pltpu.roll(acc_odd, half_lanes, 1) / acc_odd)

    def ctx_tile():
        for pr in range(2):
            accs = []
            for half in range(2):
                hd = 2 * pr + half
                s = _dot_nt(q_of(hd), k_ref[0:n_ctx, hd * LANES:(hd + 1) * LANES])
                p = jnp.exp2(s - jnp.max(s, axis=-1, keepdims=True))
                accs.append(_dot(p.astype(BF16), v_ref[0:n_ctx, v_blocks[hd] * LANES:(v_blocks[hd] + 1) * LANES]))
            o_ref[:, pr * LANES:(pr + 1) * LANES] = normalized_pair(*accs).astype(BF16)

    def scores_chunk(hd, kc, ref=None):
        slot = hd % 2
        s = _dot_nt(q_of(hd, ref), k_ref[kc * tk:(kc + 1) * tk, hd * LANES:(hd + 1) * LANES])
        s_ref[slot, :, kc * tk:(kc + 1) * tk] = s
        mx = s[:, 0:LANES]
        for i in range(1, tk // LANES):
            mx = jnp.maximum(mx, s[:, i * LANES:(i + 1) * LANES])
        m_ref[slot] = mx if kc == 0 else jnp.maximum(m_ref[slot], mx)

    def scores_finish(hd):
        slot = hd % 2
        mb_ref[slot] = jnp.broadcast_to(jnp.max(m_ref[slot], axis=-1, keepdims=True), (tq, LANES))

    def values_chunk(hd, kc):
        slot = hd % 2
        mb = mb_ref[slot]
        p = jnp.concatenate(
            [jnp.exp2(s_ref[slot, :, kc * tk + i * LANES:kc * tk + (i + 1) * LANES] - mb).astype(BF16)
             for i in range(tk // LANES)], axis=1)
        pv = _dot(p, v_ref[kc * tk:(kc + 1) * tk, v_blocks[hd] * LANES:(v_blocks[hd] + 1) * LANES])
        acc_ref[slot] = pv if kc == 0 else acc_ref[slot] + pv

    def latent_tile():
        n_chunks = n_keys // tk

        @pl.when(pl.program_id(1) == n_ctx // tq)
        def _():
            for kc in range(n_chunks):
                scores_chunk(0, kc)
            scores_finish(0)

        for hd in range(4):
            nxt = (hd + 1) % 4
            for kc in range(n_chunks):
                values_chunk(hd, kc)
                scores_chunk(nxt, kc, q_ref if hd < 3 else qn_ref)
            scores_finish(nxt)
            if hd % 2 == 1:
                pr = hd // 2
                o_ref[:, pr * LANES:(pr + 1) * LANES] = normalized_pair(acc_ref[0], acc_ref[1]).astype(BF16)

    is_ctx = pl.program_id(1) == 0
    pl.when(is_ctx)(ctx_tile)
    pl.when(jnp.logical_not(is_ctx))(latent_tile)


def _attention(q, k, v, q_blocks, v_blocks, n_ctx, name):
    b, t, qw = q.shape
    tq = TOKEN_TILE
    return pl.pallas_call(
        functools.partial(_attn_kernel, q_blocks=q_blocks, v_blocks=v_blocks, n_ctx=n_ctx, tk=256),
        grid=(b, t // tq),
        in_specs=[pl.BlockSpec((None, tq, qw), lambda i, j: (i, j, 0)),
                  pl.BlockSpec((None, tq, qw), lambda i, j: (i, jnp.minimum(j + 1, t // tq - 1), 0)),
                  pl.BlockSpec((None, t, k.shape[2]), lambda i, j: (i, 0, 0)),
                  pl.BlockSpec((None, t, v.shape[2]), lambda i, j: (i, 0, 0))],
        out_specs=pl.BlockSpec((None, tq, 256), lambda i, j: (i, j, 0)),
        out_shape=jax.ShapeDtypeStruct((b, t, 256), BF16),
        scratch_shapes=[pltpu.VMEM((2, tq, t), F32), pltpu.VMEM((2, tq, LANES), F32),
                        pltpu.VMEM((2, tq, LANES), F32), pltpu.VMEM((2, tq, LANES), F32)],
        compiler_params=_cparams(2),
        name=name,
    )(q, q, k, v)


def _log_sigmoid(x):
    return jnp.minimum(x, 0.0) - jnp.log(1.0 + jnp.exp(-jnp.abs(x)))


def _ret_kernel(q_ref, k_ref, v_ref, gf_ref, gb_ref, dq_ref, dv_ref, dc_ref, ng_ref, o_ref, acc_ref, *, n_ctx):
    c = RET_CHUNK
    t = q_ref.shape[0]
    n_ctx_chunks = n_ctx // c
    n_chunks = t // c
    qk_w = RET_HEADS * RET_QK
    v_w = RET_HEADS * RET_V

    row_i = lax.broadcasted_iota(jnp.int32, (c, 1), 0).astype(F32)
    qk_head = lax.broadcasted_iota(jnp.int32, (1, qk_w), 1) // RET_QK
    v_head = lax.broadcasted_iota(jnp.int32, (1, v_w), 1) // RET_V
    qk_masks = [(qk_head == hd).astype(F32) for hd in range(RET_HEADS)]
    v_masks = [(v_head == hd).astype(F32) for hd in range(RET_HEADS)]
    state_rows = lax.broadcasted_iota(jnp.int32, (qk_w, 1), 0) // RET_QK
    bd_mask = (state_rows == v_head).astype(F32)
    ci = lax.broadcasted_iota(jnp.int32, (c, RET_HEADS * c), 0).astype(F32)
    cj = (lax.broadcasted_iota(jnp.int32, (c, RET_HEADS * c), 1) % c).astype(F32)

    def make_step(d, gate_ref):
        lg_q = _log_sigmoid(dq_ref[d:d + 1, :])
        lg_v = _log_sigmoid(dv_ref[d:d + 1, :])
        lg_c = _log_sigmoid(dc_ref[d:d + 1, :])
        gain = ng_ref[d:d + 1, :]
        if d == 0:
            diff = ci - cj
            q_dec = jnp.exp(lg_q * (row_i + 1.0))
            k_dec = jnp.exp(lg_q * (c - 1.0 - row_i))
        else:
            diff = cj - ci
            q_dec = jnp.exp(lg_q * (c - row_i))
            k_dec = jnp.exp(lg_q * row_i)
        decay = jnp.where(diff >= 0.0, jnp.exp(lg_c * jnp.maximum(diff, 0.0)), 0.0)
        chunk_dec = jnp.exp(lg_v * float(c))

        def step(r0, state):
            q = q_ref[pl.ds(r0, c), :]
            k = k_ref[pl.ds(r0, c), :]
            v = v_ref[pl.ds(r0, c), :]
            k_bd = jnp.concatenate([k * m for m in qk_masks], axis=0).astype(BF16)
            scores = _dot_nt(q.astype(BF16), k_bd) * decay
            v_bd = jnp.concatenate([v * m for m in v_masks], axis=0).astype(BF16)
            inner = _dot(scores.astype(BF16), v_bd)
            cross = _dot((q * q_dec).astype(BF16), state.astype(BF16))
            o = inner + cross
            kv = _dot_tn((k * k_dec).astype(BF16), v.astype(BF16))
            new_state = state * chunk_dec + kv * bd_mask
            o2 = o * o
            r = jnp.zeros_like(o)
            for m in v_masks:
                ss = jnp.sum(o2 * m, axis=-1, keepdims=True)
                r = r + m * lax.rsqrt(ss * (1.0 / RET_V) + EPS)
            acc_ref[d, pl.ds(r0, c), :] = o * r * gain * gate_ref[pl.ds(r0, c), :]
            return new_state

        return step

    fwd, bwd = make_step(0, gf_ref), make_step(1, gb_ref)

    def segment(lo_chunk, n, states):
        def body(i, st):
            sf = fwd(pl.multiple_of((lo_chunk + i) * c, c), st[0])
            sb = bwd(pl.multiple_of((lo_chunk + n - 1 - i) * c, c), st[1])
            return sf, sb
        return lax.fori_loop(0, n, body, states)

    zero = jnp.zeros((qk_w, v_w), F32)
    states = segment(0, n_ctx_chunks, (zero, zero))
    segment(n_ctx_chunks, n_chunks - n_ctx_chunks, states)

    def combine(i, carry):
        r0 = pl.multiple_of(i * TOKEN_TILE, TOKEN_TILE)
        o_ref[pl.ds(r0, TOKEN_TILE), :] = (acc_ref[0, pl.ds(r0, TOKEN_TILE), :]
                                           + acc_ref[1, pl.ds(r0, TOKEN_TILE), :]).astype(BF16)
        return carry

    lax.fori_loop(0, t // TOKEN_TILE, combine, 0)


def _retention(rq, rk, rv, gf, gb, dec_q, dec_v, dec_c, norm_g, n_ctx):
    b, t, _ = rq.shape
    seq = lambda w: pl.BlockSpec((None, t, w), lambda i: (i, 0, 0))
    full = lambda a: pl.BlockSpec(a.shape, lambda i: (0,) * a.ndim)
    return pl.pallas_call(
        functools.partial(_ret_kernel, n_ctx=n_ctx),
        grid=(b,),
        in_specs=[seq(128), seq(128), seq(256), seq(256), seq(256),
                  full(dec_q), full(dec_v), full(dec_c), full(norm_g)],
        out_specs=seq(256),
        out_shape=jax.ShapeDtypeStruct((b, t, 256), BF16),
        scratch_shapes=[pltpu.VMEM((2, t, 256), F32)],
        compiler_params=_cparams(1),
        name="retention",
    )(rq, rk, rv, gf, gb, dec_q, dec_v, dec_c, norm_g)


def _out_mlp_kernel(x_ref, ya_ref, yb_ref, yc_ref, yd_ref, mod_ref, wo_ref, w1_ref, w2_ref, *rest, ff_chunk):
    o_ref = rest[-1]
    x = x_ref[...]
    gw = GROUP_W
    y = (_dot(ya_ref[...], wo_ref[0 * gw:1 * gw, :]) + _dot(yb_ref[...], wo_ref[1 * gw:2 * gw, :])
         + _dot(yc_ref[...], wo_ref[2 * gw:3 * gw, :]) + _dot(yd_ref[...], wo_ref[3 * gw:4 * gw, :]))
    x1 = x + mod_ref[2:3, :] * y
    h = (_rms(x1) * (1.0 + mod_ref[4:5, :]) + mod_ref[3:4, :]).astype(BF16)
    acc = jnp.zeros_like(x1)
    for j in range(w1_ref.shape[1] // ff_chunk):
        a = jnp.maximum(_dot(h, w1_ref[:, j * ff_chunk:(j + 1) * ff_chunk]), 0.0)
        acc = acc + _dot((a * a).astype(BF16), w2_ref[j * ff_chunk:(j + 1) * ff_chunk, :])
    x2 = x1 + mod_ref[5:6, :] * acc
    o_ref[...] = x2 if len(rest) == 1 else _rms(x2) * rest[0][...]


def _out_mlp(xs, ya, yb, yc, yd, mods, w_out, w1, w2, n_ctx, final_gain=None):
    b, t, d = xs.shape
    tm = TOKEN_TILE
    skip = 0 if final_gain is None else n_ctx // tm
    tok = lambda w: pl.BlockSpec((None, tm, w), lambda i, j: (i, j + skip, 0))
    full = lambda a: pl.BlockSpec(a.shape, lambda i, j: (0,) * a.ndim)
    in_specs = [tok(d), tok(256), tok(256), tok(256), tok(256),
                pl.BlockSpec((None, None, N_MOD, d), lambda i, j: (i, jnp.minimum(j + skip, 1), 0, 0)),
                full(w_out), full(w1), full(w2)]
    args = [xs, ya, yb, yc, yd, mods, w_out, w1, w2]
    if final_gain is not None:
        in_specs.append(full(final_gain))
        args.append(final_gain)
    return pl.pallas_call(
        functools.partial(_out_mlp_kernel, ff_chunk=1024),
        grid=(b, t // tm - skip),
        in_specs=in_specs,
        out_specs=pl.BlockSpec((None, tm, d), lambda i, j: (i, j, 0)),
        out_shape=jax.ShapeDtypeStruct((b, t - skip * tm, d), F32),
        input_output_aliases={0: 0} if final_gain is None else {},
        compiler_params=_cparams(2),
        name="out_mlp",
    )(*args)


def _rope_tables(n_ctx, n_lat):
    t = n_ctx + n_lat
    n = np.arange(n_lat)
    row, col = (n // GRID_W).astype(np.float64), (n % GRID_W).astype(np.float64)

    def block(rot_start, rot_dim):
        cos = np.ones((t, LANES)); up = np.zeros((t, LANES)); dn = np.zeros((t, LANES))
        half = rot_dim // 2
        q = half // 2
        freqs = ROPE_BASE ** (-np.arange(q, dtype=np.float64) / q)
        for axis, pos in enumerate((row, col)):
            ang = pos[:, None] * freqs[None, :]
            base = rot_start + axis * half
            cos[n_ctx:, base:base + q] = np.cos(ang)
            cos[n_ctx:, base + q:base + 2 * q] = np.cos(ang)
            up[n_ctx:, base:base + q] = -np.sin(ang)
            dn[n_ctx:, base + q:base + 2 * q] = np.sin(ang)
        return cos, up, dn

    def merge(blocks):
        cos = np.ones((t, LANES)); up = np.zeros((t, LANES)); dn = np.zeros((t, LANES))
        for c_, u_, d_ in blocks:
            cos = np.where(c_ != 1.0, c_, cos); up = up + u_; dn = dn + d_
        return np.stack([cos, up, dn]).astype(np.float32)

    gqa = merge([block(0, GQA_HEAD_DIM), block(GQA_HEAD_DIM, GQA_HEAD_DIM)])
    mla = merge([block(MLA_NOPE, MLA_ROPE)])
    return jnp.asarray(gqa), jnp.asarray(mla)


def _layout_w_in(w_in):
    d = w_in.shape[0]
    kpe_src = OFF_KPE
    w = w_in.astype(BF16)
    out = jnp.concatenate([w[:, :kpe_src], jnp.zeros((d, MLA_NOPE), BF16), w[:, kpe_src:kpe_src + MLA_ROPE],
                           jnp.zeros((d, LANES - MLA_NOPE - MLA_ROPE), BF16), w[:, kpe_src + MLA_ROPE:]], axis=1)
    assert out.shape[1] == N_IN
    return out


def _layout_w_uq(w_uq):
    r = w_uq.shape[0]
    w = w_uq.reshape(r, MLA_HEADS, MLA_NOPE + MLA_ROPE)
    w = jnp.pad(w, ((0, 0), (0, 0), (0, LANES - MLA_NOPE - MLA_ROPE)))
    return w.reshape(r, MLA_HEADS * LANES).astype(BF16)


def _layout_w_ukv(w_ukv):
    r = w_ukv.shape[0]
    w = w_ukv.reshape(r, MLA_HEADS, MLA_NOPE + MLA_V)
    k = jnp.pad(w[:, :, :MLA_NOPE], ((0, 0), (0, 0), (0, LANES - MLA_NOPE))).reshape(r, MLA_HEADS * LANES)
    v = jnp.pad(w[:, :, MLA_NOPE:], ((0, 0), (0, 0), (0, LANES - MLA_V))).reshape(r, MLA_HEADS * LANES)
    return jnp.concatenate([k, v], axis=1).astype(BF16)


def kernel(x, c, ctx, c_ctx, w_mod, b_mod, w_in, w_out, conv_dw, conv_b, conv_ln_g, conv_ln_b, conv_pw, mla_q_g, mla_kv_g, mla_uq, mla_ukv, gqa_q_g, gqa_k_g, ret_decay, ret_norm_g, mlp_w1, mlp_w2, final_g):
    b, n_lat, d = x.shape
    n_ctx = ctx.shape[1]
    depth = w_mod.shape[0]
    assert n_ctx % TOKEN_TILE == 0 and n_lat % TOKEN_TILE == 0 and TOKEN_TILE % RET_CHUNK == 0
    assert n_ctx // TOKEN_TILE == 1, "the modulation row select assumes one context tile"

    xs = jnp.concatenate([ctx, x], axis=1)
    rows = ((b + 1 + 7) // 8) * 8
    cond = jnp.zeros((rows, d), F32).at[:b].set(c).at[b].set(c_ctx)
    mod_all = _adaln(cond, w_mod, b_mod).reshape(depth, rows, N_MOD, d)
    tab_gqa, tab_mla = _rope_tables(n_ctx, n_lat)

    for layer in range(depth):
        m = mod_all[layer]
        mods = jnp.stack([jnp.broadcast_to(m[b], (b, N_MOD, d)), m[:b]], axis=1)
        (yglu, mq, mk, mv, gq, gk, gv, rq, rk, rv, gf, gb) = _in_proj(
            xs, mods, _layout_w_in(w_in[layer]), _layout_w_uq(mla_uq[layer]), _layout_w_ukv(mla_ukv[layer]),
            mla_q_g[layer][None, :], mla_kv_g[layer][None, :],
            jnp.tile(gqa_q_g[layer], 2)[None, :], jnp.tile(gqa_k_g[layer], 2)[None, :], tab_gqa, tab_mla)
        ya = _conv(yglu, conv_dw[layer], conv_b[layer][None, :], conv_ln_g[layer][None, :],
                   conv_ln_b[layer][None, :], conv_pw[layer].astype(BF16), n_ctx)
        yb = _attention(mq, mk, mv, (0, 1, 2, 3), (0, 1, 2, 3), n_ctx, "mla_attn")
        yc = _attention(gq, gk, gv, (0, 0, 1, 1), (0, 0, 1, 1), n_ctx, "gqa_attn")
        dec = ret_decay[layer].astype(F32)
        yd = _retention(rq, rk, rv, gf, gb,
                        jnp.repeat(dec, RET_QK, axis=1), jnp.repeat(dec, RET_V, axis=1),
                        jnp.repeat(dec, RET_CHUNK, axis=1),
                        ret_norm_g[layer].reshape(2, RET_HEADS * RET_V), n_ctx)
        xs = _out_mlp(xs, ya, yb, yc, yd, mods, w_out[layer].astype(BF16),
                      mlp_w1[layer].astype(BF16), mlp_w2[layer].astype(BF16), n_ctx,
                      final_gain=final_g[None, :] if layer == depth - 1 else None)
    return xs
```

```python
import functools

import numpy as np
import jax
import jax.numpy as jnp
from jax import lax
from jax.experimental import pallas as pl
from jax.experimental.pallas import tpu as pltpu

F32 = jnp.float32
BF16 = jnp.bfloat16

GRID_W = 64
N_MOD = 6
EPS = 1e-6
ROPE_BASE = 10000.0
GROUP_W = 256
CONV_WIDTH = 31
CONV_PAD = 16
MLA_HEADS = 4
MLA_NOPE = 64
MLA_ROPE = 32
MLA_V = 64
MLA_Q_RANK = 256
MLA_KV_RANK = 128
GQA_HEADS = 4
GQA_KV_HEADS = 2
GQA_HEAD_DIM = 64
RET_HEADS = 4
RET_QK = 32
RET_V = 64
RET_CHUNK = 128
LANES = 128
SUBLANES = 8
LOG2_E = 1.4426950408889634
LAT_TILE = 512
KEY_CHUNK = 256
NORM_ROWS = 256
VMEM_LIMIT = 56 * 1024 * 1024

OFF_A, OFF_CQ, OFF_CKV, OFF_GQ, OFF_GK = 0, 512, 768, 1024, 1280
OFF_RQ, OFF_RV, OFF_GF, OFF_GB, N_IN = 1536, 1792, 2048, 2304, 2560
A16_MQ, A16_MK, A16_MV, A16_GK, A16_GQ, A16_GV, A16_W = 0, 512, 1024, 1536, 2048, 2304, 2560
A32_Y, A32_RQ, A32_RK, A32_RV, A32_GF, A32_GB, A32_W = 0, 256, 384, 512, 768, 1024, 1280


def _cparams(n_axes):
    return pltpu.CompilerParams(dimension_semantics=("arbitrary",) * n_axes,
                                vmem_limit_bytes=VMEM_LIMIT)


def _resident(a, n_axes):
    return pl.BlockSpec(a.shape, lambda *_: (0,) * a.ndim, pipeline_mode=pl.Buffered(1))


def _rms(x):
    return x * lax.rsqrt(jnp.mean(x * x, axis=-1, keepdims=True) + EPS)


def _sigmoid(x):
    return 1.0 / (1.0 + jnp.exp(-x))


def _silu(x):
    return x * _sigmoid(x)


def _dot(a, b):
    return jnp.dot(a, b, preferred_element_type=F32)


def _dot_nt(a, b):
    return lax.dot_general(a, b, (((1,), (1,)), ((), ())), preferred_element_type=F32)


def _dot_tn(a, b):
    return lax.dot_general(a, b, (((0,), (0,)), ((), ())), preferred_element_type=F32)


def _rope(x, cos, sin_up, sin_dn, shift):
    n = x.shape[-1]
    return x * cos + pltpu.roll(x, n - shift, 1) * sin_up + pltpu.roll(x, shift, 1) * sin_dn


def _adaln_kernel(cond_ref, w_ref, b_ref, o_ref):
    cond = cond_ref[...]
    o_ref[...] = _dot(_silu(cond).astype(BF16), w_ref[...].astype(BF16)) + b_ref[...]


def _adaln(cond, w_mod, b_mod):
    depth, d, n = w_mod.shape
    r = cond.shape[0]
    tn = 1536
    return pl.pallas_call(
        _adaln_kernel,
        grid=(depth, n // tn),
        in_specs=[pl.BlockSpec((r, d), lambda l, j: (0, 0)),
                  pl.BlockSpec((None, d, tn), lambda l, j: (l, 0, j)),
                  pl.BlockSpec((None, 1, tn), lambda l, j: (l, 0, j))],
        out_specs=pl.BlockSpec((None, r, tn), lambda l, j: (l, 0, j)),
        out_shape=jax.ShapeDtypeStruct((depth, r, n), F32),
        compiler_params=_cparams(2),
        name="adaln",
    )(cond, w_mod, b_mod.reshape(depth, 1, n))


def _in_proj_kernel(x_ref, mod_ref, w_in_ref, w_uq_ref, w_ukv_ref, g_mq_ref, g_mkv_ref, g_gq_ref, g_gk_ref,
                    *rest, rotary):
    a16_ref, a32_ref = rest[-2:]
    x = x_ref[...]
    h = (_rms(x) * (1.0 + mod_ref[1:2, :]) + mod_ref[0:1, :]).astype(BF16)

    def proj(off, width):
        return _dot(h, w_in_ref[:, off:off + width])

    def put16(off, val):
        a16_ref[:, off:off + val.shape[1]] = val.astype(BF16)

    def put32(off, val):
        a32_ref[:, off:off + val.shape[1]] = val

    lo = lax.broadcasted_iota(jnp.int32, (1, LANES), 1) < (LANES // 2)
    if rotary:
        tg_ref, tm_ref = rest[:2]
        rope_g = lambda v: _rope(v, tg_ref[0], tg_ref[1], tg_ref[2], GQA_HEAD_DIM // 4)
        rope_m = lambda v: _rope(v, tm_ref[0], tm_ref[1], tm_ref[2], MLA_ROPE // 4)
    else:
        rope_g = rope_m = lambda v: v

    a = proj(OFF_A, 2 * GROUP_W)
    put32(A32_Y, a[:, :GROUP_W] * _sigmoid(a[:, GROUP_W:]))

    cq = (_rms(proj(OFF_CQ, MLA_Q_RANK)) * g_mq_ref[...]).astype(BF16)
    q = _dot(cq, w_uq_ref[...])
    q_scale = (MLA_NOPE + MLA_ROPE) ** -0.5 * LOG2_E
    for hd in range(MLA_HEADS):
        put16(A16_MQ + hd * LANES, rope_m(q[:, hd * LANES:(hd + 1) * LANES]) * q_scale)
    ckv_kpe = proj(OFF_CKV, 2 * LANES)
    ckv = (_rms(ckv_kpe[:, :LANES]) * g_mkv_ref[...]).astype(BF16)
    kv = _dot(ckv, w_ukv_ref[...])
    kpe = rope_m(ckv_kpe[:, LANES:])
    for hd in range(MLA_HEADS):
        put16(A16_MK + hd * LANES, kv[:, hd * LANES:(hd + 1) * LANES] + kpe)
        put16(A16_MV + hd * LANES, jnp.where(lo, kv[:, (MLA_HEADS + hd) * LANES:(MLA_HEADS + hd + 1) * LANES], 1.0))

    def pair_norm_rope(xp, gain):
        x2 = xp * xp
        s_lo = jnp.sum(jnp.where(lo, x2, 0.0), axis=-1, keepdims=True)
        s_hi = jnp.sum(jnp.where(lo, 0.0, x2), axis=-1, keepdims=True)
        r = jnp.where(lo, lax.rsqrt(s_lo * (1.0 / GQA_HEAD_DIM) + EPS), lax.rsqrt(s_hi * (1.0 / GQA_HEAD_DIM) + EPS))
        return rope_g(xp * r * gain)

    gq = proj(OFF_GQ, GQA_HEADS * GQA_HEAD_DIM)
    for pr in range(2):
        qp = pair_norm_rope(gq[:, pr * LANES:(pr + 1) * LANES], g_gq_ref[...])
        put16(A16_GQ + pr * LANES, qp * (GQA_HEAD_DIM ** -0.5 * LOG2_E))
    gkv = proj(OFF_GK, 2 * LANES)
    kp = pair_norm_rope(gkv[:, :LANES], g_gk_ref[...])
    kp_sw = pltpu.roll(kp, LANES // 2, 1)
    put16(A16_GK + 0 * LANES, jnp.where(lo, kp, 0.0))
    put16(A16_GK + 1 * LANES, jnp.where(lo, 0.0, kp_sw))
    put16(A16_GK + 2 * LANES, jnp.where(lo, kp_sw, 0.0))
    put16(A16_GK + 3 * LANES, jnp.where(lo, 0.0, kp))
    vp = gkv[:, LANES:]
    put16(A16_GV + 0 * LANES, jnp.where(lo, vp, 1.0))
    put16(A16_GV + 1 * LANES, jnp.where(lo, pltpu.roll(vp, LANES // 2, 1), 1.0))

    rqk = proj(OFF_RQ, 2 * LANES)
    put32(A32_RQ, rqk[:, :LANES])
    put32(A32_RK, rqk[:, LANES:] * RET_QK ** -0.5)
    put32(A32_RV, proj(OFF_RV, RET_HEADS * RET_V))
    put32(A32_GF, _silu(proj(OFF_GF, GROUP_W)))
    put32(A32_GB, _silu(proj(OFF_GB, GROUP_W)))


def _in_proj(xseq, mods, weights, tables, tm):
    b, n, d = xseq.shape
    per_batch = mods.shape[0] == b
    tok = lambda w: pl.BlockSpec((None, tm, w), lambda i, j: (i, j, 0))
    in_specs = [tok(d), pl.BlockSpec((None, N_MOD, d), lambda i, j: (i if per_batch else 0, 0, 0))]
    in_specs += [_resident(w, 2) for w in weights]
    args = [xseq, mods, *weights]
    if tables is not None:
        in_specs += [pl.BlockSpec((3, tm, LANES), lambda i, j: (0, j, 0))] * 2
        args += list(tables)
    return pl.pallas_call(
        functools.partial(_in_proj_kernel, rotary=tables is not None),
        grid=(b, n // tm),
        in_specs=in_specs,
        out_specs=[tok(A16_W), tok(A32_W)],
        out_shape=[jax.ShapeDtypeStruct((b, n, A16_W), BF16), jax.ShapeDtypeStruct((b, n, A32_W), F32)],
        compiler_params=_cparams(2),
        name="in_proj",
    )(*args)


def _conv_kernel(yl_ref, yc_ref, dw_ref, b_ref, lg_ref, lb_ref, pw_ref, ol_ref, oc_ref, sh_ref, *, rows):
    zeros = jnp.zeros((CONV_PAD, GROUP_W), F32)
    bias = b_ref[...]
    first_off = CONV_PAD - CONV_WIDTH // 2
    tail = ((first_off + CONV_WIDTH - 1) // SUBLANES) * SUBLANES

    def sequence(y_ref, o_ref):
        length = y_ref.shape[0]
        sh_ref[0, 0:CONV_PAD, :] = zeros
        sh_ref[0, CONV_PAD:CONV_PAD + length, :] = y_ref[...]
        sh_ref[0, CONV_PAD + length:2 * CONV_PAD + length, :] = zeros

        def shift_block(c, carry):
            base = pl.multiple_of(c * rows, rows)
            win = sh_ref[0, pl.ds(base, rows + SUBLANES), :]
            for r in range(1, SUBLANES):
                sh_ref[r, pl.ds(base, rows), :] = win[r:r + rows, :]
            return carry

        lax.fori_loop(0, length // rows, shift_block, 0)
        win = sh_ref[0, length:length + tail + SUBLANES, :]
        for r in range(1, SUBLANES):
            sh_ref[r, length:length + tail, :] = win[r:r + tail, :]

        def taps(c, carry):
            base = pl.multiple_of(c * rows, rows)
            acc = jnp.zeros((rows, GROUP_W), F32) + bias
            for k in range(CONV_WIDTH):
                off = first_off + k
                slab = sh_ref[off % SUBLANES, pl.ds(base + (off // SUBLANES) * SUBLANES, rows), :]
                acc = acc + slab * dw_ref[k:k + 1, :]
            sh_ref[0, pl.ds(base, rows), :] = acc
            return carry

        lax.fori_loop(0, length // rows, taps, 0)

        def finish(c, carry):
            base = pl.multiple_of(c * NORM_ROWS, NORM_ROWS)
            acc = sh_ref[0, pl.ds(base, NORM_ROWS), :]
            mu = jnp.mean(acc, axis=-1, keepdims=True)
            cen = acc - mu
            var = jnp.mean(cen * cen, axis=-1, keepdims=True)
            z = _silu(cen * lax.rsqrt(var + EPS) * lg_ref[...] + lb_ref[...])
            o_ref[pl.ds(base, NORM_ROWS), :] = _dot(z.astype(BF16), pw_ref[...]).astype(BF16)
            return carry

        n_norm = length // NORM_ROWS
        lax.fori_loop(0, n_norm, finish, 0, unroll=min(n_norm, 4))

    sequence(yc_ref, oc_ref)
    sequence(yl_ref, ol_ref)


def _conv(a32_lat, a32_ctx, dw, bias, ln_g, ln_b, pw):
    b, n_lat, _ = a32_lat.shape
    n_ctx = a32_ctx.shape[1]
    col = A32_Y // GROUP_W
    seq = lambda n: pl.BlockSpec((None, n, GROUP_W), lambda i: (i, 0, col))
    out = lambda n: pl.BlockSpec((None, n, GROUP_W), lambda i: (i, 0, 0))
    return pl.pallas_call(
        functools.partial(_conv_kernel, rows=64),
        grid=(b,),
        in_specs=[seq(n_lat), seq(n_ctx)] + [_resident(w, 1) for w in (dw, bias, ln_g, ln_b, pw)],
        out_specs=[out(n_lat), out(n_ctx)],
        out_shape=[jax.ShapeDtypeStruct((b, n_lat, GROUP_W), BF16), jax.ShapeDtypeStruct((b, n_ctx, GROUP_W), BF16)],
        scratch_shapes=[pltpu.VMEM((SUBLANES, max(n_lat, n_ctx) + 2 * CONV_PAD, GROUP_W), F32)],
        compiler_params=_cparams(1),
        name="conv",
    )(a32_lat, a32_ctx, dw, bias, ln_g, ln_b, pw)


def _normalized_pair(acc_even, acc_odd):
    half = LANES // 2
    lo = lax.broadcasted_iota(jnp.int32, (1, LANES), 1) < half
    return jnp.where(lo, acc_even / pltpu.roll(acc_even, half, 1), pltpu.roll(acc_odd, half, 1) / acc_odd)


def _attn_lat_kernel(q_ref, qn_ref, kl_ref, kc_ref, vl_ref, vc_ref, o_ref, s_ref, m_ref, mb_ref, acc_ref,
                     *, q_blocks, v_blocks):
    tq = q_ref.shape[0]
    tk = KEY_CHUNK
    chunks = [(kl_ref, vl_ref, c * tk) for c in range(kl_ref.shape[0] // tk)]
    chunks += [(kc_ref, vc_ref, c * tk) for c in range(kc_ref.shape[0] // tk)]

    def scores_chunk(hd, ci, ref):
        slot = hd % 2
        k_ref, _, r0 = chunks[ci]
        q = ref[:, q_blocks[hd] * LANES:(q_blocks[hd] + 1) * LANES]
        s = _dot_nt(q, k_ref[r0:r0 + tk, hd * LANES:(hd + 1) * LANES])
        s_ref[slot, :, ci * tk:(ci + 1) * tk] = s
        mx = s[:, 0:LANES]
        for i in range(1, tk // LANES):
            mx = jnp.maximum(mx, s[:, i * LANES:(i + 1) * LANES])
        m_ref[slot] = mx if ci == 0 else jnp.maximum(m_ref[slot], mx)

    def scores_finish(hd):
        slot = hd % 2
        mb_ref[slot] = jnp.broadcast_to(jnp.max(m_ref[slot], axis=-1, keepdims=True), (tq, LANES))

    def values_chunk(hd, ci):
        slot = hd % 2
        _, v_ref, r0 = chunks[ci]
        mb = mb_ref[slot]
        p = jnp.concatenate(
            [jnp.exp2(s_ref[slot, :, ci * tk + i * LANES:ci * tk + (i + 1) * LANES] - mb).astype(BF16)
             for i in range(tk // LANES)], axis=1)
        pv = _dot(p, v_ref[r0:r0 + tk, v_blocks[hd] * LANES:(v_blocks[hd] + 1) * LANES])
        acc_ref[slot] = pv if ci == 0 else acc_ref[slot] + pv

    @pl.when(pl.program_id(1) == 0)
    def _():
        for ci in range(len(chunks)):
            scores_chunk(0, ci, q_ref)
        scores_finish(0)

    for hd in range(4):
        nxt = (hd + 1) % 4
        for ci in range(len(chunks)):
            values_chunk(hd, ci)
            scores_chunk(nxt, ci, q_ref if hd < 3 else qn_ref)
        scores_finish(nxt)
        if hd % 2 == 1:
            pr = hd // 2
            o_ref[:, pr * LANES:(pr + 1) * LANES] = _normalized_pair(acc_ref[0], acc_ref[1]).astype(BF16)


def _attention_lat(a16_lat, a16_ctx, q_off, q_width, k_off, v_off, v_width, q_blocks, v_blocks, name):
    b, n_lat, _ = a16_lat.shape
    n_ctx = a16_ctx.shape[1]
    tq = LAT_TILE
    n_tiles = n_lat // tq
    kw = 4 * LANES
    qc, kc, vc = q_off // q_width, k_off // kw, v_off // v_width
    return pl.pallas_call(
        functools.partial(_attn_lat_kernel, q_blocks=q_blocks, v_blocks=v_blocks),
        grid=(b, n_tiles),
        in_specs=[pl.BlockSpec((None, tq, q_width), lambda i, j: (i, j, qc)),
                  pl.BlockSpec((None, tq, q_width), lambda i, j: (i, jnp.minimum(j + 1, n_tiles - 1), qc)),
                  pl.BlockSpec((None, n_lat, kw), lambda i, j: (i, 0, kc)),
                  pl.BlockSpec((None, n_ctx, kw), lambda i, j: (i, 0, kc)),
                  pl.BlockSpec((None, n_lat, v_width), lambda i, j: (i, 0, vc)),
                  pl.BlockSpec((None, n_ctx, v_width), lambda i, j: (i, 0, vc))],
        out_specs=pl.BlockSpec((None, tq, 2 * LANES), lambda i, j: (i, j, 0)),
        out_shape=jax.ShapeDtypeStruct((b, n_lat, 2 * LANES), BF16),
        scratch_shapes=[pltpu.VMEM((2, tq, n_lat + n_ctx), F32), pltpu.VMEM((2, tq, LANES), F32),
                        pltpu.VMEM((2, tq, LANES), F32), pltpu.VMEM((2, tq, LANES), F32)],
        compiler_params=_cparams(2),
        name=name,
    )(a16_lat, a16_lat, a16_lat, a16_ctx, a16_lat, a16_ctx)


def _attn_ctx_kernel(q_ref, k_ref, v_ref, o_ref, *, q_blocks, v_blocks):
    for pr in range(2):
        accs = []
        for half in range(2):
            hd = 2 * pr + half
            q = q_ref[:, q_blocks[hd] * LANES:(q_blocks[hd] + 1) * LANES]
            s = _dot_nt(q, k_ref[:, hd * LANES:(hd + 1) * LANES])
            p = jnp.exp2(s - jnp.max(s, axis=-1, keepdims=True))
            accs.append(_dot(p.astype(BF16), v_ref[:, v_blocks[hd] * LANES:(v_blocks[hd] + 1) * LANES]))
        o_ref[:, pr * LANES:(pr + 1) * LANES] = _normalized_pair(*accs).astype(BF16)


def _attention_ctx(a16_ctx, q_off, q_width, k_off, v_off, v_width, q_blocks, v_blocks, name):
    b, n_ctx, _ = a16_ctx.shape
    kw = 4 * LANES
    qc, kc, vc = q_off // q_width, k_off // kw, v_off // v_width
    return pl.pallas_call(
        functools.partial(_attn_ctx_kernel, q_blocks=q_blocks, v_blocks=v_blocks),
        grid=(b,),
        in_specs=[pl.BlockSpec((None, n_ctx, q_width), lambda i: (i, 0, qc)),
                  pl.BlockSpec((None, n_ctx, kw), lambda i: (i, 0, kc)),
                  pl.BlockSpec((None, n_ctx, v_width), lambda i: (i, 0, vc))],
        out_specs=pl.BlockSpec((None, n_ctx, 2 * LANES), lambda i: (i, 0, 0)),
        out_shape=jax.ShapeDtypeStruct((b, n_ctx, 2 * LANES), BF16),
        compiler_params=_cparams(1),
        name=name,
    )(a16_ctx, a16_ctx, a16_ctx)


def _log_sigmoid(x):
    return jnp.minimum(x, 0.0) - jnp.log(1.0 + jnp.exp(-jnp.abs(x)))


def _ret_kernel(ql_ref, kl_ref, vl_ref, gfl_ref, gbl_ref, qc_ref, kc_ref, vc_ref, gfc_ref, gbc_ref,
                dq_ref, dv_ref, dc_ref, ng_ref, ol_ref, oc_ref, acc_ref):
    c = RET_CHUNK
    n_lat, n_ctx = ql_ref.shape[0], qc_ref.shape[0]
    qk_w = RET_HEADS * RET_QK
    v_w = RET_HEADS * RET_V

    row_i = lax.broadcasted_iota(jnp.int32, (c, 1), 0).astype(F32)
    qk_head = lax.broadcasted_iota(jnp.int32, (1, qk_w), 1) // RET_QK
    v_head = lax.broadcasted_iota(jnp.int32, (1, v_w), 1) // RET_V
    qk_masks = [(qk_head == hd).astype(F32) for hd in range(RET_HEADS)]
    v_masks = [(v_head == hd).astype(F32) for hd in range(RET_HEADS)]
    state_rows = lax.broadcasted_iota(jnp.int32, (qk_w, 1), 0) // RET_QK
    bd_mask = (state_rows == v_head).astype(F32)
    ci = lax.broadcasted_iota(jnp.int32, (c, RET_HEADS * c), 0).astype(F32)
    cj = (lax.broadcasted_iota(jnp.int32, (c, RET_HEADS * c), 1) % c).astype(F32)

    def make_step(d):
        lg_q = _log_sigmoid(dq_ref[d:d + 1, :])
        lg_v = _log_sigmoid(dv_ref[d:d + 1, :])
        lg_c = _log_sigmoid(dc_ref[d:d + 1, :])
        gain = ng_ref[d:d + 1, :]
        if d == 0:
            diff = ci - cj
            q_dec = jnp.exp(lg_q * (row_i + 1.0))
            k_dec = jnp.exp(lg_q * (c - 1.0 - row_i))
        else:
            diff = cj - ci
            q_dec = jnp.exp(lg_q * (c - row_i))
            k_dec = jnp.exp(lg_q * row_i)
        decay = jnp.where(diff >= 0.0, jnp.exp(lg_c * jnp.maximum(diff, 0.0)), 0.0)
        chunk_dec = jnp.exp(lg_v * float(c))

        def step(refs, r0, acc_r0, state):
            q_ref, k_ref, v_ref, gate_ref = refs
            q = q_ref[pl.ds(r0, c), :]
            k = k_ref[pl.ds(r0, c), :]
            v = v_ref[pl.ds(r0, c), :]
            k_bd = jnp.concatenate([k * m for m in qk_masks], axis=0).astype(BF16)
            scores = _dot_nt(q.astype(BF16), k_bd) * decay
            v_bd = jnp.concatenate([v * m for m in v_masks], axis=0).astype(BF16)
            inner = _dot(scores.astype(BF16), v_bd)
            cross = _dot((q * q_dec).astype(BF16), state.astype(BF16))
            o = inner + cross
            kv = _dot_tn((k * k_dec).astype(BF16), v.astype(BF16))
            new_state = state * chunk_dec + kv * bd_mask
            o2 = o * o
            r = jnp.zeros_like(o)
            for m in v_masks:
                ss = jnp.sum(o2 * m, axis=-1, keepdims=True)
                r = r + m * lax.rsqrt(ss * (1.0 / RET_V) + EPS)
            acc_ref[d, pl.ds(acc_r0, c), :] = o * r * gain * gate_ref[pl.ds(r0, c), :]
            return new_state

        return step

    fwd, bwd = make_step(0), make_step(1)

    def scan(n, acc_base, fwd_refs, bwd_refs, states):
        def body(i, st):
            rf = pl.multiple_of(i * c, c)
            rb = pl.multiple_of((n - 1 - i) * c, c)
            return fwd(fwd_refs, rf, acc_base + rf, st[0]), bwd(bwd_refs, rb, acc_base + rb, st[1])
        return lax.fori_loop(0, n, body, states)

    zero = jnp.zeros((qk_w, v_w), F32)
    states = scan(n_ctx // c, n_lat, (qc_ref, kc_ref, vc_ref, gfc_ref), (qc_ref, kc_ref, vc_ref, gbc_ref), (zero, zero))
    scan(n_lat // c, 0, (ql_ref, kl_ref, vl_ref, gfl_ref), (ql_ref, kl_ref, vl_ref, gbl_ref), states)

    def combine(o_ref, acc_base, rows):
        def body(i, carry):
            r0 = pl.multiple_of(i * rows, rows)
            o_ref[pl.ds(r0, rows), :] = (acc_ref[0, pl.ds(acc_base + r0, rows), :]
                                         + acc_ref[1, pl.ds(acc_base + r0, rows), :]).astype(BF16)
            return carry
        lax.fori_loop(0, o_ref.shape[0] // rows, body, 0)

    combine(ol_ref, 0, 2 * c)
    combine(oc_ref, n_lat, 2 * c)


def _retention(a32_lat, a32_ctx, dec_q, dec_v, dec_c, norm_g):
    b, n_lat, _ = a32_lat.shape
    n_ctx = a32_ctx.shape[1]

    def seq_specs(n):
        spec = lambda off, w: pl.BlockSpec((None, n, w), lambda i: (i, 0, off // w))
        return [spec(A32_RQ, 128), spec(A32_RK, 128), spec(A32_RV, 256), spec(A32_GF, 256), spec(A32_GB, 256)]

    smalls = (dec_q, dec_v, dec_c, norm_g)
    out = lambda n: pl.BlockSpec((None, n, 256), lambda i: (i, 0, 0))
    return pl.pallas_call(
        _ret_kernel,
        grid=(b,),
        in_specs=seq_specs(n_lat) + seq_specs(n_ctx) + [_resident(a, 1) for a in smalls],
        out_specs=[out(n_lat), out(n_ctx)],
        out_shape=[jax.ShapeDtypeStruct((b, n_lat, 256), BF16), jax.ShapeDtypeStruct((b, n_ctx, 256), BF16)],
        scratch_shapes=[pltpu.VMEM((2, n_lat + n_ctx, 256), F32)],
        compiler_params=_cparams(1),
        name="retention",
    )(*([a32_lat] * 5), *([a32_ctx] * 5), *smalls)


def _out_mlp_kernel(x_ref, ya_ref, yb_ref, yc_ref, yd_ref, mod_ref, wo_ref, w1_ref, w2_ref, *rest, ff_chunk):
    o_ref = rest[-1]
    x = x_ref[...]
    gw = GROUP_W
    y = (_dot(ya_ref[...], wo_ref[0 * gw:1 * gw, :]) + _dot(yb_ref[...], wo_ref[1 * gw:2 * gw, :])
         + _dot(yc_ref[...], wo_ref[2 * gw:3 * gw, :]) + _dot(yd_ref[...], wo_ref[3 * gw:4 * gw, :]))
    x1 = x + mod_ref[2:3, :] * y
    h = (_rms(x1) * (1.0 + mod_ref[4:5, :]) + mod_ref[3:4, :]).astype(BF16)
    acc = jnp.zeros_like(x1)
    for j in range(w1_ref.shape[1] // ff_chunk):
        a = jnp.maximum(_dot(h, w1_ref[:, j * ff_chunk:(j + 1) * ff_chunk]), 0.0)
        acc = acc + _dot((a * a).astype(BF16), w2_ref[j * ff_chunk:(j + 1) * ff_chunk, :])
    x2 = x1 + mod_ref[5:6, :] * acc
    o_ref[...] = x2 if len(rest) == 1 else _rms(x2) * rest[0][...]


def _out_mlp(xseq, ys, mods, weights, tm, in_place, final_gain=None):
    b, n, d = xseq.shape
    per_batch = mods.shape[0] == b
    tok = lambda w: pl.BlockSpec((None, tm, w), lambda i, j: (i, j, 0))
    in_specs = [tok(d)] + [tok(GROUP_W)] * 4
    in_specs += [pl.BlockSpec((None, N_MOD, d), lambda i, j: (i if per_batch else 0, 0, 0))]
    in_specs += [_resident(w, 2) for w in weights]
    args = [xseq, *ys, mods, *weights]
    if final_gain is not None:
        in_specs.append(_resident(final_gain, 2))
        args.append(final_gain)
    return pl.pallas_call(
        functools.partial(_out_mlp_kernel, ff_chunk=1024),
        grid=(b, n // tm),
        in_specs=in_specs,
        out_specs=tok(d),
        out_shape=jax.ShapeDtypeStruct((b, n, d), F32),
        input_output_aliases={0: 0} if in_place else {},
        compiler_params=_cparams(2),
        name="out_mlp",
    )(*args)


def _rope_tables(n_lat):
    n = np.arange(n_lat)
    row, col = (n // GRID_W).astype(np.float64), (n % GRID_W).astype(np.float64)

    def table(groups):
        cos = np.ones((n_lat, LANES)); up = np.zeros((n_lat, LANES)); dn = np.zeros((n_lat, LANES))
        for rot_start, rot_dim in groups:
            half = rot_dim // 2
            q = half // 2
            freqs = ROPE_BASE ** (-np.arange(q, dtype=np.float64) / q)
            for axis, pos in enumerate((row, col)):
                ang = pos[:, None] * freqs[None, :]
                base = rot_start + axis * half
                cos[:, base:base + q] = np.cos(ang)
                cos[:, base + q:base + 2 * q] = np.cos(ang)
                up[:, base:base + q] = -np.sin(ang)
                dn[:, base + q:base + 2 * q] = np.sin(ang)
        return jnp.asarray(np.stack([cos, up, dn]).astype(np.float32))

    return table([(0, GQA_HEAD_DIM), (GQA_HEAD_DIM, GQA_HEAD_DIM)]), table([(MLA_NOPE, MLA_ROPE)])


def _layout_w_in(w_in):
    d = w_in.shape[0]
    kpe_src = OFF_CKV + MLA_KV_RANK
    w = w_in.astype(BF16)
    out = jnp.concatenate([w[:, :kpe_src], jnp.zeros((d, MLA_NOPE), BF16), w[:, kpe_src:kpe_src + MLA_ROPE],
                           jnp.zeros((d, LANES - MLA_NOPE - MLA_ROPE), BF16), w[:, kpe_src + MLA_ROPE:]], axis=1)
    assert out.shape[1] == N_IN
    return out


def _layout_w_uq(w_uq):
    r = w_uq.shape[0]
    w = w_uq.reshape(r, MLA_HEADS, MLA_NOPE + MLA_ROPE)
    w = jnp.pad(w, ((0, 0), (0, 0), (0, LANES - MLA_NOPE - MLA_ROPE)))
    return w.reshape(r, MLA_HEADS * LANES).astype(BF16)


def _layout_w_ukv(w_ukv):
    r = w_ukv.shape[0]
    w = w_ukv.reshape(r, MLA_HEADS, MLA_NOPE + MLA_V)
    k = jnp.pad(w[:, :, :MLA_NOPE], ((0, 0), (0, 0), (0, LANES - MLA_NOPE))).reshape(r, MLA_HEADS * LANES)
    v = jnp.pad(w[:, :, MLA_NOPE:], ((0, 0), (0, 0), (0, LANES - MLA_V))).reshape(r, MLA_HEADS * LANES)
    return jnp.concatenate([k, v], axis=1).astype(BF16)


def kernel(x, c, ctx, c_ctx, w_mod, b_mod, w_in, w_out, conv_dw, conv_b, conv_ln_g, conv_ln_b, conv_pw, mla_q_g, mla_kv_g, mla_uq, mla_ukv, gqa_q_g, gqa_k_g, ret_decay, ret_norm_g, mlp_w1, mlp_w2, final_g):
    b, n_lat, d = x.shape
    n_ctx = ctx.shape[1]
    depth = w_mod.shape[0]
    assert n_lat % LAT_TILE == 0 and n_lat % KEY_CHUNK == 0 and n_ctx % KEY_CHUNK == 0
    assert n_ctx % NORM_ROWS == 0 and n_lat % NORM_ROWS == 0 and NORM_ROWS % (2 * RET_CHUNK) == 0

    rows = ((b + 1 + SUBLANES - 1) // SUBLANES) * SUBLANES
    cond = jnp.zeros((rows, d), F32).at[:b].set(c).at[b].set(c_ctx)
    mod_all = _adaln(cond, w_mod, b_mod).reshape(depth, rows, N_MOD, d)
    tables = _rope_tables(n_lat)
    mla = (A16_MQ, 4 * LANES, A16_MK, A16_MV, 4 * LANES, (0, 1, 2, 3), (0, 1, 2, 3))
    gqa = (A16_GQ, 2 * LANES, A16_GK, A16_GV, 2 * LANES, (0, 0, 1, 1), (0, 0, 1, 1))

    cx = ctx
    for layer in range(depth):
        last = layer == depth - 1
        mods_lat, mods_ctx = mod_all[layer, :b], mod_all[layer, b:b + 1]
        w_proj = (_layout_w_in(w_in[layer]), _layout_w_uq(mla_uq[layer]), _layout_w_ukv(mla_ukv[layer]),
                  mla_q_g[layer][None, :], mla_kv_g[layer][None, :],
                  jnp.tile(gqa_q_g[layer], 2)[None, :], jnp.tile(gqa_k_g[layer], 2)[None, :])
        a16_lat, a32_lat = _in_proj(x, mods_lat, w_proj, tables, LAT_TILE)
        a16_ctx, a32_ctx = _in_proj(cx, mods_ctx, w_proj, None, n_ctx)
        ya_lat, ya_ctx = _conv(a32_lat, a32_ctx, conv_dw[layer], conv_b[layer][None, :], conv_ln_g[layer][None, :],
                               conv_ln_b[layer][None, :], conv_pw[layer].astype(BF16))
        yb_lat = _attention_lat(a16_lat, a16_ctx, *mla, "mla_attn")
        yc_lat = _attention_lat(a16_lat, a16_ctx, *gqa, "gqa_attn")
        dec = ret_decay[layer].astype(F32)
        yd_lat, yd_ctx = _retention(a32_lat, a32_ctx, jnp.repeat(dec, RET_QK, axis=1), jnp.repeat(dec, RET_V, axis=1),
                                    jnp.repeat(dec, RET_CHUNK, axis=1), ret_norm_g[layer].reshape(2, RET_HEADS * RET_V))
        w_mlp = (w_out[layer].astype(BF16), mlp_w1[layer].astype(BF16), mlp_w2[layer].astype(BF16))
        x = _out_mlp(x, (ya_lat, yb_lat, yc_lat, yd_lat), mods_lat, w_mlp, LAT_TILE, layer > 0,
                     final_gain=final_g[None, :] if last else None)
        if not last:
            yb_ctx = _attention_ctx(a16_ctx, *mla, "mla_attn_ctx")
            yc_ctx = _attention_ctx(a16_ctx, *gqa, "gqa_attn_ctx")
            cx = _out_mlp(cx, (ya_ctx, yb_ctx, yc_ctx, yd_ctx), mods_ctx, w_mlp, n_ctx, layer > 0)
    return x
```

```python
import functools

import numpy as np
import jax
import jax.numpy as jnp
from jax import lax
from jax.experimental import pallas as pl
from jax.experimental.pallas import tpu as pltpu

F32 = jnp.float32
BF16 = jnp.bfloat16

GRID_W = 64
N_MOD = 6
EPS = 1e-6
ROPE_BASE = 10000.0
GROUP_W = 256
CONV_WIDTH = 31
CONV_PAD = 16
MLA_HEADS = 4
MLA_NOPE = 64
MLA_ROPE = 32
MLA_V = 64
MLA_Q_RANK = 256
MLA_KV_RANK = 128
GQA_HEADS = 4
GQA_KV_HEADS = 2
GQA_HEAD_DIM = 64
RET_HEADS = 4
RET_QK = 32
RET_V = 64
RET_CHUNK = 128
LANES = 128
SUBLANES = 8
LOG2_E = 1.4426950408889634
LAT_TILE = 512
KEY_CHUNK = 256
NORM_ROWS = 256
VMEM_LIMIT = 56 * 1024 * 1024

OFF_A, OFF_CQ, OFF_CKV, OFF_GQ, OFF_GK = 0, 512, 768, 1024, 1280
OFF_RQ, OFF_RV, OFF_GF, OFF_GB, N_IN = 1536, 1792, 2048, 2304, 2560
A16_MQ, A16_MK, A16_MV, A16_GK, A16_GQ, A16_GV, A16_W = 0, 512, 1024, 1536, 2048, 2304, 2560
A32_Y, A32_RQ, A32_RK, A32_RV, A32_GF, A32_GB, A32_W = 0, 256, 384, 512, 768, 1024, 1280


def _cparams(n_axes):
    return pltpu.CompilerParams(dimension_semantics=("arbitrary",) * n_axes,
                                vmem_limit_bytes=VMEM_LIMIT)


def _resident(a, n_axes):
    return pl.BlockSpec(a.shape, lambda *_: (0,) * a.ndim, pipeline_mode=pl.Buffered(1))


def _rms(x):
    return x * lax.rsqrt(jnp.mean(x * x, axis=-1, keepdims=True) + EPS)


def _sigmoid(x):
    return 1.0 / (1.0 + jnp.exp(-x))


def _silu(x):
    return x * _sigmoid(x)


def _dot(a, b):
    return jnp.dot(a, b, preferred_element_type=F32)


def _dot_nt(a, b):
    return lax.dot_general(a, b, (((1,), (1,)), ((), ())), preferred_element_type=F32)


def _dot_tn(a, b):
    return lax.dot_general(a, b, (((0,), (0,)), ((), ())), preferred_element_type=F32)


def _rope(x, cos, sin_up, sin_dn, shift):
    n = x.shape[-1]
    return x * cos + pltpu.roll(x, n - shift, 1) * sin_up + pltpu.roll(x, shift, 1) * sin_dn


def _adaln_kernel(cond_ref, w_ref, b_ref, o_ref):
    cond = cond_ref[...]
    o_ref[...] = _dot(_silu(cond).astype(BF16), w_ref[...].astype(BF16)) + b_ref[...]


def _adaln(cond, w_mod, b_mod):
    depth, d, n = w_mod.shape
    r = cond.shape[0]
    tn = 1536
    return pl.pallas_call(
        _adaln_kernel,
        grid=(depth, n // tn),
        in_specs=[pl.BlockSpec((r, d), lambda l, j: (0, 0)),
                  pl.BlockSpec((None, d, tn), lambda l, j: (l, 0, j)),
                  pl.BlockSpec((None, 1, tn), lambda l, j: (l, 0, j))],
        out_specs=pl.BlockSpec((None, r, tn), lambda l, j: (l, 0, j)),
        out_shape=jax.ShapeDtypeStruct((depth, r, n), F32),
        compiler_params=_cparams(2),
        name="adaln",
    )(cond, w_mod, b_mod.reshape(depth, 1, n))


def _in_proj_kernel(x_ref, mod_ref, w_in_ref, w_uq_ref, w_ukv_ref, g_mq_ref, g_mkv_ref, g_gq_ref, g_gk_ref,
                    *rest, rotary, sub_tiles):
    a16_ref, a32_ref = rest[-2:]
    lo = lax.broadcasted_iota(jnp.int32, (1, LANES), 1) < (LANES // 2)
    q_scale = (MLA_NOPE + MLA_ROPE) ** -0.5 * LOG2_E
    n_rows = x_ref.shape[0] // sub_tiles

    for st in range(sub_tiles):
        rows = slice(st * n_rows, (st + 1) * n_rows)
        h = (_rms(x_ref[rows, :]) * (1.0 + mod_ref[1:2, :]) + mod_ref[0:1, :]).astype(BF16)

        def proj(off, width, h=h):
            return _dot(h, w_in_ref[:, off:off + width])

        def put16(off, val, rows=rows):
            a16_ref[rows, off:off + val.shape[1]] = val.astype(BF16)

        def put32(off, val, rows=rows):
            a32_ref[rows, off:off + val.shape[1]] = val

        if rotary:
            tg_ref, tm_ref = rest[:2]
            rope_g = lambda v, rows=rows: _rope(v, tg_ref[0, rows, :], tg_ref[1, rows, :], tg_ref[2, rows, :],
                                                GQA_HEAD_DIM // 4)
            rope_m = lambda v, rows=rows: _rope(v, tm_ref[0, rows, :], tm_ref[1, rows, :], tm_ref[2, rows, :],
                                                MLA_ROPE // 4)
        else:
            rope_g = rope_m = lambda v: v

        def pair_norm_rope(xp, gain, rope_g=rope_g):
            x2 = xp * xp
            s_lo = jnp.sum(jnp.where(lo, x2, 0.0), axis=-1, keepdims=True)
            s_hi = jnp.sum(jnp.where(lo, 0.0, x2), axis=-1, keepdims=True)
            r = jnp.where(lo, lax.rsqrt(s_lo * (1.0 / GQA_HEAD_DIM) + EPS),
                          lax.rsqrt(s_hi * (1.0 / GQA_HEAD_DIM) + EPS))
            return rope_g(xp * r * gain)

        cq = (_rms(proj(OFF_CQ, MLA_Q_RANK)) * g_mq_ref[...]).astype(BF16)
        ckv_kpe = proj(OFF_CKV, 2 * LANES)
        ckv = (_rms(ckv_kpe[:, :LANES]) * g_mkv_ref[...]).astype(BF16)
        gq = proj(OFF_GQ, GQA_HEADS * GQA_HEAD_DIM)
        gkv = proj(OFF_GK, 2 * LANES)

        a = proj(OFF_A, 2 * GROUP_W)
        put32(A32_Y, a[:, :GROUP_W] * _sigmoid(a[:, GROUP_W:]))
        rqk = proj(OFF_RQ, 2 * LANES)
        put32(A32_RQ, rqk[:, :LANES])
        put32(A32_RK, rqk[:, LANES:] * RET_QK ** -0.5)
        put32(A32_RV, proj(OFF_RV, RET_HEADS * RET_V))
        put32(A32_GF, _silu(proj(OFF_GF, GROUP_W)))
        put32(A32_GB, _silu(proj(OFF_GB, GROUP_W)))

        q = _dot(cq, w_uq_ref[...])
        for hd in range(MLA_HEADS):
            put16(A16_MQ + hd * LANES, rope_m(q[:, hd * LANES:(hd + 1) * LANES]) * q_scale)
        kv = _dot(ckv, w_ukv_ref[...])
        kpe = rope_m(ckv_kpe[:, LANES:])
        for hd in range(MLA_HEADS):
            put16(A16_MK + hd * LANES, kv[:, hd * LANES:(hd + 1) * LANES] + kpe)
            put16(A16_MV + hd * LANES,
                  jnp.where(lo, kv[:, (MLA_HEADS + hd) * LANES:(MLA_HEADS + hd + 1) * LANES], 1.0))

        for pr in range(2):
            qp = pair_norm_rope(gq[:, pr * LANES:(pr + 1) * LANES], g_gq_ref[...])
            put16(A16_GQ + pr * LANES, qp * (GQA_HEAD_DIM ** -0.5 * LOG2_E))
        kp = pair_norm_rope(gkv[:, :LANES], g_gk_ref[...])
        kp_sw = pltpu.roll(kp, LANES // 2, 1)
        put16(A16_GK + 0 * LANES, jnp.where(lo, kp, 0.0))
        put16(A16_GK + 1 * LANES, jnp.where(lo, 0.0, kp_sw))
        put16(A16_GK + 2 * LANES, jnp.where(lo, kp_sw, 0.0))
        put16(A16_GK + 3 * LANES, jnp.where(lo, 0.0, kp))
        vp = gkv[:, LANES:]
        put16(A16_GV + 0 * LANES, jnp.where(lo, vp, 1.0))
        put16(A16_GV + 1 * LANES, jnp.where(lo, pltpu.roll(vp, LANES // 2, 1), 1.0))


def _in_proj(xseq, mods, weights, tables, tm):
    b, n, d = xseq.shape
    per_batch = mods.shape[0] == b
    tok = lambda w: pl.BlockSpec((None, tm, w), lambda i, j: (i, j, 0))
    in_specs = [tok(d), pl.BlockSpec((None, N_MOD, d), lambda i, j: (i if per_batch else 0, 0, 0))]
    in_specs += [_resident(w, 2) for w in weights]
    args = [xseq, mods, *weights]
    if tables is not None:
        in_specs += [pl.BlockSpec((3, tm, LANES), lambda i, j: (0, j, 0))] * 2
        args += list(tables)
    return pl.pallas_call(
        functools.partial(_in_proj_kernel, rotary=tables is not None, sub_tiles=tm // 256),
        grid=(b, n // tm),
        in_specs=in_specs,
        out_specs=[tok(A16_W), tok(A32_W)],
        out_shape=[jax.ShapeDtypeStruct((b, n, A16_W), BF16), jax.ShapeDtypeStruct((b, n, A32_W), F32)],
        compiler_params=_cparams(2),
        name="in_proj",
    )(*args)


def _conv_kernel(yl_ref, yc_ref, dw_ref, b_ref, lg_ref, lb_ref, pw_ref, ol_ref, oc_ref, sh_ref, *, rows):
    zeros = jnp.zeros((CONV_PAD, GROUP_W), F32)
    bias = b_ref[...]
    first_off = CONV_PAD - CONV_WIDTH // 2
    tail = ((first_off + CONV_WIDTH - 1) // SUBLANES) * SUBLANES

    def sequence(y_ref, o_ref):
        length = y_ref.shape[0]
        sh_ref[0, 0:CONV_PAD, :] = zeros
        sh_ref[0, CONV_PAD:CONV_PAD + length, :] = y_ref[...]
        sh_ref[0, CONV_PAD + length:2 * CONV_PAD + length, :] = zeros

        def shift_block(c, carry):
            base = pl.multiple_of(c * rows, rows)
            win = sh_ref[0, pl.ds(base, rows + SUBLANES), :]
            for r in range(1, SUBLANES):
                sh_ref[r, pl.ds(base, rows), :] = win[r:r + rows, :]
            return carry

        lax.fori_loop(0, length // rows, shift_block, 0)
        win = sh_ref[0, length:length + tail + SUBLANES, :]
        for r in range(1, SUBLANES):
            sh_ref[r, length:length + tail, :] = win[r:r + tail, :]

        def taps(c, carry):
            base = pl.multiple_of(c * rows, rows)
            acc = jnp.zeros((rows, GROUP_W), F32) + bias
            for k in range(CONV_WIDTH):
                off = first_off + k
                slab = sh_ref[off % SUBLANES, pl.ds(base + (off // SUBLANES) * SUBLANES, rows), :]
                acc = acc + slab * dw_ref[k:k + 1, :]
            sh_ref[0, pl.ds(base, rows), :] = acc
            return carry

        lax.fori_loop(0, length // rows, taps, 0)

        def finish(c, carry):
            base = pl.multiple_of(c * NORM_ROWS, NORM_ROWS)
            acc = sh_ref[0, pl.ds(base, NORM_ROWS), :]
            mu = jnp.mean(acc, axis=-1, keepdims=True)
            cen = acc - mu
            var = jnp.mean(cen * cen, axis=-1, keepdims=True)
            z = _silu(cen * lax.rsqrt(var + EPS) * lg_ref[...] + lb_ref[...])
            o_ref[pl.ds(base, NORM_ROWS), :] = _dot(z.astype(BF16), pw_ref[...]).astype(BF16)
            return carry

        n_norm = length // NORM_ROWS
        lax.fori_loop(0, n_norm, finish, 0, unroll=min(n_norm, 4))

    sequence(yc_ref, oc_ref)
    sequence(yl_ref, ol_ref)


def _conv(a32_lat, a32_ctx, dw, bias, ln_g, ln_b, pw):
    b, n_lat, _ = a32_lat.shape
    n_ctx = a32_ctx.shape[1]
    col = A32_Y // GROUP_W
    seq = lambda n: pl.BlockSpec((None, n, GROUP_W), lambda i: (i, 0, col))
    out = lambda n: pl.BlockSpec((None, n, GROUP_W), lambda i: (i, 0, 0))
    return pl.pallas_call(
        functools.partial(_conv_kernel, rows=64),
        grid=(b,),
        in_specs=[seq(n_lat), seq(n_ctx)] + [_resident(w, 1) for w in (dw, bias, ln_g, ln_b, pw)],
        out_specs=[out(n_lat), out(n_ctx)],
        out_shape=[jax.ShapeDtypeStruct((b, n_lat, GROUP_W), BF16), jax.ShapeDtypeStruct((b, n_ctx, GROUP_W), BF16)],
        scratch_shapes=[pltpu.VMEM((SUBLANES, max(n_lat, n_ctx) + 2 * CONV_PAD, GROUP_W), F32)],
        compiler_params=_cparams(1),
        name="conv",
    )(a32_lat, a32_ctx, dw, bias, ln_g, ln_b, pw)


def _normalized_pair(acc_even, acc_odd):
    half = LANES // 2
    lo = lax.broadcasted_iota(jnp.int32, (1, LANES), 1) < half
    return jnp.where(lo, acc_even / pltpu.roll(acc_even, half, 1), pltpu.roll(acc_odd, half, 1) / acc_odd)


def _attn_lat_kernel(q_ref, qn_ref, kl_ref, kc_ref, vl_ref, vc_ref, o_ref, s_ref, m_ref, mb_ref, acc_ref,
                     *, q_blocks, v_blocks):
    tq = q_ref.shape[0]
    tk = KEY_CHUNK
    chunks = [(kl_ref, vl_ref, c * tk) for c in range(kl_ref.shape[0] // tk)]
    chunks += [(kc_ref, vc_ref, c * tk) for c in range(kc_ref.shape[0] // tk)]

    def scores_chunk(hd, ci, ref):
        slot = hd % 2
        k_ref, _, r0 = chunks[ci]
        q = ref[:, q_blocks[hd] * LANES:(q_blocks[hd] + 1) * LANES]
        s = _dot_nt(q, k_ref[r0:r0 + tk, hd * LANES:(hd + 1) * LANES])
        s_ref[slot, :, ci * tk:(ci + 1) * tk] = s
        mx = s[:, 0:LANES]
        for i in range(1, tk // LANES):
            mx = jnp.maximum(mx, s[:, i * LANES:(i + 1) * LANES])
        m_ref[slot] = mx if ci == 0 else jnp.maximum(m_ref[slot], mx)

    def scores_finish(hd):
        slot = hd % 2
        mb_ref[slot] = jnp.broadcast_to(jnp.max(m_ref[slot], axis=-1, keepdims=True), (tq, LANES))

    def values_chunk(hd, ci):
        slot = hd % 2
        _, v_ref, r0 = chunks[ci]
        mb = mb_ref[slot]
        p = jnp.concatenate(
            [jnp.exp2(s_ref[slot, :, ci * tk + i * LANES:ci * tk + (i + 1) * LANES] - mb).astype(BF16)
             for i in range(tk // LANES)], axis=1)
        pv = _dot(p, v_ref[r0:r0 + tk, v_blocks[hd] * LANES:(v_blocks[hd] + 1) * LANES])
        acc_ref[slot] = pv if ci == 0 else acc_ref[slot] + pv

    @pl.when(pl.program_id(1) == 0)
    def _():
        for ci in range(len(chunks)):
            scores_chunk(0, ci, q_ref)
        scores_finish(0)

    for hd in range(4):
        nxt = (hd + 1) % 4
        for ci in range(len(chunks)):
            values_chunk(hd, ci)
            scores_chunk(nxt, ci, q_ref if hd < 3 else qn_ref)
        scores_finish(nxt)
        if hd % 2 == 1:
            pr = hd // 2
            o_ref[:, pr * LANES:(pr + 1) * LANES] = _normalized_pair(acc_ref[0], acc_ref[1]).astype(BF16)


def _attention_lat(a16_lat, a16_ctx, q_off, q_width, k_off, v_off, v_width, q_blocks, v_blocks, name):
    b, n_lat, _ = a16_lat.shape
    n_ctx = a16_ctx.shape[1]
    tq = LAT_TILE
    n_tiles = n_lat // tq
    kw = 4 * LANES
    qc, kc, vc = q_off // q_width, k_off // kw, v_off // v_width
    return pl.pallas_call(
        functools.partial(_attn_lat_kernel, q_blocks=q_blocks, v_blocks=v_blocks),
        grid=(b, n_tiles),
        in_specs=[pl.BlockSpec((None, tq, q_width), lambda i, j: (i, j, qc)),
                  pl.BlockSpec((None, tq, q_width), lambda i, j: (i, jnp.minimum(j + 1, n_tiles - 1), qc)),
                  pl.BlockSpec((None, n_lat, kw), lambda i, j: (i, 0, kc)),
                  pl.BlockSpec((None, n_ctx, kw), lambda i, j: (i, 0, kc)),
                  pl.BlockSpec((None, n_lat, v_width), lambda i, j: (i, 0, vc)),
                  pl.BlockSpec((None, n_ctx, v_width), lambda i, j: (i, 0, vc))],
        out_specs=pl.BlockSpec((None, tq, 2 * LANES), lambda i, j: (i, j, 0)),
        out_shape=jax.ShapeDtypeStruct((b, n_lat, 2 * LANES), BF16),
        scratch_shapes=[pltpu.VMEM((2, tq, n_lat + n_ctx), F32), pltpu.VMEM((2, tq, LANES), F32),
                        pltpu.VMEM((2, tq, LANES), F32), pltpu.VMEM((2, tq, LANES), F32)],
        compiler_params=_cparams(2),
        name=name,
    )(a16_lat, a16_lat, a16_lat, a16_ctx, a16_lat, a16_ctx)


def _attn_ctx_kernel(q_ref, k_ref, v_ref, o_ref, *, q_blocks, v_blocks):
    for pr in range(2):
        accs = []
        for half in range(2):
            hd = 2 * pr + half
            q = q_ref[:, q_blocks[hd] * LANES:(q_blocks[hd] + 1) * LANES]
            s = _dot_nt(q, k_ref[:, hd * LANES:(hd + 1) * LANES])
            p = jnp.exp2(s - jnp.max(s, axis=-1, keepdims=True))
            accs.append(_dot(p.astype(BF16), v_ref[:, v_blocks[hd] * LANES:(v_blocks[hd] + 1) * LANES]))
        o_ref[:, pr * LANES:(pr + 1) * LANES] = _normalized_pair(*accs).astype(BF16)


def _attention_ctx(a16_ctx, q_off, q_width, k_off, v_off, v_width, q_blocks, v_blocks, name):
    b, n_ctx, _ = a16_ctx.shape
    kw = 4 * LANES
    qc, kc, vc = q_off // q_width, k_off // kw, v_off // v_width
    return pl.pallas_call(
        functools.partial(_attn_ctx_kernel, q_blocks=q_blocks, v_blocks=v_blocks),
        grid=(b,),
        in_specs=[pl.BlockSpec((None, n_ctx, q_width), lambda i: (i, 0, qc)),
                  pl.BlockSpec((None, n_ctx, kw), lambda i: (i, 0, kc)),
                  pl.BlockSpec((None, n_ctx, v_width), lambda i: (i, 0, vc))],
        out_specs=pl.BlockSpec((None, n_ctx, 2 * LANES), lambda i: (i, 0, 0)),
        out_shape=jax.ShapeDtypeStruct((b, n_ctx, 2 * LANES), BF16),
        compiler_params=_cparams(1),
        name=name,
    )(a16_ctx, a16_ctx, a16_ctx)


def _log_sigmoid(x):
    return jnp.minimum(x, 0.0) - jnp.log(1.0 + jnp.exp(-jnp.abs(x)))


def _ret_kernel(ql_ref, kl_ref, vl_ref, gfl_ref, gbl_ref, qc_ref, kc_ref, vc_ref, gfc_ref, gbc_ref,
                dq_ref, dv_ref, dc_ref, ng_ref, ol_ref, oc_ref, acc_ref):
    c = RET_CHUNK
    n_lat, n_ctx = ql_ref.shape[0], qc_ref.shape[0]
    qk_w = RET_HEADS * RET_QK
    v_w = RET_HEADS * RET_V

    row_i = lax.broadcasted_iota(jnp.int32, (c, 1), 0).astype(F32)
    qk_head = lax.broadcasted_iota(jnp.int32, (1, qk_w), 1) // RET_QK
    v_head = lax.broadcasted_iota(jnp.int32, (1, v_w), 1) // RET_V
    qk_masks = [qk_head == hd for hd in range(RET_HEADS)]
    v_masks = [v_head == hd for hd in range(RET_HEADS)]
    state_rows = lax.broadcasted_iota(jnp.int32, (qk_w, 1), 0) // RET_QK
    bd_mask = (state_rows == v_head).astype(F32)
    ones_rows = lax.broadcasted_iota(jnp.int32, (v_w, 1), 0) // RET_V
    head_ones = (ones_rows == v_head).astype(BF16)
    ci = lax.broadcasted_iota(jnp.int32, (c, RET_HEADS * c), 0).astype(F32)
    cj = (lax.broadcasted_iota(jnp.int32, (c, RET_HEADS * c), 1) % c).astype(F32)

    def make_step(d):
        lg_q = _log_sigmoid(dq_ref[d:d + 1, :])
        lg_v = _log_sigmoid(dv_ref[d:d + 1, :])
        lg_c = _log_sigmoid(dc_ref[d:d + 1, :])
        if d == 0:
            diff = ci - cj
            q_dec = jnp.exp(lg_q * (row_i + 1.0))
            k_dec = jnp.exp(lg_q * (c - 1.0 - row_i))
        else:
            diff = cj - ci
            q_dec = jnp.exp(lg_q * (c - row_i))
            k_dec = jnp.exp(lg_q * row_i)
        decay = jnp.where(diff >= 0.0, jnp.exp(lg_c * jnp.maximum(diff, 0.0)), 0.0)
        chunk_dec = jnp.exp(lg_v * float(c))

        def step(refs, r0, acc_r0, state):
            q_ref, k_ref, v_ref = refs
            q = q_ref[pl.ds(r0, c), :]
            k = k_ref[pl.ds(r0, c), :]
            v = v_ref[pl.ds(r0, c), :]
            kb, vb = k.astype(BF16), v.astype(BF16)
            k_bd = jnp.concatenate([jnp.where(m, kb, 0) for m in qk_masks], axis=0)
            scores = _dot_nt(q.astype(BF16), k_bd) * decay
            v_bd = jnp.concatenate([jnp.where(m, vb, 0) for m in v_masks], axis=0)
            inner = _dot(scores.astype(BF16), v_bd)
            cross = _dot((q * q_dec).astype(BF16), state.astype(BF16))
            kv = _dot_tn((k * k_dec).astype(BF16), vb)
            new_state = state * chunk_dec + kv * bd_mask
            acc_ref[d, pl.ds(acc_r0, c), :] = inner + cross
            return new_state

        return step

    fwd, bwd = make_step(0), make_step(1)

    def scan(n, acc_base, fwd_refs, bwd_refs, states):
        def body(i, st):
            rf = pl.multiple_of(i * c, c)
            rb = pl.multiple_of((n - 1 - i) * c, c)
            return fwd(fwd_refs, rf, acc_base + rf, st[0]), bwd(bwd_refs, rb, acc_base + rb, st[1])
        return lax.fori_loop(0, n, body, states, unroll=min(n, 4))

    zero = jnp.zeros((qk_w, v_w), F32)
    states = scan(n_ctx // c, n_lat, (qc_ref, kc_ref, vc_ref), (qc_ref, kc_ref, vc_ref), (zero, zero))
    scan(n_lat // c, 0, (ql_ref, kl_ref, vl_ref), (ql_ref, kl_ref, vl_ref), states)

    def readout(o_ref, gate_refs, acc_base):
        rows = NORM_ROWS

        def body(i, carry):
            r0 = pl.multiple_of(i * rows, rows)
            y = None
            for d in range(2):
                o = acc_ref[d, pl.ds(acc_base + r0, rows), :]
                o2 = o * o
                o2_hi = o2.astype(BF16)
                o2_lo = (o2 - o2_hi.astype(F32)).astype(BF16)
                ss = _dot(o2_hi, head_ones) + _dot(o2_lo, head_ones)
                yd = o * lax.rsqrt(ss * (1.0 / RET_V) + EPS) * ng_ref[d:d + 1, :] * gate_refs[d][pl.ds(r0, rows), :]
                y = yd if y is None else y + yd
            o_ref[pl.ds(r0, rows), :] = y.astype(BF16)
            return carry

        n = o_ref.shape[0] // rows
        lax.fori_loop(0, n, body, 0, unroll=min(n, 2))

    readout(ol_ref, (gfl_ref, gbl_ref), 0)
    readout(oc_ref, (gfc_ref, gbc_ref), n_lat)


def _retention(a32_lat, a32_ctx, dec_q, dec_v, dec_c, norm_g):
    b, n_lat, _ = a32_lat.shape
    n_ctx = a32_ctx.shape[1]

    def seq_specs(n):
        spec = lambda off, w: pl.BlockSpec((None, n, w), lambda i: (i, 0, off // w))
        return [spec(A32_RQ, 128), spec(A32_RK, 128), spec(A32_RV, 256), spec(A32_GF, 256), spec(A32_GB, 256)]

    smalls = (dec_q, dec_v, dec_c, norm_g)
    out = lambda n: pl.BlockSpec((None, n, 256), lambda i: (i, 0, 0))
    return pl.pallas_call(
        _ret_kernel,
        grid=(b,),
        in_specs=seq_specs(n_lat) + seq_specs(n_ctx) + [_resident(a, 1) for a in smalls],
        out_specs=[out(n_lat), out(n_ctx)],
        out_shape=[jax.ShapeDtypeStruct((b, n_lat, 256), BF16), jax.ShapeDtypeStruct((b, n_ctx, 256), BF16)],
        scratch_shapes=[pltpu.VMEM((2, n_lat + n_ctx, 256), F32)],
        compiler_params=_cparams(1),
        name="retention",
    )(*([a32_lat] * 5), *([a32_ctx] * 5), *smalls)


def _out_mlp_kernel(x_ref, ya_ref, yb_ref, yc_ref, yd_ref, mod_ref, wo_ref, w1_ref, w2_ref, *rest, ff_chunk):
    o_ref = rest[-1]
    x = x_ref[...]
    gw = GROUP_W
    y = (_dot(ya_ref[...], wo_ref[0 * gw:1 * gw, :]) + _dot(yb_ref[...], wo_ref[1 * gw:2 * gw, :])
         + _dot(yc_ref[...], wo_ref[2 * gw:3 * gw, :]) + _dot(yd_ref[...], wo_ref[3 * gw:4 * gw, :]))
    x1 = x + mod_ref[2:3, :] * y
    h = (_rms(x1) * (1.0 + mod_ref[4:5, :]) + mod_ref[3:4, :]).astype(BF16)
    acc = jnp.zeros_like(x1)
    for j in range(w1_ref.shape[1] // ff_chunk):
        a = jnp.maximum(_dot(h, w1_ref[:, j * ff_chunk:(j + 1) * ff_chunk]), 0.0)
        acc = acc + _dot((a * a).astype(BF16), w2_ref[j * ff_chunk:(j + 1) * ff_chunk, :])
    x2 = x1 + mod_ref[5:6, :] * acc
    o_ref[...] = x2 if len(rest) == 1 else _rms(x2) * rest[0][...]


def _out_mlp(xseq, ys, mods, weights, tm, in_place, final_gain=None):
    b, n, d = xseq.shape
    per_batch = mods.shape[0] == b
    tok = lambda w: pl.BlockSpec((None, tm, w), lambda i, j: (i, j, 0))
    in_specs = [tok(d)] + [tok(GROUP_W)] * 4
    in_specs += [pl.BlockSpec((None, N_MOD, d), lambda i, j: (i if per_batch else 0, 0, 0))]
    in_specs += [_resident(w, 2) for w in weights]
    args = [xseq, *ys, mods, *weights]
    if final_gain is not None:
        in_specs.append(_resident(final_gain, 2))
        args.append(final_gain)
    return pl.pallas_call(
        functools.partial(_out_mlp_kernel, ff_chunk=1024),
        grid=(b, n // tm),
        in_specs=in_specs,
        out_specs=tok(d),
        out_shape=jax.ShapeDtypeStruct((b, n, d), F32),
        input_output_aliases={0: 0} if in_place else {},
        compiler_params=_cparams(2),
        name="out_mlp",
    )(*args)


def _rope_tables(n_lat):
    n = np.arange(n_lat)
    row, col = (n // GRID_W).astype(np.float64), (n % GRID_W).astype(np.float64)

    def table(groups):
        cos = np.ones((n_lat, LANES)); up = np.zeros((n_lat, LANES)); dn = np.zeros((n_lat, LANES))
        for rot_start, rot_dim in groups:
            half = rot_dim // 2
            q = half // 2
            freqs = ROPE_BASE ** (-np.arange(q, dtype=np.float64) / q)
            for axis, pos in enumerate((row, col)):
                ang = pos[:, None] * freqs[None, :]
                base = rot_start + axis * half
                cos[:, base:base + q] = np.cos(ang)
                cos[:, base + q:base + 2 * q] = np.cos(ang)
                up[:, base:base + q] = -np.sin(ang)
                dn[:, base + q:base + 2 * q] = np.sin(ang)
        return jnp.asarray(np.stack([cos, up, dn]).astype(np.float32))

    return table([(0, GQA_HEAD_DIM), (GQA_HEAD_DIM, GQA_HEAD_DIM)]), table([(MLA_NOPE, MLA_ROPE)])


def _layout_w_in(w_in):
    d = w_in.shape[0]
    kpe_src = OFF_CKV + MLA_KV_RANK
    w = w_in.astype(BF16)
    out = jnp.concatenate([w[:, :kpe_src], jnp.zeros((d, MLA_NOPE), BF16), w[:, kpe_src:kpe_src + MLA_ROPE],
                           jnp.zeros((d, LANES - MLA_NOPE - MLA_ROPE), BF16), w[:, kpe_src + MLA_ROPE:]], axis=1)
    assert out.shape[1] == N_IN
    return out


def _layout_w_uq(w_uq):
    r = w_uq.shape[0]
    w = w_uq.reshape(r, MLA_HEADS, MLA_NOPE + MLA_ROPE)
    w = jnp.pad(w, ((0, 0), (0, 0), (0, LANES - MLA_NOPE - MLA_ROPE)))
    return w.reshape(r, MLA_HEADS * LANES).astype(BF16)


def _layout_w_ukv(w_ukv):
    r = w_ukv.shape[0]
    w = w_ukv.reshape(r, MLA_HEADS, MLA_NOPE + MLA_V)
    k = jnp.pad(w[:, :, :MLA_NOPE], ((0, 0), (0, 0), (0, LANES - MLA_NOPE))).reshape(r, MLA_HEADS * LANES)
    v = jnp.pad(w[:, :, MLA_NOPE:], ((0, 0), (0, 0), (0, LANES - MLA_V))).reshape(r, MLA_HEADS * LANES)
    return jnp.concatenate([k, v], axis=1).astype(BF16)


def kernel(x, c, ctx, c_ctx, w_mod, b_mod, w_in, w_out, conv_dw, conv_b, conv_ln_g, conv_ln_b, conv_pw, mla_q_g, mla_kv_g, mla_uq, mla_ukv, gqa_q_g, gqa_k_g, ret_decay, ret_norm_g, mlp_w1, mlp_w2, final_g):
    b, n_lat, d = x.shape
    n_ctx = ctx.shape[1]
    depth = w_mod.shape[0]
    assert n_lat % LAT_TILE == 0 and n_lat % KEY_CHUNK == 0 and n_ctx % KEY_CHUNK == 0
    assert n_ctx % NORM_ROWS == 0 and n_lat % NORM_ROWS == 0 and NORM_ROWS % (2 * RET_CHUNK) == 0

    rows = ((b + 1 + SUBLANES - 1) // SUBLANES) * SUBLANES
    cond = jnp.zeros((rows, d), F32).at[:b].set(c).at[b].set(c_ctx)
    mod_all = _adaln(cond, w_mod, b_mod).reshape(depth, rows, N_MOD, d)
    tables = _rope_tables(n_lat)
    mla = (A16_MQ, 4 * LANES, A16_MK, A16_MV, 4 * LANES, (0, 1, 2, 3), (0, 1, 2, 3))
    gqa = (A16_GQ, 2 * LANES, A16_GK, A16_GV, 2 * LANES, (0, 0, 1, 1), (0, 0, 1, 1))

    cx = ctx
    for layer in range(depth):
        last = layer == depth - 1
        mods_lat, mods_ctx = mod_all[layer, :b], mod_all[layer, b:b + 1]
        w_proj = (_layout_w_in(w_in[layer]), _layout_w_uq(mla_uq[layer]), _layout_w_ukv(mla_ukv[layer]),
                  mla_q_g[layer][None, :], mla_kv_g[layer][None, :],
                  jnp.tile(gqa_q_g[layer], 2)[None, :], jnp.tile(gqa_k_g[layer], 2)[None, :])
        a16_lat, a32_lat = _in_proj(x, mods_lat, w_proj, tables, LAT_TILE)
        a16_ctx, a32_ctx = _in_proj(cx, mods_ctx, w_proj, None, n_ctx)
        ya_lat, ya_ctx = _conv(a32_lat, a32_ctx, conv_dw[layer], conv_b[layer][None, :], conv_ln_g[layer][None, :],
                               conv_ln_b[layer][None, :], conv_pw[layer].astype(BF16))
        yb_lat = _attention_lat(a16_lat, a16_ctx, *mla, "mla_attn")
        yc_lat = _attention_lat(a16_lat, a16_ctx, *gqa, "gqa_attn")
        dec = ret_decay[layer].astype(F32)
        yd_lat, yd_ctx = _retention(a32_lat, a32_ctx, jnp.repeat(dec, RET_QK, axis=1), jnp.repeat(dec, RET_V, axis=1),
                                    jnp.repeat(dec, RET_CHUNK, axis=1), ret_norm_g[layer].reshape(2, RET_HEADS * RET_V))
        w_mlp = (w_out[layer].astype(BF16), mlp_w1[layer].astype(BF16), mlp_w2[layer].astype(BF16))
        x = _out_mlp(x, (ya_lat, yb_lat, yc_lat, yd_lat), mods_lat, w_mlp, LAT_TILE, layer > 0,
                     final_gain=final_g[None, :] if last else None)
        if not last:
            yb_ctx = _attention_ctx(a16_ctx, *mla, "mla_attn_ctx")
            yc_ctx = _attention_ctx(a16_ctx, *gqa, "gqa_attn_ctx")
            cx = _out_mlp(cx, (ya_ctx, yb_ctx, yc_ctx, yd_ctx), mods_ctx, w_mlp, n_ctx, layer > 0)
    return x
```

```python
import functools

import numpy as np
import jax
import jax.numpy as jnp
from jax import lax
from jax.experimental import pallas as pl
from jax.experimental.pallas import tpu as pltpu

F32 = jnp.float32
BF16 = jnp.bfloat16

GRID_W = 64
N_MOD = 6
EPS = 1e-6
ROPE_BASE = 10000.0
GROUP_W = 256
CONV_WIDTH = 31
CONV_PAD = 16
MLA_HEADS = 4
MLA_NOPE = 64
MLA_ROPE = 32
MLA_V = 64
MLA_Q_RANK = 256
MLA_KV_RANK = 128
GQA_HEADS = 4
GQA_KV_HEADS = 2
GQA_HEAD_DIM = 64
RET_HEADS = 4
RET_QK = 32
RET_V = 64
RET_CHUNK = 128
LANES = 128
SUBLANES = 8
LOG2_E = 1.4426950408889634
LAT_TILE = 512
KEY_CHUNK = 256
NORM_ROWS = 256
VMEM_LIMIT = 56 * 1024 * 1024

OFF_A, OFF_CQ, OFF_CKV, OFF_GQ, OFF_GK = 0, 512, 768, 1024, 1280
OFF_RQ, OFF_RV, OFF_GF, OFF_GB, N_IN = 1536, 1792, 2048, 2304, 2560
A16_MQ, A16_MK, A16_MV, A16_GK, A16_GQ, A16_GV, A16_W = 0, 512, 1024, 1536, 2048, 2304, 2560
A32_Y, A32_RQ, A32_RK, A32_RV, A32_GF, A32_GB, A32_W = 0, 256, 384, 512, 768, 1024, 1280


def _cparams(n_axes):
    return pltpu.CompilerParams(dimension_semantics=("arbitrary",) * n_axes,
                                vmem_limit_bytes=VMEM_LIMIT)


def _resident(a, layer=None):
    if layer is None:
        return pl.BlockSpec(a.shape, lambda *_: (0,) * a.ndim, pipeline_mode=pl.Buffered(1))
    return pl.BlockSpec((None,) + a.shape[1:], lambda *_: (layer,) + (0,) * (a.ndim - 1),
                        pipeline_mode=pl.Buffered(1))


def _rms(x):
    return x * lax.rsqrt(jnp.mean(x * x, axis=-1, keepdims=True) + EPS)


def _sigmoid(x):
    return 1.0 / (1.0 + jnp.exp(-x))


def _silu(x):
    return x * _sigmoid(x)


def _dot(a, b):
    return jnp.dot(a, b, preferred_element_type=F32)


def _dot_nt(a, b):
    return lax.dot_general(a, b, (((1,), (1,)), ((), ())), preferred_element_type=F32)


def _dot_tn(a, b):
    return lax.dot_general(a, b, (((0,), (0,)), ((), ())), preferred_element_type=F32)


def _rope(x, cos, sin_up, sin_dn, shift):
    n = x.shape[-1]
    return x * cos + pltpu.roll(x, n - shift, 1) * sin_up + pltpu.roll(x, shift, 1) * sin_dn


def _adaln_kernel(cond_ref, w_ref, b_ref, o_ref):
    cond = cond_ref[...]
    o_ref[...] = _dot(_silu(cond).astype(BF16), w_ref[...].astype(BF16)) + b_ref[...]


def _adaln(cond, w_mod, b_mod):
    depth, d, n = w_mod.shape
    r = cond.shape[0]
    tn = 1536
    return pl.pallas_call(
        _adaln_kernel,
        grid=(depth, n // tn),
        in_specs=[pl.BlockSpec((r, d), lambda l, j: (0, 0)),
                  pl.BlockSpec((None, d, tn), lambda l, j: (l, 0, j)),
                  pl.BlockSpec((None, 1, tn), lambda l, j: (l, 0, j))],
        out_specs=pl.BlockSpec((None, r, tn), lambda l, j: (l, 0, j)),
        out_shape=jax.ShapeDtypeStruct((depth, r, n), F32),
        compiler_params=_cparams(2),
        name="adaln",
    )(cond, w_mod, b_mod.reshape(depth, 1, n))


def _in_proj_kernel(x_ref, mod_ref, w_in_ref, w_uq_ref, w_ukv_ref, g_mq_ref, g_mkv_ref, g_gq_ref, g_gk_ref,
                    *rest, rotary, sub_tiles):
    a16_ref, a32_ref = rest[-2:]
    lo = lax.broadcasted_iota(jnp.int32, (1, LANES), 1) < (LANES // 2)
    q_scale = (MLA_NOPE + MLA_ROPE) ** -0.5 * LOG2_E
    n_rows = x_ref.shape[0] // sub_tiles

    for st in range(sub_tiles):
        rows = slice(st * n_rows, (st + 1) * n_rows)
        h = (_rms(x_ref[rows, :]) * (1.0 + mod_ref[1:2, :]) + mod_ref[0:1, :]).astype(BF16)

        def proj(off, width, h=h):
            return _dot(h, w_in_ref[:, off:off + width])

        def put16(off, val, rows=rows):
            a16_ref[rows, off:off + val.shape[1]] = val.astype(BF16)

        def put32(off, val, rows=rows):
            a32_ref[rows, off:off + val.shape[1]] = val

        if rotary:
            tg_ref, tm_ref = rest[:2]
            rope_g = lambda v, rows=rows: _rope(v, tg_ref[0, rows, :], tg_ref[1, rows, :], tg_ref[2, rows, :],
                                                GQA_HEAD_DIM // 4)
            rope_m = lambda v, rows=rows: _rope(v, tm_ref[0, rows, :], tm_ref[1, rows, :], tm_ref[2, rows, :],
                                                MLA_ROPE // 4)
        else:
            rope_g = rope_m = lambda v: v

        def pair_norm_rope(xp, gain, rope_g=rope_g):
            x2 = xp * xp
            s_lo = jnp.sum(jnp.where(lo, x2, 0.0), axis=-1, keepdims=True)
            s_hi = jnp.sum(jnp.where(lo, 0.0, x2), axis=-1, keepdims=True)
            r = jnp.where(lo, lax.rsqrt(s_lo * (1.0 / GQA_HEAD_DIM) + EPS),
                          lax.rsqrt(s_hi * (1.0 / GQA_HEAD_DIM) + EPS))
            return rope_g(xp * r * gain)

        cq = (_rms(proj(OFF_CQ, MLA_Q_RANK)) * g_mq_ref[...]).astype(BF16)
        ckv_kpe = proj(OFF_CKV, 2 * LANES)
        ckv = (_rms(ckv_kpe[:, :LANES]) * g_mkv_ref[...]).astype(BF16)
        gq = proj(OFF_GQ, GQA_HEADS * GQA_HEAD_DIM)
        gkv = proj(OFF_GK, 2 * LANES)

        a = proj(OFF_A, 2 * GROUP_W)
        put32(A32_Y, a[:, :GROUP_W] * _sigmoid(a[:, GROUP_W:]))
        rqk = proj(OFF_RQ, 2 * LANES)
        put32(A32_RQ, rqk[:, :LANES])
        put32(A32_RK, rqk[:, LANES:] * RET_QK ** -0.5)
        put32(A32_RV, proj(OFF_RV, RET_HEADS * RET_V))
        put32(A32_GF, _silu(proj(OFF_GF, GROUP_W)))
        put32(A32_GB, _silu(proj(OFF_GB, GROUP_W)))

        q = _dot(cq, w_uq_ref[...])
        for hd in range(MLA_HEADS):
            put16(A16_MQ + hd * LANES, rope_m(q[:, hd * LANES:(hd + 1) * LANES]) * q_scale)
        kv = _dot(ckv, w_ukv_ref[...])
        kpe = rope_m(ckv_kpe[:, LANES:])
        for hd in range(MLA_HEADS):
            put16(A16_MK + hd * LANES, kv[:, hd * LANES:(hd + 1) * LANES] + kpe)
            put16(A16_MV + hd * LANES,
                  jnp.where(lo, kv[:, (MLA_HEADS + hd) * LANES:(MLA_HEADS + hd + 1) * LANES], 1.0))

        for pr in range(2):
            qp = pair_norm_rope(gq[:, pr * LANES:(pr + 1) * LANES], g_gq_ref[...])
            put16(A16_GQ + pr * LANES, qp * (GQA_HEAD_DIM ** -0.5 * LOG2_E))
        kp = pair_norm_rope(gkv[:, :LANES], g_gk_ref[...])
        kp_sw = pltpu.roll(kp, LANES // 2, 1)
        put16(A16_GK + 0 * LANES, jnp.where(lo, kp, 0.0))
        put16(A16_GK + 1 * LANES, jnp.where(lo, 0.0, kp_sw))
        put16(A16_GK + 2 * LANES, jnp.where(lo, kp_sw, 0.0))
        put16(A16_GK + 3 * LANES, jnp.where(lo, 0.0, kp))
        vp = gkv[:, LANES:]
        put16(A16_GV + 0 * LANES, jnp.where(lo, vp, 1.0))
        put16(A16_GV + 1 * LANES, jnp.where(lo, pltpu.roll(vp, LANES // 2, 1), 1.0))


def _in_proj(xseq, mods, weights, layer, tables, tm):
    b, n, d = xseq.shape
    per_batch = mods.shape[0] == b
    tok = lambda w: pl.BlockSpec((None, tm, w), lambda i, j: (i, j, 0))
    in_specs = [tok(d), pl.BlockSpec((None, N_MOD, d), lambda i, j: (i if per_batch else 0, 0, 0))]
    in_specs += [_resident(w, layer) for w in weights]
    args = [xseq, mods, *weights]
    if tables is not None:
        in_specs += [pl.BlockSpec((3, tm, LANES), lambda i, j: (0, j, 0))] * 2
        args += list(tables)
    return pl.pallas_call(
        functools.partial(_in_proj_kernel, rotary=tables is not None, sub_tiles=tm // 256),
        grid=(b, n // tm),
        in_specs=in_specs,
        out_specs=[tok(A16_W), tok(A32_W)],
        out_shape=[jax.ShapeDtypeStruct((b, n, A16_W), BF16), jax.ShapeDtypeStruct((b, n, A32_W), F32)],
        compiler_params=_cparams(2),
        name="in_proj",
    )(*args)


def _conv_kernel(yl_ref, yc_ref, dw_ref, b_ref, lg_ref, lb_ref, pw_ref, ol_ref, oc_ref, sh_ref, *, rows):
    zeros = jnp.zeros((CONV_PAD, GROUP_W), F32)
    bias = b_ref[...]
    first_off = CONV_PAD - CONV_WIDTH // 2
    tail = ((first_off + CONV_WIDTH - 1) // SUBLANES) * SUBLANES

    def sequence(y_ref, o_ref):
        length = y_ref.shape[0]
        sh_ref[0, 0:CONV_PAD, :] = zeros
        sh_ref[0, CONV_PAD:CONV_PAD + length, :] = y_ref[...]
        sh_ref[0, CONV_PAD + length:2 * CONV_PAD + length, :] = zeros

        def shift_block(c, carry):
            base = pl.multiple_of(c * rows, rows)
            win = sh_ref[0, pl.ds(base, rows + SUBLANES), :]
            for r in range(1, SUBLANES):
                sh_ref[r, pl.ds(base, rows), :] = win[r:r + rows, :]
            return carry

        lax.fori_loop(0, length // rows, shift_block, 0)
        win = sh_ref[0, length:length + tail + SUBLANES, :]
        for r in range(1, SUBLANES):
            sh_ref[r, length:length + tail, :] = win[r:r + tail, :]

        def taps(c, carry):
            base = pl.multiple_of(c * rows, rows)
            acc = jnp.zeros((rows, GROUP_W), F32) + bias
            for k in range(CONV_WIDTH):
                off = first_off + k
                slab = sh_ref[off % SUBLANES, pl.ds(base + (off // SUBLANES) * SUBLANES, rows), :]
                acc = acc + slab * dw_ref[k:k + 1, :]
            sh_ref[0, pl.ds(base, rows), :] = acc
            return carry

        lax.fori_loop(0, length // rows, taps, 0)

        def finish(c, carry):
            base = pl.multiple_of(c * NORM_ROWS, NORM_ROWS)
            acc = sh_ref[0, pl.ds(base, NORM_ROWS), :]
            mu = jnp.mean(acc, axis=-1, keepdims=True)
            cen = acc - mu
            var = jnp.mean(cen * cen, axis=-1, keepdims=True)
            z = _silu(cen * lax.rsqrt(var + EPS) * lg_ref[...] + lb_ref[...])
            o_ref[pl.ds(base, NORM_ROWS), :] = _dot(z.astype(BF16), pw_ref[...]).astype(BF16)
            return carry

        n_norm = length // NORM_ROWS
        lax.fori_loop(0, n_norm, finish, 0, unroll=min(n_norm, 4))

    sequence(yc_ref, oc_ref)
    sequence(yl_ref, ol_ref)


def _conv(a32_lat, a32_ctx, weights, layer):
    b, n_lat, _ = a32_lat.shape
    n_ctx = a32_ctx.shape[1]
    col = A32_Y // GROUP_W
    seq = lambda n: pl.BlockSpec((None, n, GROUP_W), lambda i: (i, 0, col))
    out = lambda n: pl.BlockSpec((None, n, GROUP_W), lambda i: (i, 0, 0))
    return pl.pallas_call(
        functools.partial(_conv_kernel, rows=64),
        grid=(b,),
        in_specs=[seq(n_lat), seq(n_ctx)] + [_resident(w, layer) for w in weights],
        out_specs=[out(n_lat), out(n_ctx)],
        out_shape=[jax.ShapeDtypeStruct((b, n_lat, GROUP_W), BF16), jax.ShapeDtypeStruct((b, n_ctx, GROUP_W), BF16)],
        scratch_shapes=[pltpu.VMEM((SUBLANES, max(n_lat, n_ctx) + 2 * CONV_PAD, GROUP_W), F32)],
        compiler_params=_cparams(1),
        name="conv",
    )(a32_lat, a32_ctx, *weights)


def _normalized_pair(acc_even, acc_odd):
    half = LANES // 2
    lo = lax.broadcasted_iota(jnp.int32, (1, LANES), 1) < half
    return jnp.where(lo, acc_even / pltpu.roll(acc_even, half, 1), pltpu.roll(acc_odd, half, 1) / acc_odd)


def _attn_lat_kernel(q_ref, qn_ref, kl_ref, kc_ref, vl_ref, vc_ref, o_ref, s_ref, m_ref, mb_ref, acc_ref,
                     *, q_blocks, v_blocks):
    tq = q_ref.shape[0]
    tk = KEY_CHUNK
    chunks = [(kl_ref, vl_ref, c * tk) for c in range(kl_ref.shape[0] // tk)]
    chunks += [(kc_ref, vc_ref, c * tk) for c in range(kc_ref.shape[0] // tk)]

    def scores_chunk(hd, ci, ref):
        slot = hd % 2
        k_ref, _, r0 = chunks[ci]
        q = ref[:, q_blocks[hd] * LANES:(q_blocks[hd] + 1) * LANES]
        s = _dot_nt(q, k_ref[r0:r0 + tk, hd * LANES:(hd + 1) * LANES])
        s_ref[slot, :, ci * tk:(ci + 1) * tk] = s
        mx = s[:, 0:LANES]
        for i in range(1, tk // LANES):
            mx = jnp.maximum(mx, s[:, i * LANES:(i + 1) * LANES])
        m_ref[slot] = mx if ci == 0 else jnp.maximum(m_ref[slot], mx)

    def scores_finish(hd):
        slot = hd % 2
        mb_ref[slot] = jnp.broadcast_to(jnp.max(m_ref[slot], axis=-1, keepdims=True), (tq, LANES))

    def values_chunk(hd, ci):
        slot = hd % 2
        _, v_ref, r0 = chunks[ci]
        mb = mb_ref[slot]
        p = jnp.concatenate(
            [jnp.exp2(s_ref[slot, :, ci * tk + i * LANES:ci * tk + (i + 1) * LANES] - mb).astype(BF16)
             for i in range(tk // LANES)], axis=1)
        pv = _dot(p, v_ref[r0:r0 + tk, v_blocks[hd] * LANES:(v_blocks[hd] + 1) * LANES])
        acc_ref[slot] = pv if ci == 0 else acc_ref[slot] + pv

    n = len(chunks)

    def scores_ahead(unit, ci):
        scores_chunk(unit % 4, ci, q_ref if unit < 4 else qn_ref)

    @pl.when(pl.program_id(1) == 0)
    def _():
        for ci in range(n):
            scores_chunk(0, ci, q_ref)
        scores_finish(0)
        scores_chunk(1, 0, q_ref)

    for hd in range(4):
        for ci in range(n):
            values_chunk(hd, ci)
            if ci + 1 < n:
                scores_ahead(hd + 1, ci + 1)
            else:
                scores_ahead(hd + 2, 0)
            if ci + 2 == n:
                scores_finish((hd + 1) % 4)
        if hd % 2 == 1:
            pr = hd // 2
            o_ref[:, pr * LANES:(pr + 1) * LANES] = _normalized_pair(acc_ref[0], acc_ref[1]).astype(BF16)


def _attention_lat(a16_lat, a16_ctx, q_off, q_width, k_off, v_off, v_width, q_blocks, v_blocks, name):
    b, n_lat, _ = a16_lat.shape
    n_ctx = a16_ctx.shape[1]
    tq = LAT_TILE
    n_tiles = n_lat // tq
    kw = 4 * LANES
    qc, kc, vc = q_off // q_width, k_off // kw, v_off // v_width
    return pl.pallas_call(
        functools.partial(_attn_lat_kernel, q_blocks=q_blocks, v_blocks=v_blocks),
        grid=(b, n_tiles),
        in_specs=[pl.BlockSpec((None, tq, q_width), lambda i, j: (i, j, qc)),
                  pl.BlockSpec((None, tq, q_width), lambda i, j: (i, jnp.minimum(j + 1, n_tiles - 1), qc)),
                  pl.BlockSpec((None, n_lat, kw), lambda i, j: (i, 0, kc)),
                  pl.BlockSpec((None, n_ctx, kw), lambda i, j: (i, 0, kc)),
                  pl.BlockSpec((None, n_lat, v_width), lambda i, j: (i, 0, vc)),
                  pl.BlockSpec((None, n_ctx, v_width), lambda i, j: (i, 0, vc))],
        out_specs=pl.BlockSpec((None, tq, 2 * LANES), lambda i, j: (i, j, 0)),
        out_shape=jax.ShapeDtypeStruct((b, n_lat, 2 * LANES), BF16),
        scratch_shapes=[pltpu.VMEM((2, tq, n_lat + n_ctx), F32), pltpu.VMEM((2, tq, LANES), F32),
                        pltpu.VMEM((2, tq, LANES), F32), pltpu.VMEM((2, tq, LANES), F32)],
        compiler_params=_cparams(2),
        name=name,
    )(a16_lat, a16_lat, a16_lat, a16_ctx, a16_lat, a16_ctx)


def _attn_ctx_kernel(q_ref, k_ref, v_ref, o_ref, *, q_blocks, v_blocks):
    for pr in range(2):
        accs = []
        for half in range(2):
            hd = 2 * pr + half
            q = q_ref[:, q_blocks[hd] * LANES:(q_blocks[hd] + 1) * LANES]
            s = _dot_nt(q, k_ref[:, hd * LANES:(hd + 1) * LANES])
            p = jnp.exp2(s - jnp.max(s, axis=-1, keepdims=True))
            accs.append(_dot(p.astype(BF16), v_ref[:, v_blocks[hd] * LANES:(v_blocks[hd] + 1) * LANES]))
        o_ref[:, pr * LANES:(pr + 1) * LANES] = _normalized_pair(*accs).astype(BF16)


def _attention_ctx(a16_ctx, q_off, q_width, k_off, v_off, v_width, q_blocks, v_blocks, name):
    b, n_ctx, _ = a16_ctx.shape
    kw = 4 * LANES
    qc, kc, vc = q_off // q_width, k_off // kw, v_off // v_width
    return pl.pallas_call(
        functools.partial(_attn_ctx_kernel, q_blocks=q_blocks, v_blocks=v_blocks),
        grid=(b,),
        in_specs=[pl.BlockSpec((None, n_ctx, q_width), lambda i: (i, 0, qc)),
                  pl.BlockSpec((None, n_ctx, kw), lambda i: (i, 0, kc)),
                  pl.BlockSpec((None, n_ctx, v_width), lambda i: (i, 0, vc))],
        out_specs=pl.BlockSpec((None, n_ctx, 2 * LANES), lambda i: (i, 0, 0)),
        out_shape=jax.ShapeDtypeStruct((b, n_ctx, 2 * LANES), BF16),
        compiler_params=_cparams(1),
        name=name,
    )(a16_ctx, a16_ctx, a16_ctx)


def _log_sigmoid(x):
    return jnp.minimum(x, 0.0) - jnp.log(1.0 + jnp.exp(-jnp.abs(x)))


def _ret_kernel(ql_ref, kl_ref, vl_ref, gfl_ref, gbl_ref, qc_ref, kc_ref, vc_ref, gfc_ref, gbc_ref,
                dq_ref, dv_ref, dc_ref, ng_ref, ol_ref, oc_ref, acc_ref):
    c = RET_CHUNK
    n_lat, n_ctx = ql_ref.shape[0], qc_ref.shape[0]
    qk_w = RET_HEADS * RET_QK
    v_w = RET_HEADS * RET_V

    row_i = lax.broadcasted_iota(jnp.int32, (c, 1), 0).astype(F32)
    qk_head = lax.broadcasted_iota(jnp.int32, (1, qk_w), 1) // RET_QK
    v_head = lax.broadcasted_iota(jnp.int32, (1, v_w), 1) // RET_V
    qk_masks = [qk_head == hd for hd in range(RET_HEADS)]
    v_masks = [v_head == hd for hd in range(RET_HEADS)]
    state_rows = lax.broadcasted_iota(jnp.int32, (qk_w, 1), 0) // RET_QK
    bd_mask = (state_rows == v_head).astype(F32)
    ones_rows = lax.broadcasted_iota(jnp.int32, (v_w, 1), 0) // RET_V
    head_ones = (ones_rows == v_head).astype(BF16)
    ci = lax.broadcasted_iota(jnp.int32, (c, RET_HEADS * c), 0).astype(F32)
    cj = (lax.broadcasted_iota(jnp.int32, (c, RET_HEADS * c), 1) % c).astype(F32)

    def make_step(d):
        lg_q = _log_sigmoid(dq_ref[d:d + 1, :])
        lg_v = _log_sigmoid(dv_ref[d:d + 1, :])
        lg_c = _log_sigmoid(dc_ref[d:d + 1, :])
        if d == 0:
            diff = ci - cj
            q_dec = jnp.exp(lg_q * (row_i + 1.0))
            k_dec = jnp.exp(lg_q * (c - 1.0 - row_i))
        else:
            diff = cj - ci
            q_dec = jnp.exp(lg_q * (c - row_i))
            k_dec = jnp.exp(lg_q * row_i)
        decay = jnp.where(diff >= 0.0, jnp.exp(lg_c * jnp.maximum(diff, 0.0)), 0.0)
        chunk_dec = jnp.exp(lg_v * float(c))

        def step(refs, r0, acc_r0, state):
            q_ref, k_ref, v_ref = refs
            q = q_ref[pl.ds(r0, c), :]
            k = k_ref[pl.ds(r0, c), :]
            v = v_ref[pl.ds(r0, c), :]
            kb, vb = k.astype(BF16), v.astype(BF16)
            k_bd = jnp.concatenate([jnp.where(m, kb, 0) for m in qk_masks], axis=0)
            scores = _dot_nt(q.astype(BF16), k_bd) * decay
            v_bd = jnp.concatenate([jnp.where(m, vb, 0) for m in v_masks], axis=0)
            inner = _dot(scores.astype(BF16), v_bd)
            cross = _dot((q * q_dec).astype(BF16), state.astype(BF16))
            kv = _dot_tn((k * k_dec).astype(BF16), vb)
            new_state = state * chunk_dec + kv * bd_mask
            acc_ref[d, pl.ds(acc_r0, c), :] = inner + cross
            return new_state

        return step

    fwd, bwd = make_step(0), make_step(1)

    def scan(n, acc_base, fwd_refs, bwd_refs, states):
        def body(i, st):
            rf = pl.multiple_of(i * c, c)
            rb = pl.multiple_of((n - 1 - i) * c, c)
            return fwd(fwd_refs, rf, acc_base + rf, st[0]), bwd(bwd_refs, rb, acc_base + rb, st[1])
        return lax.fori_loop(0, n, body, states, unroll=min(n, 4))

    zero = jnp.zeros((qk_w, v_w), F32)
    states = scan(n_ctx // c, n_lat, (qc_ref, kc_ref, vc_ref), (qc_ref, kc_ref, vc_ref), (zero, zero))
    scan(n_lat // c, 0, (ql_ref, kl_ref, vl_ref), (ql_ref, kl_ref, vl_ref), states)

    def readout(o_ref, gate_refs, acc_base):
        rows = NORM_ROWS

        def body(i, carry):
            r0 = pl.multiple_of(i * rows, rows)
            y = None
            for d in range(2):
                o = acc_ref[d, pl.ds(acc_base + r0, rows), :]
                o2 = o * o
                o2_hi = o2.astype(BF16)
                o2_lo = (o2 - o2_hi.astype(F32)).astype(BF16)
                ss = _dot(o2_hi, head_ones) + _dot(o2_lo, head_ones)
                yd = o * lax.rsqrt(ss * (1.0 / RET_V) + EPS) * ng_ref[d:d + 1, :] * gate_refs[d][pl.ds(r0, rows), :]
                y = yd if y is None else y + yd
            o_ref[pl.ds(r0, rows), :] = y.astype(BF16)
            return carry

        n = o_ref.shape[0] // rows
        lax.fori_loop(0, n, body, 0, unroll=min(n, 2))

    readout(ol_ref, (gfl_ref, gbl_ref), 0)
    readout(oc_ref, (gfc_ref, gbc_ref), n_lat)


def _retention(a32_lat, a32_ctx, smalls, layer):
    b, n_lat, _ = a32_lat.shape
    n_ctx = a32_ctx.shape[1]

    def seq_specs(n):
        spec = lambda off, w: pl.BlockSpec((None, n, w), lambda i: (i, 0, off // w))
        return [spec(A32_RQ, 128), spec(A32_RK, 128), spec(A32_RV, 256), spec(A32_GF, 256), spec(A32_GB, 256)]

    out = lambda n: pl.BlockSpec((None, n, 256), lambda i: (i, 0, 0))
    return pl.pallas_call(
        _ret_kernel,
        grid=(b,),
        in_specs=seq_specs(n_lat) + seq_specs(n_ctx) + [_resident(a, layer) for a in smalls],
        out_specs=[out(n_lat), out(n_ctx)],
        out_shape=[jax.ShapeDtypeStruct((b, n_lat, 256), BF16), jax.ShapeDtypeStruct((b, n_ctx, 256), BF16)],
        scratch_shapes=[pltpu.VMEM((2, n_lat + n_ctx, 256), F32)],
        compiler_params=_cparams(1),
        name="retention",
    )(*([a32_lat] * 5), *([a32_ctx] * 5), *smalls)


def _out_mlp_kernel(x_ref, ya_ref, yb_ref, yc_ref, yd_ref, mod_ref, wo_ref, w1_ref, w2_ref, *rest, ff_chunk):
    o_ref = rest[-1]
    x = x_ref[...]
    gw = GROUP_W
    y = (_dot(ya_ref[...], wo_ref[0 * gw:1 * gw, :]) + _dot(yb_ref[...], wo_ref[1 * gw:2 * gw, :])
         + _dot(yc_ref[...], wo_ref[2 * gw:3 * gw, :]) + _dot(yd_ref[...], wo_ref[3 * gw:4 * gw, :]))
    x1 = x + mod_ref[2:3, :] * y
    h = (_rms(x1) * (1.0 + mod_ref[4:5, :]) + mod_ref[3:4, :]).astype(BF16)
    acc = jnp.zeros_like(x1)
    for j in range(w1_ref.shape[1] // ff_chunk):
        a = jnp.maximum(_dot(h, w1_ref[:, j * ff_chunk:(j + 1) * ff_chunk]), 0.0)
        acc = acc + _dot((a * a).astype(BF16), w2_ref[j * ff_chunk:(j + 1) * ff_chunk, :])
    x2 = x1 + mod_ref[5:6, :] * acc
    o_ref[...] = x2 if len(rest) == 1 else _rms(x2) * rest[0][...]


def _out_mlp(xseq, ys, mods, weights, layer, tm, in_place, final_gain=None):
    b, n, d = xseq.shape
    per_batch = mods.shape[0] == b
    tok = lambda w: pl.BlockSpec((None, tm, w), lambda i, j: (i, j, 0))
    in_specs = [tok(d)] + [tok(GROUP_W)] * 4
    in_specs += [pl.BlockSpec((None, N_MOD, d), lambda i, j: (i if per_batch else 0, 0, 0))]
    in_specs += [_resident(w, layer) for w in weights]
    args = [xseq, *ys, mods, *weights]
    if final_gain is not None:
        in_specs.append(_resident(final_gain))
        args.append(final_gain)
    return pl.pallas_call(
        functools.partial(_out_mlp_kernel, ff_chunk=1024),
        grid=(b, n // tm),
        in_specs=in_specs,
        out_specs=tok(d),
        out_shape=jax.ShapeDtypeStruct((b, n, d), F32),
        input_output_aliases={0: 0} if in_place else {},
        compiler_params=_cparams(2),
        name="out_mlp",
    )(*args)


def _rope_tables(n_lat):
    n = np.arange(n_lat)
    row, col = (n // GRID_W).astype(np.float64), (n % GRID_W).astype(np.float64)

    def table(groups):
        cos = np.ones((n_lat, LANES)); up = np.zeros((n_lat, LANES)); dn = np.zeros((n_lat, LANES))
        for rot_start, rot_dim in groups:
            half = rot_dim // 2
            q = half // 2
            freqs = ROPE_BASE ** (-np.arange(q, dtype=np.float64) / q)
            for axis, pos in enumerate((row, col)):
                ang = pos[:, None] * freqs[None, :]
                base = rot_start + axis * half
                cos[:, base:base + q] = np.cos(ang)
                cos[:, base + q:base + 2 * q] = np.cos(ang)
                up[:, base:base + q] = -np.sin(ang)
                dn[:, base + q:base + 2 * q] = np.sin(ang)
        return jnp.asarray(np.stack([cos, up, dn]).astype(np.float32))

    return table([(0, GQA_HEAD_DIM), (GQA_HEAD_DIM, GQA_HEAD_DIM)]), table([(MLA_NOPE, MLA_ROPE)])


def _layout_w_in(w_in):
    lead = w_in.shape[:-1]
    kpe_src = OFF_CKV + MLA_KV_RANK
    w = w_in.astype(BF16)
    out = jnp.concatenate([w[..., :kpe_src], jnp.zeros(lead + (MLA_NOPE,), BF16),
                           w[..., kpe_src:kpe_src + MLA_ROPE],
                           jnp.zeros(lead + (LANES - MLA_NOPE - MLA_ROPE,), BF16),
                           w[..., kpe_src + MLA_ROPE:]], axis=-1)
    assert out.shape[-1] == N_IN
    return out


def _layout_w_uq(w_uq):
    lead = w_uq.shape[:-1]
    w = w_uq.reshape(lead + (MLA_HEADS, MLA_NOPE + MLA_ROPE))
    w = jnp.pad(w, ((0, 0),) * (w.ndim - 1) + ((0, LANES - MLA_NOPE - MLA_ROPE),))
    return w.reshape(lead + (MLA_HEADS * LANES,)).astype(BF16)


def _layout_w_ukv(w_ukv):
    lead = w_ukv.shape[:-1]
    w = w_ukv.reshape(lead + (MLA_HEADS, MLA_NOPE + MLA_V))
    pad = lambda a: jnp.pad(a, ((0, 0),) * (a.ndim - 1) + ((0, LANES - a.shape[-1]),)).reshape(
        lead + (MLA_HEADS * LANES,))
    return jnp.concatenate([pad(w[..., :MLA_NOPE]), pad(w[..., MLA_NOPE:])], axis=-1).astype(BF16)


def kernel(x, c, ctx, c_ctx, w_mod, b_mod, w_in, w_out, conv_dw, conv_b, conv_ln_g, conv_ln_b, conv_pw, mla_q_g, mla_kv_g, mla_uq, mla_ukv, gqa_q_g, gqa_k_g, ret_decay, ret_norm_g, mlp_w1, mlp_w2, final_g):
    b, n_lat, d = x.shape
    n_ctx = ctx.shape[1]
    depth = w_mod.shape[0]
    assert n_lat % LAT_TILE == 0 and n_lat % KEY_CHUNK == 0 and n_ctx % KEY_CHUNK == 0
    assert n_ctx % NORM_ROWS == 0 and n_lat % NORM_ROWS == 0 and NORM_ROWS % (2 * RET_CHUNK) == 0

    rows = ((b + 1 + SUBLANES - 1) // SUBLANES) * SUBLANES
    cond = jnp.zeros((rows, d), F32).at[:b].set(c).at[b].set(c_ctx)
    mod_all = _adaln(cond, w_mod, b_mod).reshape(depth, rows, N_MOD, d)
    tables = _rope_tables(n_lat)
    mla = (A16_MQ, 4 * LANES, A16_MK, A16_MV, 4 * LANES, (0, 1, 2, 3), (0, 1, 2, 3))
    gqa = (A16_GQ, 2 * LANES, A16_GK, A16_GV, 2 * LANES, (0, 0, 1, 1), (0, 0, 1, 1))

    row = lambda a: a[:, None, :]
    w_proj = (_layout_w_in(w_in), _layout_w_uq(mla_uq), _layout_w_ukv(mla_ukv), row(mla_q_g), row(mla_kv_g),
              row(jnp.tile(gqa_q_g, (1, 2))), row(jnp.tile(gqa_k_g, (1, 2))))
    w_conv = (conv_dw, row(conv_b), row(conv_ln_g), row(conv_ln_b), conv_pw.astype(BF16))
    dec = ret_decay.astype(F32)
    w_ret = (jnp.repeat(dec, RET_QK, axis=2), jnp.repeat(dec, RET_V, axis=2), jnp.repeat(dec, RET_CHUNK, axis=2),
             ret_norm_g.reshape(depth, 2, RET_HEADS * RET_V))
    w_mlp = (w_out.astype(BF16), mlp_w1.astype(BF16), mlp_w2.astype(BF16))

    cx = ctx
    for layer in range(depth):
        last = layer == depth - 1
        mods_lat, mods_ctx = mod_all[layer, :b], mod_all[layer, b:b + 1]
        a16_lat, a32_lat = _in_proj(x, mods_lat, w_proj, layer, tables, LAT_TILE)
        a16_ctx, a32_ctx = _in_proj(cx, mods_ctx, w_proj, layer, None, n_ctx)
        ya_lat, ya_ctx = _conv(a32_lat, a32_ctx, w_conv, layer)
        yb_lat = _attention_lat(a16_lat, a16_ctx, *mla, "mla_attn")
        yc_lat = _attention_lat(a16_lat, a16_ctx, *gqa, "gqa_attn")
        yd_lat, yd_ctx = _retention(a32_lat, a32_ctx, w_ret, layer)
        x = _out_mlp(x, (ya_lat, yb_lat, yc_lat, yd_lat), mods_lat, w_mlp, layer, LAT_TILE, layer > 0,
                     final_gain=final_g[None, :] if last else None)
        if not last:
            yb_ctx = _attention_ctx(a16_ctx, *mla, "mla_attn_ctx")
            yc_ctx = _attention_ctx(a16_ctx, *gqa, "gqa_attn_ctx")
            cx = _out_mlp(cx, (ya_ctx, yb_ctx, yc_ctx, yd_ctx), mods_ctx, w_mlp, layer, n_ctx, layer > 0)
    return x
```

```python
import functools

import numpy as np
import jax
import jax.numpy as jnp
from jax import lax
from jax.experimental import pallas as pl
from jax.experimental.pallas import tpu as pltpu

F32 = jnp.float32
BF16 = jnp.bfloat16

GRID_W = 64
N_MOD = 6
EPS = 1e-6
ROPE_BASE = 10000.0
GROUP_W = 256
CONV_WIDTH = 31
CONV_PAD = 16
MLA_HEADS = 4
MLA_NOPE = 64
MLA_ROPE = 32
MLA_V = 64
MLA_Q_RANK = 256
MLA_KV_RANK = 128
GQA_HEADS = 4
GQA_KV_HEADS = 2
GQA_HEAD_DIM = 64
RET_HEADS = 4
RET_QK = 32
RET_V = 64
RET_CHUNK = 128
LANES = 128
SUBLANES = 8
LOG2_E = 1.4426950408889634
LAT_TILE = 512
PROJ_TILE = 1024
KEY_CHUNK = 256
NORM_ROWS = 256
VMEM_LIMIT = 56 * 1024 * 1024

OFF_A, OFF_CQ, OFF_CKV, OFF_GQ, OFF_GK = 0, 512, 768, 1024, 1280
OFF_RQ, OFF_RV, OFF_GF, OFF_GB, N_IN = 1536, 1792, 2048, 2304, 2560
A16_MQ, A16_MK, A16_MV, A16_GK, A16_GQ, A16_GV, A16_W = 0, 512, 1024, 1536, 2048, 2304, 2560
A32_Y, A32_RQ, A32_RK, A32_RV, A32_GF, A32_GB, A32_W = 0, 256, 384, 512, 768, 1024, 1280


def _cparams(n_axes):
    return pltpu.CompilerParams(dimension_semantics=("arbitrary",) * n_axes,
                                vmem_limit_bytes=VMEM_LIMIT)


def _resident(a, layer=None):
    if layer is None:
        return pl.BlockSpec(a.shape, lambda *_: (0,) * a.ndim, pipeline_mode=pl.Buffered(1))
    return pl.BlockSpec((None,) + a.shape[1:], lambda *_: (layer,) + (0,) * (a.ndim - 1),
                        pipeline_mode=pl.Buffered(1))


def _rms(x):
    return x * lax.rsqrt(jnp.mean(x * x, axis=-1, keepdims=True) + EPS)


def _sigmoid(x):
    return 1.0 / (1.0 + jnp.exp(-x))


def _silu(x):
    return x * _sigmoid(x)


def _dot(a, b):
    return jnp.dot(a, b, preferred_element_type=F32)


def _dot_nt(a, b):
    return lax.dot_general(a, b, (((1,), (1,)), ((), ())), preferred_element_type=F32)


def _dot_tn(a, b):
    return lax.dot_general(a, b, (((0,), (0,)), ((), ())), preferred_element_type=F32)


def _rope(x, cos, sin_up, sin_dn, shift):
    n = x.shape[-1]
    return x * cos + pltpu.roll(x, n - shift, 1) * sin_up + pltpu.roll(x, shift, 1) * sin_dn


def _adaln_kernel(cond_ref, w_ref, b_ref, o_ref):
    cond = cond_ref[...]
    o_ref[...] = _dot(_silu(cond).astype(BF16), w_ref[...].astype(BF16)) + b_ref[...]


def _adaln(cond, w_mod, b_mod):
    depth, d, n = w_mod.shape
    r = cond.shape[0]
    tn = 1536
    return pl.pallas_call(
        _adaln_kernel,
        grid=(depth, n // tn),
        in_specs=[pl.BlockSpec((r, d), lambda l, j: (0, 0)),
                  pl.BlockSpec((None, d, tn), lambda l, j: (l, 0, j)),
                  pl.BlockSpec((None, 1, tn), lambda l, j: (l, 0, j))],
        out_specs=pl.BlockSpec((None, r, tn), lambda l, j: (l, 0, j)),
        out_shape=jax.ShapeDtypeStruct((depth, r, n), F32),
        compiler_params=_cparams(2),
        name="adaln",
    )(cond, w_mod, b_mod.reshape(depth, 1, n))


def _in_proj_kernel(x_ref, mod_ref, w_in_ref, w_uq_ref, w_ukv_ref, g_mq_ref, g_mkv_ref, g_gq_ref, g_gk_ref,
                    *rest, rotary, sub_tiles):
    a16_ref, a32_ref = rest[-2:]
    lo = lax.broadcasted_iota(jnp.int32, (1, LANES), 1) < (LANES // 2)
    q_scale = (MLA_NOPE + MLA_ROPE) ** -0.5 * LOG2_E
    n_rows = x_ref.shape[0] // sub_tiles

    for st in range(sub_tiles):
        rows = slice(st * n_rows, (st + 1) * n_rows)
        h = (_rms(x_ref[rows, :]) * (1.0 + mod_ref[1:2, :]) + mod_ref[0:1, :]).astype(BF16)

        def proj(off, width, h=h):
            return _dot(h, w_in_ref[:, off:off + width])

        def put16(off, val, rows=rows):
            a16_ref[rows, off:off + val.shape[1]] = val.astype(BF16)

        def put32(off, val, rows=rows):
            a32_ref[rows, off:off + val.shape[1]] = val

        if rotary:
            tg_ref, tm_ref = rest[:2]
            rope_g = lambda v, rows=rows: _rope(v, tg_ref[0, rows, :], tg_ref[1, rows, :], tg_ref[2, rows, :],
                                                GQA_HEAD_DIM // 4)
            rope_m = lambda v, rows=rows: _rope(v, tm_ref[0, rows, :], tm_ref[1, rows, :], tm_ref[2, rows, :],
                                                MLA_ROPE // 4)
        else:
            rope_g = rope_m = lambda v: v

        def pair_norm_rope(xp, gain, rope_g=rope_g):
            x2 = xp * xp
            s_lo = jnp.sum(jnp.where(lo, x2, 0.0), axis=-1, keepdims=True)
            s_hi = jnp.sum(jnp.where(lo, 0.0, x2), axis=-1, keepdims=True)
            r = jnp.where(lo, lax.rsqrt(s_lo * (1.0 / GQA_HEAD_DIM) + EPS),
                          lax.rsqrt(s_hi * (1.0 / GQA_HEAD_DIM) + EPS))
            return rope_g(xp * r * gain)

        cq = (_rms(proj(OFF_CQ, MLA_Q_RANK)) * g_mq_ref[...]).astype(BF16)
        ckv_kpe = proj(OFF_CKV, 2 * LANES)
        ckv = (_rms(ckv_kpe[:, :LANES]) * g_mkv_ref[...]).astype(BF16)
        gq = proj(OFF_GQ, GQA_HEADS * GQA_HEAD_DIM)
        gkv = proj(OFF_GK, 2 * LANES)

        a = proj(OFF_A, 2 * GROUP_W)
        put32(A32_Y, a[:, :GROUP_W] * _sigmoid(a[:, GROUP_W:]))
        rqk = proj(OFF_RQ, 2 * LANES)
        put32(A32_RQ, rqk[:, :LANES])
        put32(A32_RK, rqk[:, LANES:] * RET_QK ** -0.5)
        put32(A32_RV, proj(OFF_RV, RET_HEADS * RET_V))
        put32(A32_GF, _silu(proj(OFF_GF, GROUP_W)))
        put32(A32_GB, _silu(proj(OFF_GB, GROUP_W)))

        q = _dot(cq, w_uq_ref[...])
        for hd in range(MLA_HEADS):
            put16(A16_MQ + hd * LANES, rope_m(q[:, hd * LANES:(hd + 1) * LANES]) * q_scale)
        kv = _dot(ckv, w_ukv_ref[...])
        kpe = rope_m(ckv_kpe[:, LANES:])
        for hd in range(MLA_HEADS):
            put16(A16_MK + hd * LANES, kv[:, hd * LANES:(hd + 1) * LANES] + kpe)
            put16(A16_MV + hd * LANES,
                  jnp.where(lo, kv[:, (MLA_HEADS + hd) * LANES:(MLA_HEADS + hd + 1) * LANES], 1.0))

        for pr in range(2):
            qp = pair_norm_rope(gq[:, pr * LANES:(pr + 1) * LANES], g_gq_ref[...])
            put16(A16_GQ + pr * LANES, qp * (GQA_HEAD_DIM ** -0.5 * LOG2_E))
        kp = pair_norm_rope(gkv[:, :LANES], g_gk_ref[...])
        kp_sw = pltpu.roll(kp, LANES // 2, 1)
        put16(A16_GK + 0 * LANES, jnp.where(lo, kp, 0.0))
        put16(A16_GK + 1 * LANES, jnp.where(lo, 0.0, kp_sw))
        put16(A16_GK + 2 * LANES, jnp.where(lo, kp_sw, 0.0))
        put16(A16_GK + 3 * LANES, jnp.where(lo, 0.0, kp))
        vp = gkv[:, LANES:]
        put16(A16_GV + 0 * LANES, jnp.where(lo, vp, 1.0))
        put16(A16_GV + 1 * LANES, jnp.where(lo, pltpu.roll(vp, LANES // 2, 1), 1.0))


def _in_proj(xseq, mods, weights, layer, tables, tm):
    b, n, d = xseq.shape
    per_batch = mods.shape[0] == b
    tok = lambda w: pl.BlockSpec((None, tm, w), lambda i, j: (i, j, 0))
    in_specs = [tok(d), pl.BlockSpec((None, N_MOD, d), lambda i, j: (i if per_batch else 0, 0, 0))]
    in_specs += [_resident(w, layer) for w in weights]
    args = [xseq, mods, *weights]
    if tables is not None:
        in_specs += [pl.BlockSpec((3, tm, LANES), lambda i, j: (0, j, 0))] * 2
        args += list(tables)
    return pl.pallas_call(
        functools.partial(_in_proj_kernel, rotary=tables is not None, sub_tiles=tm // 256),
        grid=(b, n // tm),
        in_specs=in_specs,
        out_specs=[tok(A16_W), tok(A32_W)],
        out_shape=[jax.ShapeDtypeStruct((b, n, A16_W), BF16), jax.ShapeDtypeStruct((b, n, A32_W), F32)],
        compiler_params=_cparams(2),
        name="in_proj",
    )(*args)


def _conv_kernel(yl_ref, yc_ref, dw_ref, b_ref, lg_ref, lb_ref, pw_ref, ol_ref, oc_ref, sh_ref, *, rows):
    zeros = jnp.zeros((CONV_PAD, GROUP_W), F32)
    bias = b_ref[...]
    first_off = CONV_PAD - CONV_WIDTH // 2
    tail = ((first_off + CONV_WIDTH - 1) // SUBLANES) * SUBLANES

    def sequence(y_ref, o_ref):
        length = y_ref.shape[0]
        sh_ref[0, 0:CONV_PAD, :] = zeros
        sh_ref[0, CONV_PAD:CONV_PAD + length, :] = y_ref[...]
        sh_ref[0, CONV_PAD + length:2 * CONV_PAD + length, :] = zeros

        def shift_block(c, carry):
            base = pl.multiple_of(c * rows, rows)
            win = sh_ref[0, pl.ds(base, rows + SUBLANES), :]
            for r in range(1, SUBLANES):
                sh_ref[r, pl.ds(base, rows), :] = win[r:r + rows, :]
            return carry

        lax.fori_loop(0, length // rows, shift_block, 0)
        win = sh_ref[0, length:length + tail + SUBLANES, :]
        for r in range(1, SUBLANES):
            sh_ref[r, length:length + tail, :] = win[r:r + tail, :]

        def taps(c, carry):
            base = pl.multiple_of(c * rows, rows)
            acc = jnp.zeros((rows, GROUP_W), F32) + bias
            for k in range(CONV_WIDTH):
                off = first_off + k
                slab = sh_ref[off % SUBLANES, pl.ds(base + (off // SUBLANES) * SUBLANES, rows), :]
                acc = acc + slab * dw_ref[k:k + 1, :]
            sh_ref[0, pl.ds(base, rows), :] = acc
            return carry

        lax.fori_loop(0, length // rows, taps, 0)

        def finish(c, carry):
            base = pl.multiple_of(c * NORM_ROWS, NORM_ROWS)
            acc = sh_ref[0, pl.ds(base, NORM_ROWS), :]
            mu = jnp.mean(acc, axis=-1, keepdims=True)
            cen = acc - mu
            var = jnp.mean(cen * cen, axis=-1, keepdims=True)
            z = _silu(cen * lax.rsqrt(var + EPS) * lg_ref[...] + lb_ref[...])
            o_ref[pl.ds(base, NORM_ROWS), :] = _dot(z.astype(BF16), pw_ref[...]).astype(BF16)
            return carry

        n_norm = length // NORM_ROWS
        lax.fori_loop(0, n_norm, finish, 0, unroll=min(n_norm, 4))

    sequence(yc_ref, oc_ref)
    sequence(yl_ref, ol_ref)


def _conv(a32_lat, a32_ctx, weights, layer):
    b, n_lat, _ = a32_lat.shape
    n_ctx = a32_ctx.shape[1]
    col = A32_Y // GROUP_W
    seq = lambda n: pl.BlockSpec((None, n, GROUP_W), lambda i: (i, 0, col))
    out = lambda n: pl.BlockSpec((None, n, GROUP_W), lambda i: (i, 0, 0))
    return pl.pallas_call(
        functools.partial(_conv_kernel, rows=64),
        grid=(b,),
        in_specs=[seq(n_lat), seq(n_ctx)] + [_resident(w, layer) for w in weights],
        out_specs=[out(n_lat), out(n_ctx)],
        out_shape=[jax.ShapeDtypeStruct((b, n_lat, GROUP_W), BF16), jax.ShapeDtypeStruct((b, n_ctx, GROUP_W), BF16)],
        scratch_shapes=[pltpu.VMEM((SUBLANES, max(n_lat, n_ctx) + 2 * CONV_PAD, GROUP_W), F32)],
        compiler_params=_cparams(1),
        name="conv",
    )(a32_lat, a32_ctx, *weights)


def _normalized_pair(acc_even, acc_odd):
    half = LANES // 2
    lo = lax.broadcasted_iota(jnp.int32, (1, LANES), 1) < half
    return jnp.where(lo, acc_even / pltpu.roll(acc_even, half, 1), pltpu.roll(acc_odd, half, 1) / acc_odd)


def _attn_lat_kernel(q_ref, qn_ref, kl_ref, kc_ref, vl_ref, vc_ref, o_ref, s_ref, m_ref, mb_ref, acc_ref,
                     *, q_blocks, v_blocks):
    tq = q_ref.shape[0]
    tk = KEY_CHUNK
    chunks = [(kl_ref, vl_ref, c * tk) for c in range(kl_ref.shape[0] // tk)]
    chunks += [(kc_ref, vc_ref, c * tk) for c in range(kc_ref.shape[0] // tk)]

    def scores_chunk(hd, ci, ref):
        slot = hd % 2
        k_ref, _, r0 = chunks[ci]
        q = ref[:, q_blocks[hd] * LANES:(q_blocks[hd] + 1) * LANES]
        s = _dot_nt(q, k_ref[r0:r0 + tk, hd * LANES:(hd + 1) * LANES])
        s_ref[slot, :, ci * tk:(ci + 1) * tk] = s
        mx = s[:, 0:LANES]
        for i in range(1, tk // LANES):
            mx = jnp.maximum(mx, s[:, i * LANES:(i + 1) * LANES])
        m_ref[slot] = mx if ci == 0 else jnp.maximum(m_ref[slot], mx)

    def scores_finish(hd):
        slot = hd % 2
        mb_ref[slot] = jnp.broadcast_to(jnp.max(m_ref[slot], axis=-1, keepdims=True), (tq, LANES))

    def values_chunk(hd, ci):
        slot = hd % 2
        _, v_ref, r0 = chunks[ci]
        mb = mb_ref[slot]
        p = jnp.concatenate(
            [jnp.exp2(s_ref[slot, :, ci * tk + i * LANES:ci * tk + (i + 1) * LANES] - mb).astype(BF16)
             for i in range(tk // LANES)], axis=1)
        pv = _dot(p, v_ref[r0:r0 + tk, v_blocks[hd] * LANES:(v_blocks[hd] + 1) * LANES])
        acc_ref[slot] = pv if ci == 0 else acc_ref[slot] + pv

    n = len(chunks)

    def scores_ahead(unit, ci):
        scores_chunk(unit % 4, ci, q_ref if unit < 4 else qn_ref)

    @pl.when(pl.program_id(1) == 0)
    def _():
        for ci in range(n):
            scores_chunk(0, ci, q_ref)
        scores_finish(0)
        scores_chunk(1, 0, q_ref)

    for hd in range(4):
        for ci in range(n):
            values_chunk(hd, ci)
            if ci + 1 < n:
                scores_ahead(hd + 1, ci + 1)
            else:
                scores_ahead(hd + 2, 0)
            if ci + 2 == n:
                scores_finish((hd + 1) % 4)
        if hd % 2 == 1:
            pr = hd // 2
            o_ref[:, pr * LANES:(pr + 1) * LANES] = _normalized_pair(acc_ref[0], acc_ref[1]).astype(BF16)


def _attention_lat(a16_lat, a16_ctx, q_off, q_width, k_off, v_off, v_width, q_blocks, v_blocks, name):
    b, n_lat, _ = a16_lat.shape
    n_ctx = a16_ctx.shape[1]
    tq = LAT_TILE
    n_tiles = n_lat // tq
    kw = 4 * LANES
    qc, kc, vc = q_off // q_width, k_off // kw, v_off // v_width
    return pl.pallas_call(
        functools.partial(_attn_lat_kernel, q_blocks=q_blocks, v_blocks=v_blocks),
        grid=(b, n_tiles),
        in_specs=[pl.BlockSpec((None, tq, q_width), lambda i, j: (i, j, qc)),
                  pl.BlockSpec((None, tq, q_width), lambda i, j: (i, jnp.minimum(j + 1, n_tiles - 1), qc)),
                  pl.BlockSpec((None, n_lat, kw), lambda i, j: (i, 0, kc)),
                  pl.BlockSpec((None, n_ctx, kw), lambda i, j: (i, 0, kc)),
                  pl.BlockSpec((None, n_lat, v_width), lambda i, j: (i, 0, vc)),
                  pl.BlockSpec((None, n_ctx, v_width), lambda i, j: (i, 0, vc))],
        out_specs=pl.BlockSpec((None, tq, 2 * LANES), lambda i, j: (i, j, 0)),
        out_shape=jax.ShapeDtypeStruct((b, n_lat, 2 * LANES), BF16),
        scratch_shapes=[pltpu.VMEM((2, tq, n_lat + n_ctx), F32), pltpu.VMEM((2, tq, LANES), F32),
                        pltpu.VMEM((2, tq, LANES), F32), pltpu.VMEM((2, tq, LANES), F32)],
        compiler_params=_cparams(2),
        name=name,
    )(a16_lat, a16_lat, a16_lat, a16_ctx, a16_lat, a16_ctx)


def _attn_ctx_kernel(q_ref, k_ref, v_ref, o_ref, *, q_blocks, v_blocks):
    for pr in range(2):
        accs = []
        for half in range(2):
            hd = 2 * pr + half
            q = q_ref[:, q_blocks[hd] * LANES:(q_blocks[hd] + 1) * LANES]
            s = _dot_nt(q, k_ref[:, hd * LANES:(hd + 1) * LANES])
            p = jnp.exp2(s - jnp.max(s, axis=-1, keepdims=True))
            accs.append(_dot(p.astype(BF16), v_ref[:, v_blocks[hd] * LANES:(v_blocks[hd] + 1) * LANES]))
        o_ref[:, pr * LANES:(pr + 1) * LANES] = _normalized_pair(*accs).astype(BF16)


def _attention_ctx(a16_ctx, q_off, q_width, k_off, v_off, v_width, q_blocks, v_blocks, name):
    b, n_ctx, _ = a16_ctx.shape
    kw = 4 * LANES
    qc, kc, vc = q_off // q_width, k_off // kw, v_off // v_width
    return pl.pallas_call(
        functools.partial(_attn_ctx_kernel, q_blocks=q_blocks, v_blocks=v_blocks),
        grid=(b,),
        in_specs=[pl.BlockSpec((None, n_ctx, q_width), lambda i: (i, 0, qc)),
                  pl.BlockSpec((None, n_ctx, kw), lambda i: (i, 0, kc)),
                  pl.BlockSpec((None, n_ctx, v_width), lambda i: (i, 0, vc))],
        out_specs=pl.BlockSpec((None, n_ctx, 2 * LANES), lambda i: (i, 0, 0)),
        out_shape=jax.ShapeDtypeStruct((b, n_ctx, 2 * LANES), BF16),
        compiler_params=_cparams(1),
        name=name,
    )(a16_ctx, a16_ctx, a16_ctx)


def _log_sigmoid(x):
    return jnp.minimum(x, 0.0) - jnp.log(1.0 + jnp.exp(-jnp.abs(x)))


def _ret_kernel(ql_ref, kl_ref, vl_ref, gfl_ref, gbl_ref, qc_ref, kc_ref, vc_ref, gfc_ref, gbc_ref,
                dq_ref, dv_ref, dc_ref, ng_ref, ol_ref, oc_ref, acc_ref):
    c = RET_CHUNK
    n_lat, n_ctx = ql_ref.shape[0], qc_ref.shape[0]
    qk_w = RET_HEADS * RET_QK
    v_w = RET_HEADS * RET_V

    row_i = lax.broadcasted_iota(jnp.int32, (c, 1), 0).astype(F32)
    qk_head = lax.broadcasted_iota(jnp.int32, (1, qk_w), 1) // RET_QK
    v_head = lax.broadcasted_iota(jnp.int32, (1, v_w), 1) // RET_V
    qk_masks = [qk_head == hd for hd in range(RET_HEADS)]
    v_masks = [v_head == hd for hd in range(RET_HEADS)]
    state_rows = lax.broadcasted_iota(jnp.int32, (qk_w, 1), 0) // RET_QK
    bd_mask = (state_rows == v_head).astype(F32)
    ones_rows = lax.broadcasted_iota(jnp.int32, (v_w, 1), 0) // RET_V
    head_ones = (ones_rows == v_head).astype(BF16)
    ci = lax.broadcasted_iota(jnp.int32, (c, RET_HEADS * c), 0).astype(F32)
    cj = (lax.broadcasted_iota(jnp.int32, (c, RET_HEADS * c), 1) % c).astype(F32)

    def make_step(d):
        lg_q = _log_sigmoid(dq_ref[d:d + 1, :])
        lg_v = _log_sigmoid(dv_ref[d:d + 1, :])
        lg_c = _log_sigmoid(dc_ref[d:d + 1, :])
        if d == 0:
            diff = ci - cj
            q_dec = jnp.exp(lg_q * (row_i + 1.0))
            k_dec = jnp.exp(lg_q * (c - 1.0 - row_i))
        else:
            diff = cj - ci
            q_dec = jnp.exp(lg_q * (c - row_i))
            k_dec = jnp.exp(lg_q * row_i)
        decay = jnp.where(diff >= 0.0, jnp.exp(lg_c * jnp.maximum(diff, 0.0)), 0.0)
        chunk_dec = jnp.exp(lg_v * float(c))

        def step(refs, r0, acc_r0, state):
            q_ref, k_ref, v_ref = refs
            q = q_ref[pl.ds(r0, c), :]
            k = k_ref[pl.ds(r0, c), :]
            v = v_ref[pl.ds(r0, c), :]
            kb, vb = k.astype(BF16), v.astype(BF16)
            k_bd = jnp.concatenate([jnp.where(m, kb, 0) for m in qk_masks], axis=0)
            scores = _dot_nt(q.astype(BF16), k_bd) * decay
            v_bd = jnp.concatenate([jnp.where(m, vb, 0) for m in v_masks], axis=0)
            inner = _dot(scores.astype(BF16), v_bd)
            cross = _dot((q * q_dec).astype(BF16), state.astype(BF16))
            kv = _dot_tn((k * k_dec).astype(BF16), vb)
            new_state = state * chunk_dec + kv * bd_mask
            acc_ref[d, pl.ds(acc_r0, c), :] = inner + cross
            return new_state

        return step

    fwd, bwd = make_step(0), make_step(1)

    def scan(n, acc_base, fwd_refs, bwd_refs, states):
        def body(i, st):
            rf = pl.multiple_of(i * c, c)
            rb = pl.multiple_of((n - 1 - i) * c, c)
            return fwd(fwd_refs, rf, acc_base + rf, st[0]), bwd(bwd_refs, rb, acc_base + rb, st[1])
        return lax.fori_loop(0, n, body, states, unroll=min(n, 4))

    zero = jnp.zeros((qk_w, v_w), F32)
    states = scan(n_ctx // c, n_lat, (qc_ref, kc_ref, vc_ref), (qc_ref, kc_ref, vc_ref), (zero, zero))
    scan(n_lat // c, 0, (ql_ref, kl_ref, vl_ref), (ql_ref, kl_ref, vl_ref), states)

    def readout(o_ref, gate_refs, acc_base):
        rows = NORM_ROWS

        def body(i, carry):
            r0 = pl.multiple_of(i * rows, rows)
            y = None
            for d in range(2):
                o = acc_ref[d, pl.ds(acc_base + r0, rows), :]
                o2 = o * o
                o2_hi = o2.astype(BF16)
                o2_lo = (o2 - o2_hi.astype(F32)).astype(BF16)
                ss = _dot(o2_hi, head_ones) + _dot(o2_lo, head_ones)
                yd = o * lax.rsqrt(ss * (1.0 / RET_V) + EPS) * ng_ref[d:d + 1, :] * gate_refs[d][pl.ds(r0, rows), :]
                y = yd if y is None else y + yd
            o_ref[pl.ds(r0, rows), :] = y.astype(BF16)
            return carry

        n = o_ref.shape[0] // rows
        lax.fori_loop(0, n, body, 0, unroll=min(n, 2))

    readout(ol_ref, (gfl_ref, gbl_ref), 0)
    readout(oc_ref, (gfc_ref, gbc_ref), n_lat)


def _retention(a32_lat, a32_ctx, smalls, layer):
    b, n_lat, _ = a32_lat.shape
    n_ctx = a32_ctx.shape[1]

    def seq_specs(n):
        spec = lambda off, w: pl.BlockSpec((None, n, w), lambda i: (i, 0, off // w))
        return [spec(A32_RQ, 128), spec(A32_RK, 128), spec(A32_RV, 256), spec(A32_GF, 256), spec(A32_GB, 256)]

    out = lambda n: pl.BlockSpec((None, n, 256), lambda i: (i, 0, 0))
    return pl.pallas_call(
        _ret_kernel,
        grid=(b,),
        in_specs=seq_specs(n_lat) + seq_specs(n_ctx) + [_resident(a, layer) for a in smalls],
        out_specs=[out(n_lat), out(n_ctx)],
        out_shape=[jax.ShapeDtypeStruct((b, n_lat, 256), BF16), jax.ShapeDtypeStruct((b, n_ctx, 256), BF16)],
        scratch_shapes=[pltpu.VMEM((2, n_lat + n_ctx, 256), F32)],
        compiler_params=_cparams(1),
        name="retention",
    )(*([a32_lat] * 5), *([a32_ctx] * 5), *smalls)


def _out_mlp_kernel(x_ref, ya_ref, yb_ref, yc_ref, yd_ref, mod_ref, wo_ref, w1_ref, w2_ref, *rest, ff_chunk):
    o_ref = rest[-1]
    x = x_ref[...]
    gw = GROUP_W
    y = (_dot(ya_ref[...], wo_ref[0 * gw:1 * gw, :]) + _dot(yb_ref[...], wo_ref[1 * gw:2 * gw, :])
         + _dot(yc_ref[...], wo_ref[2 * gw:3 * gw, :]) + _dot(yd_ref[...], wo_ref[3 * gw:4 * gw, :]))
    x1 = x + mod_ref[2:3, :] * y
    h = (_rms(x1) * (1.0 + mod_ref[4:5, :]) + mod_ref[3:4, :]).astype(BF16)
    acc = jnp.zeros_like(x1)
    for j in range(w1_ref.shape[1] // ff_chunk):
        a = jnp.maximum(_dot(h, w1_ref[:, j * ff_chunk:(j + 1) * ff_chunk]), 0.0)
        acc = acc + _dot((a * a).astype(BF16), w2_ref[j * ff_chunk:(j + 1) * ff_chunk, :])
    x2 = x1 + mod_ref[5:6, :] * acc
    o_ref[...] = x2 if len(rest) == 1 else _rms(x2) * rest[0][...]


def _out_mlp(xseq, ys, mods, weights, layer, tm, in_place, final_gain=None):
    b, n, d = xseq.shape
    per_batch = mods.shape[0] == b
    tok = lambda w: pl.BlockSpec((None, tm, w), lambda i, j: (i, j, 0))
    in_specs = [tok(d)] + [tok(GROUP_W)] * 4
    in_specs += [pl.BlockSpec((None, N_MOD, d), lambda i, j: (i if per_batch else 0, 0, 0))]
    in_specs += [_resident(w, layer) for w in weights]
    args = [xseq, *ys, mods, *weights]
    if final_gain is not None:
        in_specs.append(_resident(final_gain))
        args.append(final_gain)
    return pl.pallas_call(
        functools.partial(_out_mlp_kernel, ff_chunk=1024),
        grid=(b, n // tm),
        in_specs=in_specs,
        out_specs=tok(d),
        out_shape=jax.ShapeDtypeStruct((b, n, d), F32),
        input_output_aliases={0: 0} if in_place else {},
        compiler_params=_cparams(2),
        name="out_mlp",
    )(*args)


def _rope_tables(n_lat):
    n = np.arange(n_lat)
    row, col = (n // GRID_W).astype(np.float64), (n % GRID_W).astype(np.float64)

    def table(groups):
        cos = np.ones((n_lat, LANES)); up = np.zeros((n_lat, LANES)); dn = np.zeros((n_lat, LANES))
        for rot_start, rot_dim in groups:
            half = rot_dim // 2
            q = half // 2
            freqs = ROPE_BASE ** (-np.arange(q, dtype=np.float64) / q)
            for axis, pos in enumerate((row, col)):
                ang = pos[:, None] * freqs[None, :]
                base = rot_start + axis * half
                cos[:, base:base + q] = np.cos(ang)
                cos[:, base + q:base + 2 * q] = np.cos(ang)
                up[:, base:base + q] = -np.sin(ang)
                dn[:, base + q:base + 2 * q] = np.sin(ang)
        return jnp.asarray(np.stack([cos, up, dn]).astype(np.float32))

    return table([(0, GQA_HEAD_DIM), (GQA_HEAD_DIM, GQA_HEAD_DIM)]), table([(MLA_NOPE, MLA_ROPE)])


def _layout_w_in(w_in):
    lead = w_in.shape[:-1]
    kpe_src = OFF_CKV + MLA_KV_RANK
    w = w_in.astype(BF16)
    out = jnp.concatenate([w[..., :kpe_src], jnp.zeros(lead + (MLA_NOPE,), BF16),
                           w[..., kpe_src:kpe_src + MLA_ROPE],
                           jnp.zeros(lead + (LANES - MLA_NOPE - MLA_ROPE,), BF16),
                           w[..., kpe_src + MLA_ROPE:]], axis=-1)
    assert out.shape[-1] == N_IN
    return out


def _layout_w_uq(w_uq):
    lead = w_uq.shape[:-1]
    w = w_uq.reshape(lead + (MLA_HEADS, MLA_NOPE + MLA_ROPE))
    w = jnp.pad(w, ((0, 0),) * (w.ndim - 1) + ((0, LANES - MLA_NOPE - MLA_ROPE),))
    return w.reshape(lead + (MLA_HEADS * LANES,)).astype(BF16)


def _layout_w_ukv(w_ukv):
    lead = w_ukv.shape[:-1]
    w = w_ukv.reshape(lead + (MLA_HEADS, MLA_NOPE + MLA_V))
    pad = lambda a: jnp.pad(a, ((0, 0),) * (a.ndim - 1) + ((0, LANES - a.shape[-1]),)).reshape(
        lead + (MLA_HEADS * LANES,))
    return jnp.concatenate([pad(w[..., :MLA_NOPE]), pad(w[..., MLA_NOPE:])], axis=-1).astype(BF16)


def kernel(x, c, ctx, c_ctx, w_mod, b_mod, w_in, w_out, conv_dw, conv_b, conv_ln_g, conv_ln_b, conv_pw, mla_q_g, mla_kv_g, mla_uq, mla_ukv, gqa_q_g, gqa_k_g, ret_decay, ret_norm_g, mlp_w1, mlp_w2, final_g):
    b, n_lat, d = x.shape
    n_ctx = ctx.shape[1]
    depth = w_mod.shape[0]
    assert n_lat % LAT_TILE == 0 and n_lat % PROJ_TILE == 0 and n_lat % KEY_CHUNK == 0 and n_ctx % KEY_CHUNK == 0
    assert n_ctx % NORM_ROWS == 0 and n_lat % NORM_ROWS == 0 and NORM_ROWS % (2 * RET_CHUNK) == 0

    rows = ((b + 1 + SUBLANES - 1) // SUBLANES) * SUBLANES
    cond = jnp.zeros((rows, d), F32).at[:b].set(c).at[b].set(c_ctx)
    mod_all = _adaln(cond, w_mod, b_mod).reshape(depth, rows, N_MOD, d)
    tables = _rope_tables(n_lat)
    mla = (A16_MQ, 4 * LANES, A16_MK, A16_MV, 4 * LANES, (0, 1, 2, 3), (0, 1, 2, 3))
    gqa = (A16_GQ, 2 * LANES, A16_GK, A16_GV, 2 * LANES, (0, 0, 1, 1), (0, 0, 1, 1))

    row = lambda a: a[:, None, :]
    w_proj = (_layout_w_in(w_in), _layout_w_uq(mla_uq), _layout_w_ukv(mla_ukv), row(mla_q_g), row(mla_kv_g),
              row(jnp.tile(gqa_q_g, (1, 2))), row(jnp.tile(gqa_k_g, (1, 2))))
    w_conv = (conv_dw, row(conv_b), row(conv_ln_g), row(conv_ln_b), conv_pw.astype(BF16))
    dec = ret_decay.astype(F32)
    w_ret = (jnp.repeat(dec, RET_QK, axis=2), jnp.repeat(dec, RET_V, axis=2), jnp.repeat(dec, RET_CHUNK, axis=2),
             ret_norm_g.reshape(depth, 2, RET_HEADS * RET_V))
    w_mlp = (w_out.astype(BF16), mlp_w1.astype(BF16), mlp_w2.astype(BF16))

    cx = ctx
    for layer in range(depth):
        last = layer == depth - 1
        mods_lat, mods_ctx = mod_all[layer, :b], mod_all[layer, b:b + 1]
        a16_lat, a32_lat = _in_proj(x, mods_lat, w_proj, layer, tables, PROJ_TILE)
        a16_ctx, a32_ctx = _in_proj(cx, mods_ctx, w_proj, layer, None, n_ctx)
        ya_lat, ya_ctx = _conv(a32_lat, a32_ctx, w_conv, layer)
        yb_lat = _attention_lat(a16_lat, a16_ctx, *mla, "mla_attn")
        yc_lat = _attention_lat(a16_lat, a16_ctx, *gqa, "gqa_attn")
        yd_lat, yd_ctx = _retention(a32_lat, a32_ctx, w_ret, layer)
        x = _out_mlp(x, (ya_lat, yb_lat, yc_lat, yd_lat), mods_lat, w_mlp, layer, LAT_TILE, layer > 0,
                     final_gain=final_g[None, :] if last else None)
        if not last:
            yb_ctx = _attention_ctx(a16_ctx, *mla, "mla_attn_ctx")
            yc_ctx = _attention_ctx(a16_ctx, *gqa, "gqa_attn_ctx")
            cx = _out_mlp(cx, (ya_ctx, yb_ctx, yc_ctx, yd_ctx), mods_ctx, w_mlp, layer, n_ctx, layer > 0)
    return x
```

```python
import functools

import numpy as np
import jax
import jax.numpy as jnp
from jax import lax
from jax.experimental import pallas as pl
from jax.experimental.pallas import tpu as pltpu

F32 = jnp.float32
BF16 = jnp.bfloat16

GRID_W = 64
N_MOD = 6
EPS = 1e-6
ROPE_BASE = 10000.0
GROUP_W = 256
CONV_WIDTH = 31
CONV_PAD = 16
MLA_HEADS = 4
MLA_NOPE = 64
MLA_ROPE = 32
MLA_V = 64
MLA_Q_RANK = 256
MLA_KV_RANK = 128
GQA_HEADS = 4
GQA_KV_HEADS = 2
GQA_HEAD_DIM = 64
RET_HEADS = 4
RET_QK = 32
RET_V = 64
RET_CHUNK = 128
LANES = 128
SUBLANES = 8
LOG2_E = 1.4426950408889634
LAT_TILE = 512
PROJ_TILE = 1024
KEY_CHUNK = 256
NORM_ROWS = 256
VMEM_LIMIT = 56 * 1024 * 1024

OFF_A, OFF_CQ, OFF_CKV, OFF_GQ, OFF_GK = 0, 512, 768, 1024, 1280
OFF_RQ, OFF_RV, OFF_GF, OFF_GB, N_IN = 1536, 1792, 2048, 2304, 2560
A16_MQ, A16_MK, A16_MV, A16_GK, A16_GQ, A16_GV, A16_W = 0, 512, 1024, 1536, 2048, 2304, 2560
A32_Y, A32_RQ, A32_RK, A32_RV, A32_GF, A32_GB, A32_W = 0, 256, 384, 512, 768, 1024, 1280


def _cparams(n_axes):
    return pltpu.CompilerParams(dimension_semantics=("arbitrary",) * n_axes,
                                vmem_limit_bytes=VMEM_LIMIT)


def _resident(a, layer=None):
    if layer is None:
        return pl.BlockSpec(a.shape, lambda *_: (0,) * a.ndim, pipeline_mode=pl.Buffered(1))
    return pl.BlockSpec((None,) + a.shape[1:], lambda *_: (layer,) + (0,) * (a.ndim - 1),
                        pipeline_mode=pl.Buffered(1))


def _rms(x):
    return x * lax.rsqrt(jnp.mean(x * x, axis=-1, keepdims=True) + EPS)


def _sigmoid(x):
    return 1.0 / (1.0 + jnp.exp(-x))


def _silu(x):
    return x * _sigmoid(x)


def _dot(a, b):
    return jnp.dot(a, b, preferred_element_type=F32)


def _dot_nt(a, b):
    return lax.dot_general(a, b, (((1,), (1,)), ((), ())), preferred_element_type=F32)


def _dot_tn(a, b):
    return lax.dot_general(a, b, (((0,), (0,)), ((), ())), preferred_element_type=F32)


def _rope(x, cos, sin_up, sin_dn, shift):
    n = x.shape[-1]
    return x * cos + pltpu.roll(x, n - shift, 1) * sin_up + pltpu.roll(x, shift, 1) * sin_dn


def _adaln_kernel(cond_ref, w_ref, b_ref, o_ref):
    cond = cond_ref[...]
    o_ref[...] = _dot(_silu(cond).astype(BF16), w_ref[...].astype(BF16)) + b_ref[...]


def _adaln(cond, w_mod, b_mod):
    depth, d, n = w_mod.shape
    r = cond.shape[0]
    tn = 1536
    return pl.pallas_call(
        _adaln_kernel,
        grid=(depth, n // tn),
        in_specs=[pl.BlockSpec((r, d), lambda l, j: (0, 0)),
                  pl.BlockSpec((None, d, tn), lambda l, j: (l, 0, j)),
                  pl.BlockSpec((None, 1, tn), lambda l, j: (l, 0, j))],
        out_specs=pl.BlockSpec((None, r, tn), lambda l, j: (l, 0, j)),
        out_shape=jax.ShapeDtypeStruct((depth, r, n), F32),
        compiler_params=_cparams(2),
        name="adaln",
    )(cond, w_mod, b_mod.reshape(depth, 1, n))


def _in_proj_kernel(x_ref, mod_ref, w_in_ref, w_uq_ref, w_ukv_ref, g_mq_ref, g_mkv_ref, g_gq_ref, g_gk_ref,
                    *rest, rotary, sub_tiles):
    a16_ref, a32_ref = rest[-2:]
    lo = lax.broadcasted_iota(jnp.int32, (1, LANES), 1) < (LANES // 2)
    q_scale = (MLA_NOPE + MLA_ROPE) ** -0.5 * LOG2_E
    n_rows = x_ref.shape[0] // sub_tiles

    for st in range(sub_tiles):
        rows = slice(st * n_rows, (st + 1) * n_rows)
        h = (_rms(x_ref[rows, :]) * (1.0 + mod_ref[1:2, :]) + mod_ref[0:1, :]).astype(BF16)

        def proj(off, width, h=h):
            return _dot(h, w_in_ref[:, off:off + width])

        def put16(off, val, rows=rows):
            a16_ref[rows, off:off + val.shape[1]] = val.astype(BF16)

        def put32(off, val, rows=rows):
            a32_ref[rows, off:off + val.shape[1]] = val

        if rotary:
            tg_ref, tm_ref = rest[:2]
            rope_g = lambda v, rows=rows: _rope(v, tg_ref[0, rows, :], tg_ref[1, rows, :], tg_ref[2, rows, :],
                                                GQA_HEAD_DIM // 4)
            rope_m = lambda v, rows=rows: _rope(v, tm_ref[0, rows, :], tm_ref[1, rows, :], tm_ref[2, rows, :],
                                                MLA_ROPE // 4)
        else:
            rope_g = rope_m = lambda v: v

        def pair_norm_rope(xp, gain, rope_g=rope_g):
            x2 = xp * xp
            s_lo = jnp.sum(jnp.where(lo, x2, 0.0), axis=-1, keepdims=True)
            s_hi = jnp.sum(jnp.where(lo, 0.0, x2), axis=-1, keepdims=True)
            r = jnp.where(lo, lax.rsqrt(s_lo * (1.0 / GQA_HEAD_DIM) + EPS),
                          lax.rsqrt(s_hi * (1.0 / GQA_HEAD_DIM) + EPS))
            return rope_g(xp * r * gain)

        cq = (_rms(proj(OFF_CQ, MLA_Q_RANK)) * g_mq_ref[...]).astype(BF16)
        ckv_kpe = proj(OFF_CKV, 2 * LANES)
        ckv = (_rms(ckv_kpe[:, :LANES]) * g_mkv_ref[...]).astype(BF16)
        gq = proj(OFF_GQ, GQA_HEADS * GQA_HEAD_DIM)
        gkv = proj(OFF_GK, 2 * LANES)

        a = proj(OFF_A, 2 * GROUP_W)
        put32(A32_Y, a[:, :GROUP_W] * _sigmoid(a[:, GROUP_W:]))
        rqk = proj(OFF_RQ, 2 * LANES)
        put32(A32_RQ, rqk[:, :LANES])
        put32(A32_RK, rqk[:, LANES:] * RET_QK ** -0.5)
        put32(A32_RV, proj(OFF_RV, RET_HEADS * RET_V))
        put32(A32_GF, _silu(proj(OFF_GF, GROUP_W)))
        put32(A32_GB, _silu(proj(OFF_GB, GROUP_W)))

        q = _dot(cq, w_uq_ref[...])
        for hd in range(MLA_HEADS):
            put16(A16_MQ + hd * LANES, rope_m(q[:, hd * LANES:(hd + 1) * LANES]) * q_scale)
        kv = _dot(ckv, w_ukv_ref[...])
        kpe = rope_m(ckv_kpe[:, LANES:])
        for hd in range(MLA_HEADS):
            put16(A16_MK + hd * LANES, kv[:, hd * LANES:(hd + 1) * LANES] + kpe)
            put16(A16_MV + hd * LANES,
                  jnp.where(lo, kv[:, (MLA_HEADS + hd) * LANES:(MLA_HEADS + hd + 1) * LANES], 1.0))

        for pr in range(2):
            qp = pair_norm_rope(gq[:, pr * LANES:(pr + 1) * LANES], g_gq_ref[...])
            put16(A16_GQ + pr * LANES, qp * (GQA_HEAD_DIM ** -0.5 * LOG2_E))
        kp = pair_norm_rope(gkv[:, :LANES], g_gk_ref[...])
        kp_sw = pltpu.roll(kp, LANES // 2, 1)
        put16(A16_GK + 0 * LANES, jnp.where(lo, kp, 0.0))
        put16(A16_GK + 1 * LANES, jnp.where(lo, 0.0, kp_sw))
        put16(A16_GK + 2 * LANES, jnp.where(lo, kp_sw, 0.0))
        put16(A16_GK + 3 * LANES, jnp.where(lo, 0.0, kp))
        vp = gkv[:, LANES:]
        put16(A16_GV + 0 * LANES, jnp.where(lo, vp, 1.0))
        put16(A16_GV + 1 * LANES, jnp.where(lo, pltpu.roll(vp, LANES // 2, 1), 1.0))


def _in_proj(xseq, mods, weights, layer, tables, tm):
    b, n, d = xseq.shape
    per_batch = mods.shape[0] == b
    tok = lambda w: pl.BlockSpec((None, tm, w), lambda i, j: (i, j, 0))
    in_specs = [tok(d), pl.BlockSpec((None, N_MOD, d), lambda i, j: (i if per_batch else 0, 0, 0))]
    in_specs += [_resident(w, layer) for w in weights]
    args = [xseq, mods, *weights]
    if tables is not None:
        in_specs += [pl.BlockSpec((3, tm, LANES), lambda i, j: (0, j, 0))] * 2
        args += list(tables)
    return pl.pallas_call(
        functools.partial(_in_proj_kernel, rotary=tables is not None, sub_tiles=tm // 256),
        grid=(b, n // tm),
        in_specs=in_specs,
        out_specs=[tok(A16_W), tok(A32_W)],
        out_shape=[jax.ShapeDtypeStruct((b, n, A16_W), BF16), jax.ShapeDtypeStruct((b, n, A32_W), F32)],
        compiler_params=_cparams(2),
        name="in_proj",
    )(*args)


CONV_ROWS = 64
CONV_FIRST = CONV_PAD - CONV_WIDTH // 2
CONV_TAIL = ((CONV_FIRST + CONV_WIDTH - 1) // SUBLANES) * SUBLANES


def _conv_steps(sh_ref, dw_ref, b_ref, lg_ref, lb_ref, pw_ref, o_ref, length):
    def fill(before, body, after):
        sh_ref[0, 0:CONV_PAD, :] = before
        sh_ref[0, CONV_PAD:CONV_PAD + length, :] = body
        sh_ref[0, CONV_PAD + length:2 * CONV_PAD + length, :] = after

    def shift(base, n=CONV_ROWS):
        win = sh_ref[0, pl.ds(base, n + SUBLANES), :]
        for r in range(1, SUBLANES):
            sh_ref[r, pl.ds(base, n), :] = win[r:r + n, :]

    def shift_tail():
        shift(length, CONV_TAIL)

    def taps(base):
        acc = jnp.zeros((CONV_ROWS, GROUP_W), F32) + b_ref[...]
        for k in range(CONV_WIDTH):
            off = CONV_FIRST + k
            slab = sh_ref[off % SUBLANES, pl.ds(base + (off // SUBLANES) * SUBLANES, CONV_ROWS), :]
            acc = acc + slab * dw_ref[k:k + 1, :]
        sh_ref[0, pl.ds(base, CONV_ROWS), :] = acc

    def finish(base):
        acc = sh_ref[0, pl.ds(base, NORM_ROWS), :]
        mu = jnp.mean(acc, axis=-1, keepdims=True)
        cen = acc - mu
        var = jnp.mean(cen * cen, axis=-1, keepdims=True)
        z = _silu(cen * lax.rsqrt(var + EPS) * lg_ref[...] + lb_ref[...])
        o_ref[pl.ds(base, NORM_ROWS), :] = _dot(z.astype(BF16), pw_ref[...]).astype(BF16)

    return fill, shift, shift_tail, taps, finish


def _conv_kernel(y_ref, dw_ref, b_ref, lg_ref, lb_ref, pw_ref, o_ref, sh_ref):
    length = y_ref.shape[0]
    fill, shift, shift_tail, taps, finish = _conv_steps(sh_ref, dw_ref, b_ref, lg_ref, lb_ref, pw_ref, o_ref, length)
    zeros = jnp.zeros((CONV_PAD, GROUP_W), F32)
    fill(zeros, y_ref[...], zeros)

    def loop(step, rows, **kw):
        def body(c, carry):
            step(pl.multiple_of(c * rows, rows))
            return carry
        lax.fori_loop(0, length // rows, body, 0, **kw)

    loop(shift, CONV_ROWS)
    shift_tail()
    loop(taps, CONV_ROWS)
    loop(finish, NORM_ROWS, unroll=min(length // NORM_ROWS, 4))


def _conv(a32, weights, layer):
    b, n, _ = a32.shape
    return pl.pallas_call(
        _conv_kernel,
        grid=(b,),
        in_specs=[pl.BlockSpec((None, n, GROUP_W), lambda i: (i, 0, A32_Y // GROUP_W))]
        + [_resident(w, layer) for w in weights],
        out_specs=pl.BlockSpec((None, n, GROUP_W), lambda i: (i, 0, 0)),
        out_shape=jax.ShapeDtypeStruct((b, n, GROUP_W), BF16),
        scratch_shapes=[pltpu.VMEM((SUBLANES, n + 2 * CONV_PAD, GROUP_W), F32)],
        compiler_params=_cparams(1),
        name="conv",
    )(a32, *weights)


def _normalized_pair(acc_even, acc_odd):
    half = LANES // 2
    lo = lax.broadcasted_iota(jnp.int32, (1, LANES), 1) < half
    return jnp.where(lo, acc_even / pltpu.roll(acc_even, half, 1), pltpu.roll(acc_odd, half, 1) / acc_odd)


def _attn_lat_kernel(q_ref, qn_ref, kl_ref, kc_ref, vl_ref, vc_ref, *rest, q_blocks, v_blocks, with_conv):
    if with_conv:
        y_ref, yb_ref, ya_next_ref, dw_ref, cb_ref, lg_ref, lb_ref, pw_ref, o_ref, ya_ref = rest[:10]
        s_ref, m_ref, mb_ref, acc_ref, sh_ref = rest[10:]
    else:
        o_ref, s_ref, m_ref, mb_ref, acc_ref = rest
    tq = q_ref.shape[0]
    tk = KEY_CHUNK
    chunks = [(kl_ref, vl_ref, c * tk) for c in range(kl_ref.shape[0] // tk)]
    chunks += [(kc_ref, vc_ref, c * tk) for c in range(kc_ref.shape[0] // tk)]

    def scores_chunk(hd, ci, ref):
        slot = hd % 2
        k_ref, _, r0 = chunks[ci]
        q = ref[:, q_blocks[hd] * LANES:(q_blocks[hd] + 1) * LANES]
        s = _dot_nt(q, k_ref[r0:r0 + tk, hd * LANES:(hd + 1) * LANES])
        s_ref[slot, :, ci * tk:(ci + 1) * tk] = s
        mx = s[:, 0:LANES]
        for i in range(1, tk // LANES):
            mx = jnp.maximum(mx, s[:, i * LANES:(i + 1) * LANES])
        m_ref[slot] = mx if ci == 0 else jnp.maximum(m_ref[slot], mx)

    def scores_finish(hd):
        slot = hd % 2
        mb_ref[slot] = jnp.broadcast_to(jnp.max(m_ref[slot], axis=-1, keepdims=True), (tq, LANES))

    def values_chunk(hd, ci):
        slot = hd % 2
        _, v_ref, r0 = chunks[ci]
        mb = mb_ref[slot]
        p = jnp.concatenate(
            [jnp.exp2(s_ref[slot, :, ci * tk + i * LANES:ci * tk + (i + 1) * LANES] - mb).astype(BF16)
             for i in range(tk // LANES)], axis=1)
        pv = _dot(p, v_ref[r0:r0 + tk, v_blocks[hd] * LANES:(v_blocks[hd] + 1) * LANES])
        acc_ref[slot] = pv if ci == 0 else acc_ref[slot] + pv

    n = len(chunks)

    def scores_ahead(unit, ci):
        scores_chunk(unit % 4, ci, q_ref if unit < 4 else qn_ref)

    @pl.when(pl.program_id(1) == 0)
    def _():
        for ci in range(n):
            scores_chunk(0, ci, q_ref)
        scores_finish(0)
        scores_chunk(1, 0, q_ref)

    side_steps = []
    if with_conv:
        fill, shift, shift_tail, taps, finish = _conv_steps(sh_ref, dw_ref, cb_ref, lg_ref, lb_ref, pw_ref, ya_ref, tq)
        j, last = pl.program_id(1), pl.num_programs(1) - 1
        side_steps.append(lambda: fill(jnp.where(j > 0, yb_ref[...], 0.0), y_ref[...],
                                       jnp.where(j < last, ya_next_ref[...], 0.0)))
        side_steps += [functools.partial(shift, c * CONV_ROWS) for c in range(tq // CONV_ROWS)] + [shift_tail]
        side_steps += [functools.partial(taps, c * CONV_ROWS) for c in range(tq // CONV_ROWS)]
        side_steps += [functools.partial(finish, c * NORM_ROWS) for c in range(tq // NORM_ROWS)]
    done = 0

    for hd in range(4):
        for ci in range(n):
            values_chunk(hd, ci)
            if ci + 1 < n:
                scores_ahead(hd + 1, ci + 1)
            else:
                scores_ahead(hd + 2, 0)
            if ci + 2 == n:
                scores_finish((hd + 1) % 4)
            due = ((hd * n + ci + 1) * len(side_steps)) // (4 * n)
            for step in side_steps[done:due]:
                step()
            done = due
        if hd % 2 == 1:
            pr = hd // 2
            o_ref[:, pr * LANES:(pr + 1) * LANES] = _normalized_pair(acc_ref[0], acc_ref[1]).astype(BF16)


def _attention_lat(a16_lat, a16_ctx, q_off, q_width, k_off, v_off, v_width, q_blocks, v_blocks, name, conv=None):
    b, n_lat, _ = a16_lat.shape
    n_ctx = a16_ctx.shape[1]
    tq = LAT_TILE
    n_tiles = n_lat // tq
    kw = 4 * LANES
    qc, kc, vc = q_off // q_width, k_off // kw, v_off // v_width
    in_specs = [pl.BlockSpec((None, tq, q_width), lambda i, j: (i, j, qc)),
                pl.BlockSpec((None, tq, q_width), lambda i, j: (i, jnp.minimum(j + 1, n_tiles - 1), qc)),
                pl.BlockSpec((None, n_lat, kw), lambda i, j: (i, 0, kc)),
                pl.BlockSpec((None, n_ctx, kw), lambda i, j: (i, 0, kc)),
                pl.BlockSpec((None, n_lat, v_width), lambda i, j: (i, 0, vc)),
                pl.BlockSpec((None, n_ctx, v_width), lambda i, j: (i, 0, vc))]
    args = [a16_lat, a16_lat, a16_lat, a16_ctx, a16_lat, a16_ctx]
    tile = lambda w: pl.BlockSpec((None, tq, w), lambda i, j: (i, j, 0))
    out_specs = [tile(2 * LANES)]
    out_shape = [jax.ShapeDtypeStruct((b, n_lat, 2 * LANES), BF16)]
    scratch = [pltpu.VMEM((2, tq, n_lat + n_ctx), F32), pltpu.VMEM((2, tq, LANES), F32),
               pltpu.VMEM((2, tq, LANES), F32), pltpu.VMEM((2, tq, LANES), F32)]
    if conv is not None:
        a32_lat, weights, layer = conv
        yc = A32_Y // GROUP_W
        per_tile = tq // CONV_PAD
        halo = lambda index: pl.BlockSpec((None, CONV_PAD, GROUP_W), lambda i, j: (i, index(j), yc))
        in_specs += [pl.BlockSpec((None, tq, GROUP_W), lambda i, j: (i, j, yc)),
                     halo(lambda j: jnp.maximum(j * per_tile - 1, 0)),
                     halo(lambda j: jnp.minimum((j + 1) * per_tile, n_lat // CONV_PAD - 1))]
        in_specs += [_resident(w, layer) for w in weights]
        args += [a32_lat, a32_lat, a32_lat, *weights]
        out_specs.append(tile(GROUP_W))
        out_shape.append(jax.ShapeDtypeStruct((b, n_lat, GROUP_W), BF16))
        scratch.append(pltpu.VMEM((SUBLANES, tq + 2 * CONV_PAD, GROUP_W), F32))
    return pl.pallas_call(
        functools.partial(_attn_lat_kernel, q_blocks=q_blocks, v_blocks=v_blocks, with_conv=conv is not None),
        grid=(b, n_tiles),
        in_specs=in_specs,
        out_specs=out_specs,
        out_shape=out_shape,
        scratch_shapes=scratch,
        compiler_params=_cparams(2),
        name=name,
    )(*args)


def _attn_ctx_kernel(q_ref, k_ref, v_ref, o_ref, *, q_blocks, v_blocks):
    for pr in range(2):
        accs = []
        for half in range(2):
            hd = 2 * pr + half
            q = q_ref[:, q_blocks[hd] * LANES:(q_blocks[hd] + 1) * LANES]
            s = _dot_nt(q, k_ref[:, hd * LANES:(hd + 1) * LANES])
            p = jnp.exp2(s - jnp.max(s, axis=-1, keepdims=True))
            accs.append(_dot(p.astype(BF16), v_ref[:, v_blocks[hd] * LANES:(v_blocks[hd] + 1) * LANES]))
        o_ref[:, pr * LANES:(pr + 1) * LANES] = _normalized_pair(*accs).astype(BF16)


def _attention_ctx(a16_ctx, q_off, q_width, k_off, v_off, v_width, q_blocks, v_blocks, name):
    b, n_ctx, _ = a16_ctx.shape
    kw = 4 * LANES
    qc, kc, vc = q_off // q_width, k_off // kw, v_off // v_width
    return pl.pallas_call(
        functools.partial(_attn_ctx_kernel, q_blocks=q_blocks, v_blocks=v_blocks),
        grid=(b,),
        in_specs=[pl.BlockSpec((None, n_ctx, q_width), lambda i: (i, 0, qc)),
                  pl.BlockSpec((None, n_ctx, kw), lambda i: (i, 0, kc)),
                  pl.BlockSpec((None, n_ctx, v_width), lambda i: (i, 0, vc))],
        out_specs=pl.BlockSpec((None, n_ctx, 2 * LANES), lambda i: (i, 0, 0)),
        out_shape=jax.ShapeDtypeStruct((b, n_ctx, 2 * LANES), BF16),
        compiler_params=_cparams(1),
        name=name,
    )(a16_ctx, a16_ctx, a16_ctx)


def _log_sigmoid(x):
    return jnp.minimum(x, 0.0) - jnp.log(1.0 + jnp.exp(-jnp.abs(x)))


def _ret_kernel(ql_ref, kl_ref, vl_ref, gfl_ref, gbl_ref, qc_ref, kc_ref, vc_ref, gfc_ref, gbc_ref,
                dq_ref, dv_ref, dc_ref, ng_ref, ol_ref, oc_ref, acc_ref):
    c = RET_CHUNK
    n_lat, n_ctx = ql_ref.shape[0], qc_ref.shape[0]
    qk_w = RET_HEADS * RET_QK
    v_w = RET_HEADS * RET_V

    row_i = lax.broadcasted_iota(jnp.int32, (c, 1), 0).astype(F32)
    qk_head = lax.broadcasted_iota(jnp.int32, (1, qk_w), 1) // RET_QK
    v_head = lax.broadcasted_iota(jnp.int32, (1, v_w), 1) // RET_V
    qk_masks = [qk_head == hd for hd in range(RET_HEADS)]
    v_masks = [v_head == hd for hd in range(RET_HEADS)]
    state_rows = lax.broadcasted_iota(jnp.int32, (qk_w, 1), 0) // RET_QK
    bd_mask = (state_rows == v_head).astype(F32)
    ones_rows = lax.broadcasted_iota(jnp.int32, (v_w, 1), 0) // RET_V
    head_ones = (ones_rows == v_head).astype(BF16)
    ci = lax.broadcasted_iota(jnp.int32, (c, RET_HEADS * c), 0).astype(F32)
    cj = (lax.broadcasted_iota(jnp.int32, (c, RET_HEADS * c), 1) % c).astype(F32)

    def make_step(d):
        lg_q = _log_sigmoid(dq_ref[d:d + 1, :])
        lg_v = _log_sigmoid(dv_ref[d:d + 1, :])
        lg_c = _log_sigmoid(dc_ref[d:d + 1, :])
        if d == 0:
            diff = ci - cj
            q_dec = jnp.exp(lg_q * (row_i + 1.0))
            k_dec = jnp.exp(lg_q * (c - 1.0 - row_i))
        else:
            diff = cj - ci
            q_dec = jnp.exp(lg_q * (c - row_i))
            k_dec = jnp.exp(lg_q * row_i)
        decay = jnp.where(diff >= 0.0, jnp.exp(lg_c * jnp.maximum(diff, 0.0)), 0.0)
        chunk_dec = jnp.exp(lg_v * float(c))

        def step(refs, r0, acc_r0, state):
            q_ref, k_ref, v_ref = refs
            q = q_ref[pl.ds(r0, c), :]
            k = k_ref[pl.ds(r0, c), :]
            v = v_ref[pl.ds(r0, c), :]
            kb, vb = k.astype(BF16), v.astype(BF16)
            k_bd = jnp.concatenate([jnp.where(m, kb, 0) for m in qk_masks], axis=0)
            scores = _dot_nt(q.astype(BF16), k_bd) * decay
            v_bd = jnp.concatenate([jnp.where(m, vb, 0) for m in v_masks], axis=0)
            inner = _dot(scores.astype(BF16), v_bd)
            cross = _dot((q * q_dec).astype(BF16), state.astype(BF16))
            kv = _dot_tn((k * k_dec).astype(BF16), vb)
            new_state = state * chunk_dec + kv * bd_mask
            acc_ref[d, pl.ds(acc_r0, c), :] = inner + cross
            return new_state

        return step

    fwd, bwd = make_step(0), make_step(1)

    def scan(n, acc_base, fwd_refs, bwd_refs, states):
        def body(i, st):
            rf = pl.multiple_of(i * c, c)
            rb = pl.multiple_of((n - 1 - i) * c, c)
            return fwd(fwd_refs, rf, acc_base + rf, st[0]), bwd(bwd_refs, rb, acc_base + rb, st[1])
        return lax.fori_loop(0, n, body, states, unroll=min(n, 4))

    zero = jnp.zeros((qk_w, v_w), F32)
    states = scan(n_ctx // c, n_lat, (qc_ref, kc_ref, vc_ref), (qc_ref, kc_ref, vc_ref), (zero, zero))
    scan(n_lat // c, 0, (ql_ref, kl_ref, vl_ref), (ql_ref, kl_ref, vl_ref), states)

    def readout(o_ref, gate_refs, acc_base):
        rows = NORM_ROWS

        def body(i, carry):
            r0 = pl.multiple_of(i * rows, rows)
            y = None
            for d in range(2):
                o = acc_ref[d, pl.ds(acc_base + r0, rows), :]
                o2 = o * o
                o2_hi = o2.astype(BF16)
                o2_lo = (o2 - o2_hi.astype(F32)).astype(BF16)
                ss = _dot(o2_hi, head_ones) + _dot(o2_lo, head_ones)
                yd = o * lax.rsqrt(ss * (1.0 / RET_V) + EPS) * ng_ref[d:d + 1, :] * gate_refs[d][pl.ds(r0, rows), :]
                y = yd if y is None else y + yd
            o_ref[pl.ds(r0, rows), :] = y.astype(BF16)
            return carry

        n = o_ref.shape[0] // rows
        lax.fori_loop(0, n, body, 0, unroll=min(n, 2))

    readout(ol_ref, (gfl_ref, gbl_ref), 0)
    readout(oc_ref, (gfc_ref, gbc_ref), n_lat)


def _retention(a32_lat, a32_ctx, smalls, layer):
    b, n_lat, _ = a32_lat.shape
    n_ctx = a32_ctx.shape[1]

    def seq_specs(n):
        spec = lambda off, w: pl.BlockSpec((None, n, w), lambda i: (i, 0, off // w))
        return [spec(A32_RQ, 128), spec(A32_RK, 128), spec(A32_RV, 256), spec(A32_GF, 256), spec(A32_GB, 256)]

    out = lambda n: pl.BlockSpec((None, n, 256), lambda i: (i, 0, 0))
    return pl.pallas_call(
        _ret_kernel,
        grid=(b,),
        in_specs=seq_specs(n_lat) + seq_specs(n_ctx) + [_resident(a, layer) for a in smalls],
        out_specs=[out(n_lat), out(n_ctx)],
        out_shape=[jax.ShapeDtypeStruct((b, n_lat, 256), BF16), jax.ShapeDtypeStruct((b, n_ctx, 256), BF16)],
        scratch_shapes=[pltpu.VMEM((2, n_lat + n_ctx, 256), F32)],
        compiler_params=_cparams(1),
        name="retention",
    )(*([a32_lat] * 5), *([a32_ctx] * 5), *smalls)


def _out_mlp_kernel(x_ref, ya_ref, yb_ref, yc_ref, yd_ref, mod_ref, wo_ref, w1_ref, w2_ref, *rest, ff_chunk):
    o_ref = rest[-1]
    x = x_ref[...]
    gw = GROUP_W
    y = (_dot(ya_ref[...], wo_ref[0 * gw:1 * gw, :]) + _dot(yb_ref[...], wo_ref[1 * gw:2 * gw, :])
         + _dot(yc_ref[...], wo_ref[2 * gw:3 * gw, :]) + _dot(yd_ref[...], wo_ref[3 * gw:4 * gw, :]))
    x1 = x + mod_ref[2:3, :] * y
    h = (_rms(x1) * (1.0 + mod_ref[4:5, :]) + mod_ref[3:4, :]).astype(BF16)
    acc = jnp.zeros_like(x1)
    for j in range(w1_ref.shape[1] // ff_chunk):
        a = jnp.maximum(_dot(h, w1_ref[:, j * ff_chunk:(j + 1) * ff_chunk]), 0.0)
        acc = acc + _dot((a * a).astype(BF16), w2_ref[j * ff_chunk:(j + 1) * ff_chunk, :])
    x2 = x1 + mod_ref[5:6, :] * acc
    o_ref[...] = x2 if len(rest) == 1 else _rms(x2) * rest[0][...]


def _out_mlp(xseq, ys, mods, weights, layer, tm, in_place, final_gain=None):
    b, n, d = xseq.shape
    per_batch = mods.shape[0] == b
    tok = lambda w: pl.BlockSpec((None, tm, w), lambda i, j: (i, j, 0))
    in_specs = [tok(d)] + [tok(GROUP_W)] * 4
    in_specs += [pl.BlockSpec((None, N_MOD, d), lambda i, j: (i if per_batch else 0, 0, 0))]
    in_specs += [_resident(w, layer) for w in weights]
    args = [xseq, *ys, mods, *weights]
    if final_gain is not None:
        in_specs.append(_resident(final_gain))
        args.append(final_gain)
    return pl.pallas_call(
        functools.partial(_out_mlp_kernel, ff_chunk=1024),
        grid=(b, n // tm),
        in_specs=in_specs,
        out_specs=tok(d),
        out_shape=jax.ShapeDtypeStruct((b, n, d), F32),
        input_output_aliases={0: 0} if in_place else {},
        compiler_params=_cparams(2),
        name="out_mlp",
    )(*args)


def _rope_tables(n_lat):
    n = np.arange(n_lat)
    row, col = (n // GRID_W).astype(np.float64), (n % GRID_W).astype(np.float64)

    def table(groups):
        cos = np.ones((n_lat, LANES)); up = np.zeros((n_lat, LANES)); dn = np.zeros((n_lat, LANES))
        for rot_start, rot_dim in groups:
            half = rot_dim // 2
            q = half // 2
            freqs = ROPE_BASE ** (-np.arange(q, dtype=np.float64) / q)
            for axis, pos in enumerate((row, col)):
                ang = pos[:, None] * freqs[None, :]
                base = rot_start + axis * half
                cos[:, base:base + q] = np.cos(ang)
                cos[:, base + q:base + 2 * q] = np.cos(ang)
                up[:, base:base + q] = -np.sin(ang)
                dn[:, base + q:base + 2 * q] = np.sin(ang)
        return jnp.asarray(np.stack([cos, up, dn]).astype(np.float32))

    return table([(0, GQA_HEAD_DIM), (GQA_HEAD_DIM, GQA_HEAD_DIM)]), table([(MLA_NOPE, MLA_ROPE)])


def _layout_w_in(w_in):
    lead = w_in.shape[:-1]
    kpe_src = OFF_CKV + MLA_KV_RANK
    w = w_in.astype(BF16)
    out = jnp.concatenate([w[..., :kpe_src], jnp.zeros(lead + (MLA_NOPE,), BF16),
                           w[..., kpe_src:kpe_src + MLA_ROPE],
                           jnp.zeros(lead + (LANES - MLA_NOPE - MLA_ROPE,), BF16),
                           w[..., kpe_src + MLA_ROPE:]], axis=-1)
    assert out.shape[-1] == N_IN
    return out


def _layout_w_uq(w_uq):
    lead = w_uq.shape[:-1]
    w = w_uq.reshape(lead + (MLA_HEADS, MLA_NOPE + MLA_ROPE))
    w = jnp.pad(w, ((0, 0),) * (w.ndim - 1) + ((0, LANES - MLA_NOPE - MLA_ROPE),))
    return w.reshape(lead + (MLA_HEADS * LANES,)).astype(BF16)


def _layout_w_ukv(w_ukv):
    lead = w_ukv.shape[:-1]
    w = w_ukv.reshape(lead + (MLA_HEADS, MLA_NOPE + MLA_V))
    pad = lambda a: jnp.pad(a, ((0, 0),) * (a.ndim - 1) + ((0, LANES - a.shape[-1]),)).reshape(
        lead + (MLA_HEADS * LANES,))
    return jnp.concatenate([pad(w[..., :MLA_NOPE]), pad(w[..., MLA_NOPE:])], axis=-1).astype(BF16)


def kernel(x, c, ctx, c_ctx, w_mod, b_mod, w_in, w_out, conv_dw, conv_b, conv_ln_g, conv_ln_b, conv_pw, mla_q_g, mla_kv_g, mla_uq, mla_ukv, gqa_q_g, gqa_k_g, ret_decay, ret_norm_g, mlp_w1, mlp_w2, final_g):
    b, n_lat, d = x.shape
    n_ctx = ctx.shape[1]
    depth = w_mod.shape[0]
    assert n_lat % LAT_TILE == 0 and n_lat % PROJ_TILE == 0 and n_lat % KEY_CHUNK == 0 and n_ctx % KEY_CHUNK == 0
    assert n_ctx % NORM_ROWS == 0 and n_lat % NORM_ROWS == 0 and NORM_ROWS % (2 * RET_CHUNK) == 0

    rows = ((b + 1 + SUBLANES - 1) // SUBLANES) * SUBLANES
    cond = jnp.zeros((rows, d), F32).at[:b].set(c).at[b].set(c_ctx)
    mod_all = _adaln(cond, w_mod, b_mod).reshape(depth, rows, N_MOD, d)
    tables = _rope_tables(n_lat)
    mla = (A16_MQ, 4 * LANES, A16_MK, A16_MV, 4 * LANES, (0, 1, 2, 3), (0, 1, 2, 3))
    gqa = (A16_GQ, 2 * LANES, A16_GK, A16_GV, 2 * LANES, (0, 0, 1, 1), (0, 0, 1, 1))

    row = lambda a: a[:, None, :]
    w_proj = (_layout_w_in(w_in), _layout_w_uq(mla_uq), _layout_w_ukv(mla_ukv), row(mla_q_g), row(mla_kv_g),
              row(jnp.tile(gqa_q_g, (1, 2))), row(jnp.tile(gqa_k_g, (1, 2))))
    w_conv = (conv_dw, row(conv_b), row(conv_ln_g), row(conv_ln_b), conv_pw.astype(BF16))
    dec = ret_decay.astype(F32)
    w_ret = (jnp.repeat(dec, RET_QK, axis=2), jnp.repeat(dec, RET_V, axis=2), jnp.repeat(dec, RET_CHUNK, axis=2),
             ret_norm_g.reshape(depth, 2, RET_HEADS * RET_V))
    w_mlp = (w_out.astype(BF16), mlp_w1.astype(BF16), mlp_w2.astype(BF16))

    cx = ctx
    for layer in range(depth):
        last = layer == depth - 1
        mods_lat, mods_ctx = mod_all[layer, :b], mod_all[layer, b:b + 1]
        a16_lat, a32_lat = _in_proj(x, mods_lat, w_proj, layer, tables, PROJ_TILE)
        a16_ctx, a32_ctx = _in_proj(cx, mods_ctx, w_proj, layer, None, n_ctx)
        ya_ctx = _conv(a32_ctx, w_conv, layer)
        yb_lat, = _attention_lat(a16_lat, a16_ctx, *mla, "mla_attn")
        yc_lat, ya_lat = _attention_lat(a16_lat, a16_ctx, *gqa, "gqa_attn", conv=(a32_lat, w_conv, layer))
        yd_lat, yd_ctx = _retention(a32_lat, a32_ctx, w_ret, layer)
        x = _out_mlp(x, (ya_lat, yb_lat, yc_lat, yd_lat), mods_lat, w_mlp, layer, LAT_TILE, layer > 0,
                     final_gain=final_g[None, :] if last else None)
        if not last:
            yb_ctx = _attention_ctx(a16_ctx, *mla, "mla_attn_ctx")
            yc_ctx = _attention_ctx(a16_ctx, *gqa, "gqa_attn_ctx")
            cx = _out_mlp(cx, (ya_ctx, yb_ctx, yc_ctx, yd_ctx), mods_ctx, w_mlp, layer, n_ctx, layer > 0)
    return x
```

```python
import functools

import numpy as np
import jax
import jax.numpy as jnp
from jax import lax
from jax.experimental import pallas as pl
from jax.experimental.pallas import tpu as pltpu

F32 = jnp.float32
BF16 = jnp.bfloat16

GRID_W = 64
N_MOD = 6
EPS = 1e-6
ROPE_BASE = 10000.0
GROUP_W = 256
CONV_WIDTH = 31
CONV_PAD = 16
MLA_HEADS = 4
MLA_NOPE = 64
MLA_ROPE = 32
MLA_V = 64
MLA_Q_RANK = 256
MLA_KV_RANK = 128
GQA_HEADS = 4
GQA_KV_HEADS = 2
GQA_HEAD_DIM = 64
RET_HEADS = 4
RET_QK = 32
RET_V = 64
RET_CHUNK = 128
LANES = 128
SUBLANES = 8
LOG2_E = 1.4426950408889634
LAT_TILE = 512
PROJ_TILE = 1024
KEY_CHUNK = 256
NORM_ROWS = 256
VMEM_LIMIT = 56 * 1024 * 1024

OFF_A, OFF_CQ, OFF_CKV, OFF_GQ, OFF_GK = 0, 512, 768, 1024, 1280
OFF_RQ, OFF_RV, OFF_GF, OFF_GB, N_IN = 1536, 1792, 2048, 2304, 2560
A16_MQ, A16_MK, A16_MV, A16_GK, A16_GQ, A16_GV, A16_W = 0, 512, 1024, 1536, 2048, 2304, 2560
A32_Y, A32_RQ, A32_RK, A32_RV, A32_GF, A32_GB, A32_W = 0, 256, 384, 512, 768, 1024, 1280


def _cparams(n_axes):
    return pltpu.CompilerParams(dimension_semantics=("arbitrary",) * n_axes,
                                vmem_limit_bytes=VMEM_LIMIT)


def _resident(a, layer=None):
    if layer is None:
        return pl.BlockSpec(a.shape, lambda *_: (0,) * a.ndim, pipeline_mode=pl.Buffered(1))
    return pl.BlockSpec((None,) + a.shape[1:], lambda *_: (layer,) + (0,) * (a.ndim - 1),
                        pipeline_mode=pl.Buffered(1))


def _rms(x):
    return x * lax.rsqrt(jnp.mean(x * x, axis=-1, keepdims=True) + EPS)


def _sigmoid(x):
    return 1.0 / (1.0 + jnp.exp(-x))


def _silu(x):
    return x * _sigmoid(x)


def _dot(a, b):
    return jnp.dot(a, b, preferred_element_type=F32)


def _dot_nt(a, b):
    return lax.dot_general(a, b, (((1,), (1,)), ((), ())), preferred_element_type=F32)


def _dot_tn(a, b):
    return lax.dot_general(a, b, (((0,), (0,)), ((), ())), preferred_element_type=F32)


def _rope(x, cos, sin_up, sin_dn, shift):
    n = x.shape[-1]
    return x * cos + pltpu.roll(x, n - shift, 1) * sin_up + pltpu.roll(x, shift, 1) * sin_dn


def _adaln_kernel(cond_ref, w_ref, b_ref, o_ref):
    cond = cond_ref[...]
    o_ref[...] = _dot(_silu(cond).astype(BF16), w_ref[...].astype(BF16)) + b_ref[...]


def _adaln(cond, w_mod, b_mod):
    depth, d, n = w_mod.shape
    r = cond.shape[0]
    tn = 1536
    return pl.pallas_call(
        _adaln_kernel,
        grid=(depth, n // tn),
        in_specs=[pl.BlockSpec((r, d), lambda l, j: (0, 0)),
                  pl.BlockSpec((None, d, tn), lambda l, j: (l, 0, j)),
                  pl.BlockSpec((None, 1, tn), lambda l, j: (l, 0, j))],
        out_specs=pl.BlockSpec((None, r, tn), lambda l, j: (l, 0, j)),
        out_shape=jax.ShapeDtypeStruct((depth, r, n), F32),
        compiler_params=_cparams(2),
        name="adaln",
    )(cond, w_mod, b_mod.reshape(depth, 1, n))


def _in_proj_kernel(x_ref, mod_ref, w_in_ref, w_uq_ref, w_ukv_ref, g_mq_ref, g_mkv_ref, g_gq_ref, g_gk_ref,
                    *rest, rotary, sub_tiles):
    a16_ref, a32_ref = rest[-2:]
    lo = lax.broadcasted_iota(jnp.int32, (1, LANES), 1) < (LANES // 2)
    q_scale = (MLA_NOPE + MLA_ROPE) ** -0.5 * LOG2_E
    n_rows = x_ref.shape[0] // sub_tiles

    for st in range(sub_tiles):
        rows = slice(st * n_rows, (st + 1) * n_rows)
        h = (_rms(x_ref[rows, :]) * (1.0 + mod_ref[1:2, :]) + mod_ref[0:1, :]).astype(BF16)

        def proj(off, width, h=h):
            return _dot(h, w_in_ref[:, off:off + width])

        def put16(off, val, rows=rows):
            a16_ref[rows, off:off + val.shape[1]] = val.astype(BF16)

        def put32(off, val, rows=rows):
            a32_ref[rows, off:off + val.shape[1]] = val

        if rotary:
            tg_ref, tm_ref = rest[:2]
            rope_g = lambda v, rows=rows: _rope(v, tg_ref[0, rows, :], tg_ref[1, rows, :], tg_ref[2, rows, :],
                                                GQA_HEAD_DIM // 4)
            rope_m = lambda v, rows=rows: _rope(v, tm_ref[0, rows, :], tm_ref[1, rows, :], tm_ref[2, rows, :],
                                                MLA_ROPE // 4)
        else:
            rope_g = rope_m = lambda v: v

        def pair_norm_rope(xp, gain, rope_g=rope_g):
            x2 = xp * xp
            s_lo = jnp.sum(jnp.where(lo, x2, 0.0), axis=-1, keepdims=True)
            s_hi = jnp.sum(jnp.where(lo, 0.0, x2), axis=-1, keepdims=True)
            r = jnp.where(lo, lax.rsqrt(s_lo * (1.0 / GQA_HEAD_DIM) + EPS),
                          lax.rsqrt(s_hi * (1.0 / GQA_HEAD_DIM) + EPS))
            return rope_g(xp * r * gain)

        cq = (_rms(proj(OFF_CQ, MLA_Q_RANK)) * g_mq_ref[...]).astype(BF16)
        ckv_kpe = proj(OFF_CKV, 2 * LANES)
        ckv = (_rms(ckv_kpe[:, :LANES]) * g_mkv_ref[...]).astype(BF16)
        gq = proj(OFF_GQ, GQA_HEADS * GQA_HEAD_DIM)
        gkv = proj(OFF_GK, 2 * LANES)

        a = proj(OFF_A, 2 * GROUP_W)
        put32(A32_Y, a[:, :GROUP_W] * _sigmoid(a[:, GROUP_W:]))
        rqk = proj(OFF_RQ, 2 * LANES)
        put32(A32_RQ, rqk[:, :LANES])
        put32(A32_RK, rqk[:, LANES:] * RET_QK ** -0.5)
        put32(A32_RV, proj(OFF_RV, RET_HEADS * RET_V))
        put32(A32_GF, _silu(proj(OFF_GF, GROUP_W)))
        put32(A32_GB, _silu(proj(OFF_GB, GROUP_W)))

        q = _dot(cq, w_uq_ref[...])
        for hd in range(MLA_HEADS):
            put16(A16_MQ + hd * LANES, rope_m(q[:, hd * LANES:(hd + 1) * LANES]) * q_scale)
        kv = _dot(ckv, w_ukv_ref[...])
        kpe = rope_m(ckv_kpe[:, LANES:])
        for hd in range(MLA_HEADS):
            put16(A16_MK + hd * LANES, kv[:, hd * LANES:(hd + 1) * LANES] + kpe)
            put16(A16_MV + hd * LANES,
                  jnp.where(lo, kv[:, (MLA_HEADS + hd) * LANES:(MLA_HEADS + hd + 1) * LANES], 1.0))

        for pr in range(2):
            qp = pair_norm_rope(gq[:, pr * LANES:(pr + 1) * LANES], g_gq_ref[...])
            put16(A16_GQ + pr * LANES, qp * (GQA_HEAD_DIM ** -0.5 * LOG2_E))
        kp = pair_norm_rope(gkv[:, :LANES], g_gk_ref[...])
        kp_sw = pltpu.roll(kp, LANES // 2, 1)
        put16(A16_GK + 0 * LANES, jnp.where(lo, kp, 0.0))
        put16(A16_GK + 1 * LANES, jnp.where(lo, 0.0, kp_sw))
        put16(A16_GK + 2 * LANES, jnp.where(lo, kp_sw, 0.0))
        put16(A16_GK + 3 * LANES, jnp.where(lo, 0.0, kp))
        vp = gkv[:, LANES:]
        put16(A16_GV + 0 * LANES, jnp.where(lo, vp, 1.0))
        put16(A16_GV + 1 * LANES, jnp.where(lo, pltpu.roll(vp, LANES // 2, 1), 1.0))


def _in_proj(xseq, mods, weights, layer, tables, tm):
    b, n, d = xseq.shape
    per_batch = mods.shape[0] == b
    tok = lambda w: pl.BlockSpec((None, tm, w), lambda i, j: (i, j, 0))
    in_specs = [tok(d), pl.BlockSpec((None, N_MOD, d), lambda i, j: (i if per_batch else 0, 0, 0))]
    in_specs += [_resident(w, layer) for w in weights]
    args = [xseq, mods, *weights]
    if tables is not None:
        in_specs += [pl.BlockSpec((3, tm, LANES), lambda i, j: (0, j, 0))] * 2
        args += list(tables)
    return pl.pallas_call(
        functools.partial(_in_proj_kernel, rotary=tables is not None, sub_tiles=tm // 256),
        grid=(b, n // tm),
        in_specs=in_specs,
        out_specs=[tok(A16_W), tok(A32_W)],
        out_shape=[jax.ShapeDtypeStruct((b, n, A16_W), BF16), jax.ShapeDtypeStruct((b, n, A32_W), F32)],
        compiler_params=_cparams(2),
        name="in_proj",
    )(*args)


CONV_ROWS = 64
CONV_FIRST = CONV_PAD - CONV_WIDTH // 2
CONV_TAIL = ((CONV_FIRST + CONV_WIDTH - 1) // SUBLANES) * SUBLANES


def _conv_steps(sh_ref, dw_ref, b_ref, lg_ref, lb_ref, pw_ref, o_ref, length):
    def fill(before, body, after):
        sh_ref[0, 0:CONV_PAD, :] = before
        sh_ref[0, CONV_PAD:CONV_PAD + length, :] = body
        sh_ref[0, CONV_PAD + length:2 * CONV_PAD + length, :] = after

    def shift(base, n=CONV_ROWS):
        win = sh_ref[0, pl.ds(base, n + SUBLANES), :]
        for r in range(1, SUBLANES):
            sh_ref[r, pl.ds(base, n), :] = win[r:r + n, :]

    def shift_tail():
        shift(length, CONV_TAIL)

    def taps(base):
        acc = jnp.zeros((CONV_ROWS, GROUP_W), F32) + b_ref[...]
        for k in range(CONV_WIDTH):
            off = CONV_FIRST + k
            slab = sh_ref[off % SUBLANES, pl.ds(base + (off // SUBLANES) * SUBLANES, CONV_ROWS), :]
            acc = acc + slab * dw_ref[k:k + 1, :]
        sh_ref[0, pl.ds(base, CONV_ROWS), :] = acc

    def finish(base):
        acc = sh_ref[0, pl.ds(base, NORM_ROWS), :]
        mu = jnp.mean(acc, axis=-1, keepdims=True)
        cen = acc - mu
        var = jnp.mean(cen * cen, axis=-1, keepdims=True)
        z = _silu(cen * lax.rsqrt(var + EPS) * lg_ref[...] + lb_ref[...])
        o_ref[pl.ds(base, NORM_ROWS), :] = _dot(z.astype(BF16), pw_ref[...]).astype(BF16)

    return fill, shift, shift_tail, taps, finish


def _conv_kernel(y_ref, dw_ref, b_ref, lg_ref, lb_ref, pw_ref, o_ref, sh_ref):
    length = y_ref.shape[0]
    fill, shift, shift_tail, taps, finish = _conv_steps(sh_ref, dw_ref, b_ref, lg_ref, lb_ref, pw_ref, o_ref, length)
    zeros = jnp.zeros((CONV_PAD, GROUP_W), F32)
    fill(zeros, y_ref[...], zeros)

    def loop(step, rows, **kw):
        def body(c, carry):
            step(pl.multiple_of(c * rows, rows))
            return carry
        lax.fori_loop(0, length // rows, body, 0, **kw)

    loop(shift, CONV_ROWS)
    shift_tail()
    loop(taps, CONV_ROWS)
    loop(finish, NORM_ROWS, unroll=min(length // NORM_ROWS, 4))


def _conv(a32, weights, layer):
    b, n, _ = a32.shape
    return pl.pallas_call(
        _conv_kernel,
        grid=(b,),
        in_specs=[pl.BlockSpec((None, n, GROUP_W), lambda i: (i, 0, A32_Y // GROUP_W))]
        + [_resident(w, layer) for w in weights],
        out_specs=pl.BlockSpec((None, n, GROUP_W), lambda i: (i, 0, 0)),
        out_shape=jax.ShapeDtypeStruct((b, n, GROUP_W), BF16),
        scratch_shapes=[pltpu.VMEM((SUBLANES, n + 2 * CONV_PAD, GROUP_W), F32)],
        compiler_params=_cparams(1),
        name="conv",
    )(a32, *weights)


def _normalized_pair(acc_even, acc_odd):
    half = LANES // 2
    lo = lax.broadcasted_iota(jnp.int32, (1, LANES), 1) < half
    return jnp.where(lo, acc_even / pltpu.roll(acc_even, half, 1), pltpu.roll(acc_odd, half, 1) / acc_odd)


def _attn_lat_kernel(q_ref, qn_ref, kl_ref, kc_ref, vl_ref, vc_ref, o_ref, s_ref, m_ref, mb_ref, acc_ref,
                     *, q_blocks, v_blocks):
    tq = q_ref.shape[0]
    tk = KEY_CHUNK
    chunks = [(kl_ref, vl_ref, c * tk) for c in range(kl_ref.shape[0] // tk)]
    chunks += [(kc_ref, vc_ref, c * tk) for c in range(kc_ref.shape[0] // tk)]

    def scores_chunk(hd, ci, ref):
        slot = hd % 2
        k_ref, _, r0 = chunks[ci]
        q = ref[:, q_blocks[hd] * LANES:(q_blocks[hd] + 1) * LANES]
        s = _dot_nt(q, k_ref[r0:r0 + tk, hd * LANES:(hd + 1) * LANES])
        s_ref[slot, :, ci * tk:(ci + 1) * tk] = s
        mx = s[:, 0:LANES]
        for i in range(1, tk // LANES):
            mx = jnp.maximum(mx, s[:, i * LANES:(i + 1) * LANES])
        m_ref[slot] = mx if ci == 0 else jnp.maximum(m_ref[slot], mx)

    def scores_finish(hd):
        slot = hd % 2
        mb_ref[slot] = jnp.broadcast_to(jnp.max(m_ref[slot], axis=-1, keepdims=True), (tq, LANES))

    def values_chunk(hd, ci):
        slot = hd % 2
        _, v_ref, r0 = chunks[ci]
        mb = mb_ref[slot]
        p = jnp.concatenate(
            [jnp.exp2(s_ref[slot, :, ci * tk + i * LANES:ci * tk + (i + 1) * LANES] - mb).astype(BF16)
             for i in range(tk // LANES)], axis=1)
        pv = _dot(p, v_ref[r0:r0 + tk, v_blocks[hd] * LANES:(v_blocks[hd] + 1) * LANES])
        acc_ref[slot] = pv if ci == 0 else acc_ref[slot] + pv

    n = len(chunks)

    def scores_ahead(unit, ci):
        scores_chunk(unit % 4, ci, q_ref if unit < 4 else qn_ref)

    @pl.when(pl.program_id(1) == 0)
    def _():
        for ci in range(n):
            scores_chunk(0, ci, q_ref)
        scores_finish(0)
        scores_chunk(1, 0, q_ref)

    for hd in range(4):
        for ci in range(n):
            values_chunk(hd, ci)
            if ci + 1 < n:
                scores_ahead(hd + 1, ci + 1)
            else:
                scores_ahead(hd + 2, 0)
            if ci + 2 == n:
                scores_finish((hd + 1) % 4)
        if hd % 2 == 1:
            pr = hd // 2
            o_ref[:, pr * LANES:(pr + 1) * LANES] = _normalized_pair(acc_ref[0], acc_ref[1]).astype(BF16)


def _attention_lat(a16_lat, a16_ctx, q_off, q_width, k_off, v_off, v_width, q_blocks, v_blocks, name):
    b, n_lat, _ = a16_lat.shape
    n_ctx = a16_ctx.shape[1]
    tq = LAT_TILE
    n_tiles = n_lat // tq
    kw = 4 * LANES
    qc, kc, vc = q_off // q_width, k_off // kw, v_off // v_width
    return pl.pallas_call(
        functools.partial(_attn_lat_kernel, q_blocks=q_blocks, v_blocks=v_blocks),
        grid=(b, n_tiles),
        in_specs=[pl.BlockSpec((None, tq, q_width), lambda i, j: (i, j, qc)),
                  pl.BlockSpec((None, tq, q_width), lambda i, j: (i, jnp.minimum(j + 1, n_tiles - 1), qc)),
                  pl.BlockSpec((None, n_lat, kw), lambda i, j: (i, 0, kc)),
                  pl.BlockSpec((None, n_ctx, kw), lambda i, j: (i, 0, kc)),
                  pl.BlockSpec((None, n_lat, v_width), lambda i, j: (i, 0, vc)),
                  pl.BlockSpec((None, n_ctx, v_width), lambda i, j: (i, 0, vc))],
        out_specs=pl.BlockSpec((None, tq, 2 * LANES), lambda i, j: (i, j, 0)),
        out_shape=jax.ShapeDtypeStruct((b, n_lat, 2 * LANES), BF16),
        scratch_shapes=[pltpu.VMEM((2, tq, n_lat + n_ctx), F32), pltpu.VMEM((2, tq, LANES), F32),
                        pltpu.VMEM((2, tq, LANES), F32), pltpu.VMEM((2, tq, LANES), F32)],
        compiler_params=_cparams(2),
        name=name,
    )(a16_lat, a16_lat, a16_lat, a16_ctx, a16_lat, a16_ctx)


def _attn_ctx_kernel(q_ref, k_ref, v_ref, o_ref, *, q_blocks, v_blocks):
    for pr in range(2):
        accs = []
        for half in range(2):
            hd = 2 * pr + half
            q = q_ref[:, q_blocks[hd] * LANES:(q_blocks[hd] + 1) * LANES]
            s = _dot_nt(q, k_ref[:, hd * LANES:(hd + 1) * LANES])
            p = jnp.exp2(s - jnp.max(s, axis=-1, keepdims=True))
            accs.append(_dot(p.astype(BF16), v_ref[:, v_blocks[hd] * LANES:(v_blocks[hd] + 1) * LANES]))
        o_ref[:, pr * LANES:(pr + 1) * LANES] = _normalized_pair(*accs).astype(BF16)


def _attention_ctx(a16_ctx, q_off, q_width, k_off, v_off, v_width, q_blocks, v_blocks, name):
    b, n_ctx, _ = a16_ctx.shape
    kw = 4 * LANES
    qc, kc, vc = q_off // q_width, k_off // kw, v_off // v_width
    return pl.pallas_call(
        functools.partial(_attn_ctx_kernel, q_blocks=q_blocks, v_blocks=v_blocks),
        grid=(b,),
        in_specs=[pl.BlockSpec((None, n_ctx, q_width), lambda i: (i, 0, qc)),
                  pl.BlockSpec((None, n_ctx, kw), lambda i: (i, 0, kc)),
                  pl.BlockSpec((None, n_ctx, v_width), lambda i: (i, 0, vc))],
        out_specs=pl.BlockSpec((None, n_ctx, 2 * LANES), lambda i: (i, 0, 0)),
        out_shape=jax.ShapeDtypeStruct((b, n_ctx, 2 * LANES), BF16),
        compiler_params=_cparams(1),
        name=name,
    )(a16_ctx, a16_ctx, a16_ctx)


def _log_sigmoid(x):
    return jnp.minimum(x, 0.0) - jnp.log(1.0 + jnp.exp(-jnp.abs(x)))


def _ret_kernel(ql_ref, kl_ref, vl_ref, gfl_ref, gbl_ref, qc_ref, kc_ref, vc_ref, gfc_ref, gbc_ref,
                dq_ref, dv_ref, dc_ref, ng_ref, ol_ref, oc_ref, acc_ref):
    c = RET_CHUNK
    n_lat, n_ctx = ql_ref.shape[0], qc_ref.shape[0]
    qk_w = RET_HEADS * RET_QK
    v_w = RET_HEADS * RET_V

    row_i = lax.broadcasted_iota(jnp.int32, (c, 1), 0).astype(F32)
    qk_head = lax.broadcasted_iota(jnp.int32, (1, qk_w), 1) // RET_QK
    v_head = lax.broadcasted_iota(jnp.int32, (1, v_w), 1) // RET_V
    qk_masks = [qk_head == hd for hd in range(RET_HEADS)]
    v_masks = [v_head == hd for hd in range(RET_HEADS)]
    state_rows = lax.broadcasted_iota(jnp.int32, (qk_w, 1), 0) // RET_QK
    bd_mask = (state_rows == v_head).astype(F32)
    ones_rows = lax.broadcasted_iota(jnp.int32, (v_w, 1), 0) // RET_V
    head_ones = (ones_rows == v_head).astype(BF16)
    ci = lax.broadcasted_iota(jnp.int32, (c, RET_HEADS * c), 0).astype(F32)
    cj = (lax.broadcasted_iota(jnp.int32, (c, RET_HEADS * c), 1) % c).astype(F32)

    def make_step(d):
        lg_q = _log_sigmoid(dq_ref[d:d + 1, :])
        lg_v = _log_sigmoid(dv_ref[d:d + 1, :])
        lg_c = _log_sigmoid(dc_ref[d:d + 1, :])
        if d == 0:
            diff = ci - cj
            q_dec = jnp.exp(lg_q * (row_i + 1.0))
            k_dec = jnp.exp(lg_q * (c - 1.0 - row_i))
        else:
            diff = cj - ci
            q_dec = jnp.exp(lg_q * (c - row_i))
            k_dec = jnp.exp(lg_q * row_i)
        decay = jnp.where(diff >= 0.0, jnp.exp(lg_c * jnp.maximum(diff, 0.0)), 0.0)
        chunk_dec = jnp.exp(lg_v * float(c))

        def step(refs, r0, acc_r0, state):
            q_ref, k_ref, v_ref = refs
            q = q_ref[pl.ds(r0, c), :]
            k = k_ref[pl.ds(r0, c), :]
            v = v_ref[pl.ds(r0, c), :]
            kb, vb = k.astype(BF16), v.astype(BF16)
            k_bd = jnp.concatenate([jnp.where(m, kb, 0) for m in qk_masks], axis=0)
            scores = _dot_nt(q.astype(BF16), k_bd) * decay
            v_bd = jnp.concatenate([jnp.where(m, vb, 0) for m in v_masks], axis=0)
            inner = _dot(scores.astype(BF16), v_bd)
            cross = _dot((q * q_dec).astype(BF16), state.astype(BF16))
            kv = _dot_tn((k * k_dec).astype(BF16), vb)
            new_state = state * chunk_dec + kv * bd_mask
            acc_ref[d, pl.ds(acc_r0, c), :] = inner + cross
            return new_state

        return step

    fwd, bwd = make_step(0), make_step(1)

    def scan(n, acc_base, fwd_refs, bwd_refs, states):
        def body(i, st):
            rf = pl.multiple_of(i * c, c)
            rb = pl.multiple_of((n - 1 - i) * c, c)
            return fwd(fwd_refs, rf, acc_base + rf, st[0]), bwd(bwd_refs, rb, acc_base + rb, st[1])
        return lax.fori_loop(0, n, body, states, unroll=min(n, 4))

    zero = jnp.zeros((qk_w, v_w), F32)
    states = scan(n_ctx // c, n_lat, (qc_ref, kc_ref, vc_ref), (qc_ref, kc_ref, vc_ref), (zero, zero))
    scan(n_lat // c, 0, (ql_ref, kl_ref, vl_ref), (ql_ref, kl_ref, vl_ref), states)

    def readout(o_ref, gate_refs, acc_base):
        rows = NORM_ROWS

        def body(i, carry):
            r0 = pl.multiple_of(i * rows, rows)
            y = None
            for d in range(2):
                o = acc_ref[d, pl.ds(acc_base + r0, rows), :]
                o2 = o * o
                o2_hi = o2.astype(BF16)
                o2_lo = (o2 - o2_hi.astype(F32)).astype(BF16)
                ss = _dot(o2_hi, head_ones) + _dot(o2_lo, head_ones)
                yd = o * lax.rsqrt(ss * (1.0 / RET_V) + EPS) * ng_ref[d:d + 1, :] * gate_refs[d][pl.ds(r0, rows), :]
                y = yd if y is None else y + yd
            o_ref[pl.ds(r0, rows), :] = y.astype(BF16)
            return carry

        n = o_ref.shape[0] // rows
        lax.fori_loop(0, n, body, 0, unroll=min(n, 2))

    readout(ol_ref, (gfl_ref, gbl_ref), 0)
    readout(oc_ref, (gfc_ref, gbc_ref), n_lat)


def _retention(a32_lat, a32_ctx, smalls, layer):
    b, n_lat, _ = a32_lat.shape
    n_ctx = a32_ctx.shape[1]

    def seq_specs(n):
        spec = lambda off, w: pl.BlockSpec((None, n, w), lambda i: (i, 0, off // w))
        return [spec(A32_RQ, 128), spec(A32_RK, 128), spec(A32_RV, 256), spec(A32_GF, 256), spec(A32_GB, 256)]

    out = lambda n: pl.BlockSpec((None, n, 256), lambda i: (i, 0, 0))
    return pl.pallas_call(
        _ret_kernel,
        grid=(b,),
        in_specs=seq_specs(n_lat) + seq_specs(n_ctx) + [_resident(a, layer) for a in smalls],
        out_specs=[out(n_lat), out(n_ctx)],
        out_shape=[jax.ShapeDtypeStruct((b, n_lat, 256), BF16), jax.ShapeDtypeStruct((b, n_ctx, 256), BF16)],
        scratch_shapes=[pltpu.VMEM((2, n_lat + n_ctx, 256), F32)],
        compiler_params=_cparams(1),
        name="retention",
    )(*([a32_lat] * 5), *([a32_ctx] * 5), *smalls)


def _out_mlp_kernel(x_ref, ya_ref, yb_ref, yc_ref, yd_ref, mod_ref, wo_ref, w1_ref, w2_ref, *rest, ff_chunk):
    o_ref = rest[-1]
    x = x_ref[...]
    gw = GROUP_W
    y = (_dot(ya_ref[...], wo_ref[0 * gw:1 * gw, :]) + _dot(yb_ref[...], wo_ref[1 * gw:2 * gw, :])
         + _dot(yc_ref[...], wo_ref[2 * gw:3 * gw, :]) + _dot(yd_ref[...], wo_ref[3 * gw:4 * gw, :]))
    x1 = x + mod_ref[2:3, :] * y
    h = (_rms(x1) * (1.0 + mod_ref[4:5, :]) + mod_ref[3:4, :]).astype(BF16)
    acc = jnp.zeros_like(x1)
    for j in range(w1_ref.shape[1] // ff_chunk):
        a = jnp.maximum(_dot(h, w1_ref[:, j * ff_chunk:(j + 1) * ff_chunk]), 0.0)
        acc = acc + _dot((a * a).astype(BF16), w2_ref[j * ff_chunk:(j + 1) * ff_chunk, :])
    x2 = x1 + mod_ref[5:6, :] * acc
    o_ref[...] = x2 if len(rest) == 1 else _rms(x2) * rest[0][...]


def _out_mlp(xseq, ys, mods, weights, layer, tm, in_place, final_gain=None):
    b, n, d = xseq.shape
    per_batch = mods.shape[0] == b
    tok = lambda w: pl.BlockSpec((None, tm, w), lambda i, j: (i, j, 0))
    in_specs = [tok(d)] + [tok(GROUP_W)] * 4
    in_specs += [pl.BlockSpec((None, N_MOD, d), lambda i, j: (i if per_batch else 0, 0, 0))]
    in_specs += [_resident(w, layer) for w in weights]
    args = [xseq, *ys, mods, *weights]
    if final_gain is not None:
        in_specs.append(_resident(final_gain))
        args.append(final_gain)
    return pl.pallas_call(
        functools.partial(_out_mlp_kernel, ff_chunk=1024),
        grid=(b, n // tm),
        in_specs=in_specs,
        out_specs=tok(d),
        out_shape=jax.ShapeDtypeStruct((b, n, d), F32),
        input_output_aliases={0: 0} if in_place else {},
        compiler_params=_cparams(2),
        name="out_mlp",
    )(*args)


def _rope_tables(n_lat):
    n = np.arange(n_lat)
    row, col = (n // GRID_W).astype(np.float64), (n % GRID_W).astype(np.float64)

    def table(groups):
        cos = np.ones((n_lat, LANES)); up = np.zeros((n_lat, LANES)); dn = np.zeros((n_lat, LANES))
        for rot_start, rot_dim in groups:
            half = rot_dim // 2
            q = half // 2
            freqs = ROPE_BASE ** (-np.arange(q, dtype=np.float64) / q)
            for axis, pos in enumerate((row, col)):
                ang = pos[:, None] * freqs[None, :]
                base = rot_start + axis * half
                cos[:, base:base + q] = np.cos(ang)
                cos[:, base + q:base + 2 * q] = np.cos(ang)
                up[:, base:base + q] = -np.sin(ang)
                dn[:, base + q:base + 2 * q] = np.sin(ang)
        return jnp.asarray(np.stack([cos, up, dn]).astype(np.float32))

    return table([(0, GQA_HEAD_DIM), (GQA_HEAD_DIM, GQA_HEAD_DIM)]), table([(MLA_NOPE, MLA_ROPE)])


def _layout_w_in(w_in):
    lead = w_in.shape[:-1]
    kpe_src = OFF_CKV + MLA_KV_RANK
    w = w_in.astype(BF16)
    out = jnp.concatenate([w[..., :kpe_src], jnp.zeros(lead + (MLA_NOPE,), BF16),
                           w[..., kpe_src:kpe_src + MLA_ROPE],
                           jnp.zeros(lead + (LANES - MLA_NOPE - MLA_ROPE,), BF16),
                           w[..., kpe_src + MLA_ROPE:]], axis=-1)
    assert out.shape[-1] == N_IN
    return out


def _layout_w_uq(w_uq):
    lead = w_uq.shape[:-1]
    w = w_uq.reshape(lead + (MLA_HEADS, MLA_NOPE + MLA_ROPE))
    w = jnp.pad(w, ((0, 0),) * (w.ndim - 1) + ((0, LANES - MLA_NOPE - MLA_ROPE),))
    return w.reshape(lead + (MLA_HEADS * LANES,)).astype(BF16)


def _layout_w_ukv(w_ukv):
    lead = w_ukv.shape[:-1]
    w = w_ukv.reshape(lead + (MLA_HEADS, MLA_NOPE + MLA_V))
    pad = lambda a: jnp.pad(a, ((0, 0),) * (a.ndim - 1) + ((0, LANES - a.shape[-1]),)).reshape(
        lead + (MLA_HEADS * LANES,))
    return jnp.concatenate([pad(w[..., :MLA_NOPE]), pad(w[..., MLA_NOPE:])], axis=-1).astype(BF16)


def kernel(x, c, ctx, c_ctx, w_mod, b_mod, w_in, w_out, conv_dw, conv_b, conv_ln_g, conv_ln_b, conv_pw, mla_q_g, mla_kv_g, mla_uq, mla_ukv, gqa_q_g, gqa_k_g, ret_decay, ret_norm_g, mlp_w1, mlp_w2, final_g):
    b, n_lat, d = x.shape
    n_ctx = ctx.shape[1]
    depth = w_mod.shape[0]
    assert n_lat % LAT_TILE == 0 and n_lat % PROJ_TILE == 0 and n_lat % KEY_CHUNK == 0 and n_ctx % KEY_CHUNK == 0
    assert n_ctx % NORM_ROWS == 0 and n_lat % NORM_ROWS == 0 and NORM_ROWS % (2 * RET_CHUNK) == 0

    rows = ((b + 1 + SUBLANES - 1) // SUBLANES) * SUBLANES
    cond = jnp.zeros((rows, d), F32).at[:b].set(c).at[b].set(c_ctx)
    mod_all = _adaln(cond, w_mod, b_mod).reshape(depth, rows, N_MOD, d)
    tables = _rope_tables(n_lat)
    mla = (A16_MQ, 4 * LANES, A16_MK, A16_MV, 4 * LANES, (0, 1, 2, 3), (0, 1, 2, 3))
    gqa = (A16_GQ, 2 * LANES, A16_GK, A16_GV, 2 * LANES, (0, 0, 1, 1), (0, 0, 1, 1))

    row = lambda a: a[:, None, :]
    w_proj = (_layout_w_in(w_in), _layout_w_uq(mla_uq), _layout_w_ukv(mla_ukv), row(mla_q_g), row(mla_kv_g),
              row(jnp.tile(gqa_q_g, (1, 2))), row(jnp.tile(gqa_k_g, (1, 2))))
    w_conv = (conv_dw, row(conv_b), row(conv_ln_g), row(conv_ln_b), conv_pw.astype(BF16))
    dec = ret_decay.astype(F32)
    w_ret = (jnp.repeat(dec, RET_QK, axis=2), jnp.repeat(dec, RET_V, axis=2), jnp.repeat(dec, RET_CHUNK, axis=2),
             ret_norm_g.reshape(depth, 2, RET_HEADS * RET_V))
    w_mlp = (w_out.astype(BF16), mlp_w1.astype(BF16), mlp_w2.astype(BF16))

    cx = ctx
    for layer in range(depth):
        last = layer == depth - 1
        mods_lat, mods_ctx = mod_all[layer, :b], mod_all[layer, b:b + 1]
        a16_lat, a32_lat = _in_proj(x, mods_lat, w_proj, layer, tables, PROJ_TILE)
        a16_ctx, a32_ctx = _in_proj(cx, mods_ctx, w_proj, layer, None, n_ctx)
        ya_lat, ya_ctx = _conv(a32_lat, w_conv, layer), _conv(a32_ctx, w_conv, layer)
        yb_lat = _attention_lat(a16_lat, a16_ctx, *mla, "mla_attn")
        yc_lat = _attention_lat(a16_lat, a16_ctx, *gqa, "gqa_attn")
        yd_lat, yd_ctx = _retention(a32_lat, a32_ctx, w_ret, layer)
        x = _out_mlp(x, (ya_lat, yb_lat, yc_lat, yd_lat), mods_lat, w_mlp, layer, LAT_TILE, layer > 0,
                     final_gain=final_g[None, :] if last else None)
        if not last:
            yb_ctx = _attention_ctx(a16_ctx, *mla, "mla_attn_ctx")
            yc_ctx = _attention_ctx(a16_ctx, *gqa, "gqa_attn_ctx")
            cx = _out_mlp(cx, (ya_ctx, yb_ctx, yc_ctx, yd_ctx), mods_ctx, w_mlp, layer, n_ctx, layer > 0)
    return x
```

```python
import functools

import numpy as np
import jax
import jax.numpy as jnp
from jax import lax
from jax.experimental import pallas as pl
from jax.experimental.pallas import tpu as pltpu

F32 = jnp.float32
BF16 = jnp.bfloat16

GRID_W = 64
N_MOD = 6
EPS = 1e-6
ROPE_BASE = 10000.0
GROUP_W = 256
CONV_WIDTH = 31
CONV_PAD = 16
MLA_HEADS = 4
MLA_NOPE = 64
MLA_ROPE = 32
MLA_V = 64
MLA_Q_RANK = 256
MLA_KV_RANK = 128
GQA_HEADS = 4
GQA_KV_HEADS = 2
GQA_HEAD_DIM = 64
RET_HEADS = 4
RET_QK = 32
RET_V = 64
RET_CHUNK = 128
LANES = 128
SUBLANES = 8
LOG2_E = 1.4426950408889634
LAT_TILE = 512
PROJ_TILE = 1024
KEY_CHUNK = 256
NORM_ROWS = 256
VMEM_LIMIT = 56 * 1024 * 1024

OFF_A, OFF_CQ, OFF_CKV, OFF_GQ, OFF_GK = 0, 512, 768, 1024, 1280
OFF_RQ, OFF_RV, OFF_GF, OFF_GB, N_IN = 1536, 1792, 2048, 2304, 2560
A16_MQ, A16_MK, A16_MV, A16_GK, A16_GQ, A16_GV, A16_W = 0, 512, 1024, 1536, 2048, 2304, 2560
A32_Y, A32_RQ, A32_RK, A32_RV, A32_GF, A32_GB, A32_W = 0, 256, 384, 512, 768, 1024, 1280


def _cparams(n_axes):
    return pltpu.CompilerParams(dimension_semantics=("arbitrary",) * n_axes,
                                vmem_limit_bytes=VMEM_LIMIT)


def _resident(a, layer=None):
    if layer is None:
        return pl.BlockSpec(a.shape, lambda *_: (0,) * a.ndim, pipeline_mode=pl.Buffered(1))
    return pl.BlockSpec((None,) + a.shape[1:], lambda *_: (layer,) + (0,) * (a.ndim - 1),
                        pipeline_mode=pl.Buffered(1))


def _rms(x):
    return x * lax.rsqrt(jnp.mean(x * x, axis=-1, keepdims=True) + EPS)


def _sigmoid(x):
    return 1.0 / (1.0 + jnp.exp(-x))


def _silu(x):
    return x * _sigmoid(x)


def _dot(a, b):
    return jnp.dot(a, b, preferred_element_type=F32)


def _dot_nt(a, b):
    return lax.dot_general(a, b, (((1,), (1,)), ((), ())), preferred_element_type=F32)


def _dot_tn(a, b):
    return lax.dot_general(a, b, (((0,), (0,)), ((), ())), preferred_element_type=F32)


def _rope(x, cos, sin_up, sin_dn, shift):
    n = x.shape[-1]
    return x * cos + pltpu.roll(x, n - shift, 1) * sin_up + pltpu.roll(x, shift, 1) * sin_dn


def _adaln_kernel(cond_ref, w_ref, b_ref, o_ref):
    cond = cond_ref[...]
    o_ref[...] = _dot(_silu(cond).astype(BF16), w_ref[...].astype(BF16)) + b_ref[...]


def _adaln(cond, w_mod, b_mod):
    depth, d, n = w_mod.shape
    r = cond.shape[0]
    tn = 1536
    return pl.pallas_call(
        _adaln_kernel,
        grid=(depth, n // tn),
        in_specs=[pl.BlockSpec((r, d), lambda l, j: (0, 0)),
                  pl.BlockSpec((None, d, tn), lambda l, j: (l, 0, j)),
                  pl.BlockSpec((None, 1, tn), lambda l, j: (l, 0, j))],
        out_specs=pl.BlockSpec((None, r, tn), lambda l, j: (l, 0, j)),
        out_shape=jax.ShapeDtypeStruct((depth, r, n), F32),
        compiler_params=_cparams(2),
        name="adaln",
    )(cond, w_mod, b_mod.reshape(depth, 1, n))


def _in_proj_kernel(x_ref, mod_ref, w_in_ref, w_uq_ref, w_ukv_ref, g_mq_ref, g_mkv_ref, g_gq_ref, g_gk_ref,
                    *rest, rotary, sub_tiles):
    a16_ref, a32_ref = rest[-2:]
    lo = lax.broadcasted_iota(jnp.int32, (1, LANES), 1) < (LANES // 2)
    q_scale = (MLA_NOPE + MLA_ROPE) ** -0.5 * LOG2_E
    n_rows = x_ref.shape[0] // sub_tiles

    for st in range(sub_tiles):
        rows = slice(st * n_rows, (st + 1) * n_rows)
        h = (_rms(x_ref[rows, :]) * (1.0 + mod_ref[1:2, :]) + mod_ref[0:1, :]).astype(BF16)

        def proj(off, width, h=h):
            return _dot(h, w_in_ref[:, off:off + width])

        def put16(off, val, rows=rows):
            a16_ref[rows, off:off + val.shape[1]] = val.astype(BF16)

        def put32(off, val, rows=rows):
            a32_ref[rows, off:off + val.shape[1]] = val

        if rotary:
            tg_ref, tm_ref = rest[:2]
            rope_g = lambda v, rows=rows: _rope(v, tg_ref[0, rows, :], tg_ref[1, rows, :], tg_ref[2, rows, :],
                                                GQA_HEAD_DIM // 4)
            rope_m = lambda v, rows=rows: _rope(v, tm_ref[0, rows, :], tm_ref[1, rows, :], tm_ref[2, rows, :],
                                                MLA_ROPE // 4)
        else:
            rope_g = rope_m = lambda v: v

        def pair_norm_rope(xp, gain, rope_g=rope_g):
            x2 = xp * xp
            s_lo = jnp.sum(jnp.where(lo, x2, 0.0), axis=-1, keepdims=True)
            s_hi = jnp.sum(jnp.where(lo, 0.0, x2), axis=-1, keepdims=True)
            r = jnp.where(lo, lax.rsqrt(s_lo * (1.0 / GQA_HEAD_DIM) + EPS),
                          lax.rsqrt(s_hi * (1.0 / GQA_HEAD_DIM) + EPS))
            return rope_g(xp * r * gain)

        cq = (_rms(proj(OFF_CQ, MLA_Q_RANK)) * g_mq_ref[...]).astype(BF16)
        ckv_kpe = proj(OFF_CKV, 2 * LANES)
        ckv = (_rms(ckv_kpe[:, :LANES]) * g_mkv_ref[...]).astype(BF16)
        gq = proj(OFF_GQ, GQA_HEADS * GQA_HEAD_DIM)
        gkv = proj(OFF_GK, 2 * LANES)

        a = proj(OFF_A, 2 * GROUP_W)
        put32(A32_Y, a[:, :GROUP_W] * _sigmoid(a[:, GROUP_W:]))
        rqk = proj(OFF_RQ, 2 * LANES)
        put32(A32_RQ, rqk[:, :LANES])
        put32(A32_RK, rqk[:, LANES:] * RET_QK ** -0.5)
        put32(A32_RV, proj(OFF_RV, RET_HEADS * RET_V))
        put32(A32_GF, _silu(proj(OFF_GF, GROUP_W)))
        put32(A32_GB, _silu(proj(OFF_GB, GROUP_W)))

        q = _dot(cq, w_uq_ref[...])
        for hd in range(MLA_HEADS):
            put16(A16_MQ + hd * LANES, rope_m(q[:, hd * LANES:(hd + 1) * LANES]) * q_scale)
        kv = _dot(ckv, w_ukv_ref[...])
        kpe = rope_m(ckv_kpe[:, LANES:])
        for hd in range(MLA_HEADS):
            put16(A16_MK + hd * LANES, kv[:, hd * LANES:(hd + 1) * LANES] + kpe)
            put16(A16_MV + hd * LANES,
                  jnp.where(lo, kv[:, (MLA_HEADS + hd) * LANES:(MLA_HEADS + hd + 1) * LANES], 1.0))

        for pr in range(2):
            qp = pair_norm_rope(gq[:, pr * LANES:(pr + 1) * LANES], g_gq_ref[...])
            put16(A16_GQ + pr * LANES, qp * (GQA_HEAD_DIM ** -0.5 * LOG2_E))
        kp = pair_norm_rope(gkv[:, :LANES], g_gk_ref[...])
        kp_sw = pltpu.roll(kp, LANES // 2, 1)
        put16(A16_GK + 0 * LANES, jnp.where(lo, kp, 0.0))
        put16(A16_GK + 1 * LANES, jnp.where(lo, 0.0, kp_sw))
        put16(A16_GK + 2 * LANES, jnp.where(lo, kp_sw, 0.0))
        put16(A16_GK + 3 * LANES, jnp.where(lo, 0.0, kp))
        vp = gkv[:, LANES:]
        put16(A16_GV + 0 * LANES, jnp.where(lo, vp, 1.0))
        put16(A16_GV + 1 * LANES, jnp.where(lo, pltpu.roll(vp, LANES // 2, 1), 1.0))


def _in_proj(xseq, mods, weights, layer, tables, tm):
    b, n, d = xseq.shape
    per_batch = mods.shape[0] == b
    tok = lambda w: pl.BlockSpec((None, tm, w), lambda i, j: (i, j, 0))
    in_specs = [tok(d), pl.BlockSpec((None, N_MOD, d), lambda i, j: (i if per_batch else 0, 0, 0))]
    in_specs += [_resident(w, layer) for w in weights]
    args = [xseq, mods, *weights]
    if tables is not None:
        in_specs += [pl.BlockSpec((3, tm, LANES), lambda i, j: (0, j, 0))] * 2
        args += list(tables)
    return pl.pallas_call(
        functools.partial(_in_proj_kernel, rotary=tables is not None, sub_tiles=tm // 256),
        grid=(b, n // tm),
        in_specs=in_specs,
        out_specs=[tok(A16_W), tok(A32_W)],
        out_shape=[jax.ShapeDtypeStruct((b, n, A16_W), BF16), jax.ShapeDtypeStruct((b, n, A32_W), F32)],
        compiler_params=_cparams(2),
        name="in_proj",
    )(*args)


CONV_ROWS = 64
CONV_FIRST = CONV_PAD - CONV_WIDTH // 2
CONV_TAIL = ((CONV_FIRST + CONV_WIDTH - 1) // SUBLANES) * SUBLANES


def _conv_steps(sh_ref, dw_ref, b_ref, lg_ref, lb_ref, pw_ref, o_ref, length):
    def fill(before, body, after):
        sh_ref[0, 0:CONV_PAD, :] = before
        sh_ref[0, CONV_PAD:CONV_PAD + length, :] = body
        sh_ref[0, CONV_PAD + length:2 * CONV_PAD + length, :] = after

    def shift(base, n=CONV_ROWS):
        win = sh_ref[0, pl.ds(base, n + SUBLANES), :]
        for r in range(1, SUBLANES):
            sh_ref[r, pl.ds(base, n), :] = win[r:r + n, :]

    def shift_tail():
        shift(length, CONV_TAIL)

    def taps(base):
        acc = jnp.zeros((CONV_ROWS, GROUP_W), F32) + b_ref[...]
        for k in range(CONV_WIDTH):
            off = CONV_FIRST + k
            slab = sh_ref[off % SUBLANES, pl.ds(base + (off // SUBLANES) * SUBLANES, CONV_ROWS), :]
            acc = acc + slab * dw_ref[k:k + 1, :]
        sh_ref[0, pl.ds(base, CONV_ROWS), :] = acc

    def finish(base):
        acc = sh_ref[0, pl.ds(base, NORM_ROWS), :]
        mu = jnp.mean(acc, axis=-1, keepdims=True)
        cen = acc - mu
        var = jnp.mean(cen * cen, axis=-1, keepdims=True)
        z = _silu(cen * lax.rsqrt(var + EPS) * lg_ref[...] + lb_ref[...])
        o_ref[pl.ds(base, NORM_ROWS), :] = _dot(z.astype(BF16), pw_ref[...]).astype(BF16)

    return fill, shift, shift_tail, taps, finish


def _conv_kernel(y_ref, dw_ref, b_ref, lg_ref, lb_ref, pw_ref, o_ref, sh_ref):
    length = y_ref.shape[0]
    fill, shift, shift_tail, taps, finish = _conv_steps(sh_ref, dw_ref, b_ref, lg_ref, lb_ref, pw_ref, o_ref, length)
    zeros = jnp.zeros((CONV_PAD, GROUP_W), F32)
    fill(zeros, y_ref[...], zeros)

    def loop(step, rows, **kw):
        def body(c, carry):
            step(pl.multiple_of(c * rows, rows))
            return carry
        lax.fori_loop(0, length // rows, body, 0, **kw)

    loop(shift, CONV_ROWS)
    shift_tail()
    loop(taps, CONV_ROWS)
    loop(finish, NORM_ROWS, unroll=min(length // NORM_ROWS, 4))


def _conv(a32, weights, layer):
    b, n, _ = a32.shape
    return pl.pallas_call(
        _conv_kernel,
        grid=(b,),
        in_specs=[pl.BlockSpec((None, n, GROUP_W), lambda i: (i, 0, A32_Y // GROUP_W))]
        + [_resident(w, layer) for w in weights],
        out_specs=pl.BlockSpec((None, n, GROUP_W), lambda i: (i, 0, 0)),
        out_shape=jax.ShapeDtypeStruct((b, n, GROUP_W), BF16),
        scratch_shapes=[pltpu.VMEM((SUBLANES, n + 2 * CONV_PAD, GROUP_W), F32)],
        compiler_params=_cparams(1),
        name="conv",
    )(a32, *weights)


def _normalized_pair(acc_even, acc_odd):
    half = LANES // 2
    lo = lax.broadcasted_iota(jnp.int32, (1, LANES), 1) < half
    return jnp.where(lo, acc_even / pltpu.roll(acc_even, half, 1), pltpu.roll(acc_odd, half, 1) / acc_odd)


def _attn_lat_kernel(q_ref, qn_ref, kl_ref, kc_ref, vl_ref, vc_ref, o_ref, s_ref, m_ref, mb_ref, acc_ref,
                     *, q_blocks, v_blocks):
    tq = q_ref.shape[0]
    tk = KEY_CHUNK
    chunks = [(kl_ref, vl_ref, c * tk) for c in range(kl_ref.shape[0] // tk)]
    chunks += [(kc_ref, vc_ref, c * tk) for c in range(kc_ref.shape[0] // tk)]

    def scores_chunk(hd, ci, ref):
        slot = hd % 2
        k_ref, _, r0 = chunks[ci]
        q = ref[:, q_blocks[hd] * LANES:(q_blocks[hd] + 1) * LANES]
        s = _dot_nt(q, k_ref[r0:r0 + tk, hd * LANES:(hd + 1) * LANES])
        s_ref[slot, :, ci * tk:(ci + 1) * tk] = s
        mx = s[:, 0:LANES]
        for i in range(1, tk // LANES):
            mx = jnp.maximum(mx, s[:, i * LANES:(i + 1) * LANES])
        m_ref[slot] = mx if ci == 0 else jnp.maximum(m_ref[slot], mx)

    def scores_finish(hd):
        slot = hd % 2
        mb_ref[slot] = jnp.broadcast_to(jnp.max(m_ref[slot], axis=-1, keepdims=True), (tq, LANES))

    def values_chunk(hd, ci):
        slot = hd % 2
        _, v_ref, r0 = chunks[ci]
        mb = mb_ref[slot]
        p = jnp.concatenate(
            [jnp.exp2(s_ref[slot, :, ci * tk + i * LANES:ci * tk + (i + 1) * LANES] - mb).astype(BF16)
             for i in range(tk // LANES)], axis=1)
        pv = _dot(p, v_ref[r0:r0 + tk, v_blocks[hd] * LANES:(v_blocks[hd] + 1) * LANES])
        acc_ref[slot] = pv if ci == 0 else acc_ref[slot] + pv

    n = len(chunks)

    def scores_ahead(unit, ci):
        scores_chunk(unit % 4, ci, q_ref if unit < 4 else qn_ref)

    @pl.when(pl.program_id(1) == 0)
    def _():
        for ci in range(n):
            scores_chunk(0, ci, q_ref)
        scores_finish(0)
        scores_chunk(1, 0, q_ref)

    for hd in range(4):
        for ci in range(n):
            values_chunk(hd, ci)
            if ci + 1 < n:
                scores_ahead(hd + 1, ci + 1)
            else:
                scores_ahead(hd + 2, 0)
            if ci + 2 == n:
                scores_finish((hd + 1) % 4)
        if hd % 2 == 1:
            pr = hd // 2
            o_ref[:, pr * LANES:(pr + 1) * LANES] = _normalized_pair(acc_ref[0], acc_ref[1]).astype(BF16)


def _attention_lat(a16_lat, a16_ctx, q_off, q_width, k_off, v_off, v_width, q_blocks, v_blocks, name):
    b, n_lat, _ = a16_lat.shape
    n_ctx = a16_ctx.shape[1]
    tq = LAT_TILE
    n_tiles = n_lat // tq
    kw = 4 * LANES
    qc, kc, vc = q_off // q_width, k_off // kw, v_off // v_width
    return pl.pallas_call(
        functools.partial(_attn_lat_kernel, q_blocks=q_blocks, v_blocks=v_blocks),
        grid=(b, n_tiles),
        in_specs=[pl.BlockSpec((None, tq, q_width), lambda i, j: (i, j, qc)),
                  pl.BlockSpec((None, tq, q_width), lambda i, j: (i, jnp.minimum(j + 1, n_tiles - 1), qc)),
                  pl.BlockSpec((None, n_lat, kw), lambda i, j: (i, 0, kc)),
                  pl.BlockSpec((None, n_ctx, kw), lambda i, j: (i, 0, kc)),
                  pl.BlockSpec((None, n_lat, v_width), lambda i, j: (i, 0, vc)),
                  pl.BlockSpec((None, n_ctx, v_width), lambda i, j: (i, 0, vc))],
        out_specs=pl.BlockSpec((None, tq, 2 * LANES), lambda i, j: (i, j, 0)),
        out_shape=jax.ShapeDtypeStruct((b, n_lat, 2 * LANES), BF16),
        scratch_shapes=[pltpu.VMEM((2, tq, n_lat + n_ctx), F32), pltpu.VMEM((2, tq, LANES), F32),
                        pltpu.VMEM((2, tq, LANES), F32), pltpu.VMEM((2, tq, LANES), F32)],
        compiler_params=_cparams(2),
        name=name,
    )(a16_lat, a16_lat, a16_lat, a16_ctx, a16_lat, a16_ctx)


def _attn_ctx_kernel(q_ref, k_ref, v_ref, o_ref, *, q_blocks, v_blocks):
    for pr in range(2):
        accs = []
        for half in range(2):
            hd = 2 * pr + half
            q = q_ref[:, q_blocks[hd] * LANES:(q_blocks[hd] + 1) * LANES]
            s = _dot_nt(q, k_ref[:, hd * LANES:(hd + 1) * LANES])
            p = jnp.exp2(s - jnp.max(s, axis=-1, keepdims=True))
            accs.append(_dot(p.astype(BF16), v_ref[:, v_blocks[hd] * LANES:(v_blocks[hd] + 1) * LANES]))
        o_ref[:, pr * LANES:(pr + 1) * LANES] = _normalized_pair(*accs).astype(BF16)


def _attention_ctx(a16_ctx, q_off, q_width, k_off, v_off, v_width, q_blocks, v_blocks, name):
    b, n_ctx, _ = a16_ctx.shape
    kw = 4 * LANES
    qc, kc, vc = q_off // q_width, k_off // kw, v_off // v_width
    return pl.pallas_call(
        functools.partial(_attn_ctx_kernel, q_blocks=q_blocks, v_blocks=v_blocks),
        grid=(b,),
        in_specs=[pl.BlockSpec((None, n_ctx, q_width), lambda i: (i, 0, qc)),
                  pl.BlockSpec((None, n_ctx, kw), lambda i: (i, 0, kc)),
                  pl.BlockSpec((None, n_ctx, v_width), lambda i: (i, 0, vc))],
        out_specs=pl.BlockSpec((None, n_ctx, 2 * LANES), lambda i: (i, 0, 0)),
        out_shape=jax.ShapeDtypeStruct((b, n_ctx, 2 * LANES), BF16),
        compiler_params=_cparams(1),
        name=name,
    )(a16_ctx, a16_ctx, a16_ctx)


def _log_sigmoid(x):
    return jnp.minimum(x, 0.0) - jnp.log(1.0 + jnp.exp(-jnp.abs(x)))


def _ret_kernel(ql_ref, kl_ref, vl_ref, gfl_ref, gbl_ref, qc_ref, kc_ref, vc_ref, gfc_ref, gbc_ref,
                dq_ref, dv_ref, dc_ref, ng_ref, ol_ref, oc_ref, acc_ref):
    c = RET_CHUNK
    n_lat, n_ctx = ql_ref.shape[0], qc_ref.shape[0]
    qk_w = RET_HEADS * RET_QK
    v_w = RET_HEADS * RET_V

    row_i = lax.broadcasted_iota(jnp.int32, (c, 1), 0).astype(F32)
    qk_head = lax.broadcasted_iota(jnp.int32, (1, qk_w), 1) // RET_QK
    v_head = lax.broadcasted_iota(jnp.int32, (1, v_w), 1) // RET_V
    qk_masks = [qk_head == hd for hd in range(RET_HEADS)]
    v_masks = [v_head == hd for hd in range(RET_HEADS)]
    state_rows = lax.broadcasted_iota(jnp.int32, (qk_w, 1), 0) // RET_QK
    bd_mask = (state_rows == v_head).astype(F32)
    ones_rows = lax.broadcasted_iota(jnp.int32, (v_w, 1), 0) // RET_V
    head_ones = (ones_rows == v_head).astype(BF16)
    ci = lax.broadcasted_iota(jnp.int32, (c, RET_HEADS * c), 0).astype(F32)
    cj = (lax.broadcasted_iota(jnp.int32, (c, RET_HEADS * c), 1) % c).astype(F32)

    def make_step(d):
        lg_q = _log_sigmoid(dq_ref[d:d + 1, :])
        lg_v = _log_sigmoid(dv_ref[d:d + 1, :])
        lg_c = _log_sigmoid(dc_ref[d:d + 1, :])
        if d == 0:
            diff = ci - cj
            q_dec = jnp.exp(lg_q * (row_i + 1.0))
            k_dec = jnp.exp(lg_q * (c - 1.0 - row_i))
        else:
            diff = cj - ci
            q_dec = jnp.exp(lg_q * (c - row_i))
            k_dec = jnp.exp(lg_q * row_i)
        decay = jnp.where(diff >= 0.0, jnp.exp(lg_c * jnp.maximum(diff, 0.0)), 0.0)
        chunk_dec = jnp.exp(lg_v * float(c))

        def step(refs, r0, acc_r0, state):
            q_ref, k_ref, v_ref = refs
            q = q_ref[pl.ds(r0, c), :]
            k = k_ref[pl.ds(r0, c), :]
            v = v_ref[pl.ds(r0, c), :]
            kb, vb = k.astype(BF16), v.astype(BF16)
            k_bd = jnp.concatenate([jnp.where(m, kb, 0) for m in qk_masks], axis=0)
            scores = _dot_nt(q.astype(BF16), k_bd) * decay
            v_bd = jnp.concatenate([jnp.where(m, vb, 0) for m in v_masks], axis=0)
            inner = _dot(scores.astype(BF16), v_bd)
            cross = _dot((q * q_dec).astype(BF16), state.astype(BF16))
            kv = _dot_tn((k * k_dec).astype(BF16), vb)
            new_state = state * chunk_dec + kv * bd_mask
            acc_ref[d, pl.ds(acc_r0, c), :] = inner + cross
            return new_state

        return step

    fwd, bwd = make_step(0), make_step(1)

    def scan(n, acc_base, fwd_refs, bwd_refs, states):
        def body(i, st):
            rf = pl.multiple_of(i * c, c)
            rb = pl.multiple_of((n - 1 - i) * c, c)
            return fwd(fwd_refs, rf, acc_base + rf, st[0]), bwd(bwd_refs, rb, acc_base + rb, st[1])
        return lax.fori_loop(0, n, body, states, unroll=min(n, 4))

    zero = jnp.zeros((qk_w, v_w), F32)
    states = scan(n_ctx // c, n_lat, (qc_ref, kc_ref, vc_ref), (qc_ref, kc_ref, vc_ref), (zero, zero))
    scan(n_lat // c, 0, (ql_ref, kl_ref, vl_ref), (ql_ref, kl_ref, vl_ref), states)

    def readout(o_ref, gate_refs, acc_base):
        rows = NORM_ROWS

        def body(i, carry):
            r0 = pl.multiple_of(i * rows, rows)
            y = None
            for d in range(2):
                o = acc_ref[d, pl.ds(acc_base + r0, rows), :]
                o2 = o * o
                o2_hi = o2.astype(BF16)
                o2_lo = (o2 - o2_hi.astype(F32)).astype(BF16)
                ss = _dot(o2_hi, head_ones) + _dot(o2_lo, head_ones)
                yd = o * lax.rsqrt(ss * (1.0 / RET_V) + EPS) * ng_ref[d:d + 1, :] * gate_refs[d][pl.ds(r0, rows), :]
                y = yd if y is None else y + yd
            o_ref[pl.ds(r0, rows), :] = y.astype(BF16)
            return carry

        n = o_ref.shape[0] // rows
        lax.fori_loop(0, n, body, 0, unroll=min(n, 4))

    readout(ol_ref, (gfl_ref, gbl_ref), 0)
    readout(oc_ref, (gfc_ref, gbc_ref), n_lat)


def _retention(a32_lat, a32_ctx, smalls, layer):
    b, n_lat, _ = a32_lat.shape
    n_ctx = a32_ctx.shape[1]

    def seq_specs(n):
        spec = lambda off, w: pl.BlockSpec((None, n, w), lambda i: (i, 0, off // w))
        return [spec(A32_RQ, 128), spec(A32_RK, 128), spec(A32_RV, 256), spec(A32_GF, 256), spec(A32_GB, 256)]

    out = lambda n: pl.BlockSpec((None, n, 256), lambda i: (i, 0, 0))
    return pl.pallas_call(
        _ret_kernel,
        grid=(b,),
        in_specs=seq_specs(n_lat) + seq_specs(n_ctx) + [_resident(a, layer) for a in smalls],
        out_specs=[out(n_lat), out(n_ctx)],
        out_shape=[jax.ShapeDtypeStruct((b, n_lat, 256), BF16), jax.ShapeDtypeStruct((b, n_ctx, 256), BF16)],
        scratch_shapes=[pltpu.VMEM((2, n_lat + n_ctx, 256), F32)],
        compiler_params=_cparams(1),
        name="retention",
    )(*([a32_lat] * 5), *([a32_ctx] * 5), *smalls)


def _out_mlp_kernel(x_ref, ya_ref, yb_ref, yc_ref, yd_ref, mod_ref, wo_ref, w1_ref, w2_ref, *rest, ff_chunk):
    o_ref = rest[-1]
    x = x_ref[...]
    gw = GROUP_W
    y = (_dot(ya_ref[...], wo_ref[0 * gw:1 * gw, :]) + _dot(yb_ref[...], wo_ref[1 * gw:2 * gw, :])
         + _dot(yc_ref[...], wo_ref[2 * gw:3 * gw, :]) + _dot(yd_ref[...], wo_ref[3 * gw:4 * gw, :]))
    x1 = x + mod_ref[2:3, :] * y
    h = (_rms(x1) * (1.0 + mod_ref[4:5, :]) + mod_ref[3:4, :]).astype(BF16)
    acc = jnp.zeros_like(x1)
    for j in range(w1_ref.shape[1] // ff_chunk):
        a = jnp.maximum(_dot(h, w1_ref[:, j * ff_chunk:(j + 1) * ff_chunk]), 0.0)
        acc = acc + _dot((a * a).astype(BF16), w2_ref[j * ff_chunk:(j + 1) * ff_chunk, :])
    x2 = x1 + mod_ref[5:6, :] * acc
    o_ref[...] = x2 if len(rest) == 1 else _rms(x2) * rest[0][...]


def _out_mlp(xseq, ys, mods, weights, layer, tm, in_place, final_gain=None):
    b, n, d = xseq.shape
    per_batch = mods.shape[0] == b
    tok = lambda w: pl.BlockSpec((None, tm, w), lambda i, j: (i, j, 0))
    in_specs = [tok(d)] + [tok(GROUP_W)] * 4
    in_specs += [pl.BlockSpec((None, N_MOD, d), lambda i, j: (i if per_batch else 0, 0, 0))]
    in_specs += [_resident(w, layer) for w in weights]
    args = [xseq, *ys, mods, *weights]
    if final_gain is not None:
        in_specs.append(_resident(final_gain))
        args.append(final_gain)
    return pl.pallas_call(
        functools.partial(_out_mlp_kernel, ff_chunk=1024),
        grid=(b, n // tm),
        in_specs=in_specs,
        out_specs=tok(d),
        out_shape=jax.ShapeDtypeStruct((b, n, d), F32),
        input_output_aliases={0: 0} if in_place else {},
        compiler_params=_cparams(2),
        name="out_mlp",
    )(*args)


def _rope_tables(n_lat):
    n = np.arange(n_lat)
    row, col = (n // GRID_W).astype(np.float64), (n % GRID_W).astype(np.float64)

    def table(groups):
        cos = np.ones((n_lat, LANES)); up = np.zeros((n_lat, LANES)); dn = np.zeros((n_lat, LANES))
        for rot_start, rot_dim in groups:
            half = rot_dim // 2
            q = half // 2
            freqs = ROPE_BASE ** (-np.arange(q, dtype=np.float64) / q)
            for axis, pos in enumerate((row, col)):
                ang = pos[:, None] * freqs[None, :]
                base = rot_start + axis * half
                cos[:, base:base + q] = np.cos(ang)
                cos[:, base + q:base + 2 * q] = np.cos(ang)
                up[:, base:base + q] = -np.sin(ang)
                dn[:, base + q:base + 2 * q] = np.sin(ang)
        return jnp.asarray(np.stack([cos, up, dn]).astype(np.float32))

    return table([(0, GQA_HEAD_DIM), (GQA_HEAD_DIM, GQA_HEAD_DIM)]), table([(MLA_NOPE, MLA_ROPE)])


def _layout_w_in(w_in):
    lead = w_in.shape[:-1]
    kpe_src = OFF_CKV + MLA_KV_RANK
    w = w_in.astype(BF16)
    out = jnp.concatenate([w[..., :kpe_src], jnp.zeros(lead + (MLA_NOPE,), BF16),
                           w[..., kpe_src:kpe_src + MLA_ROPE],
                           jnp.zeros(lead + (LANES - MLA_NOPE - MLA_ROPE,), BF16),
                           w[..., kpe_src + MLA_ROPE:]], axis=-1)
    assert out.shape[-1] == N_IN
    return out


def _layout_w_uq(w_uq):
    lead = w_uq.shape[:-1]
    w = w_uq.reshape(lead + (MLA_HEADS, MLA_NOPE + MLA_ROPE))
    w = jnp.pad(w, ((0, 0),) * (w.ndim - 1) + ((0, LANES - MLA_NOPE - MLA_ROPE),))
    return w.reshape(lead + (MLA_HEADS * LANES,)).astype(BF16)


def _layout_w_ukv(w_ukv):
    lead = w_ukv.shape[:-1]
    w = w_ukv.reshape(lead + (MLA_HEADS, MLA_NOPE + MLA_V))
    pad = lambda a: jnp.pad(a, ((0, 0),) * (a.ndim - 1) + ((0, LANES - a.shape[-1]),)).reshape(
        lead + (MLA_HEADS * LANES,))
    return jnp.concatenate([pad(w[..., :MLA_NOPE]), pad(w[..., MLA_NOPE:])], axis=-1).astype(BF16)


def kernel(x, c, ctx, c_ctx, w_mod, b_mod, w_in, w_out, conv_dw, conv_b, conv_ln_g, conv_ln_b, conv_pw, mla_q_g, mla_kv_g, mla_uq, mla_ukv, gqa_q_g, gqa_k_g, ret_decay, ret_norm_g, mlp_w1, mlp_w2, final_g):
    b, n_lat, d = x.shape
    n_ctx = ctx.shape[1]
    depth = w_mod.shape[0]
    assert n_lat % LAT_TILE == 0 and n_lat % PROJ_TILE == 0 and n_lat % KEY_CHUNK == 0 and n_ctx % KEY_CHUNK == 0
    assert n_ctx % NORM_ROWS == 0 and n_lat % NORM_ROWS == 0 and NORM_ROWS % (2 * RET_CHUNK) == 0

    rows = ((b + 1 + SUBLANES - 1) // SUBLANES) * SUBLANES
    cond = jnp.zeros((rows, d), F32).at[:b].set(c).at[b].set(c_ctx)
    mod_all = _adaln(cond, w_mod, b_mod).reshape(depth, rows, N_MOD, d)
    tables = _rope_tables(n_lat)
    mla = (A16_MQ, 4 * LANES, A16_MK, A16_MV, 4 * LANES, (0, 1, 2, 3), (0, 1, 2, 3))
    gqa = (A16_GQ, 2 * LANES, A16_GK, A16_GV, 2 * LANES, (0, 0, 1, 1), (0, 0, 1, 1))

    row = lambda a: a[:, None, :]
    w_proj = (_layout_w_in(w_in), _layout_w_uq(mla_uq), _layout_w_ukv(mla_ukv), row(mla_q_g), row(mla_kv_g),
              row(jnp.tile(gqa_q_g, (1, 2))), row(jnp.tile(gqa_k_g, (1, 2))))
    w_conv = (conv_dw, row(conv_b), row(conv_ln_g), row(conv_ln_b), conv_pw.astype(BF16))
    dec = ret_decay.astype(F32)
    w_ret = (jnp.repeat(dec, RET_QK, axis=2), jnp.repeat(dec, RET_V, axis=2), jnp.repeat(dec, RET_CHUNK, axis=2),
             ret_norm_g.reshape(depth, 2, RET_HEADS * RET_V))
    w_mlp = (w_out.astype(BF16), mlp_w1.astype(BF16), mlp_w2.astype(BF16))

    flat = lambda a: a.reshape(1, b * n_ctx, a.shape[-1])
    unflat = lambda a: a.reshape(b, n_ctx, a.shape[-1])
    ctx_proj_tile = PROJ_TILE if (b * n_ctx) % PROJ_TILE == 0 else n_ctx
    ctx_mlp_tile = LAT_TILE if (b * n_ctx) % LAT_TILE == 0 else n_ctx

    cx = ctx
    for layer in range(depth):
        last = layer == depth - 1
        mods_lat, mods_ctx = mod_all[layer, :b], mod_all[layer, b:b + 1]
        a16_lat, a32_lat = _in_proj(x, mods_lat, w_proj, layer, tables, PROJ_TILE)
        a16_ctx, a32_ctx = (unflat(a) for a in _in_proj(flat(cx), mods_ctx, w_proj, layer, None, ctx_proj_tile))
        ya_lat, ya_ctx = _conv(a32_lat, w_conv, layer), _conv(a32_ctx, w_conv, layer)
        yb_lat = _attention_lat(a16_lat, a16_ctx, *mla, "mla_attn")
        yc_lat = _attention_lat(a16_lat, a16_ctx, *gqa, "gqa_attn")
        yd_lat, yd_ctx = _retention(a32_lat, a32_ctx, w_ret, layer)
        x = _out_mlp(x, (ya_lat, yb_lat, yc_lat, yd_lat), mods_lat, w_mlp, layer, LAT_TILE, layer > 0,
                     final_gain=final_g[None, :] if last else None)
        if not last:
            yb_ctx = _attention_ctx(a16_ctx, *mla, "mla_attn_ctx")
            yc_ctx = _attention_ctx(a16_ctx, *gqa, "gqa_attn_ctx")
            ys_ctx = tuple(flat(y) for y in (ya_ctx, yb_ctx, yc_ctx, yd_ctx))
            cx = unflat(_out_mlp(flat(cx), ys_ctx, mods_ctx, w_mlp, layer, ctx_mlp_tile, layer > 0))
    return x
```

```python
import functools

import numpy as np
import jax
import jax.numpy as jnp
from jax import lax
from jax.experimental import pallas as pl
from jax.experimental.pallas import tpu as pltpu

F32 = jnp.float32
BF16 = jnp.bfloat16

GRID_W = 64
N_MOD = 6
EPS = 1e-6
ROPE_BASE = 10000.0
GROUP_W = 256
CONV_WIDTH = 31
CONV_PAD = 16
MLA_HEADS = 4
MLA_NOPE = 64
MLA_ROPE = 32
MLA_V = 64
MLA_Q_RANK = 256
MLA_KV_RANK = 128
GQA_HEADS = 4
GQA_KV_HEADS = 2
GQA_HEAD_DIM = 64
RET_HEADS = 4
RET_QK = 32
RET_V = 64
RET_CHUNK = 128
LANES = 128
SUBLANES = 8
LOG2_E = 1.4426950408889634
LAT_TILE = 512
PROJ_TILE = 1024
KEY_CHUNK = 256
NORM_ROWS = 256
VMEM_LIMIT = 56 * 1024 * 1024

OFF_A, OFF_CQ, OFF_CKV, OFF_GQ, OFF_GK = 0, 512, 768, 1024, 1280
OFF_RQ, OFF_RV, OFF_GF, OFF_GB, N_IN = 1536, 1792, 2048, 2304, 2560
A16_MQ, A16_MK, A16_MV, A16_GK, A16_GQ, A16_GV, A16_W = 0, 512, 1024, 1536, 2048, 2304, 2560
A32_Y, A32_RQ, A32_RK, A32_RV, A32_GF, A32_GB, A32_W = 0, 256, 384, 512, 768, 1024, 1280


def _cparams(n_axes):
    return pltpu.CompilerParams(dimension_semantics=("arbitrary",) * n_axes,
                                vmem_limit_bytes=VMEM_LIMIT)


def _resident(a, layer=None):
    if layer is None:
        return pl.BlockSpec(a.shape, lambda *_: (0,) * a.ndim, pipeline_mode=pl.Buffered(1))
    return pl.BlockSpec((None,) + a.shape[1:], lambda *_: (layer,) + (0,) * (a.ndim - 1),
                        pipeline_mode=pl.Buffered(1))


def _rms(x):
    return x * lax.rsqrt(jnp.mean(x * x, axis=-1, keepdims=True) + EPS)


def _sigmoid(x):
    return 1.0 / (1.0 + jnp.exp(-x))


def _silu(x):
    return x * _sigmoid(x)


def _dot(a, b):
    return jnp.dot(a, b, preferred_element_type=F32)


def _dot_nt(a, b):
    return lax.dot_general(a, b, (((1,), (1,)), ((), ())), preferred_element_type=F32)


def _dot_tn(a, b):
    return lax.dot_general(a, b, (((0,), (0,)), ((), ())), preferred_element_type=F32)


def _rope(x, cos, sin_up, sin_dn, shift):
    n = x.shape[-1]
    return x * cos + pltpu.roll(x, n - shift, 1) * sin_up + pltpu.roll(x, shift, 1) * sin_dn


def _adaln_kernel(cond_ref, w_ref, b_ref, o_ref):
    cond = cond_ref[...]
    o_ref[...] = _dot(_silu(cond).astype(BF16), w_ref[...].astype(BF16)) + b_ref[...]


def _adaln(cond, w_mod, b_mod):
    depth, d, n = w_mod.shape
    r = cond.shape[0]
    tn = 1536
    return pl.pallas_call(
        _adaln_kernel,
        grid=(depth, n // tn),
        in_specs=[pl.BlockSpec((r, d), lambda l, j: (0, 0)),
                  pl.BlockSpec((None, d, tn), lambda l, j: (l, 0, j)),
                  pl.BlockSpec((None, 1, tn), lambda l, j: (l, 0, j))],
        out_specs=pl.BlockSpec((None, r, tn), lambda l, j: (l, 0, j)),
        out_shape=jax.ShapeDtypeStruct((depth, r, n), F32),
        compiler_params=_cparams(2),
        name="adaln",
    )(cond, w_mod, b_mod.reshape(depth, 1, n))


def _in_proj_kernel(x_ref, mod_ref, w_in_ref, w_uq_ref, w_ukv_ref, g_mq_ref, g_mkv_ref, g_gq_ref, g_gk_ref,
                    *rest, rotary, sub_tiles):
    a16_ref, a32_ref = rest[-2:]
    lo = lax.broadcasted_iota(jnp.int32, (1, LANES), 1) < (LANES // 2)
    q_scale = (MLA_NOPE + MLA_ROPE) ** -0.5 * LOG2_E
    n_rows = x_ref.shape[0] // sub_tiles

    for st in range(sub_tiles):
        rows = slice(st * n_rows, (st + 1) * n_rows)
        h = (_rms(x_ref[rows, :]) * (1.0 + mod_ref[1:2, :]) + mod_ref[0:1, :]).astype(BF16)

        def proj(off, width, h=h):
            return _dot(h, w_in_ref[:, off:off + width])

        def put16(off, val, rows=rows):
            a16_ref[rows, off:off + val.shape[1]] = val.astype(BF16)

        def put32(off, val, rows=rows):
            a32_ref[rows, off:off + val.shape[1]] = val

        if rotary:
            tg_ref, tm_ref = rest[:2]
            rope_g = lambda v, rows=rows: _rope(v, tg_ref[0, rows, :], tg_ref[1, rows, :], tg_ref[2, rows, :],
                                                GQA_HEAD_DIM // 4)
            rope_m = lambda v, rows=rows: _rope(v, tm_ref[0, rows, :], tm_ref[1, rows, :], tm_ref[2, rows, :],
                                                MLA_ROPE // 4)
        else:
            rope_g = rope_m = lambda v: v

        def pair_norm_rope(xp, gain, rope_g=rope_g):
            x2 = xp * xp
            s_lo = jnp.sum(jnp.where(lo, x2, 0.0), axis=-1, keepdims=True)
            s_hi = jnp.sum(jnp.where(lo, 0.0, x2), axis=-1, keepdims=True)
            r = jnp.where(lo, lax.rsqrt(s_lo * (1.0 / GQA_HEAD_DIM) + EPS),
                          lax.rsqrt(s_hi * (1.0 / GQA_HEAD_DIM) + EPS))
            return rope_g(xp * r * gain)

        cq = (_rms(proj(OFF_CQ, MLA_Q_RANK)) * g_mq_ref[...]).astype(BF16)
        ckv_kpe = proj(OFF_CKV, 2 * LANES)
        ckv = (_rms(ckv_kpe[:, :LANES]) * g_mkv_ref[...]).astype(BF16)
        gq = proj(OFF_GQ, GQA_HEADS * GQA_HEAD_DIM)
        gkv = proj(OFF_GK, 2 * LANES)

        a = proj(OFF_A, 2 * GROUP_W)
        put32(A32_Y, a[:, :GROUP_W] * _sigmoid(a[:, GROUP_W:]))
        rqk = proj(OFF_RQ, 2 * LANES)
        put32(A32_RQ, rqk[:, :LANES])
        put32(A32_RK, rqk[:, LANES:] * RET_QK ** -0.5)
        put32(A32_RV, proj(OFF_RV, RET_HEADS * RET_V))
        put32(A32_GF, _silu(proj(OFF_GF, GROUP_W)))
        put32(A32_GB, _silu(proj(OFF_GB, GROUP_W)))

        q = _dot(cq, w_uq_ref[...])
        for hd in range(MLA_HEADS):
            put16(A16_MQ + hd * LANES, rope_m(q[:, hd * LANES:(hd + 1) * LANES]) * q_scale)
        kv = _dot(ckv, w_ukv_ref[...])
        kpe = rope_m(ckv_kpe[:, LANES:])
        for hd in range(MLA_HEADS):
            put16(A16_MK + hd * LANES, kv[:, hd * LANES:(hd + 1) * LANES] + kpe)
            put16(A16_MV + hd * LANES,
                  jnp.where(lo, kv[:, (MLA_HEADS + hd) * LANES:(MLA_HEADS + hd + 1) * LANES], 1.0))

        for pr in range(2):
            qp = pair_norm_rope(gq[:, pr * LANES:(pr + 1) * LANES], g_gq_ref[...])
            put16(A16_GQ + pr * LANES, qp * (GQA_HEAD_DIM ** -0.5 * LOG2_E))
        kp = pair_norm_rope(gkv[:, :LANES], g_gk_ref[...])
        kp_sw = pltpu.roll(kp, LANES // 2, 1)
        put16(A16_GK + 0 * LANES, jnp.where(lo, kp, 0.0))
        put16(A16_GK + 1 * LANES, jnp.where(lo, 0.0, kp_sw))
        put16(A16_GK + 2 * LANES, jnp.where(lo, kp_sw, 0.0))
        put16(A16_GK + 3 * LANES, jnp.where(lo, 0.0, kp))
        vp = gkv[:, LANES:]
        put16(A16_GV + 0 * LANES, jnp.where(lo, vp, 1.0))
        put16(A16_GV + 1 * LANES, jnp.where(lo, pltpu.roll(vp, LANES // 2, 1), 1.0))


def _in_proj(xseq, mods, weights, layer, tables, tm):
    b, n, d = xseq.shape
    per_batch = mods.shape[0] == b
    tok = lambda w: pl.BlockSpec((None, tm, w), lambda i, j: (i, j, 0))
    in_specs = [tok(d), pl.BlockSpec((None, N_MOD, d), lambda i, j: (i if per_batch else 0, 0, 0))]
    in_specs += [_resident(w, layer) for w in weights]
    args = [xseq, mods, *weights]
    if tables is not None:
        in_specs += [pl.BlockSpec((3, tm, LANES), lambda i, j: (0, j, 0))] * 2
        args += list(tables)
    return pl.pallas_call(
        functools.partial(_in_proj_kernel, rotary=tables is not None, sub_tiles=tm // 256),
        grid=(b, n // tm),
        in_specs=in_specs,
        out_specs=[tok(A16_W), tok(A32_W)],
        out_shape=[jax.ShapeDtypeStruct((b, n, A16_W), BF16), jax.ShapeDtypeStruct((b, n, A32_W), F32)],
        compiler_params=_cparams(2),
        name="in_proj",
    )(*args)


CONV_ROWS = 64
CONV_FIRST = CONV_PAD - CONV_WIDTH // 2
CONV_TAIL = ((CONV_FIRST + CONV_WIDTH - 1) // SUBLANES) * SUBLANES


def _conv_steps(sh_ref, dw_ref, b_ref, lg_ref, lb_ref, pw_ref, o_ref, length):
    def fill(before, body, after):
        sh_ref[0, 0:CONV_PAD, :] = before
        sh_ref[0, CONV_PAD:CONV_PAD + length, :] = body
        sh_ref[0, CONV_PAD + length:2 * CONV_PAD + length, :] = after

    def shift(base, n=CONV_ROWS):
        win = sh_ref[0, pl.ds(base, n + SUBLANES), :]
        for r in range(1, SUBLANES):
            sh_ref[r, pl.ds(base, n), :] = win[r:r + n, :]

    def shift_tail():
        shift(length, CONV_TAIL)

    def taps(base):
        acc = jnp.zeros((CONV_ROWS, GROUP_W), F32) + b_ref[...]
        for k in range(CONV_WIDTH):
            off = CONV_FIRST + k
            slab = sh_ref[off % SUBLANES, pl.ds(base + (off // SUBLANES) * SUBLANES, CONV_ROWS), :]
            acc = acc + slab * dw_ref[k:k + 1, :]
        sh_ref[0, pl.ds(base, CONV_ROWS), :] = acc

    def finish(base):
        acc = sh_ref[0, pl.ds(base, NORM_ROWS), :]
        mu = jnp.mean(acc, axis=-1, keepdims=True)
        cen = acc - mu
        var = jnp.mean(cen * cen, axis=-1, keepdims=True)
        z = _silu(cen * lax.rsqrt(var + EPS) * lg_ref[...] + lb_ref[...])
        o_ref[pl.ds(base, NORM_ROWS), :] = _dot(z.astype(BF16), pw_ref[...]).astype(BF16)

    return fill, shift, shift_tail, taps, finish


def _conv_kernel(y_ref, dw_ref, b_ref, lg_ref, lb_ref, pw_ref, o_ref, sh_ref):
    length = y_ref.shape[0]
    fill, shift, shift_tail, taps, finish = _conv_steps(sh_ref, dw_ref, b_ref, lg_ref, lb_ref, pw_ref, o_ref, length)
    zeros = jnp.zeros((CONV_PAD, GROUP_W), F32)
    fill(zeros, y_ref[...], zeros)

    def loop(step, rows, **kw):
        def body(c, carry):
            step(pl.multiple_of(c * rows, rows))
            return carry
        lax.fori_loop(0, length // rows, body, 0, **kw)

    loop(shift, CONV_ROWS)
    shift_tail()
    loop(taps, CONV_ROWS)
    loop(finish, NORM_ROWS, unroll=min(length // NORM_ROWS, 4))


def _conv(a32, weights, layer):
    b, n, _ = a32.shape
    return pl.pallas_call(
        _conv_kernel,
        grid=(b,),
        in_specs=[pl.BlockSpec((None, n, GROUP_W), lambda i: (i, 0, A32_Y // GROUP_W))]
        + [_resident(w, layer) for w in weights],
        out_specs=pl.BlockSpec((None, n, GROUP_W), lambda i: (i, 0, 0)),
        out_shape=jax.ShapeDtypeStruct((b, n, GROUP_W), BF16),
        scratch_shapes=[pltpu.VMEM((SUBLANES, n + 2 * CONV_PAD, GROUP_W), F32)],
        compiler_params=_cparams(1),
        name="conv",
    )(a32, *weights)


def _normalized_pair(acc_even, acc_odd):
    half = LANES // 2
    lo = lax.broadcasted_iota(jnp.int32, (1, LANES), 1) < half
    return jnp.where(lo, acc_even / pltpu.roll(acc_even, half, 1), pltpu.roll(acc_odd, half, 1) / acc_odd)


def _attn_lat_kernel(q_ref, qn_ref, kl_ref, kc_ref, vl_ref, vc_ref, o_ref, s_ref, m_ref, mb_ref, acc_ref,
                     *, q_blocks, v_blocks):
    tq = q_ref.shape[0]
    tk = KEY_CHUNK
    chunks = [(kl_ref, vl_ref, c * tk) for c in range(kl_ref.shape[0] // tk)]
    chunks += [(kc_ref, vc_ref, c * tk) for c in range(kc_ref.shape[0] // tk)]

    def scores_chunk(hd, ci, ref):
        slot = hd % 2
        k_ref, _, r0 = chunks[ci]
        q = ref[:, q_blocks[hd] * LANES:(q_blocks[hd] + 1) * LANES]
        s = _dot_nt(q, k_ref[r0:r0 + tk, hd * LANES:(hd + 1) * LANES])
        s_ref[slot, :, ci * tk:(ci + 1) * tk] = s
        mx = s[:, 0:LANES]
        for i in range(1, tk // LANES):
            mx = jnp.maximum(mx, s[:, i * LANES:(i + 1) * LANES])
        m_ref[slot] = mx if ci == 0 else jnp.maximum(m_ref[slot], mx)

    def scores_finish(hd):
        slot = hd % 2
        mb_ref[slot] = jnp.broadcast_to(jnp.max(m_ref[slot], axis=-1, keepdims=True), (tq, LANES))

    def values_chunk(hd, ci):
        slot = hd % 2
        _, v_ref, r0 = chunks[ci]
        mb = mb_ref[slot]
        p = jnp.concatenate(
            [jnp.exp2(s_ref[slot, :, ci * tk + i * LANES:ci * tk + (i + 1) * LANES] - mb).astype(BF16)
             for i in range(tk // LANES)], axis=1)
        pv = _dot(p, v_ref[r0:r0 + tk, v_blocks[hd] * LANES:(v_blocks[hd] + 1) * LANES])
        acc_ref[slot] = pv if ci == 0 else acc_ref[slot] + pv

    n = len(chunks)

    def scores_ahead(unit, ci):
        scores_chunk(unit % 4, ci, q_ref if unit < 4 else qn_ref)

    @pl.when(pl.program_id(1) == 0)
    def _():
        for ci in range(n):
            scores_chunk(0, ci, q_ref)
        scores_finish(0)
        scores_chunk(1, 0, q_ref)

    for hd in range(4):
        for ci in range(n):
            values_chunk(hd, ci)
            if ci + 1 < n:
                scores_ahead(hd + 1, ci + 1)
            else:
                scores_ahead(hd + 2, 0)
            if ci + 2 == n:
                scores_finish((hd + 1) % 4)
        if hd % 2 == 1:
            pr = hd // 2
            o_ref[:, pr * LANES:(pr + 1) * LANES] = _normalized_pair(acc_ref[0], acc_ref[1]).astype(BF16)


def _attention_lat(a16_lat, a16_ctx, q_off, q_width, k_off, v_off, v_width, q_blocks, v_blocks, name):
    b, n_lat, _ = a16_lat.shape
    n_ctx = a16_ctx.shape[1]
    tq = LAT_TILE
    n_tiles = n_lat // tq
    kw = 4 * LANES
    qc, kc, vc = q_off // q_width, k_off // kw, v_off // v_width
    return pl.pallas_call(
        functools.partial(_attn_lat_kernel, q_blocks=q_blocks, v_blocks=v_blocks),
        grid=(b, n_tiles),
        in_specs=[pl.BlockSpec((None, tq, q_width), lambda i, j: (i, j, qc)),
                  pl.BlockSpec((None, tq, q_width), lambda i, j: (i, jnp.minimum(j + 1, n_tiles - 1), qc)),
                  pl.BlockSpec((None, n_lat, kw), lambda i, j: (i, 0, kc)),
                  pl.BlockSpec((None, n_ctx, kw), lambda i, j: (i, 0, kc)),
                  pl.BlockSpec((None, n_lat, v_width), lambda i, j: (i, 0, vc)),
                  pl.BlockSpec((None, n_ctx, v_width), lambda i, j: (i, 0, vc))],
        out_specs=pl.BlockSpec((None, tq, 2 * LANES), lambda i, j: (i, j, 0)),
        out_shape=jax.ShapeDtypeStruct((b, n_lat, 2 * LANES), BF16),
        scratch_shapes=[pltpu.VMEM((2, tq, n_lat + n_ctx), F32), pltpu.VMEM((2, tq, LANES), F32),
                        pltpu.VMEM((2, tq, LANES), F32), pltpu.VMEM((2, tq, LANES), F32)],
        compiler_params=_cparams(2),
        name=name,
    )(a16_lat, a16_lat, a16_lat, a16_ctx, a16_lat, a16_ctx)


def _attn_ctx_kernel(q_ref, k_ref, v_ref, o_ref, *, q_blocks, v_blocks):
    for pr in range(2):
        accs = []
        for half in range(2):
            hd = 2 * pr + half
            q = q_ref[:, q_blocks[hd] * LANES:(q_blocks[hd] + 1) * LANES]
            s = _dot_nt(q, k_ref[:, hd * LANES:(hd + 1) * LANES])
            p = jnp.exp2(s - jnp.max(s, axis=-1, keepdims=True))
            accs.append(_dot(p.astype(BF16), v_ref[:, v_blocks[hd] * LANES:(v_blocks[hd] + 1) * LANES]))
        o_ref[:, pr * LANES:(pr + 1) * LANES] = _normalized_pair(*accs).astype(BF16)


def _attention_ctx(a16_ctx, q_off, q_width, k_off, v_off, v_width, q_blocks, v_blocks, name):
    b, n_ctx, _ = a16_ctx.shape
    kw = 4 * LANES
    qc, kc, vc = q_off // q_width, k_off // kw, v_off // v_width
    return pl.pallas_call(
        functools.partial(_attn_ctx_kernel, q_blocks=q_blocks, v_blocks=v_blocks),
        grid=(b,),
        in_specs=[pl.BlockSpec((None, n_ctx, q_width), lambda i: (i, 0, qc)),
                  pl.BlockSpec((None, n_ctx, kw), lambda i: (i, 0, kc)),
                  pl.BlockSpec((None, n_ctx, v_width), lambda i: (i, 0, vc))],
        out_specs=pl.BlockSpec((None, n_ctx, 2 * LANES), lambda i: (i, 0, 0)),
        out_shape=jax.ShapeDtypeStruct((b, n_ctx, 2 * LANES), BF16),
        compiler_params=_cparams(1),
        name=name,
    )(a16_ctx, a16_ctx, a16_ctx)


def _log_sigmoid(x):
    return jnp.minimum(x, 0.0) - jnp.log(1.0 + jnp.exp(-jnp.abs(x)))


def _ret_kernel(ql_ref, kl_ref, vl_ref, gfl_ref, gbl_ref, qc_ref, kc_ref, vc_ref, gfc_ref, gbc_ref,
                dq_ref, dv_ref, dc_ref, ng_ref, ol_ref, oc_ref, acc_ref):
    c = RET_CHUNK
    n_lat, n_ctx = ql_ref.shape[0], qc_ref.shape[0]
    qk_w = RET_HEADS * RET_QK
    v_w = RET_HEADS * RET_V

    row_i = lax.broadcasted_iota(jnp.int32, (c, 1), 0).astype(F32)
    qk_head = lax.broadcasted_iota(jnp.int32, (1, qk_w), 1) // RET_QK
    v_head = lax.broadcasted_iota(jnp.int32, (1, v_w), 1) // RET_V
    qk_masks = [qk_head == hd for hd in range(RET_HEADS)]
    v_masks = [v_head == hd for hd in range(RET_HEADS)]
    state_rows = lax.broadcasted_iota(jnp.int32, (qk_w, 1), 0) // RET_QK
    bd_mask = (state_rows == v_head).astype(F32)
    ones_rows = lax.broadcasted_iota(jnp.int32, (v_w, 1), 0) // RET_V
    head_ones = (ones_rows == v_head).astype(BF16)
    ci = lax.broadcasted_iota(jnp.int32, (c, RET_HEADS * c), 0).astype(F32)
    cj = (lax.broadcasted_iota(jnp.int32, (c, RET_HEADS * c), 1) % c).astype(F32)

    def make_step(d):
        lg_q = _log_sigmoid(dq_ref[d:d + 1, :])
        lg_v = _log_sigmoid(dv_ref[d:d + 1, :])
        lg_c = _log_sigmoid(dc_ref[d:d + 1, :])
        if d == 0:
            diff = ci - cj
            q_dec = jnp.exp(lg_q * (row_i + 1.0))
            k_dec = jnp.exp(lg_q * (c - 1.0 - row_i))
        else:
            diff = cj - ci
            q_dec = jnp.exp(lg_q * (c - row_i))
            k_dec = jnp.exp(lg_q * row_i)
        decay = jnp.where(diff >= 0.0, jnp.exp(lg_c * jnp.maximum(diff, 0.0)), 0.0)
        chunk_dec = jnp.exp(lg_v * float(c))

        def step(refs, r0, acc_r0, state):
            q_ref, k_ref, v_ref = refs
            q = q_ref[pl.ds(r0, c), :]
            k = k_ref[pl.ds(r0, c), :]
            v = v_ref[pl.ds(r0, c), :]
            kb, vb = k.astype(BF16), v.astype(BF16)
            k_bd = jnp.concatenate([jnp.where(m, kb, 0) for m in qk_masks], axis=0)
            scores = _dot_nt(q.astype(BF16), k_bd) * decay
            v_bd = jnp.concatenate([jnp.where(m, vb, 0) for m in v_masks], axis=0)
            inner = _dot(scores.astype(BF16), v_bd)
            cross = _dot((q * q_dec).astype(BF16), state.astype(BF16))
            kv = _dot_tn((k * k_dec).astype(BF16), vb)
            new_state = state * chunk_dec + kv * bd_mask
            acc_ref[d, pl.ds(acc_r0, c), :] = inner + cross
            return new_state

        return step

    fwd, bwd = make_step(0), make_step(1)

    def scan(n, acc_base, fwd_refs, bwd_refs, states):
        def body(i, st):
            rf = pl.multiple_of(i * c, c)
            rb = pl.multiple_of((n - 1 - i) * c, c)
            return fwd(fwd_refs, rf, acc_base + rf, st[0]), bwd(bwd_refs, rb, acc_base + rb, st[1])
        return lax.fori_loop(0, n, body, states, unroll=min(n, 4))

    zero = jnp.zeros((qk_w, v_w), F32)
    states = scan(n_ctx // c, n_lat, (qc_ref, kc_ref, vc_ref), (qc_ref, kc_ref, vc_ref), (zero, zero))
    scan(n_lat // c, 0, (ql_ref, kl_ref, vl_ref), (ql_ref, kl_ref, vl_ref), states)

    def readout(o_ref, gate_refs, acc_base):
        rows = NORM_ROWS

        def body(i, carry):
            r0 = pl.multiple_of(i * rows, rows)
            y = None
            for d in range(2):
                o = acc_ref[d, pl.ds(acc_base + r0, rows), :]
                o2 = o * o
                o2_hi = o2.astype(BF16)
                o2_lo = (o2 - o2_hi.astype(F32)).astype(BF16)
                ss = _dot(o2_hi, head_ones) + _dot(o2_lo, head_ones)
                yd = o * lax.rsqrt(ss * (1.0 / RET_V) + EPS) * ng_ref[d:d + 1, :] * gate_refs[d][pl.ds(r0, rows), :]
                y = yd if y is None else y + yd
            o_ref[pl.ds(r0, rows), :] = y.astype(BF16)
            return carry

        n = o_ref.shape[0] // rows
        lax.fori_loop(0, n, body, 0, unroll=min(n, 4))

    readout(ol_ref, (gfl_ref, gbl_ref), 0)
    readout(oc_ref, (gfc_ref, gbc_ref), n_lat)


def _retention(a32_lat, a32_ctx, smalls, layer):
    b, n_lat, _ = a32_lat.shape
    n_ctx = a32_ctx.shape[1]

    def seq_specs(n):
        spec = lambda off, w: pl.BlockSpec((None, n, w), lambda i: (i, 0, off // w))
        return [spec(A32_RQ, 128), spec(A32_RK, 128), spec(A32_RV, 256), spec(A32_GF, 256), spec(A32_GB, 256)]

    out = lambda n: pl.BlockSpec((None, n, 256), lambda i: (i, 0, 0))
    return pl.pallas_call(
        _ret_kernel,
        grid=(b,),
        in_specs=seq_specs(n_lat) + seq_specs(n_ctx) + [_resident(a, layer) for a in smalls],
        out_specs=[out(n_lat), out(n_ctx)],
        out_shape=[jax.ShapeDtypeStruct((b, n_lat, 256), BF16), jax.ShapeDtypeStruct((b, n_ctx, 256), BF16)],
        scratch_shapes=[pltpu.VMEM((2, n_lat + n_ctx, 256), F32)],
        compiler_params=_cparams(1),
        name="retention",
    )(*([a32_lat] * 5), *([a32_ctx] * 5), *smalls)


def _out_mlp_kernel(x_ref, ya_ref, yb_ref, yc_ref, yd_ref, mod_ref, wo_ref, w1_ref, w2_ref, *rest, ff_chunk):
    o_ref = rest[-1]
    x = x_ref[...]
    gw = GROUP_W
    y = (_dot(ya_ref[...], wo_ref[0 * gw:1 * gw, :]) + _dot(yb_ref[...], wo_ref[1 * gw:2 * gw, :])
         + _dot(yc_ref[...], wo_ref[2 * gw:3 * gw, :]) + _dot(yd_ref[...], wo_ref[3 * gw:4 * gw, :]))
    x1 = x + mod_ref[2:3, :] * y
    h = (_rms(x1) * (1.0 + mod_ref[4:5, :]) + mod_ref[3:4, :]).astype(BF16)
    acc = jnp.zeros_like(x1)
    for j in range(w1_ref.shape[1] // ff_chunk):
        a = jnp.maximum(_dot(h, w1_ref[:, j * ff_chunk:(j + 1) * ff_chunk]), 0.0)
        acc = acc + _dot((a * a).astype(BF16), w2_ref[j * ff_chunk:(j + 1) * ff_chunk, :])
    x2 = x1 + mod_ref[5:6, :] * acc
    o_ref[...] = x2 if len(rest) == 1 else _rms(x2) * rest[0][...]


def _out_mlp(xseq, ys, mods, weights, layer, tm, in_place, final_gain=None):
    b, n, d = xseq.shape
    per_batch = mods.shape[0] == b
    tok = lambda w: pl.BlockSpec((None, tm, w), lambda i, j: (i, j, 0))
    in_specs = [tok(d)] + [tok(GROUP_W)] * 4
    in_specs += [pl.BlockSpec((None, N_MOD, d), lambda i, j: (i if per_batch else 0, 0, 0))]
    in_specs += [_resident(w, layer) for w in weights]
    args = [xseq, *ys, mods, *weights]
    if final_gain is not None:
        in_specs.append(_resident(final_gain))
        args.append(final_gain)
    return pl.pallas_call(
        functools.partial(_out_mlp_kernel, ff_chunk=1024),
        grid=(b, n // tm),
        in_specs=in_specs,
        out_specs=tok(d),
        out_shape=jax.ShapeDtypeStruct((b, n, d), F32),
        input_output_aliases={0: 0} if in_place else {},
        compiler_params=_cparams(2),
        name="out_mlp",
    )(*args)


def _rope_tables(n_lat):
    n = np.arange(n_lat)
    row, col = (n // GRID_W).astype(np.float64), (n % GRID_W).astype(np.float64)

    def table(groups):
        cos = np.ones((n_lat, LANES)); up = np.zeros((n_lat, LANES)); dn = np.zeros((n_lat, LANES))
        for rot_start, rot_dim in groups:
            half = rot_dim // 2
            q = half // 2
            freqs = ROPE_BASE ** (-np.arange(q, dtype=np.float64) / q)
            for axis, pos in enumerate((row, col)):
                ang = pos[:, None] * freqs[None, :]
                base = rot_start + axis * half
                cos[:, base:base + q] = np.cos(ang)
                cos[:, base + q:base + 2 * q] = np.cos(ang)
                up[:, base:base + q] = -np.sin(ang)
                dn[:, base + q:base + 2 * q] = np.sin(ang)
        return jnp.asarray(np.stack([cos, up, dn]).astype(np.float32))

    return table([(0, GQA_HEAD_DIM), (GQA_HEAD_DIM, GQA_HEAD_DIM)]), table([(MLA_NOPE, MLA_ROPE)])


def _layout_w_in(w_in):
    lead = w_in.shape[:-1]
    kpe_src = OFF_CKV + MLA_KV_RANK
    w = w_in.astype(BF16)
    out = jnp.concatenate([w[..., :kpe_src], jnp.zeros(lead + (MLA_NOPE,), BF16),
                           w[..., kpe_src:kpe_src + MLA_ROPE],
                           jnp.zeros(lead + (LANES - MLA_NOPE - MLA_ROPE,), BF16),
                           w[..., kpe_src + MLA_ROPE:]], axis=-1)
    assert out.shape[-1] == N_IN
    return out


def _layout_w_uq(w_uq):
    lead = w_uq.shape[:-1]
    w = w_uq.reshape(lead + (MLA_HEADS, MLA_NOPE + MLA_ROPE))
    w = jnp.pad(w, ((0, 0),) * (w.ndim - 1) + ((0, LANES - MLA_NOPE - MLA_ROPE),))
    return w.reshape(lead + (MLA_HEADS * LANES,)).astype(BF16)


def _layout_w_ukv(w_ukv):
    lead = w_ukv.shape[:-1]
    w = w_ukv.reshape(lead + (MLA_HEADS, MLA_NOPE + MLA_V))
    pad = lambda a: jnp.pad(a, ((0, 0),) * (a.ndim - 1) + ((0, LANES - a.shape[-1]),)).reshape(
        lead + (MLA_HEADS * LANES,))
    return jnp.concatenate([pad(w[..., :MLA_NOPE]), pad(w[..., MLA_NOPE:])], axis=-1).astype(BF16)


def kernel(x, c, ctx, c_ctx, w_mod, b_mod, w_in, w_out, conv_dw, conv_b, conv_ln_g, conv_ln_b, conv_pw, mla_q_g, mla_kv_g, mla_uq, mla_ukv, gqa_q_g, gqa_k_g, ret_decay, ret_norm_g, mlp_w1, mlp_w2, final_g):
    b, n_lat, d = x.shape
    n_ctx = ctx.shape[1]
    depth = w_mod.shape[0]
    assert n_lat % LAT_TILE == 0 and n_lat % PROJ_TILE == 0 and n_lat % KEY_CHUNK == 0 and n_ctx % KEY_CHUNK == 0
    assert n_ctx % NORM_ROWS == 0 and n_lat % NORM_ROWS == 0 and NORM_ROWS % (2 * RET_CHUNK) == 0

    rows = ((b + 1 + SUBLANES - 1) // SUBLANES) * SUBLANES
    cond = jnp.zeros((rows, d), F32).at[:b].set(c).at[b].set(c_ctx)
    mod_all = _adaln(cond, w_mod, b_mod).reshape(depth, rows, N_MOD, d)
    tables = _rope_tables(n_lat)
    mla = (A16_MQ, 4 * LANES, A16_MK, A16_MV, 4 * LANES, (0, 1, 2, 3), (0, 1, 2, 3))
    gqa = (A16_GQ, 2 * LANES, A16_GK, A16_GV, 2 * LANES, (0, 0, 1, 1), (0, 0, 1, 1))

    row = lambda a: a[:, None, :]
    w_proj = (_layout_w_in(w_in), _layout_w_uq(mla_uq), _layout_w_ukv(mla_ukv), row(mla_q_g), row(mla_kv_g),
              row(jnp.tile(gqa_q_g, (1, 2))), row(jnp.tile(gqa_k_g, (1, 2))))
    w_conv = (conv_dw, row(conv_b), row(conv_ln_g), row(conv_ln_b), conv_pw.astype(BF16))
    dec = ret_decay.astype(F32)
    w_ret = (jnp.repeat(dec, RET_QK, axis=2), jnp.repeat(dec, RET_V, axis=2), jnp.repeat(dec, RET_CHUNK, axis=2),
             ret_norm_g.reshape(depth, 2, RET_HEADS * RET_V))
    w_mlp = (w_out.astype(BF16), mlp_w1.astype(BF16), mlp_w2.astype(BF16))

    cx = ctx
    for layer in range(depth):
        last = layer == depth - 1
        mods_lat, mods_ctx = mod_all[layer, :b], mod_all[layer, b:b + 1]
        a16_lat, a32_lat = _in_proj(x, mods_lat, w_proj, layer, tables, PROJ_TILE)
        a16_ctx, a32_ctx = _in_proj(cx, mods_ctx, w_proj, layer, None, n_ctx)
        ya_lat, ya_ctx = _conv(a32_lat, w_conv, layer), _conv(a32_ctx, w_conv, layer)
        yb_lat = _attention_lat(a16_lat, a16_ctx, *mla, "mla_attn")
        yc_lat = _attention_lat(a16_lat, a16_ctx, *gqa, "gqa_attn")
        yd_lat, yd_ctx = _retention(a32_lat, a32_ctx, w_ret, layer)
        x = _out_mlp(x, (ya_lat, yb_lat, yc_lat, yd_lat), mods_lat, w_mlp, layer, LAT_TILE, layer > 0,
                     final_gain=final_g[None, :] if last else None)
        if not last:
            yb_ctx = _attention_ctx(a16_ctx, *mla, "mla_attn_ctx")
            yc_ctx = _attention_ctx(a16_ctx, *gqa, "gqa_attn_ctx")
            cx = _out_mlp(cx, (ya_ctx, yb_ctx, yc_ctx, yd_ctx), mods_ctx, w_mlp, layer, n_ctx, layer > 0)
    return x
```

```python
import functools

import numpy as np
import jax
import jax.numpy as jnp
from jax import lax
from jax.experimental import pallas as pl
from jax.experimental.pallas import tpu as pltpu

F32 = jnp.float32
BF16 = jnp.bfloat16

GRID_W = 64
N_MOD = 6
EPS = 1e-6
ROPE_BASE = 10000.0
GROUP_W = 256
CONV_WIDTH = 31
CONV_PAD = 16
MLA_HEADS = 4
MLA_NOPE = 64
MLA_ROPE = 32
MLA_V = 64
MLA_Q_RANK = 256
MLA_KV_RANK = 128
GQA_HEADS = 4
GQA_KV_HEADS = 2
GQA_HEAD_DIM = 64
RET_HEADS = 4
RET_QK = 32
RET_V = 64
RET_CHUNK = 128
LANES = 128
SUBLANES = 8
LOG2_E = 1.4426950408889634
LAT_TILE = 512
PROJ_TILE = 1024
KEY_CHUNK = 256
NORM_ROWS = 256
VMEM_LIMIT = 56 * 1024 * 1024

OFF_A, OFF_CQ, OFF_CKV, OFF_GQ, OFF_GK = 0, 512, 768, 1024, 1280
OFF_RQ, OFF_RV, OFF_GF, OFF_GB, N_IN = 1536, 1792, 2048, 2304, 2560
A16_MQ, A16_MK, A16_MV, A16_GK, A16_GQ, A16_GV, A16_W = 0, 512, 1024, 1536, 2048, 2304, 2560
A32_Y, A32_RQ, A32_RK, A32_RV, A32_GF, A32_GB, A32_W = 0, 256, 384, 512, 768, 1024, 1280


def _cparams(n_axes):
    return pltpu.CompilerParams(dimension_semantics=("arbitrary",) * n_axes,
                                vmem_limit_bytes=VMEM_LIMIT)


def _resident(a, layer=None):
    if layer is None:
        return pl.BlockSpec(a.shape, lambda *_: (0,) * a.ndim, pipeline_mode=pl.Buffered(1))
    return pl.BlockSpec((None,) + a.shape[1:], lambda *_: (layer,) + (0,) * (a.ndim - 1),
                        pipeline_mode=pl.Buffered(1))


def _rms(x):
    return x * lax.rsqrt(jnp.mean(x * x, axis=-1, keepdims=True) + EPS)


def _sigmoid(x):
    return 1.0 / (1.0 + jnp.exp(-x))


def _silu(x):
    return x * _sigmoid(x)


def _dot(a, b):
    return jnp.dot(a, b, preferred_element_type=F32)


def _dot_nt(a, b):
    return lax.dot_general(a, b, (((1,), (1,)), ((), ())), preferred_element_type=F32)


def _dot_tn(a, b):
    return lax.dot_general(a, b, (((0,), (0,)), ((), ())), preferred_element_type=F32)


def _rope(x, cos, sin_up, sin_dn, shift):
    n = x.shape[-1]
    return x * cos + pltpu.roll(x, n - shift, 1) * sin_up + pltpu.roll(x, shift, 1) * sin_dn


def _adaln_kernel(cond_ref, w_ref, b_ref, o_ref):
    cond = cond_ref[...]
    o_ref[...] = _dot(_silu(cond).astype(BF16), w_ref[...].astype(BF16)) + b_ref[...]


def _adaln(cond, w_mod, b_mod):
    depth, d, n = w_mod.shape
    r = cond.shape[0]
    tn = 1536
    return pl.pallas_call(
        _adaln_kernel,
        grid=(depth, n // tn),
        in_specs=[pl.BlockSpec((r, d), lambda l, j: (0, 0)),
                  pl.BlockSpec((None, d, tn), lambda l, j: (l, 0, j)),
                  pl.BlockSpec((None, 1, tn), lambda l, j: (l, 0, j))],
        out_specs=pl.BlockSpec((None, r, tn), lambda l, j: (l, 0, j)),
        out_shape=jax.ShapeDtypeStruct((depth, r, n), F32),
        compiler_params=_cparams(2),
        name="adaln",
    )(cond, w_mod, b_mod.reshape(depth, 1, n))


def _in_proj_kernel(x_ref, mod_ref, w_in_ref, w_uq_ref, w_ukv_ref, g_mq_ref, g_mkv_ref, g_gq_ref, g_gk_ref,
                    *rest, rotary, sub_tiles):
    a16_ref, a32_ref = rest[-2:]
    lo = lax.broadcasted_iota(jnp.int32, (1, LANES), 1) < (LANES // 2)
    q_scale = (MLA_NOPE + MLA_ROPE) ** -0.5 * LOG2_E
    n_rows = x_ref.shape[0] // sub_tiles

    for st in range(sub_tiles):
        rows = slice(st * n_rows, (st + 1) * n_rows)
        h = (_rms(x_ref[rows, :]) * (1.0 + mod_ref[1:2, :]) + mod_ref[0:1, :]).astype(BF16)

        def proj(off, width, h=h):
            return _dot(h, w_in_ref[:, off:off + width])

        def put16(off, val, rows=rows):
            a16_ref[rows, off:off + val.shape[1]] = val.astype(BF16)

        def put32(off, val, rows=rows):
            a32_ref[rows, off:off + val.shape[1]] = val

        if rotary:
            tg_ref, tm_ref = rest[:2]
            rope_g = lambda v, rows=rows: _rope(v, tg_ref[0, rows, :], tg_ref[1, rows, :], tg_ref[2, rows, :],
                                                GQA_HEAD_DIM // 4)
            rope_m = lambda v, rows=rows: _rope(v, tm_ref[0, rows, :], tm_ref[1, rows, :], tm_ref[2, rows, :],
                                                MLA_ROPE // 4)
        else:
            rope_g = rope_m = lambda v: v

        def pair_norm_rope(xp, gain, rope_g=rope_g):
            x2 = xp * xp
            s_lo = jnp.sum(jnp.where(lo, x2, 0.0), axis=-1, keepdims=True)
            s_hi = jnp.sum(jnp.where(lo, 0.0, x2), axis=-1, keepdims=True)
            r = jnp.where(lo, lax.rsqrt(s_lo * (1.0 / GQA_HEAD_DIM) + EPS),
                          lax.rsqrt(s_hi * (1.0 / GQA_HEAD_DIM) + EPS))
            return rope_g(xp * r * gain)

        cq = (_rms(proj(OFF_CQ, MLA_Q_RANK)) * g_mq_ref[...]).astype(BF16)
        ckv_kpe = proj(OFF_CKV, 2 * LANES)
        ckv = (_rms(ckv_kpe[:, :LANES]) * g_mkv_ref[...]).astype(BF16)
        gq = proj(OFF_GQ, GQA_HEADS * GQA_HEAD_DIM)
        gkv = proj(OFF_GK, 2 * LANES)

        a = proj(OFF_A, 2 * GROUP_W)
        put32(A32_Y, a[:, :GROUP_W] * _sigmoid(a[:, GROUP_W:]))
        rqk = proj(OFF_RQ, 2 * LANES)
        put32(A32_RQ, rqk[:, :LANES])
        put32(A32_RK, rqk[:, LANES:] * RET_QK ** -0.5)
        put32(A32_RV, proj(OFF_RV, RET_HEADS * RET_V))
        put32(A32_GF, _silu(proj(OFF_GF, GROUP_W)))
        put32(A32_GB, _silu(proj(OFF_GB, GROUP_W)))

        q = _dot(cq, w_uq_ref[...])
        for hd in range(MLA_HEADS):
            put16(A16_MQ + hd * LANES, rope_m(q[:, hd * LANES:(hd + 1) * LANES]) * q_scale)
        kv = _dot(ckv, w_ukv_ref[...])
        kpe = rope_m(ckv_kpe[:, LANES:])
        for hd in range(MLA_HEADS):
            put16(A16_MK + hd * LANES, kv[:, hd * LANES:(hd + 1) * LANES] + kpe)
            put16(A16_MV + hd * LANES,
                  jnp.where(lo, kv[:, (MLA_HEADS + hd) * LANES:(MLA_HEADS + hd + 1) * LANES], 1.0))

        for pr in range(2):
            qp = pair_norm_rope(gq[:, pr * LANES:(pr + 1) * LANES], g_gq_ref[...])
            put16(A16_GQ + pr * LANES, qp * (GQA_HEAD_DIM ** -0.5 * LOG2_E))
        kp = pair_norm_rope(gkv[:, :LANES], g_gk_ref[...])
        kp_sw = pltpu.roll(kp, LANES // 2, 1)
        put16(A16_GK + 0 * LANES, jnp.where(lo, kp, 0.0))
        put16(A16_GK + 1 * LANES, jnp.where(lo, 0.0, kp_sw))
        put16(A16_GK + 2 * LANES, jnp.where(lo, kp_sw, 0.0))
        put16(A16_GK + 3 * LANES, jnp.where(lo, 0.0, kp))
        vp = gkv[:, LANES:]
        put16(A16_GV + 0 * LANES, jnp.where(lo, vp, 1.0))
        put16(A16_GV + 1 * LANES, jnp.where(lo, pltpu.roll(vp, LANES // 2, 1), 1.0))


def _in_proj(xseq, mods, weights, layer, tables, tm):
    b, n, d = xseq.shape
    per_batch = mods.shape[0] == b
    tok = lambda w: pl.BlockSpec((None, tm, w), lambda i, j: (i, j, 0))
    in_specs = [tok(d), pl.BlockSpec((None, N_MOD, d), lambda i, j: (i if per_batch else 0, 0, 0))]
    in_specs += [_resident(w, layer) for w in weights]
    args = [xseq, mods, *weights]
    if tables is not None:
        in_specs += [pl.BlockSpec((3, tm, LANES), lambda i, j: (0, j, 0))] * 2
        args += list(tables)
    return pl.pallas_call(
        functools.partial(_in_proj_kernel, rotary=tables is not None, sub_tiles=tm // 256),
        grid=(b, n // tm),
        in_specs=in_specs,
        out_specs=[tok(A16_W), tok(A32_W)],
        out_shape=[jax.ShapeDtypeStruct((b, n, A16_W), BF16), jax.ShapeDtypeStruct((b, n, A32_W), F32)],
        compiler_params=_cparams(2),
        name="in_proj",
    )(*args)


CONV_ROWS = 64
CONV_FIRST = CONV_PAD - CONV_WIDTH // 2
CONV_TAIL = ((CONV_FIRST + CONV_WIDTH - 1) // SUBLANES) * SUBLANES


def _conv_steps(sh_ref, dw_ref, b_ref, lg_ref, lb_ref, pw_ref, o_ref, length):
    def fill(before, body, after):
        sh_ref[0, 0:CONV_PAD, :] = before
        sh_ref[0, CONV_PAD:CONV_PAD + length, :] = body
        sh_ref[0, CONV_PAD + length:2 * CONV_PAD + length, :] = after

    def shift(base, n=CONV_ROWS):
        win = sh_ref[0, pl.ds(base, n + SUBLANES), :]
        for r in range(1, SUBLANES):
            sh_ref[r, pl.ds(base, n), :] = win[r:r + n, :]

    def shift_tail():
        shift(length, CONV_TAIL)

    def taps(base):
        groups = (CONV_ROWS // SUBLANES, SUBLANES, GROUP_W)
        acc = jnp.zeros(groups, F32) + b_ref[...]
        for k in range(CONV_WIDTH):
            off = CONV_FIRST + k
            slab = sh_ref[off % SUBLANES, pl.ds(base + (off // SUBLANES) * SUBLANES, CONV_ROWS), :]
            acc = acc + slab.reshape(groups) * dw_ref[k]
        sh_ref[0, pl.ds(base, CONV_ROWS), :] = acc.reshape(CONV_ROWS, GROUP_W)

    def finish(base):
        acc = sh_ref[0, pl.ds(base, NORM_ROWS), :]
        mu = jnp.mean(acc, axis=-1, keepdims=True)
        cen = acc - mu
        var = jnp.mean(cen * cen, axis=-1, keepdims=True)
        z = _silu(cen * lax.rsqrt(var + EPS) * lg_ref[...] + lb_ref[...])
        o_ref[pl.ds(base, NORM_ROWS), :] = _dot(z.astype(BF16), pw_ref[...]).astype(BF16)

    return fill, shift, shift_tail, taps, finish


def _conv_kernel(y_ref, dw_ref, b_ref, lg_ref, lb_ref, pw_ref, o_ref, sh_ref):
    length = y_ref.shape[0]
    fill, shift, shift_tail, taps, finish = _conv_steps(sh_ref, dw_ref, b_ref, lg_ref, lb_ref, pw_ref, o_ref, length)
    zeros = jnp.zeros((CONV_PAD, GROUP_W), F32)
    fill(zeros, y_ref[...], zeros)

    def loop(step, rows, **kw):
        def body(c, carry):
            step(pl.multiple_of(c * rows, rows))
            return carry
        lax.fori_loop(0, length // rows, body, 0, **kw)

    loop(shift, CONV_ROWS)
    shift_tail()
    loop(taps, CONV_ROWS)
    loop(finish, NORM_ROWS, unroll=min(length // NORM_ROWS, 4))


def _conv(a32, weights, layer):
    b, n, _ = a32.shape
    return pl.pallas_call(
        _conv_kernel,
        grid=(b,),
        in_specs=[pl.BlockSpec((None, n, GROUP_W), lambda i: (i, 0, A32_Y // GROUP_W))]
        + [_resident(w, layer) for w in weights],
        out_specs=pl.BlockSpec((None, n, GROUP_W), lambda i: (i, 0, 0)),
        out_shape=jax.ShapeDtypeStruct((b, n, GROUP_W), BF16),
        scratch_shapes=[pltpu.VMEM((SUBLANES, n + 2 * CONV_PAD, GROUP_W), F32)],
        compiler_params=_cparams(1),
        name="conv",
    )(a32, *weights)


def _normalized_pair(acc_even, acc_odd):
    half = LANES // 2
    lo = lax.broadcasted_iota(jnp.int32, (1, LANES), 1) < half
    return jnp.where(lo, acc_even / pltpu.roll(acc_even, half, 1), pltpu.roll(acc_odd, half, 1) / acc_odd)


def _attn_lat_kernel(q_ref, qn_ref, kl_ref, kc_ref, vl_ref, vc_ref, o_ref, s_ref, m_ref, mb_ref, acc_ref,
                     *, q_blocks, v_blocks):
    tq = q_ref.shape[0]
    tk = KEY_CHUNK
    chunks = [(kl_ref, vl_ref, c * tk) for c in range(kl_ref.shape[0] // tk)]
    chunks += [(kc_ref, vc_ref, c * tk) for c in range(kc_ref.shape[0] // tk)]

    def scores_chunk(hd, ci, ref):
        slot = hd % 2
        k_ref, _, r0 = chunks[ci]
        q = ref[:, q_blocks[hd] * LANES:(q_blocks[hd] + 1) * LANES]
        s = _dot_nt(q, k_ref[r0:r0 + tk, hd * LANES:(hd + 1) * LANES])
        s_ref[slot, :, ci * tk:(ci + 1) * tk] = s
        mx = s[:, 0:LANES]
        for i in range(1, tk // LANES):
            mx = jnp.maximum(mx, s[:, i * LANES:(i + 1) * LANES])
        m_ref[slot] = mx if ci == 0 else jnp.maximum(m_ref[slot], mx)

    def scores_finish(hd):
        slot = hd % 2
        mb_ref[slot] = jnp.broadcast_to(jnp.max(m_ref[slot], axis=-1, keepdims=True), (tq, LANES))

    def values_chunk(hd, ci):
        slot = hd % 2
        _, v_ref, r0 = chunks[ci]
        mb = mb_ref[slot]
        p = jnp.concatenate(
            [jnp.exp2(s_ref[slot, :, ci * tk + i * LANES:ci * tk + (i + 1) * LANES] - mb).astype(BF16)
             for i in range(tk // LANES)], axis=1)
        pv = _dot(p, v_ref[r0:r0 + tk, v_blocks[hd] * LANES:(v_blocks[hd] + 1) * LANES])
        acc_ref[slot] = pv if ci == 0 else acc_ref[slot] + pv

    n = len(chunks)

    def scores_ahead(unit, ci):
        scores_chunk(unit % 4, ci, q_ref if unit < 4 else qn_ref)

    @pl.when(pl.program_id(1) == 0)
    def _():
        for ci in range(n):
            scores_chunk(0, ci, q_ref)
        scores_finish(0)
        scores_chunk(1, 0, q_ref)

    for hd in range(4):
        for ci in range(n):
            values_chunk(hd, ci)
            if ci + 1 < n:
                scores_ahead(hd + 1, ci + 1)
            else:
                scores_ahead(hd + 2, 0)
            if ci + 2 == n:
                scores_finish((hd + 1) % 4)
        if hd % 2 == 1:
            pr = hd // 2
            o_ref[:, pr * LANES:(pr + 1) * LANES] = _normalized_pair(acc_ref[0], acc_ref[1]).astype(BF16)


def _attention_lat(a16_lat, a16_ctx, q_off, q_width, k_off, v_off, v_width, q_blocks, v_blocks, name):
    b, n_lat, _ = a16_lat.shape
    n_ctx = a16_ctx.shape[1]
    tq = LAT_TILE
    n_tiles = n_lat // tq
    kw = 4 * LANES
    qc, kc, vc = q_off // q_width, k_off // kw, v_off // v_width
    return pl.pallas_call(
        functools.partial(_attn_lat_kernel, q_blocks=q_blocks, v_blocks=v_blocks),
        grid=(b, n_tiles),
        in_specs=[pl.BlockSpec((None, tq, q_width), lambda i, j: (i, j, qc)),
                  pl.BlockSpec((None, tq, q_width), lambda i, j: (i, jnp.minimum(j + 1, n_tiles - 1), qc)),
                  pl.BlockSpec((None, n_lat, kw), lambda i, j: (i, 0, kc)),
                  pl.BlockSpec((None, n_ctx, kw), lambda i, j: (i, 0, kc)),
                  pl.BlockSpec((None, n_lat, v_width), lambda i, j: (i, 0, vc)),
                  pl.BlockSpec((None, n_ctx, v_width), lambda i, j: (i, 0, vc))],
        out_specs=pl.BlockSpec((None, tq, 2 * LANES), lambda i, j: (i, j, 0)),
        out_shape=jax.ShapeDtypeStruct((b, n_lat, 2 * LANES), BF16),
        scratch_shapes=[pltpu.VMEM((2, tq, n_lat + n_ctx), F32), pltpu.VMEM((2, tq, LANES), F32),
                        pltpu.VMEM((2, tq, LANES), F32), pltpu.VMEM((2, tq, LANES), F32)],
        compiler_params=_cparams(2),
        name=name,
    )(a16_lat, a16_lat, a16_lat, a16_ctx, a16_lat, a16_ctx)


def _attn_ctx_kernel(q_ref, k_ref, v_ref, o_ref, *, q_blocks, v_blocks):
    for pr in range(2):
        accs = []
        for half in range(2):
            hd = 2 * pr + half
            q = q_ref[:, q_blocks[hd] * LANES:(q_blocks[hd] + 1) * LANES]
            s = _dot_nt(q, k_ref[:, hd * LANES:(hd + 1) * LANES])
            p = jnp.exp2(s - jnp.max(s, axis=-1, keepdims=True))
            accs.append(_dot(p.astype(BF16), v_ref[:, v_blocks[hd] * LANES:(v_blocks[hd] + 1) * LANES]))
        o_ref[:, pr * LANES:(pr + 1) * LANES] = _normalized_pair(*accs).astype(BF16)


def _attention_ctx(a16_ctx, q_off, q_width, k_off, v_off, v_width, q_blocks, v_blocks, name):
    b, n_ctx, _ = a16_ctx.shape
    kw = 4 * LANES
    qc, kc, vc = q_off // q_width, k_off // kw, v_off // v_width
    return pl.pallas_call(
        functools.partial(_attn_ctx_kernel, q_blocks=q_blocks, v_blocks=v_blocks),
        grid=(b,),
        in_specs=[pl.BlockSpec((None, n_ctx, q_width), lambda i: (i, 0, qc)),
                  pl.BlockSpec((None, n_ctx, kw), lambda i: (i, 0, kc)),
                  pl.BlockSpec((None, n_ctx, v_width), lambda i: (i, 0, vc))],
        out_specs=pl.BlockSpec((None, n_ctx, 2 * LANES), lambda i: (i, 0, 0)),
        out_shape=jax.ShapeDtypeStruct((b, n_ctx, 2 * LANES), BF16),
        compiler_params=_cparams(1),
        name=name,
    )(a16_ctx, a16_ctx, a16_ctx)


def _log_sigmoid(x):
    return jnp.minimum(x, 0.0) - jnp.log(1.0 + jnp.exp(-jnp.abs(x)))


def _ret_kernel(ql_ref, kl_ref, vl_ref, gfl_ref, gbl_ref, qc_ref, kc_ref, vc_ref, gfc_ref, gbc_ref,
                dq_ref, dv_ref, dc_ref, ng_ref, ol_ref, oc_ref, acc_ref):
    c = RET_CHUNK
    n_lat, n_ctx = ql_ref.shape[0], qc_ref.shape[0]
    qk_w = RET_HEADS * RET_QK
    v_w = RET_HEADS * RET_V

    row_i = lax.broadcasted_iota(jnp.int32, (c, 1), 0).astype(F32)
    qk_head = lax.broadcasted_iota(jnp.int32, (1, qk_w), 1) // RET_QK
    v_head = lax.broadcasted_iota(jnp.int32, (1, v_w), 1) // RET_V
    qk_masks = [qk_head == hd for hd in range(RET_HEADS)]
    v_masks = [v_head == hd for hd in range(RET_HEADS)]
    state_rows = lax.broadcasted_iota(jnp.int32, (qk_w, 1), 0) // RET_QK
    bd_mask = (state_rows == v_head).astype(F32)
    ones_rows = lax.broadcasted_iota(jnp.int32, (v_w, 1), 0) // RET_V
    head_ones = (ones_rows == v_head).astype(BF16)
    ci = lax.broadcasted_iota(jnp.int32, (c, RET_HEADS * c), 0).astype(F32)
    cj = (lax.broadcasted_iota(jnp.int32, (c, RET_HEADS * c), 1) % c).astype(F32)

    def make_step(d):
        lg_q = _log_sigmoid(dq_ref[d:d + 1, :])
        lg_v = _log_sigmoid(dv_ref[d:d + 1, :])
        lg_c = _log_sigmoid(dc_ref[d:d + 1, :])
        if d == 0:
            diff = ci - cj
            q_dec = jnp.exp(lg_q * (row_i + 1.0))
            k_dec = jnp.exp(lg_q * (c - 1.0 - row_i))
        else:
            diff = cj - ci
            q_dec = jnp.exp(lg_q * (c - row_i))
            k_dec = jnp.exp(lg_q * row_i)
        decay = jnp.where(diff >= 0.0, jnp.exp(lg_c * jnp.maximum(diff, 0.0)), 0.0)
        chunk_dec = jnp.exp(lg_v * float(c))

        def step(refs, r0, acc_r0, state):
            q_ref, k_ref, v_ref = refs
            q = q_ref[pl.ds(r0, c), :]
            k = k_ref[pl.ds(r0, c), :]
            v = v_ref[pl.ds(r0, c), :]
            kb, vb = k.astype(BF16), v.astype(BF16)
            k_bd = jnp.concatenate([jnp.where(m, kb, 0) for m in qk_masks], axis=0)
            scores = _dot_nt(q.astype(BF16), k_bd) * decay
            v_bd = jnp.concatenate([jnp.where(m, vb, 0) for m in v_masks], axis=0)
            inner = _dot(scores.astype(BF16), v_bd)
            cross = _dot((q * q_dec).astype(BF16), state.astype(BF16))
            kv = _dot_tn((k * k_dec).astype(BF16), vb)
            new_state = state * chunk_dec + kv * bd_mask
            acc_ref[d, pl.ds(acc_r0, c), :] = inner + cross
            return new_state

        return step

    fwd, bwd = make_step(0), make_step(1)

    def scan(n, acc_base, fwd_refs, bwd_refs, states):
        def body(i, st):
            rf = pl.multiple_of(i * c, c)
            rb = pl.multiple_of((n - 1 - i) * c, c)
            return fwd(fwd_refs, rf, acc_base + rf, st[0]), bwd(bwd_refs, rb, acc_base + rb, st[1])
        return lax.fori_loop(0, n, body, states, unroll=min(n, 4))

    zero = jnp.zeros((qk_w, v_w), F32)
    states = scan(n_ctx // c, n_lat, (qc_ref, kc_ref, vc_ref), (qc_ref, kc_ref, vc_ref), (zero, zero))
    scan(n_lat // c, 0, (ql_ref, kl_ref, vl_ref), (ql_ref, kl_ref, vl_ref), states)

    def readout(o_ref, gate_refs, acc_base):
        rows = NORM_ROWS

        def body(i, carry):
            r0 = pl.multiple_of(i * rows, rows)
            y = None
            for d in range(2):
                o = acc_ref[d, pl.ds(acc_base + r0, rows), :]
                o2 = o * o
                o2_hi = o2.astype(BF16)
                o2_lo = (o2 - o2_hi.astype(F32)).astype(BF16)
                ss = _dot(o2_hi, head_ones) + _dot(o2_lo, head_ones)
                yd = o * lax.rsqrt(ss * (1.0 / RET_V) + EPS) * ng_ref[d:d + 1, :] * gate_refs[d][pl.ds(r0, rows), :]
                y = yd if y is None else y + yd
            o_ref[pl.ds(r0, rows), :] = y.astype(BF16)
            return carry

        n = o_ref.shape[0] // rows
        lax.fori_loop(0, n, body, 0, unroll=min(n, 4))

    readout(ol_ref, (gfl_ref, gbl_ref), 0)
    readout(oc_ref, (gfc_ref, gbc_ref), n_lat)


def _retention(a32_lat, a32_ctx, smalls, layer):
    b, n_lat, _ = a32_lat.shape
    n_ctx = a32_ctx.shape[1]

    def seq_specs(n):
        spec = lambda off, w: pl.BlockSpec((None, n, w), lambda i: (i, 0, off // w))
        return [spec(A32_RQ, 128), spec(A32_RK, 128), spec(A32_RV, 256), spec(A32_GF, 256), spec(A32_GB, 256)]

    out = lambda n: pl.BlockSpec((None, n, 256), lambda i: (i, 0, 0))
    return pl.pallas_call(
        _ret_kernel,
        grid=(b,),
        in_specs=seq_specs(n_lat) + seq_specs(n_ctx) + [_resident(a, layer) for a in smalls],
        out_specs=[out(n_lat), out(n_ctx)],
        out_shape=[jax.ShapeDtypeStruct((b, n_lat, 256), BF16), jax.ShapeDtypeStruct((b, n_ctx, 256), BF16)],
        scratch_shapes=[pltpu.VMEM((2, n_lat + n_ctx, 256), F32)],
        compiler_params=_cparams(1),
        name="retention",
    )(*([a32_lat] * 5), *([a32_ctx] * 5), *smalls)


def _out_mlp_kernel(x_ref, ya_ref, yb_ref, yc_ref, yd_ref, mod_ref, wo_ref, w1_ref, w2_ref, *rest, ff_chunk):
    o_ref = rest[-1]
    x = x_ref[...]
    gw = GROUP_W
    y = (_dot(ya_ref[...], wo_ref[0 * gw:1 * gw, :]) + _dot(yb_ref[...], wo_ref[1 * gw:2 * gw, :])
         + _dot(yc_ref[...], wo_ref[2 * gw:3 * gw, :]) + _dot(yd_ref[...], wo_ref[3 * gw:4 * gw, :]))
    x1 = x + mod_ref[2:3, :] * y
    h = (_rms(x1) * (1.0 + mod_ref[4:5, :]) + mod_ref[3:4, :]).astype(BF16)
    acc = jnp.zeros_like(x1)
    for j in range(w1_ref.shape[1] // ff_chunk):
        a = jnp.maximum(_dot(h, w1_ref[:, j * ff_chunk:(j + 1) * ff_chunk]), 0.0)
        acc = acc + _dot((a * a).astype(BF16), w2_ref[j * ff_chunk:(j + 1) * ff_chunk, :])
    x2 = x1 + mod_ref[5:6, :] * acc
    o_ref[...] = x2 if len(rest) == 1 else _rms(x2) * rest[0][...]


def _out_mlp(xseq, ys, mods, weights, layer, tm, in_place, final_gain=None):
    b, n, d = xseq.shape
    per_batch = mods.shape[0] == b
    tok = lambda w: pl.BlockSpec((None, tm, w), lambda i, j: (i, j, 0))
    in_specs = [tok(d)] + [tok(GROUP_W)] * 4
    in_specs += [pl.BlockSpec((None, N_MOD, d), lambda i, j: (i if per_batch else 0, 0, 0))]
    in_specs += [_resident(w, layer) for w in weights]
    args = [xseq, *ys, mods, *weights]
    if final_gain is not None:
        in_specs.append(_resident(final_gain))
        args.append(final_gain)
    return pl.pallas_call(
        functools.partial(_out_mlp_kernel, ff_chunk=1024),
        grid=(b, n // tm),
        in_specs=in_specs,
        out_specs=tok(d),
        out_shape=jax.ShapeDtypeStruct((b, n, d), F32),
        input_output_aliases={0: 0} if in_place else {},
        compiler_params=_cparams(2),
        name="out_mlp",
    )(*args)


def _rope_tables(n_lat):
    n = np.arange(n_lat)
    row, col = (n // GRID_W).astype(np.float64), (n % GRID_W).astype(np.float64)

    def table(groups):
        cos = np.ones((n_lat, LANES)); up = np.zeros((n_lat, LANES)); dn = np.zeros((n_lat, LANES))
        for rot_start, rot_dim in groups:
            half = rot_dim // 2
            q = half // 2
            freqs = ROPE_BASE ** (-np.arange(q, dtype=np.float64) / q)
            for axis, pos in enumerate((row, col)):
                ang = pos[:, None] * freqs[None, :]
                base = rot_start + axis * half
                cos[:, base:base + q] = np.cos(ang)
                cos[:, base + q:base + 2 * q] = np.cos(ang)
                up[:, base:base + q] = -np.sin(ang)
                dn[:, base + q:base + 2 * q] = np.sin(ang)
        return jnp.asarray(np.stack([cos, up, dn]).astype(np.float32))

    return table([(0, GQA_HEAD_DIM), (GQA_HEAD_DIM, GQA_HEAD_DIM)]), table([(MLA_NOPE, MLA_ROPE)])


def _layout_w_in(w_in):
    lead = w_in.shape[:-1]
    kpe_src = OFF_CKV + MLA_KV_RANK
    w = w_in.astype(BF16)
    out = jnp.concatenate([w[..., :kpe_src], jnp.zeros(lead + (MLA_NOPE,), BF16),
                           w[..., kpe_src:kpe_src + MLA_ROPE],
                           jnp.zeros(lead + (LANES - MLA_NOPE - MLA_ROPE,), BF16),
                           w[..., kpe_src + MLA_ROPE:]], axis=-1)
    assert out.shape[-1] == N_IN
    return out


def _layout_w_uq(w_uq):
    lead = w_uq.shape[:-1]
    w = w_uq.reshape(lead + (MLA_HEADS, MLA_NOPE + MLA_ROPE))
    w = jnp.pad(w, ((0, 0),) * (w.ndim - 1) + ((0, LANES - MLA_NOPE - MLA_ROPE),))
    return w.reshape(lead + (MLA_HEADS * LANES,)).astype(BF16)


def _layout_w_ukv(w_ukv):
    lead = w_ukv.shape[:-1]
    w = w_ukv.reshape(lead + (MLA_HEADS, MLA_NOPE + MLA_V))
    pad = lambda a: jnp.pad(a, ((0, 0),) * (a.ndim - 1) + ((0, LANES - a.shape[-1]),)).reshape(
        lead + (MLA_HEADS * LANES,))
    return jnp.concatenate([pad(w[..., :MLA_NOPE]), pad(w[..., MLA_NOPE:])], axis=-1).astype(BF16)


def kernel(x, c, ctx, c_ctx, w_mod, b_mod, w_in, w_out, conv_dw, conv_b, conv_ln_g, conv_ln_b, conv_pw, mla_q_g, mla_kv_g, mla_uq, mla_ukv, gqa_q_g, gqa_k_g, ret_decay, ret_norm_g, mlp_w1, mlp_w2, final_g):
    b, n_lat, d = x.shape
    n_ctx = ctx.shape[1]
    depth = w_mod.shape[0]
    assert n_lat % LAT_TILE == 0 and n_lat % PROJ_TILE == 0 and n_lat % KEY_CHUNK == 0 and n_ctx % KEY_CHUNK == 0
    assert n_ctx % NORM_ROWS == 0 and n_lat % NORM_ROWS == 0 and NORM_ROWS % (2 * RET_CHUNK) == 0

    rows = ((b + 1 + SUBLANES - 1) // SUBLANES) * SUBLANES
    cond = jnp.zeros((rows, d), F32).at[:b].set(c).at[b].set(c_ctx)
    mod_all = _adaln(cond, w_mod, b_mod).reshape(depth, rows, N_MOD, d)
    tables = _rope_tables(n_lat)
    mla = (A16_MQ, 4 * LANES, A16_MK, A16_MV, 4 * LANES, (0, 1, 2, 3), (0, 1, 2, 3))
    gqa = (A16_GQ, 2 * LANES, A16_GK, A16_GV, 2 * LANES, (0, 0, 1, 1), (0, 0, 1, 1))

    row = lambda a: a[:, None, :]
    w_proj = (_layout_w_in(w_in), _layout_w_uq(mla_uq), _layout_w_ukv(mla_ukv), row(mla_q_g), row(mla_kv_g),
              row(jnp.tile(gqa_q_g, (1, 2))), row(jnp.tile(gqa_k_g, (1, 2))))
    w_conv = (jnp.broadcast_to(conv_dw[:, :, None, :], conv_dw.shape[:2] + (SUBLANES, GROUP_W)), row(conv_b), row(conv_ln_g), row(conv_ln_b), conv_pw.astype(BF16))
    dec = ret_decay.astype(F32)
    w_ret = (jnp.repeat(dec, RET_QK, axis=2), jnp.repeat(dec, RET_V, axis=2), jnp.repeat(dec, RET_CHUNK, axis=2),
             ret_norm_g.reshape(depth, 2, RET_HEADS * RET_V))
    w_mlp = (w_out.astype(BF16), mlp_w1.astype(BF16), mlp_w2.astype(BF16))

    cx = ctx
    for layer in range(depth):
        last = layer == depth - 1
        mods_lat, mods_ctx = mod_all[layer, :b], mod_all[layer, b:b + 1]
        a16_lat, a32_lat = _in_proj(x, mods_lat, w_proj, layer, tables, PROJ_TILE)
        a16_ctx, a32_ctx = _in_proj(cx, mods_ctx, w_proj, layer, None, n_ctx)
        ya_lat, ya_ctx = _conv(a32_lat, w_conv, layer), _conv(a32_ctx, w_conv, layer)
        yb_lat = _attention_lat(a16_lat, a16_ctx, *mla, "mla_attn")
        yc_lat = _attention_lat(a16_lat, a16_ctx, *gqa, "gqa_attn")
        yd_lat, yd_ctx = _retention(a32_lat, a32_ctx, w_ret, layer)
        x = _out_mlp(x, (ya_lat, yb_lat, yc_lat, yd_lat), mods_lat, w_mlp, layer, LAT_TILE, layer > 0,
                     final_gain=final_g[None, :] if last else None)
        if not last:
            yb_ctx = _attention_ctx(a16_ctx, *mla, "mla_attn_ctx")
            yc_ctx = _attention_ctx(a16_ctx, *gqa, "gqa_attn_ctx")
            cx = _out_mlp(cx, (ya_ctx, yb_ctx, yc_ctx, yd_ctx), mods_ctx, w_mlp, layer, n_ctx, layer > 0)
    return x
```

```python
import functools

import numpy as np
import jax
import jax.numpy as jnp
from jax import lax
from jax.experimental import pallas as pl
from jax.experimental.pallas import tpu as pltpu

F32 = jnp.float32
BF16 = jnp.bfloat16

GRID_W = 64
N_MOD = 6
EPS = 1e-6
ROPE_BASE = 10000.0
GROUP_W = 256
CONV_WIDTH = 31
CONV_PAD = 16
MLA_HEADS = 4
MLA_NOPE = 64
MLA_ROPE = 32
MLA_V = 64
MLA_Q_RANK = 256
MLA_KV_RANK = 128
GQA_HEADS = 4
GQA_KV_HEADS = 2
GQA_HEAD_DIM = 64
RET_HEADS = 4
RET_QK = 32
RET_V = 64
RET_CHUNK = 128
LANES = 128
SUBLANES = 8
LOG2_E = 1.4426950408889634
LAT_TILE = 512
PROJ_TILE = 1024
KEY_CHUNK = 256
NORM_ROWS = 256
VMEM_LIMIT = 56 * 1024 * 1024

OFF_A, OFF_CQ, OFF_CKV, OFF_GQ, OFF_GK = 0, 512, 768, 1024, 1280
OFF_RQ, OFF_RV, OFF_GF, OFF_GB, N_IN = 1536, 1792, 2048, 2304, 2560
A16_MQ, A16_MK, A16_MV, A16_GK, A16_GQ, A16_GV, A16_W = 0, 512, 1024, 1536, 2048, 2304, 2560
A32_Y, A32_RQ, A32_RK, A32_RV, A32_GF, A32_GB, A32_W = 0, 256, 384, 512, 768, 1024, 1280


def _cparams(n_axes):
    return pltpu.CompilerParams(dimension_semantics=("arbitrary",) * n_axes,
                                vmem_limit_bytes=VMEM_LIMIT)


def _resident(a, layer=None):
    if layer is None:
        return pl.BlockSpec(a.shape, lambda *_: (0,) * a.ndim, pipeline_mode=pl.Buffered(1))
    return pl.BlockSpec((None,) + a.shape[1:], lambda *_: (layer,) + (0,) * (a.ndim - 1),
                        pipeline_mode=pl.Buffered(1))


def _rms(x):
    return x * lax.rsqrt(jnp.mean(x * x, axis=-1, keepdims=True) + EPS)


def _sigmoid(x):
    return 1.0 / (1.0 + jnp.exp(-x))


def _silu(x):
    return x * _sigmoid(x)


def _dot(a, b):
    return jnp.dot(a, b, preferred_element_type=F32)


def _dot_nt(a, b):
    return lax.dot_general(a, b, (((1,), (1,)), ((), ())), preferred_element_type=F32)


def _dot_tn(a, b):
    return lax.dot_general(a, b, (((0,), (0,)), ((), ())), preferred_element_type=F32)


def _rope(x, cos, sin_up, sin_dn, shift):
    n = x.shape[-1]
    return x * cos + pltpu.roll(x, n - shift, 1) * sin_up + pltpu.roll(x, shift, 1) * sin_dn


def _adaln_kernel(cond_ref, w_ref, b_ref, o_ref):
    cond = cond_ref[...]
    o_ref[...] = _dot(_silu(cond).astype(BF16), w_ref[...].astype(BF16)) + b_ref[...]


def _adaln(cond, w_mod, b_mod):
    depth, d, n = w_mod.shape
    r = cond.shape[0]
    tn = 1536
    return pl.pallas_call(
        _adaln_kernel,
        grid=(depth, n // tn),
        in_specs=[pl.BlockSpec((r, d), lambda l, j: (0, 0)),
                  pl.BlockSpec((None, d, tn), lambda l, j: (l, 0, j)),
                  pl.BlockSpec((None, 1, tn), lambda l, j: (l, 0, j))],
        out_specs=pl.BlockSpec((None, r, tn), lambda l, j: (l, 0, j)),
        out_shape=jax.ShapeDtypeStruct((depth, r, n), F32),
        compiler_params=_cparams(2),
        name="adaln",
    )(cond, w_mod, b_mod.reshape(depth, 1, n))


def _in_proj_kernel(x_ref, mod_ref, w_in_ref, w_uq_ref, w_ukv_ref, g_mq_ref, g_mkv_ref, g_gq_ref, g_gk_ref,
                    *rest, rotary, sub_tiles):
    a16_ref, a32_ref = rest[-2:]
    lo = lax.broadcasted_iota(jnp.int32, (1, LANES), 1) < (LANES // 2)
    q_scale = (MLA_NOPE + MLA_ROPE) ** -0.5 * LOG2_E
    n_rows = x_ref.shape[0] // sub_tiles

    for st in range(sub_tiles):
        rows = slice(st * n_rows, (st + 1) * n_rows)
        h = (_rms(x_ref[rows, :]) * (1.0 + mod_ref[1:2, :]) + mod_ref[0:1, :]).astype(BF16)

        def proj(off, width, h=h):
            return _dot(h, w_in_ref[:, off:off + width])

        def put16(off, val, rows=rows):
            a16_ref[rows, off:off + val.shape[1]] = val.astype(BF16)

        def put32(off, val, rows=rows):
            a32_ref[rows, off:off + val.shape[1]] = val

        if rotary:
            tg_ref, tm_ref = rest[:2]
            rope_g = lambda v, rows=rows: _rope(v, tg_ref[0, rows, :], tg_ref[1, rows, :], tg_ref[2, rows, :],
                                                GQA_HEAD_DIM // 4)
            rope_m = lambda v, rows=rows: _rope(v, tm_ref[0, rows, :], tm_ref[1, rows, :], tm_ref[2, rows, :],
                                                MLA_ROPE // 4)
        else:
            rope_g = rope_m = lambda v: v

        def pair_norm_rope(xp, gain, rope_g=rope_g):
            x2 = xp * xp
            s_lo = jnp.sum(jnp.where(lo, x2, 0.0), axis=-1, keepdims=True)
            s_hi = jnp.sum(jnp.where(lo, 0.0, x2), axis=-1, keepdims=True)
            r = jnp.where(lo, lax.rsqrt(s_lo * (1.0 / GQA_HEAD_DIM) + EPS),
                          lax.rsqrt(s_hi * (1.0 / GQA_HEAD_DIM) + EPS))
            return rope_g(xp * r * gain)

        cq = (_rms(proj(OFF_CQ, MLA_Q_RANK)) * g_mq_ref[...]).astype(BF16)
        ckv_kpe = proj(OFF_CKV, 2 * LANES)
        ckv = (_rms(ckv_kpe[:, :LANES]) * g_mkv_ref[...]).astype(BF16)
        gq = proj(OFF_GQ, GQA_HEADS * GQA_HEAD_DIM)
        gkv = proj(OFF_GK, 2 * LANES)

        a = proj(OFF_A, 2 * GROUP_W)
        put32(A32_Y, a[:, :GROUP_W] * _sigmoid(a[:, GROUP_W:]))
        rqk = proj(OFF_RQ, 2 * LANES)
        put32(A32_RQ, rqk[:, :LANES])
        put32(A32_RK, rqk[:, LANES:] * RET_QK ** -0.5)
        put32(A32_RV, proj(OFF_RV, RET_HEADS * RET_V))
        put32(A32_GF, _silu(proj(OFF_GF, GROUP_W)))
        put32(A32_GB, _silu(proj(OFF_GB, GROUP_W)))

        q = _dot(cq, w_uq_ref[...])
        for hd in range(MLA_HEADS):
            put16(A16_MQ + hd * LANES, rope_m(q[:, hd * LANES:(hd + 1) * LANES]) * q_scale)
        kv = _dot(ckv, w_ukv_ref[...])
        kpe = rope_m(ckv_kpe[:, LANES:])
        for hd in range(MLA_HEADS):
            put16(A16_MK + hd * LANES, kv[:, hd * LANES:(hd + 1) * LANES] + kpe)
            put16(A16_MV + hd * LANES,
                  jnp.where(lo, kv[:, (MLA_HEADS + hd) * LANES:(MLA_HEADS + hd + 1) * LANES], 1.0))

        for pr in range(2):
            qp = pair_norm_rope(gq[:, pr * LANES:(pr + 1) * LANES], g_gq_ref[...])
            put16(A16_GQ + pr * LANES, qp * (GQA_HEAD_DIM ** -0.5 * LOG2_E))
        kp = pair_norm_rope(gkv[:, :LANES], g_gk_ref[...])
        kp_sw = pltpu.roll(kp, LANES // 2, 1)
        put16(A16_GK + 0 * LANES, jnp.where(lo, kp, 0.0))
        put16(A16_GK + 1 * LANES, jnp.where(lo, 0.0, kp_sw))
        put16(A16_GK + 2 * LANES, jnp.where(lo, kp_sw, 0.0))
        put16(A16_GK + 3 * LANES, jnp.where(lo, 0.0, kp))
        vp = gkv[:, LANES:]
        put16(A16_GV + 0 * LANES, jnp.where(lo, vp, 1.0))
        put16(A16_GV + 1 * LANES, jnp.where(lo, pltpu.roll(vp, LANES // 2, 1), 1.0))


def _in_proj(xseq, mods, weights, layer, tables, tm):
    b, n, d = xseq.shape
    per_batch = mods.shape[0] == b
    tok = lambda w: pl.BlockSpec((None, tm, w), lambda i, j: (i, j, 0))
    in_specs = [tok(d), pl.BlockSpec((None, N_MOD, d), lambda i, j: (i if per_batch else 0, 0, 0))]
    in_specs += [_resident(w, layer) for w in weights]
    args = [xseq, mods, *weights]
    if tables is not None:
        in_specs += [pl.BlockSpec((3, tm, LANES), lambda i, j: (0, j, 0))] * 2
        args += list(tables)
    return pl.pallas_call(
        functools.partial(_in_proj_kernel, rotary=tables is not None, sub_tiles=tm // 256),
        grid=(b, n // tm),
        in_specs=in_specs,
        out_specs=[tok(A16_W), tok(A32_W)],
        out_shape=[jax.ShapeDtypeStruct((b, n, A16_W), BF16), jax.ShapeDtypeStruct((b, n, A32_W), F32)],
        compiler_params=_cparams(2),
        name="in_proj",
    )(*args)


CONV_ROWS = 64
CONV_FIRST = CONV_PAD - CONV_WIDTH // 2
CONV_TAIL = ((CONV_FIRST + CONV_WIDTH - 1) // SUBLANES) * SUBLANES


def _conv_steps(sh_ref, dw_ref, b_ref, lg_ref, lb_ref, pw_ref, o_ref, length):
    def fill(before, body, after):
        sh_ref[0, 0:CONV_PAD, :] = before
        sh_ref[0, CONV_PAD:CONV_PAD + length, :] = body
        sh_ref[0, CONV_PAD + length:2 * CONV_PAD + length, :] = after

    def shift(base, n=CONV_ROWS):
        win = sh_ref[0, pl.ds(base, n + SUBLANES), :]
        for r in range(1, SUBLANES):
            sh_ref[r, pl.ds(base, n), :] = win[r:r + n, :]

    def shift_tail():
        shift(length, CONV_TAIL)

    def taps(base):
        groups = (CONV_ROWS // SUBLANES, SUBLANES, GROUP_W)
        acc = jnp.zeros(groups, F32) + b_ref[...]
        for k in range(CONV_WIDTH):
            off = CONV_FIRST + k
            slab = sh_ref[off % SUBLANES, pl.ds(base + (off // SUBLANES) * SUBLANES, CONV_ROWS), :]
            acc = acc + slab.reshape(groups) * dw_ref[k]
        sh_ref[0, pl.ds(base, CONV_ROWS), :] = acc.reshape(CONV_ROWS, GROUP_W)

    def finish(base):
        acc = sh_ref[0, pl.ds(base, NORM_ROWS), :]
        mu = jnp.mean(acc, axis=-1, keepdims=True)
        cen = acc - mu
        var = jnp.mean(cen * cen, axis=-1, keepdims=True)
        z = _silu(cen * lax.rsqrt(var + EPS) * lg_ref[...] + lb_ref[...])
        o_ref[pl.ds(base, NORM_ROWS), :] = _dot(z.astype(BF16), pw_ref[...]).astype(BF16)

    return fill, shift, shift_tail, taps, finish


def _conv_kernel(y_ref, dw_ref, b_ref, lg_ref, lb_ref, pw_ref, o_ref, sh_ref):
    length = y_ref.shape[0]
    fill, shift, shift_tail, taps, finish = _conv_steps(sh_ref, dw_ref, b_ref, lg_ref, lb_ref, pw_ref, o_ref, length)
    zeros = jnp.zeros((CONV_PAD, GROUP_W), F32)
    fill(zeros, y_ref[...], zeros)

    def loop(step, rows, **kw):
        def body(c, carry):
            step(pl.multiple_of(c * rows, rows))
            return carry
        lax.fori_loop(0, length // rows, body, 0, **kw)

    loop(shift, CONV_ROWS)
    shift_tail()
    loop(taps, CONV_ROWS)
    loop(finish, NORM_ROWS, unroll=min(length // NORM_ROWS, 4))


def _conv(a32, weights, layer):
    b, n, _ = a32.shape
    return pl.pallas_call(
        _conv_kernel,
        grid=(b,),
        in_specs=[pl.BlockSpec((None, n, GROUP_W), lambda i: (i, 0, A32_Y // GROUP_W))]
        + [_resident(w, layer) for w in weights],
        out_specs=pl.BlockSpec((None, n, GROUP_W), lambda i: (i, 0, 0)),
        out_shape=jax.ShapeDtypeStruct((b, n, GROUP_W), BF16),
        scratch_shapes=[pltpu.VMEM((SUBLANES, n + 2 * CONV_PAD, GROUP_W), F32)],
        compiler_params=_cparams(1),
        name="conv",
    )(a32, *weights)


def _normalized_pair(acc_even, acc_odd):
    half = LANES // 2
    lo = lax.broadcasted_iota(jnp.int32, (1, LANES), 1) < half
    return jnp.where(lo, acc_even / pltpu.roll(acc_even, half, 1), pltpu.roll(acc_odd, half, 1) / acc_odd)


def _attn_lat_kernel(q_ref, qn_ref, kl_ref, kc_ref, vl_ref, vc_ref, *rest, q_blocks, v_blocks, with_conv):
    if with_conv:
        y_ref, yb_ref, ya_next_ref, dw_ref, cb_ref, lg_ref, lb_ref, pw_ref, o_ref, ya_ref = rest[:10]
        s_ref, m_ref, mb_ref, acc_ref, sh_ref = rest[10:]
    else:
        o_ref, s_ref, m_ref, mb_ref, acc_ref = rest
    tq = q_ref.shape[0]
    tk = KEY_CHUNK
    chunks = [(kl_ref, vl_ref, c * tk) for c in range(kl_ref.shape[0] // tk)]
    chunks += [(kc_ref, vc_ref, c * tk) for c in range(kc_ref.shape[0] // tk)]

    def scores_chunk(hd, ci, ref):
        slot = hd % 2
        k_ref, _, r0 = chunks[ci]
        q = ref[:, q_blocks[hd] * LANES:(q_blocks[hd] + 1) * LANES]
        s = _dot_nt(q, k_ref[r0:r0 + tk, hd * LANES:(hd + 1) * LANES])
        s_ref[slot, :, ci * tk:(ci + 1) * tk] = s
        mx = s[:, 0:LANES]
        for i in range(1, tk // LANES):
            mx = jnp.maximum(mx, s[:, i * LANES:(i + 1) * LANES])
        m_ref[slot] = mx if ci == 0 else jnp.maximum(m_ref[slot], mx)

    def scores_finish(hd):
        slot = hd % 2
        mb_ref[slot] = jnp.broadcast_to(jnp.max(m_ref[slot], axis=-1, keepdims=True), (tq, LANES))

    def values_chunk(hd, ci):
        slot = hd % 2
        _, v_ref, r0 = chunks[ci]
        mb = mb_ref[slot]
        p = jnp.concatenate(
            [jnp.exp2(s_ref[slot, :, ci * tk + i * LANES:ci * tk + (i + 1) * LANES] - mb).astype(BF16)
             for i in range(tk // LANES)], axis=1)
        pv = _dot(p, v_ref[r0:r0 + tk, v_blocks[hd] * LANES:(v_blocks[hd] + 1) * LANES])
        acc_ref[slot] = pv if ci == 0 else acc_ref[slot] + pv

    n = len(chunks)

    def scores_ahead(unit, ci):
        scores_chunk(unit % 4, ci, q_ref if unit < 4 else qn_ref)

    @pl.when(pl.program_id(1) == 0)
    def _():
        for ci in range(n):
            scores_chunk(0, ci, q_ref)
        scores_finish(0)
        scores_chunk(1, 0, q_ref)

    side_steps = []
    if with_conv:
        fill, shift, shift_tail, taps, finish = _conv_steps(sh_ref, dw_ref, cb_ref, lg_ref, lb_ref, pw_ref, ya_ref, tq)
        j, last = pl.program_id(1), pl.num_programs(1) - 1
        side_steps.append(lambda: fill(jnp.where(j > 0, yb_ref[...], 0.0), y_ref[...],
                                       jnp.where(j < last, ya_next_ref[...], 0.0)))
        side_steps += [functools.partial(shift, c * CONV_ROWS) for c in range(tq // CONV_ROWS)] + [shift_tail]
        side_steps += [functools.partial(taps, c * CONV_ROWS) for c in range(tq // CONV_ROWS)]
        side_steps += [functools.partial(finish, c * NORM_ROWS) for c in range(tq // NORM_ROWS)]
    done = 0

    for hd in range(4):
        for ci in range(n):
            values_chunk(hd, ci)
            if ci + 1 < n:
                scores_ahead(hd + 1, ci + 1)
            else:
                scores_ahead(hd + 2, 0)
            if ci + 2 == n:
                scores_finish((hd + 1) % 4)
            due = ((hd * n + ci + 1) * len(side_steps)) // (4 * n)
            for step in side_steps[done:due]:
                step()
            done = due
        if hd % 2 == 1:
            pr = hd // 2
            o_ref[:, pr * LANES:(pr + 1) * LANES] = _normalized_pair(acc_ref[0], acc_ref[1]).astype(BF16)


def _attention_lat(a16_lat, a16_ctx, q_off, q_width, k_off, v_off, v_width, q_blocks, v_blocks, name, conv=None):
    b, n_lat, _ = a16_lat.shape
    n_ctx = a16_ctx.shape[1]
    tq = LAT_TILE
    n_tiles = n_lat // tq
    kw = 4 * LANES
    qc, kc, vc = q_off // q_width, k_off // kw, v_off // v_width
    in_specs = [pl.BlockSpec((None, tq, q_width), lambda i, j: (i, j, qc)),
                pl.BlockSpec((None, tq, q_width), lambda i, j: (i, jnp.minimum(j + 1, n_tiles - 1), qc)),
                pl.BlockSpec((None, n_lat, kw), lambda i, j: (i, 0, kc)),
                pl.BlockSpec((None, n_ctx, kw), lambda i, j: (i, 0, kc)),
                pl.BlockSpec((None, n_lat, v_width), lambda i, j: (i, 0, vc)),
                pl.BlockSpec((None, n_ctx, v_width), lambda i, j: (i, 0, vc))]
    args = [a16_lat, a16_lat, a16_lat, a16_ctx, a16_lat, a16_ctx]
    tile = lambda w: pl.BlockSpec((None, tq, w), lambda i, j: (i, j, 0))
    out_specs = [tile(2 * LANES)]
    out_shape = [jax.ShapeDtypeStruct((b, n_lat, 2 * LANES), BF16)]
    scratch = [pltpu.VMEM((2, tq, n_lat + n_ctx), F32), pltpu.VMEM((2, tq, LANES), F32),
               pltpu.VMEM((2, tq, LANES), F32), pltpu.VMEM((2, tq, LANES), F32)]
    if conv is not None:
        a32_lat, weights, layer = conv
        yc = A32_Y // GROUP_W
        per_tile = tq // CONV_PAD
        halo = lambda index: pl.BlockSpec((None, CONV_PAD, GROUP_W), lambda i, j: (i, index(j), yc))
        in_specs += [pl.BlockSpec((None, tq, GROUP_W), lambda i, j: (i, j, yc)),
                     halo(lambda j: jnp.maximum(j * per_tile - 1, 0)),
                     halo(lambda j: jnp.minimum((j + 1) * per_tile, n_lat // CONV_PAD - 1))]
        in_specs += [_resident(w, layer) for w in weights]
        args += [a32_lat, a32_lat, a32_lat, *weights]
        out_specs.append(tile(GROUP_W))
        out_shape.append(jax.ShapeDtypeStruct((b, n_lat, GROUP_W), BF16))
        scratch.append(pltpu.VMEM((SUBLANES, tq + 2 * CONV_PAD, GROUP_W), F32))
    return pl.pallas_call(
        functools.partial(_attn_lat_kernel, q_blocks=q_blocks, v_blocks=v_blocks, with_conv=conv is not None),
        grid=(b, n_tiles),
        in_specs=in_specs,
        out_specs=out_specs,
        out_shape=out_shape,
        scratch_shapes=scratch,
        compiler_params=_cparams(2),
        name=name,
    )(*args)


def _attn_ctx_kernel(q_ref, k_ref, v_ref, o_ref, *, q_blocks, v_blocks):
    for pr in range(2):
        accs = []
        for half in range(2):
            hd = 2 * pr + half
            q = q_ref[:, q_blocks[hd] * LANES:(q_blocks[hd] + 1) * LANES]
            s = _dot_nt(q, k_ref[:, hd * LANES:(hd + 1) * LANES])
            p = jnp.exp2(s - jnp.max(s, axis=-1, keepdims=True))
            accs.append(_dot(p.astype(BF16), v_ref[:, v_blocks[hd] * LANES:(v_blocks[hd] + 1) * LANES]))
        o_ref[:, pr * LANES:(pr + 1) * LANES] = _normalized_pair(*accs).astype(BF16)


def _attention_ctx(a16_ctx, q_off, q_width, k_off, v_off, v_width, q_blocks, v_blocks, name):
    b, n_ctx, _ = a16_ctx.shape
    kw = 4 * LANES
    qc, kc, vc = q_off // q_width, k_off // kw, v_off // v_width
    return pl.pallas_call(
        functools.partial(_attn_ctx_kernel, q_blocks=q_blocks, v_blocks=v_blocks),
        grid=(b,),
        in_specs=[pl.BlockSpec((None, n_ctx, q_width), lambda i: (i, 0, qc)),
                  pl.BlockSpec((None, n_ctx, kw), lambda i: (i, 0, kc)),
                  pl.BlockSpec((None, n_ctx, v_width), lambda i: (i, 0, vc))],
        out_specs=pl.BlockSpec((None, n_ctx, 2 * LANES), lambda i: (i, 0, 0)),
        out_shape=jax.ShapeDtypeStruct((b, n_ctx, 2 * LANES), BF16),
        compiler_params=_cparams(1),
        name=name,
    )(a16_ctx, a16_ctx, a16_ctx)


def _log_sigmoid(x):
    return jnp.minimum(x, 0.0) - jnp.log(1.0 + jnp.exp(-jnp.abs(x)))


def _ret_kernel(ql_ref, kl_ref, vl_ref, gfl_ref, gbl_ref, qc_ref, kc_ref, vc_ref, gfc_ref, gbc_ref,
                dq_ref, dv_ref, dc_ref, ng_ref, ol_ref, oc_ref, acc_ref):
    c = RET_CHUNK
    n_lat, n_ctx = ql_ref.shape[0], qc_ref.shape[0]
    qk_w = RET_HEADS * RET_QK
    v_w = RET_HEADS * RET_V

    row_i = lax.broadcasted_iota(jnp.int32, (c, 1), 0).astype(F32)
    qk_head = lax.broadcasted_iota(jnp.int32, (1, qk_w), 1) // RET_QK
    v_head = lax.broadcasted_iota(jnp.int32, (1, v_w), 1) // RET_V
    qk_masks = [qk_head == hd for hd in range(RET_HEADS)]
    v_masks = [v_head == hd for hd in range(RET_HEADS)]
    state_rows = lax.broadcasted_iota(jnp.int32, (qk_w, 1), 0) // RET_QK
    bd_mask = (state_rows == v_head).astype(F32)
    ones_rows = lax.broadcasted_iota(jnp.int32, (v_w, 1), 0) // RET_V
    head_ones = (ones_rows == v_head).astype(BF16)
    ci = lax.broadcasted_iota(jnp.int32, (c, RET_HEADS * c), 0).astype(F32)
    cj = (lax.broadcasted_iota(jnp.int32, (c, RET_HEADS * c), 1) % c).astype(F32)

    def make_step(d):
        lg_q = _log_sigmoid(dq_ref[d:d + 1, :])
        lg_v = _log_sigmoid(dv_ref[d:d + 1, :])
        lg_c = _log_sigmoid(dc_ref[d:d + 1, :])
        if d == 0:
            diff = ci - cj
            q_dec = jnp.exp(lg_q * (row_i + 1.0))
            k_dec = jnp.exp(lg_q * (c - 1.0 - row_i))
        else:
            diff = cj - ci
            q_dec = jnp.exp(lg_q * (c - row_i))
            k_dec = jnp.exp(lg_q * row_i)
        decay = jnp.where(diff >= 0.0, jnp.exp(lg_c * jnp.maximum(diff, 0.0)), 0.0)
        chunk_dec = jnp.exp(lg_v * float(c))

        def step(refs, r0, acc_r0, state):
            q_ref, k_ref, v_ref = refs
            q = q_ref[pl.ds(r0, c), :]
            k = k_ref[pl.ds(r0, c), :]
            v = v_ref[pl.ds(r0, c), :]
            kb, vb = k.astype(BF16), v.astype(BF16)
            k_bd = jnp.concatenate([jnp.where(m, kb, 0) for m in qk_masks], axis=0)
            scores = _dot_nt(q.astype(BF16), k_bd) * decay
            v_bd = jnp.concatenate([jnp.where(m, vb, 0) for m in v_masks], axis=0)
            inner = _dot(scores.astype(BF16), v_bd)
            cross = _dot((q * q_dec).astype(BF16), state.astype(BF16))
            kv = _dot_tn((k * k_dec).astype(BF16), vb)
            new_state = state * chunk_dec + kv * bd_mask
            acc_ref[d, pl.ds(acc_r0, c), :] = inner + cross
            return new_state

        return step

    fwd, bwd = make_step(0), make_step(1)

    def scan(n, acc_base, fwd_refs, bwd_refs, states):
        def body(i, st):
            rf = pl.multiple_of(i * c, c)
            rb = pl.multiple_of((n - 1 - i) * c, c)
            return fwd(fwd_refs, rf, acc_base + rf, st[0]), bwd(bwd_refs, rb, acc_base + rb, st[1])
        return lax.fori_loop(0, n, body, states, unroll=min(n, 4))

    zero = jnp.zeros((qk_w, v_w), F32)
    states = scan(n_ctx // c, n_lat, (qc_ref, kc_ref, vc_ref), (qc_ref, kc_ref, vc_ref), (zero, zero))
    scan(n_lat // c, 0, (ql_ref, kl_ref, vl_ref), (ql_ref, kl_ref, vl_ref), states)

    def readout(o_ref, gate_refs, acc_base):
        rows = NORM_ROWS

        def body(i, carry):
            r0 = pl.multiple_of(i * rows, rows)
            y = None
            for d in range(2):
                o = acc_ref[d, pl.ds(acc_base + r0, rows), :]
                o2 = o * o
                o2_hi = o2.astype(BF16)
                o2_lo = (o2 - o2_hi.astype(F32)).astype(BF16)
                ss = _dot(o2_hi, head_ones) + _dot(o2_lo, head_ones)
                yd = o * lax.rsqrt(ss * (1.0 / RET_V) + EPS) * ng_ref[d:d + 1, :] * gate_refs[d][pl.ds(r0, rows), :]
                y = yd if y is None else y + yd
            o_ref[pl.ds(r0, rows), :] = y.astype(BF16)
            return carry

        n = o_ref.shape[0] // rows
        lax.fori_loop(0, n, body, 0, unroll=min(n, 4))

    readout(ol_ref, (gfl_ref, gbl_ref), 0)
    readout(oc_ref, (gfc_ref, gbc_ref), n_lat)


def _retention(a32_lat, a32_ctx, smalls, layer):
    b, n_lat, _ = a32_lat.shape
    n_ctx = a32_ctx.shape[1]

    def seq_specs(n):
        spec = lambda off, w: pl.BlockSpec((None, n, w), lambda i: (i, 0, off // w))
        return [spec(A32_RQ, 128), spec(A32_RK, 128), spec(A32_RV, 256), spec(A32_GF, 256), spec(A32_GB, 256)]

    out = lambda n: pl.BlockSpec((None, n, 256), lambda i: (i, 0, 0))
    return pl.pallas_call(
        _ret_kernel,
        grid=(b,),
        in_specs=seq_specs(n_lat) + seq_specs(n_ctx) + [_resident(a, layer) for a in smalls],
        out_specs=[out(n_lat), out(n_ctx)],
        out_shape=[jax.ShapeDtypeStruct((b, n_lat, 256), BF16), jax.ShapeDtypeStruct((b, n_ctx, 256), BF16)],
        scratch_shapes=[pltpu.VMEM((2, n_lat + n_ctx, 256), F32)],
        compiler_params=_cparams(1),
        name="retention",
    )(*([a32_lat] * 5), *([a32_ctx] * 5), *smalls)


def _out_mlp_kernel(x_ref, ya_ref, yb_ref, yc_ref, yd_ref, mod_ref, wo_ref, w1_ref, w2_ref, *rest, ff_chunk):
    o_ref = rest[-1]
    x = x_ref[...]
    gw = GROUP_W
    y = (_dot(ya_ref[...], wo_ref[0 * gw:1 * gw, :]) + _dot(yb_ref[...], wo_ref[1 * gw:2 * gw, :])
         + _dot(yc_ref[...], wo_ref[2 * gw:3 * gw, :]) + _dot(yd_ref[...], wo_ref[3 * gw:4 * gw, :]))
    x1 = x + mod_ref[2:3, :] * y
    h = (_rms(x1) * (1.0 + mod_ref[4:5, :]) + mod_ref[3:4, :]).astype(BF16)
    acc = jnp.zeros_like(x1)
    for j in range(w1_ref.shape[1] // ff_chunk):
        a = jnp.maximum(_dot(h, w1_ref[:, j * ff_chunk:(j + 1) * ff_chunk]), 0.0)
        acc = acc + _dot((a * a).astype(BF16), w2_ref[j * ff_chunk:(j + 1) * ff_chunk, :])
    x2 = x1 + mod_ref[5:6, :] * acc
    o_ref[...] = x2 if len(rest) == 1 else _rms(x2) * rest[0][...]


def _out_mlp(xseq, ys, mods, weights, layer, tm, in_place, final_gain=None):
    b, n, d = xseq.shape
    per_batch = mods.shape[0] == b
    tok = lambda w: pl.BlockSpec((None, tm, w), lambda i, j: (i, j, 0))
    in_specs = [tok(d)] + [tok(GROUP_W)] * 4
    in_specs += [pl.BlockSpec((None, N_MOD, d), lambda i, j: (i if per_batch else 0, 0, 0))]
    in_specs += [_resident(w, layer) for w in weights]
    args = [xseq, *ys, mods, *weights]
    if final_gain is not None:
        in_specs.append(_resident(final_gain))
        args.append(final_gain)
    return pl.pallas_call(
        functools.partial(_out_mlp_kernel, ff_chunk=1024),
        grid=(b, n // tm),
        in_specs=in_specs,
        out_specs=tok(d),
        out_shape=jax.ShapeDtypeStruct((b, n, d), F32),
        input_output_aliases={0: 0} if in_place else {},
        compiler_params=_cparams(2),
        name="out_mlp",
    )(*args)


def _rope_tables(n_lat):
    n = np.arange(n_lat)
    row, col = (n // GRID_W).astype(np.float64), (n % GRID_W).astype(np.float64)

    def table(groups):
        cos = np.ones((n_lat, LANES)); up = np.zeros((n_lat, LANES)); dn = np.zeros((n_lat, LANES))
        for rot_start, rot_dim in groups:
            half = rot_dim // 2
            q = half // 2
            freqs = ROPE_BASE ** (-np.arange(q, dtype=np.float64) / q)
            for axis, pos in enumerate((row, col)):
                ang = pos[:, None] * freqs[None, :]
                base = rot_start + axis * half
                cos[:, base:base + q] = np.cos(ang)
                cos[:, base + q:base + 2 * q] = np.cos(ang)
                up[:, base:base + q] = -np.sin(ang)
                dn[:, base + q:base + 2 * q] = np.sin(ang)
        return jnp.asarray(np.stack([cos, up, dn]).astype(np.float32))

    return table([(0, GQA_HEAD_DIM), (GQA_HEAD_DIM, GQA_HEAD_DIM)]), table([(MLA_NOPE, MLA_ROPE)])


def _layout_w_in(w_in):
    lead = w_in.shape[:-1]
    kpe_src = OFF_CKV + MLA_KV_RANK
    w = w_in.astype(BF16)
    out = jnp.concatenate([w[..., :kpe_src], jnp.zeros(lead + (MLA_NOPE,), BF16),
                           w[..., kpe_src:kpe_src + MLA_ROPE],
                           jnp.zeros(lead + (LANES - MLA_NOPE - MLA_ROPE,), BF16),
                           w[..., kpe_src + MLA_ROPE:]], axis=-1)
    assert out.shape[-1] == N_IN
    return out


def _layout_w_uq(w_uq):
    lead = w_uq.shape[:-1]
    w = w_uq.reshape(lead + (MLA_HEADS, MLA_NOPE + MLA_ROPE))
    w = jnp.pad(w, ((0, 0),) * (w.ndim - 1) + ((0, LANES - MLA_NOPE - MLA_ROPE),))
    return w.reshape(lead + (MLA_HEADS * LANES,)).astype(BF16)


def _layout_w_ukv(w_ukv):
    lead = w_ukv.shape[:-1]
    w = w_ukv.reshape(lead + (MLA_HEADS, MLA_NOPE + MLA_V))
    pad = lambda a: jnp.pad(a, ((0, 0),) * (a.ndim - 1) + ((0, LANES - a.shape[-1]),)).reshape(
        lead + (MLA_HEADS * LANES,))
    return jnp.concatenate([pad(w[..., :MLA_NOPE]), pad(w[..., MLA_NOPE:])], axis=-1).astype(BF16)


def kernel(x, c, ctx, c_ctx, w_mod, b_mod, w_in, w_out, conv_dw, conv_b, conv_ln_g, conv_ln_b, conv_pw, mla_q_g, mla_kv_g, mla_uq, mla_ukv, gqa_q_g, gqa_k_g, ret_decay, ret_norm_g, mlp_w1, mlp_w2, final_g):
    b, n_lat, d = x.shape
    n_ctx = ctx.shape[1]
    depth = w_mod.shape[0]
    assert n_lat % LAT_TILE == 0 and n_lat % PROJ_TILE == 0 and n_lat % KEY_CHUNK == 0 and n_ctx % KEY_CHUNK == 0
    assert n_ctx % NORM_ROWS == 0 and n_lat % NORM_ROWS == 0 and NORM_ROWS % (2 * RET_CHUNK) == 0

    rows = ((b + 1 + SUBLANES - 1) // SUBLANES) * SUBLANES
    cond = jnp.zeros((rows, d), F32).at[:b].set(c).at[b].set(c_ctx)
    mod_all = _adaln(cond, w_mod, b_mod).reshape(depth, rows, N_MOD, d)
    tables = _rope_tables(n_lat)
    mla = (A16_MQ, 4 * LANES, A16_MK, A16_MV, 4 * LANES, (0, 1, 2, 3), (0, 1, 2, 3))
    gqa = (A16_GQ, 2 * LANES, A16_GK, A16_GV, 2 * LANES, (0, 0, 1, 1), (0, 0, 1, 1))

    row = lambda a: a[:, None, :]
    w_proj = (_layout_w_in(w_in), _layout_w_uq(mla_uq), _layout_w_ukv(mla_ukv), row(mla_q_g), row(mla_kv_g),
              row(jnp.tile(gqa_q_g, (1, 2))), row(jnp.tile(gqa_k_g, (1, 2))))
    w_conv = (jnp.broadcast_to(conv_dw[:, :, None, :], conv_dw.shape[:2] + (SUBLANES, GROUP_W)), row(conv_b), row(conv_ln_g), row(conv_ln_b), conv_pw.astype(BF16))
    dec = ret_decay.astype(F32)
    w_ret = (jnp.repeat(dec, RET_QK, axis=2), jnp.repeat(dec, RET_V, axis=2), jnp.repeat(dec, RET_CHUNK, axis=2),
             ret_norm_g.reshape(depth, 2, RET_HEADS * RET_V))
    w_mlp = (w_out.astype(BF16), mlp_w1.astype(BF16), mlp_w2.astype(BF16))

    cx = ctx
    for layer in range(depth):
        last = layer == depth - 1
        mods_lat, mods_ctx = mod_all[layer, :b], mod_all[layer, b:b + 1]
        a16_lat, a32_lat = _in_proj(x, mods_lat, w_proj, layer, tables, PROJ_TILE)
        a16_ctx, a32_ctx = _in_proj(cx, mods_ctx, w_proj, layer, None, n_ctx)
        ya_ctx = _conv(a32_ctx, w_conv, layer)
        yb_lat, = _attention_lat(a16_lat, a16_ctx, *mla, "mla_attn")
        yc_lat, ya_lat = _attention_lat(a16_lat, a16_ctx, *gqa, "gqa_attn", conv=(a32_lat, w_conv, layer))
        yd_lat, yd_ctx = _retention(a32_lat, a32_ctx, w_ret, layer)
        x = _out_mlp(x, (ya_lat, yb_lat, yc_lat, yd_lat), mods_lat, w_mlp, layer, LAT_TILE, layer > 0,
                     final_gain=final_g[None, :] if last else None)
        if not last:
            yb_ctx = _attention_ctx(a16_ctx, *mla, "mla_attn_ctx")
            yc_ctx = _attention_ctx(a16_ctx, *gqa, "gqa_attn_ctx")
            cx = _out_mlp(cx, (ya_ctx, yb_ctx, yc_ctx, yd_ctx), mods_ctx, w_mlp, layer, n_ctx, layer > 0)
    return x
```

```python
import functools

import numpy as np
import jax
import jax.numpy as jnp
from jax import lax
from jax.experimental import pallas as pl
from jax.experimental.pallas import tpu as pltpu

F32 = jnp.float32
BF16 = jnp.bfloat16

GRID_W = 64
N_MOD = 6
EPS = 1e-6
ROPE_BASE = 10000.0
GROUP_W = 256
CONV_WIDTH = 31
CONV_PAD = 16
MLA_HEADS = 4
MLA_NOPE = 64
MLA_ROPE = 32
MLA_V = 64
MLA_Q_RANK = 256
MLA_KV_RANK = 128
GQA_HEADS = 4
GQA_KV_HEADS = 2
GQA_HEAD_DIM = 64
RET_HEADS = 4
RET_QK = 32
RET_V = 64
RET_CHUNK = 128
LANES = 128
SUBLANES = 8
LOG2_E = 1.4426950408889634
LAT_TILE = 512
PROJ_TILE = 1024
KEY_CHUNK = 256
NORM_ROWS = 256
VMEM_LIMIT = 56 * 1024 * 1024

OFF_A, OFF_CQ, OFF_CKV, OFF_GQ, OFF_GK = 0, 512, 768, 1024, 1280
OFF_RQ, OFF_RV, OFF_GF, OFF_GB, N_IN = 1536, 1792, 2048, 2304, 2560
A16_MQ, A16_MK, A16_MV, A16_GK, A16_GQ, A16_GV, A16_W = 0, 512, 1024, 1536, 2048, 2304, 2560
A32_Y, A32_RQ, A32_RK, A32_RV, A32_GF, A32_GB, A32_W = 0, 256, 384, 512, 768, 1024, 1280


def _cparams(n_axes):
    return pltpu.CompilerParams(dimension_semantics=("arbitrary",) * n_axes,
                                vmem_limit_bytes=VMEM_LIMIT)


def _resident(a, layer=None):
    if layer is None:
        return pl.BlockSpec(a.shape, lambda *_: (0,) * a.ndim, pipeline_mode=pl.Buffered(1))
    return pl.BlockSpec((None,) + a.shape[1:], lambda *_: (layer,) + (0,) * (a.ndim - 1),
                        pipeline_mode=pl.Buffered(1))


def _rms(x):
    return x * lax.rsqrt(jnp.mean(x * x, axis=-1, keepdims=True) + EPS)


def _sigmoid(x):
    return 1.0 / (1.0 + jnp.exp(-x))


def _silu(x):
    return x * _sigmoid(x)


def _dot(a, b):
    return jnp.dot(a, b, preferred_element_type=F32)


def _dot_nt(a, b):
    return lax.dot_general(a, b, (((1,), (1,)), ((), ())), preferred_element_type=F32)


def _dot_tn(a, b):
    return lax.dot_general(a, b, (((0,), (0,)), ((), ())), preferred_element_type=F32)


def _rope(x, cos, sin_up, sin_dn, shift):
    n = x.shape[-1]
    return x * cos + pltpu.roll(x, n - shift, 1) * sin_up + pltpu.roll(x, shift, 1) * sin_dn


def _adaln_kernel(cond_ref, w_ref, b_ref, o_ref):
    cond = cond_ref[...]
    o_ref[...] = _dot(_silu(cond).astype(BF16), w_ref[...].astype(BF16)) + b_ref[...]


def _adaln(cond, w_mod, b_mod):
    depth, d, n = w_mod.shape
    r = cond.shape[0]
    tn = 1536
    return pl.pallas_call(
        _adaln_kernel,
        grid=(depth, n // tn),
        in_specs=[pl.BlockSpec((r, d), lambda l, j: (0, 0)),
                  pl.BlockSpec((None, d, tn), lambda l, j: (l, 0, j)),
                  pl.BlockSpec((None, 1, tn), lambda l, j: (l, 0, j))],
        out_specs=pl.BlockSpec((None, r, tn), lambda l, j: (l, 0, j)),
        out_shape=jax.ShapeDtypeStruct((depth, r, n), F32),
        compiler_params=_cparams(2),
        name="adaln",
    )(cond, w_mod, b_mod.reshape(depth, 1, n))


def _in_proj_kernel(x_ref, mod_ref, w_in_ref, w_uq_ref, w_ukv_ref, g_mq_ref, g_mkv_ref, g_gq_ref, g_gk_ref,
                    *rest, rotary, sub_tiles):
    a16_ref, a32_ref = rest[-2:]
    lo = lax.broadcasted_iota(jnp.int32, (1, LANES), 1) < (LANES // 2)
    q_scale = (MLA_NOPE + MLA_ROPE) ** -0.5 * LOG2_E
    n_rows = x_ref.shape[0] // sub_tiles

    for st in range(sub_tiles):
        rows = slice(st * n_rows, (st + 1) * n_rows)
        h = (_rms(x_ref[rows, :]) * (1.0 + mod_ref[1:2, :]) + mod_ref[0:1, :]).astype(BF16)

        def proj(off, width, h=h):
            return _dot(h, w_in_ref[:, off:off + width])

        def put16(off, val, rows=rows):
            a16_ref[rows, off:off + val.shape[1]] = val.astype(BF16)

        def put32(off, val, rows=rows):
            a32_ref[rows, off:off + val.shape[1]] = val

        if rotary:
            tg_ref, tm_ref = rest[:2]
            rope_g = lambda v, rows=rows: _rope(v, tg_ref[0, rows, :], tg_ref[1, rows, :], tg_ref[2, rows, :],
                                                GQA_HEAD_DIM // 4)
            rope_m = lambda v, rows=rows: _rope(v, tm_ref[0, rows, :], tm_ref[1, rows, :], tm_ref[2, rows, :],
                                                MLA_ROPE // 4)
        else:
            rope_g = rope_m = lambda v: v

        def pair_norm_rope(xp, gain, rope_g=rope_g):
            x2 = xp * xp
            s_lo = jnp.sum(jnp.where(lo, x2, 0.0), axis=-1, keepdims=True)
            s_hi = jnp.sum(jnp.where(lo, 0.0, x2), axis=-1, keepdims=True)
            r = jnp.where(lo, lax.rsqrt(s_lo * (1.0 / GQA_HEAD_DIM) + EPS),
                          lax.rsqrt(s_hi * (1.0 / GQA_HEAD_DIM) + EPS))
            return rope_g(xp * r * gain)

        cq = (_rms(proj(OFF_CQ, MLA_Q_RANK)) * g_mq_ref[...]).astype(BF16)
        ckv_kpe = proj(OFF_CKV, 2 * LANES)
        ckv = (_rms(ckv_kpe[:, :LANES]) * g_mkv_ref[...]).astype(BF16)
        gq = proj(OFF_GQ, GQA_HEADS * GQA_HEAD_DIM)
        gkv = proj(OFF_GK, 2 * LANES)

        a = proj(OFF_A, 2 * GROUP_W)
        put32(A32_Y, a[:, :GROUP_W] * _sigmoid(a[:, GROUP_W:]))
        rqk = proj(OFF_RQ, 2 * LANES)
        put32(A32_RQ, rqk[:, :LANES])
        put32(A32_RK, rqk[:, LANES:] * RET_QK ** -0.5)
        put32(A32_RV, proj(OFF_RV, RET_HEADS * RET_V))
        put32(A32_GF, _silu(proj(OFF_GF, GROUP_W)))
        put32(A32_GB, _silu(proj(OFF_GB, GROUP_W)))

        q = _dot(cq, w_uq_ref[...])
        for hd in range(MLA_HEADS):
            put16(A16_MQ + hd * LANES, rope_m(q[:, hd * LANES:(hd + 1) * LANES]) * q_scale)
        kv = _dot(ckv, w_ukv_ref[...])
        kpe = rope_m(ckv_kpe[:, LANES:])
        for hd in range(MLA_HEADS):
            put16(A16_MK + hd * LANES, kv[:, hd * LANES:(hd + 1) * LANES] + kpe)
            put16(A16_MV + hd * LANES,
                  jnp.where(lo, kv[:, (MLA_HEADS + hd) * LANES:(MLA_HEADS + hd + 1) * LANES], 1.0))

        for pr in range(2):
            qp = pair_norm_rope(gq[:, pr * LANES:(pr + 1) * LANES], g_gq_ref[...])
            put16(A16_GQ + pr * LANES, qp * (GQA_HEAD_DIM ** -0.5 * LOG2_E))
        kp = pair_norm_rope(gkv[:, :LANES], g_gk_ref[...])
        kp_sw = pltpu.roll(kp, LANES // 2, 1)
        put16(A16_GK + 0 * LANES, jnp.where(lo, kp, 0.0))
        put16(A16_GK + 1 * LANES, jnp.where(lo, 0.0, kp_sw))
        put16(A16_GK + 2 * LANES, jnp.where(lo, kp_sw, 0.0))
        put16(A16_GK + 3 * LANES, jnp.where(lo, 0.0, kp))
        vp = gkv[:, LANES:]
        put16(A16_GV + 0 * LANES, jnp.where(lo, vp, 1.0))
        put16(A16_GV + 1 * LANES, jnp.where(lo, pltpu.roll(vp, LANES // 2, 1), 1.0))


def _in_proj(xseq, mods, weights, layer, tables, tm):
    b, n, d = xseq.shape
    per_batch = mods.shape[0] == b
    tok = lambda w: pl.BlockSpec((None, tm, w), lambda i, j: (i, j, 0))
    in_specs = [tok(d), pl.BlockSpec((None, N_MOD, d), lambda i, j: (i if per_batch else 0, 0, 0))]
    in_specs += [_resident(w, layer) for w in weights]
    args = [xseq, mods, *weights]
    if tables is not None:
        in_specs += [pl.BlockSpec((3, tm, LANES), lambda i, j: (0, j, 0))] * 2
        args += list(tables)
    return pl.pallas_call(
        functools.partial(_in_proj_kernel, rotary=tables is not None, sub_tiles=tm // 256),
        grid=(b, n // tm),
        in_specs=in_specs,
        out_specs=[tok(A16_W), tok(A32_W)],
        out_shape=[jax.ShapeDtypeStruct((b, n, A16_W), BF16), jax.ShapeDtypeStruct((b, n, A32_W), F32)],
        compiler_params=_cparams(2),
        name="in_proj",
    )(*args)


CONV_ROWS = 64
CONV_FIRST = CONV_PAD - CONV_WIDTH // 2
CONV_TAIL = ((CONV_FIRST + CONV_WIDTH - 1) // SUBLANES) * SUBLANES


def _conv_steps(sh_ref, dw_ref, b_ref, lg_ref, lb_ref, pw_ref, o_ref, length):
    def fill(before, body, after):
        sh_ref[0, 0:CONV_PAD, :] = before
        sh_ref[0, CONV_PAD:CONV_PAD + length, :] = body
        sh_ref[0, CONV_PAD + length:2 * CONV_PAD + length, :] = after

    def shift(base, n=CONV_ROWS):
        win = sh_ref[0, pl.ds(base, n + SUBLANES), :]
        for r in range(1, SUBLANES):
            sh_ref[r, pl.ds(base, n), :] = win[r:r + n, :]

    def shift_tail():
        shift(length, CONV_TAIL)

    def taps(base):
        groups = (CONV_ROWS // SUBLANES, SUBLANES, GROUP_W)
        acc = jnp.zeros(groups, F32) + b_ref[...]
        for k in range(CONV_WIDTH):
            off = CONV_FIRST + k
            slab = sh_ref[off % SUBLANES, pl.ds(base + (off // SUBLANES) * SUBLANES, CONV_ROWS), :]
            acc = acc + slab.reshape(groups) * dw_ref[k]
        sh_ref[0, pl.ds(base, CONV_ROWS), :] = acc.reshape(CONV_ROWS, GROUP_W)

    def finish(base):
        acc = sh_ref[0, pl.ds(base, NORM_ROWS), :]
        mu = jnp.mean(acc, axis=-1, keepdims=True)
        cen = acc - mu
        var = jnp.mean(cen * cen, axis=-1, keepdims=True)
        z = _silu(cen * lax.rsqrt(var + EPS) * lg_ref[...] + lb_ref[...])
        o_ref[pl.ds(base, NORM_ROWS), :] = _dot(z.astype(BF16), pw_ref[...]).astype(BF16)

    return fill, shift, shift_tail, taps, finish


def _conv_kernel(y_ref, dw_ref, b_ref, lg_ref, lb_ref, pw_ref, o_ref, sh_ref):
    length = y_ref.shape[0]
    fill, shift, shift_tail, taps, finish = _conv_steps(sh_ref, dw_ref, b_ref, lg_ref, lb_ref, pw_ref, o_ref, length)
    zeros = jnp.zeros((CONV_PAD, GROUP_W), F32)
    fill(zeros, y_ref[...], zeros)

    def loop(step, rows, **kw):
        def body(c, carry):
            step(pl.multiple_of(c * rows, rows))
            return carry
        lax.fori_loop(0, length // rows, body, 0, **kw)

    loop(shift, CONV_ROWS)
    shift_tail()
    loop(taps, CONV_ROWS)
    loop(finish, NORM_ROWS, unroll=min(length // NORM_ROWS, 4))


def _conv(a32, weights, layer):
    b, n, _ = a32.shape
    return pl.pallas_call(
        _conv_kernel,
        grid=(b,),
        in_specs=[pl.BlockSpec((None, n, GROUP_W), lambda i: (i, 0, A32_Y // GROUP_W))]
        + [_resident(w, layer) for w in weights],
        out_specs=pl.BlockSpec((None, n, GROUP_W), lambda i: (i, 0, 0)),
        out_shape=jax.ShapeDtypeStruct((b, n, GROUP_W), BF16),
        scratch_shapes=[pltpu.VMEM((SUBLANES, n + 2 * CONV_PAD, GROUP_W), F32)],
        compiler_params=_cparams(1),
        name="conv",
    )(a32, *weights)


def _normalized_pair(acc_even, acc_odd):
    half = LANES // 2
    lo = lax.broadcasted_iota(jnp.int32, (1, LANES), 1) < half
    return jnp.where(lo, acc_even / pltpu.roll(acc_even, half, 1), pltpu.roll(acc_odd, half, 1) / acc_odd)


def _attn_lat_kernel(q_ref, qn_ref, kl_ref, kc_ref, kln_ref, kcn_ref, vl_ref, vc_ref, *rest,
                     q_blocks, v_blocks, with_conv):
    if with_conv:
        y_ref, yb_ref, ya_next_ref, dw_ref, cb_ref, lg_ref, lb_ref, pw_ref, o_ref, ya_ref = rest[:10]
        s_ref, m_ref, mb_ref, acc_ref, sh_ref = rest[10:]
    else:
        o_ref, s_ref, m_ref, mb_ref, acc_ref = rest
    tq = q_ref.shape[0]
    tk = KEY_CHUNK
    last_tile = pl.program_id(1) == pl.num_programs(1) - 1
    chunks = [(kl_ref, kln_ref, vl_ref, c * tk) for c in range(kl_ref.shape[0] // tk)]
    chunks += [(kc_ref, kcn_ref, vc_ref, c * tk) for c in range(kc_ref.shape[0] // tk)]

    def scores_chunk(hd, ci, ref, next_tile=False):
        slot = hd % 2
        k_ref, kn_ref, _, r0 = chunks[ci]
        q = ref[:, q_blocks[hd] * LANES:(q_blocks[hd] + 1) * LANES]
        k = k_ref[r0:r0 + tk, hd * LANES:(hd + 1) * LANES]
        if next_tile:
            k = jnp.where(last_tile, kn_ref[r0:r0 + tk, hd * LANES:(hd + 1) * LANES], k)
        s = _dot_nt(q, k)
        s_ref[slot, :, ci * tk:(ci + 1) * tk] = s
        mx = s[:, 0:LANES]
        for i in range(1, tk // LANES):
            mx = jnp.maximum(mx, s[:, i * LANES:(i + 1) * LANES])
        m_ref[slot] = mx if ci == 0 else jnp.maximum(m_ref[slot], mx)

    def scores_finish(hd):
        slot = hd % 2
        mb_ref[slot] = jnp.broadcast_to(jnp.max(m_ref[slot], axis=-1, keepdims=True), (tq, LANES))

    def values_chunk(hd, ci):
        slot = hd % 2
        _, _, v_ref, r0 = chunks[ci]
        mb = mb_ref[slot]
        p = jnp.concatenate(
            [jnp.exp2(s_ref[slot, :, ci * tk + i * LANES:ci * tk + (i + 1) * LANES] - mb).astype(BF16)
             for i in range(tk // LANES)], axis=1)
        pv = _dot(p, v_ref[r0:r0 + tk, v_blocks[hd] * LANES:(v_blocks[hd] + 1) * LANES])
        acc_ref[slot] = pv if ci == 0 else acc_ref[slot] + pv

    n = len(chunks)

    def scores_ahead(unit, ci):
        if unit < 4:
            scores_chunk(unit, ci, q_ref)
        else:
            scores_chunk(unit - 4, ci, qn_ref, next_tile=True)

    @pl.when((pl.program_id(0) == 0) & (pl.program_id(1) == 0))
    def _():
        for ci in range(n):
            scores_chunk(0, ci, q_ref)
        scores_finish(0)
        scores_chunk(1, 0, q_ref)

    side_steps = []
    if with_conv:
        fill, shift, shift_tail, taps, finish = _conv_steps(sh_ref, dw_ref, cb_ref, lg_ref, lb_ref, pw_ref, ya_ref, tq)
        j, last = pl.program_id(1), pl.num_programs(1) - 1
        side_steps.append(lambda: fill(jnp.where(j > 0, yb_ref[...], 0.0), y_ref[...],
                                       jnp.where(j < last, ya_next_ref[...], 0.0)))
        side_steps += [functools.partial(shift, c * CONV_ROWS) for c in range(tq // CONV_ROWS)] + [shift_tail]
        side_steps += [functools.partial(taps, c * CONV_ROWS) for c in range(tq // CONV_ROWS)]
        side_steps += [functools.partial(finish, c * NORM_ROWS) for c in range(tq // NORM_ROWS)]
    done = 0

    for hd in range(4):
        for ci in range(n):
            values_chunk(hd, ci)
            if ci + 1 < n:
                scores_ahead(hd + 1, ci + 1)
            else:
                scores_ahead(hd + 2, 0)
            if ci + 2 == n:
                scores_finish((hd + 1) % 4)
            due = ((hd * n + ci + 1) * len(side_steps)) // (4 * n)
            for step in side_steps[done:due]:
                step()
            done = due
        if hd % 2 == 1:
            pr = hd // 2
            o_ref[:, pr * LANES:(pr + 1) * LANES] = _normalized_pair(acc_ref[0], acc_ref[1]).astype(BF16)


def _attention_lat(a16_lat, a16_ctx, q_off, q_width, k_off, v_off, v_width, q_blocks, v_blocks, name, conv=None):
    b, n_lat, _ = a16_lat.shape
    n_ctx = a16_ctx.shape[1]
    tq = LAT_TILE
    n_tiles = n_lat // tq
    kw = 4 * LANES
    qc, kc, vc = q_off // q_width, k_off // kw, v_off // v_width
    next_row = lambda i: jnp.minimum(i + 1, b - 1)
    wraps = lambda j: j == n_tiles - 1
    in_specs = [pl.BlockSpec((None, tq, q_width), lambda i, j: (i, j, qc)),
                pl.BlockSpec((None, tq, q_width),
                             lambda i, j: (jnp.where(wraps(j), next_row(i), i), jnp.where(wraps(j), 0, j + 1), qc)),
                pl.BlockSpec((None, n_lat, kw), lambda i, j: (i, 0, kc)),
                pl.BlockSpec((None, n_ctx, kw), lambda i, j: (i, 0, kc)),
                pl.BlockSpec((None, n_lat, kw), lambda i, j: (next_row(i), 0, kc)),
                pl.BlockSpec((None, n_ctx, kw), lambda i, j: (next_row(i), 0, kc)),
                pl.BlockSpec((None, n_lat, v_width), lambda i, j: (i, 0, vc)),
                pl.BlockSpec((None, n_ctx, v_width), lambda i, j: (i, 0, vc))]
    args = [a16_lat, a16_lat, a16_lat, a16_ctx, a16_lat, a16_ctx, a16_lat, a16_ctx]
    tile = lambda w: pl.BlockSpec((None, tq, w), lambda i, j: (i, j, 0))
    out_specs = [tile(2 * LANES)]
    out_shape = [jax.ShapeDtypeStruct((b, n_lat, 2 * LANES), BF16)]
    scratch = [pltpu.VMEM((2, tq, n_lat + n_ctx), F32), pltpu.VMEM((2, tq, LANES), F32),
               pltpu.VMEM((2, tq, LANES), F32), pltpu.VMEM((2, tq, LANES), F32)]
    if conv is not None:
        a32_lat, weights, layer = conv
        yc = A32_Y // GROUP_W
        per_tile = tq // CONV_PAD
        halo = lambda index: pl.BlockSpec((None, CONV_PAD, GROUP_W), lambda i, j: (i, index(j), yc))
        in_specs += [pl.BlockSpec((None, tq, GROUP_W), lambda i, j: (i, j, yc)),
                     halo(lambda j: jnp.maximum(j * per_tile - 1, 0)),
                     halo(lambda j: jnp.minimum((j + 1) * per_tile, n_lat // CONV_PAD - 1))]
        in_specs += [_resident(w, layer) for w in weights]
        args += [a32_lat, a32_lat, a32_lat, *weights]
        out_specs.append(tile(GROUP_W))
        out_shape.append(jax.ShapeDtypeStruct((b, n_lat, GROUP_W), BF16))
        scratch.append(pltpu.VMEM((SUBLANES, tq + 2 * CONV_PAD, GROUP_W), F32))
    return pl.pallas_call(
        functools.partial(_attn_lat_kernel, q_blocks=q_blocks, v_blocks=v_blocks, with_conv=conv is not None),
        grid=(b, n_tiles),
        in_specs=in_specs,
        out_specs=out_specs,
        out_shape=out_shape,
        scratch_shapes=scratch,
        compiler_params=_cparams(2),
        name=name,
    )(*args)


def _attn_ctx_kernel(q_ref, k_ref, v_ref, o_ref, *, q_blocks, v_blocks):
    for pr in range(2):
        accs = []
        for half in range(2):
            hd = 2 * pr + half
            q = q_ref[:, q_blocks[hd] * LANES:(q_blocks[hd] + 1) * LANES]
            s = _dot_nt(q, k_ref[:, hd * LANES:(hd + 1) * LANES])
            p = jnp.exp2(s - jnp.max(s, axis=-1, keepdims=True))
            accs.append(_dot(p.astype(BF16), v_ref[:, v_blocks[hd] * LANES:(v_blocks[hd] + 1) * LANES]))
        o_ref[:, pr * LANES:(pr + 1) * LANES] = _normalized_pair(*accs).astype(BF16)


def _attention_ctx(a16_ctx, q_off, q_width, k_off, v_off, v_width, q_blocks, v_blocks, name):
    b, n_ctx, _ = a16_ctx.shape
    kw = 4 * LANES
    qc, kc, vc = q_off // q_width, k_off // kw, v_off // v_width
    return pl.pallas_call(
        functools.partial(_attn_ctx_kernel, q_blocks=q_blocks, v_blocks=v_blocks),
        grid=(b,),
        in_specs=[pl.BlockSpec((None, n_ctx, q_width), lambda i: (i, 0, qc)),
                  pl.BlockSpec((None, n_ctx, kw), lambda i: (i, 0, kc)),
                  pl.BlockSpec((None, n_ctx, v_width), lambda i: (i, 0, vc))],
        out_specs=pl.BlockSpec((None, n_ctx, 2 * LANES), lambda i: (i, 0, 0)),
        out_shape=jax.ShapeDtypeStruct((b, n_ctx, 2 * LANES), BF16),
        compiler_params=_cparams(1),
        name=name,
    )(a16_ctx, a16_ctx, a16_ctx)


def _log_sigmoid(x):
    return jnp.minimum(x, 0.0) - jnp.log(1.0 + jnp.exp(-jnp.abs(x)))


def _ret_kernel(ql_ref, kl_ref, vl_ref, gfl_ref, gbl_ref, qc_ref, kc_ref, vc_ref, gfc_ref, gbc_ref,
                dq_ref, dv_ref, dc_ref, ng_ref, ol_ref, oc_ref, acc_ref):
    c = RET_CHUNK
    n_lat, n_ctx = ql_ref.shape[0], qc_ref.shape[0]
    qk_w = RET_HEADS * RET_QK
    v_w = RET_HEADS * RET_V

    row_i = lax.broadcasted_iota(jnp.int32, (c, 1), 0).astype(F32)
    qk_head = lax.broadcasted_iota(jnp.int32, (1, qk_w), 1) // RET_QK
    v_head = lax.broadcasted_iota(jnp.int32, (1, v_w), 1) // RET_V
    qk_masks = [qk_head == hd for hd in range(RET_HEADS)]
    v_masks = [v_head == hd for hd in range(RET_HEADS)]
    state_rows = lax.broadcasted_iota(jnp.int32, (qk_w, 1), 0) // RET_QK
    bd_mask = (state_rows == v_head).astype(F32)
    ones_rows = lax.broadcasted_iota(jnp.int32, (v_w, 1), 0) // RET_V
    head_ones = (ones_rows == v_head).astype(BF16)
    ci = lax.broadcasted_iota(jnp.int32, (c, RET_HEADS * c), 0).astype(F32)
    cj = (lax.broadcasted_iota(jnp.int32, (c, RET_HEADS * c), 1) % c).astype(F32)

    def make_step(d):
        lg_q = _log_sigmoid(dq_ref[d:d + 1, :])
        lg_v = _log_sigmoid(dv_ref[d:d + 1, :])
        lg_c = _log_sigmoid(dc_ref[d:d + 1, :])
        if d == 0:
            diff = ci - cj
            q_dec = jnp.exp(lg_q * (row_i + 1.0))
            k_dec = jnp.exp(lg_q * (c - 1.0 - row_i))
        else:
            diff = cj - ci
            q_dec = jnp.exp(lg_q * (c - row_i))
            k_dec = jnp.exp(lg_q * row_i)
        decay = jnp.where(diff >= 0.0, jnp.exp(lg_c * jnp.maximum(diff, 0.0)), 0.0)
        chunk_dec = jnp.exp(lg_v * float(c))

        def step(refs, r0, acc_r0, state):
            q_ref, k_ref, v_ref = refs
            q = q_ref[pl.ds(r0, c), :]
            k = k_ref[pl.ds(r0, c), :]
            v = v_ref[pl.ds(r0, c), :]
            kb, vb = k.astype(BF16), v.astype(BF16)
            k_bd = jnp.concatenate([jnp.where(m, kb, 0) for m in qk_masks], axis=0)
            scores = _dot_nt(q.astype(BF16), k_bd) * decay
            v_bd = jnp.concatenate([jnp.where(m, vb, 0) for m in v_masks], axis=0)
            inner = _dot(scores.astype(BF16), v_bd)
            cross = _dot((q * q_dec).astype(BF16), state.astype(BF16))
            kv = _dot_tn((k * k_dec).astype(BF16), vb)
            new_state = state * chunk_dec + kv * bd_mask
            acc_ref[d, pl.ds(acc_r0, c), :] = inner + cross
            return new_state

        return step

    fwd, bwd = make_step(0), make_step(1)

    def scan(n, acc_base, fwd_refs, bwd_refs, states):
        def body(i, st):
            rf = pl.multiple_of(i * c, c)
            rb = pl.multiple_of((n - 1 - i) * c, c)
            return fwd(fwd_refs, rf, acc_base + rf, st[0]), bwd(bwd_refs, rb, acc_base + rb, st[1])
        return lax.fori_loop(0, n, body, states, unroll=min(n, 4))

    zero = jnp.zeros((qk_w, v_w), F32)
    states = scan(n_ctx // c, n_lat, (qc_ref, kc_ref, vc_ref), (qc_ref, kc_ref, vc_ref), (zero, zero))
    scan(n_lat // c, 0, (ql_ref, kl_ref, vl_ref), (ql_ref, kl_ref, vl_ref), states)

    def readout(o_ref, gate_refs, acc_base):
        rows = NORM_ROWS

        def body(i, carry):
            r0 = pl.multiple_of(i * rows, rows)
            y = None
            for d in range(2):
                o = acc_ref[d, pl.ds(acc_base + r0, rows), :]
                o2 = o * o
                o2_hi = o2.astype(BF16)
                o2_lo = (o2 - o2_hi.astype(F32)).astype(BF16)
                ss = _dot(o2_hi, head_ones) + _dot(o2_lo, head_ones)
                yd = o * lax.rsqrt(ss * (1.0 / RET_V) + EPS) * ng_ref[d:d + 1, :] * gate_refs[d][pl.ds(r0, rows), :]
                y = yd if y is None else y + yd
            o_ref[pl.ds(r0, rows), :] = y.astype(BF16)
            return carry

        n = o_ref.shape[0] // rows
        lax.fori_loop(0, n, body, 0, unroll=min(n, 4))

    readout(ol_ref, (gfl_ref, gbl_ref), 0)
    readout(oc_ref, (gfc_ref, gbc_ref), n_lat)


def _retention(a32_lat, a32_ctx, smalls, layer):
    b, n_lat, _ = a32_lat.shape
    n_ctx = a32_ctx.shape[1]

    def seq_specs(n):
        spec = lambda off, w: pl.BlockSpec((None, n, w), lambda i: (i, 0, off // w))
        return [spec(A32_RQ, 128), spec(A32_RK, 128), spec(A32_RV, 256), spec(A32_GF, 256), spec(A32_GB, 256)]

    out = lambda n: pl.BlockSpec((None, n, 256), lambda i: (i, 0, 0))
    return pl.pallas_call(
        _ret_kernel,
        grid=(b,),
        in_specs=seq_specs(n_lat) + seq_specs(n_ctx) + [_resident(a, layer) for a in smalls],
        out_specs=[out(n_lat), out(n_ctx)],
        out_shape=[jax.ShapeDtypeStruct((b, n_lat, 256), BF16), jax.ShapeDtypeStruct((b, n_ctx, 256), BF16)],
        scratch_shapes=[pltpu.VMEM((2, n_lat + n_ctx, 256), F32)],
        compiler_params=_cparams(1),
        name="retention",
    )(*([a32_lat] * 5), *([a32_ctx] * 5), *smalls)


def _out_mlp_kernel(x_ref, ya_ref, yb_ref, yc_ref, yd_ref, mod_ref, wo_ref, w1_ref, w2_ref, *rest, ff_chunk):
    o_ref = rest[-1]
    x = x_ref[...]
    gw = GROUP_W
    y = (_dot(ya_ref[...], wo_ref[0 * gw:1 * gw, :]) + _dot(yb_ref[...], wo_ref[1 * gw:2 * gw, :])
         + _dot(yc_ref[...], wo_ref[2 * gw:3 * gw, :]) + _dot(yd_ref[...], wo_ref[3 * gw:4 * gw, :]))
    x1 = x + mod_ref[2:3, :] * y
    h = (_rms(x1) * (1.0 + mod_ref[4:5, :]) + mod_ref[3:4, :]).astype(BF16)
    acc = jnp.zeros_like(x1)
    for j in range(w1_ref.shape[1] // ff_chunk):
        a = jnp.maximum(_dot(h, w1_ref[:, j * ff_chunk:(j + 1) * ff_chunk]), 0.0)
        acc = acc + _dot((a * a).astype(BF16), w2_ref[j * ff_chunk:(j + 1) * ff_chunk, :])
    x2 = x1 + mod_ref[5:6, :] * acc
    o_ref[...] = x2 if len(rest) == 1 else _rms(x2) * rest[0][...]


def _out_mlp(xseq, ys, mods, weights, layer, tm, in_place, final_gain=None):
    b, n, d = xseq.shape
    per_batch = mods.shape[0] == b
    tok = lambda w: pl.BlockSpec((None, tm, w), lambda i, j: (i, j, 0))
    in_specs = [tok(d)] + [tok(GROUP_W)] * 4
    in_specs += [pl.BlockSpec((None, N_MOD, d), lambda i, j: (i if per_batch else 0, 0, 0))]
    in_specs += [_resident(w, layer) for w in weights]
    args = [xseq, *ys, mods, *weights]
    if final_gain is not None:
        in_specs.append(_resident(final_gain))
        args.append(final_gain)
    return pl.pallas_call(
        functools.partial(_out_mlp_kernel, ff_chunk=1024),
        grid=(b, n // tm),
        in_specs=in_specs,
        out_specs=tok(d),
        out_shape=jax.ShapeDtypeStruct((b, n, d), F32),
        input_output_aliases={0: 0} if in_place else {},
        compiler_params=_cparams(2),
        name="out_mlp",
    )(*args)


def _rope_tables(n_lat):
    n = np.arange(n_lat)
    row, col = (n // GRID_W).astype(np.float64), (n % GRID_W).astype(np.float64)

    def table(groups):
        cos = np.ones((n_lat, LANES)); up = np.zeros((n_lat, LANES)); dn = np.zeros((n_lat, LANES))
        for rot_start, rot_dim in groups:
            half = rot_dim // 2
            q = half // 2
            freqs = ROPE_BASE ** (-np.arange(q, dtype=np.float64) / q)
            for axis, pos in enumerate((row, col)):
                ang = pos[:, None] * freqs[None, :]
                base = rot_start + axis * half
                cos[:, base:base + q] = np.cos(ang)
                cos[:, base + q:base + 2 * q] = np.cos(ang)
                up[:, base:base + q] = -np.sin(ang)
                dn[:, base + q:base + 2 * q] = np.sin(ang)
        return jnp.asarray(np.stack([cos, up, dn]).astype(np.float32))

    return table([(0, GQA_HEAD_DIM), (GQA_HEAD_DIM, GQA_HEAD_DIM)]), table([(MLA_NOPE, MLA_ROPE)])


def _layout_w_in(w_in):
    lead = w_in.shape[:-1]
    kpe_src = OFF_CKV + MLA_KV_RANK
    w = w_in.astype(BF16)
    out = jnp.concatenate([w[..., :kpe_src], jnp.zeros(lead + (MLA_NOPE,), BF16),
                           w[..., kpe_src:kpe_src + MLA_ROPE],
                           jnp.zeros(lead + (LANES - MLA_NOPE - MLA_ROPE,), BF16),
                           w[..., kpe_src + MLA_ROPE:]], axis=-1)
    assert out.shape[-1] == N_IN
    return out


def _layout_w_uq(w_uq):
    lead = w_uq.shape[:-1]
    w = w_uq.reshape(lead + (MLA_HEADS, MLA_NOPE + MLA_ROPE))
    w = jnp.pad(w, ((0, 0),) * (w.ndim - 1) + ((0, LANES - MLA_NOPE - MLA_ROPE),))
    return w.reshape(lead + (MLA_HEADS * LANES,)).astype(BF16)


def _layout_w_ukv(w_ukv):
    lead = w_ukv.shape[:-1]
    w = w_ukv.reshape(lead + (MLA_HEADS, MLA_NOPE + MLA_V))
    pad = lambda a: jnp.pad(a, ((0, 0),) * (a.ndim - 1) + ((0, LANES - a.shape[-1]),)).reshape(
        lead + (MLA_HEADS * LANES,))
    return jnp.concatenate([pad(w[..., :MLA_NOPE]), pad(w[..., MLA_NOPE:])], axis=-1).astype(BF16)


def kernel(x, c, ctx, c_ctx, w_mod, b_mod, w_in, w_out, conv_dw, conv_b, conv_ln_g, conv_ln_b, conv_pw, mla_q_g, mla_kv_g, mla_uq, mla_ukv, gqa_q_g, gqa_k_g, ret_decay, ret_norm_g, mlp_w1, mlp_w2, final_g):
    b, n_lat, d = x.shape
    n_ctx = ctx.shape[1]
    depth = w_mod.shape[0]
    assert n_lat % LAT_TILE == 0 and n_lat % PROJ_TILE == 0 and n_lat % KEY_CHUNK == 0 and n_ctx % KEY_CHUNK == 0
    assert n_ctx % NORM_ROWS == 0 and n_lat % NORM_ROWS == 0 and NORM_ROWS % (2 * RET_CHUNK) == 0

    rows = ((b + 1 + SUBLANES - 1) // SUBLANES) * SUBLANES
    cond = jnp.zeros((rows, d), F32).at[:b].set(c).at[b].set(c_ctx)
    mod_all = _adaln(cond, w_mod, b_mod).reshape(depth, rows, N_MOD, d)
    tables = _rope_tables(n_lat)
    mla = (A16_MQ, 4 * LANES, A16_MK, A16_MV, 4 * LANES, (0, 1, 2, 3), (0, 1, 2, 3))
    gqa = (A16_GQ, 2 * LANES, A16_GK, A16_GV, 2 * LANES, (0, 0, 1, 1), (0, 0, 1, 1))

    row = lambda a: a[:, None, :]
    w_proj = (_layout_w_in(w_in), _layout_w_uq(mla_uq), _layout_w_ukv(mla_ukv), row(mla_q_g), row(mla_kv_g),
              row(jnp.tile(gqa_q_g, (1, 2))), row(jnp.tile(gqa_k_g, (1, 2))))
    w_conv = (jnp.broadcast_to(conv_dw[:, :, None, :], conv_dw.shape[:2] + (SUBLANES, GROUP_W)), row(conv_b), row(conv_ln_g), row(conv_ln_b), conv_pw.astype(BF16))
    dec = ret_decay.astype(F32)
    w_ret = (jnp.repeat(dec, RET_QK, axis=2), jnp.repeat(dec, RET_V, axis=2), jnp.repeat(dec, RET_CHUNK, axis=2),
             ret_norm_g.reshape(depth, 2, RET_HEADS * RET_V))
    w_mlp = (w_out.astype(BF16), mlp_w1.astype(BF16), mlp_w2.astype(BF16))

    cx = ctx
    for layer in range(depth):
        last = layer == depth - 1
        mods_lat, mods_ctx = mod_all[layer, :b], mod_all[layer, b:b + 1]
        a16_lat, a32_lat = _in_proj(x, mods_lat, w_proj, layer, tables, PROJ_TILE)
        a16_ctx, a32_ctx = _in_proj(cx, mods_ctx, w_proj, layer, None, n_ctx)
        ya_ctx = _conv(a32_ctx, w_conv, layer)
        yb_lat, = _attention_lat(a16_lat, a16_ctx, *mla, "mla_attn")
        yc_lat, ya_lat = _attention_lat(a16_lat, a16_ctx, *gqa, "gqa_attn", conv=(a32_lat, w_conv, layer))
        yd_lat, yd_ctx = _retention(a32_lat, a32_ctx, w_ret, layer)
        x = _out_mlp(x, (ya_lat, yb_lat, yc_lat, yd_lat), mods_lat, w_mlp, layer, LAT_TILE, layer > 0,
                     final_gain=final_g[None, :] if last else None)
        if not last:
            yb_ctx = _attention_ctx(a16_ctx, *mla, "mla_attn_ctx")
            yc_ctx = _attention_ctx(a16_ctx, *gqa, "gqa_attn_ctx")
            cx = _out_mlp(cx, (ya_ctx, yb_ctx, yc_ctx, yd_ctx), mods_ctx, w_mlp, layer, n_ctx, layer > 0)
    return x
```

```python
import functools

import numpy as np
import jax
import jax.numpy as jnp
from jax import lax
from jax.experimental import pallas as pl
from jax.experimental.pallas import tpu as pltpu

F32 = jnp.float32
BF16 = jnp.bfloat16

GRID_W = 64
N_MOD = 6
EPS = 1e-6
ROPE_BASE = 10000.0
GROUP_W = 256
CONV_WIDTH = 31
CONV_PAD = 16
MLA_HEADS = 4
MLA_NOPE = 64
MLA_ROPE = 32
MLA_V = 64
MLA_Q_RANK = 256
MLA_KV_RANK = 128
GQA_HEADS = 4
GQA_KV_HEADS = 2
GQA_HEAD_DIM = 64
RET_HEADS = 4
RET_QK = 32
RET_V = 64
RET_CHUNK = 128
LANES = 128
SUBLANES = 8
LOG2_E = 1.4426950408889634
LAT_TILE = 512
PROJ_TILE = 1024
KEY_CHUNK = 256
NORM_ROWS = 256
VMEM_LIMIT = 56 * 1024 * 1024

OFF_A, OFF_CQ, OFF_CKV, OFF_KPE, N_IN_A = 0, 512, 768, 896, 1024
OFF_GQ, OFF_GK, OFF_RQ, OFF_RV, OFF_GF, OFF_GB, N_IN_B = 0, 256, 512, 768, 1024, 1280, 1536
A16_MQ, A16_MK, A16_MV, A16_GK, A16_GQ, A16_GV, A16_W = 0, 512, 1024, 1536, 2048, 2304, 2560
A32_Y, A32_RQ, A32_RK, A32_RV, A32_GF, A32_GB, A32_W = 0, 256, 384, 512, 768, 1024, 1280


def _cparams(n_axes):
    return pltpu.CompilerParams(dimension_semantics=("arbitrary",) * n_axes,
                                vmem_limit_bytes=VMEM_LIMIT)


def _resident(a, layer=None):
    if layer is None:
        return pl.BlockSpec(a.shape, lambda *_: (0,) * a.ndim, pipeline_mode=pl.Buffered(1))
    return pl.BlockSpec((None,) + a.shape[1:], lambda *_: (layer,) + (0,) * (a.ndim - 1),
                        pipeline_mode=pl.Buffered(1))


def _rms(x):
    return x * lax.rsqrt(jnp.mean(x * x, axis=-1, keepdims=True) + EPS)


def _sigmoid(x):
    return 1.0 / (1.0 + jnp.exp(-x))


def _silu(x):
    return x * _sigmoid(x)


def _dot(a, b):
    return jnp.dot(a, b, preferred_element_type=F32)


def _dot_nt(a, b):
    return lax.dot_general(a, b, (((1,), (1,)), ((), ())), preferred_element_type=F32)


def _dot_tn(a, b):
    return lax.dot_general(a, b, (((0,), (0,)), ((), ())), preferred_element_type=F32)


def _rope(x, cos, sin_up, sin_dn, shift):
    n = x.shape[-1]
    return x * cos + pltpu.roll(x, n - shift, 1) * sin_up + pltpu.roll(x, shift, 1) * sin_dn


def _adaln_kernel(cond_ref, w_ref, b_ref, o_ref):
    cond = cond_ref[...]
    o_ref[...] = _dot(_silu(cond).astype(BF16), w_ref[...].astype(BF16)) + b_ref[...]


def _adaln(cond, w_mod, b_mod):
    depth, d, n = w_mod.shape
    r = cond.shape[0]
    tn = 1536
    return pl.pallas_call(
        _adaln_kernel,
        grid=(depth, n // tn),
        in_specs=[pl.BlockSpec((r, d), lambda l, j: (0, 0)),
                  pl.BlockSpec((None, d, tn), lambda l, j: (l, 0, j)),
                  pl.BlockSpec((None, 1, tn), lambda l, j: (l, 0, j))],
        out_specs=pl.BlockSpec((None, r, tn), lambda l, j: (l, 0, j)),
        out_shape=jax.ShapeDtypeStruct((depth, r, n), F32),
        compiler_params=_cparams(2),
        name="adaln",
    )(cond, w_mod, b_mod.reshape(depth, 1, n))


def _in_proj_kernel(x_ref, mod_ref, w_a_ref, w_b_ref, w_uq_ref, w_ukv_ref, g_mq_ref, g_mkv_ref, g_gq_ref, g_gk_ref,
                    *rest, rotary, sub_tiles):
    a16_ref, a32_ref = rest[-2:]
    lo = lax.broadcasted_iota(jnp.int32, (1, LANES), 1) < (LANES // 2)
    q_scale = (MLA_NOPE + MLA_ROPE) ** -0.5 * LOG2_E
    n_rows = x_ref.shape[0] // sub_tiles

    for st in range(sub_tiles):
        rows = slice(st * n_rows, (st + 1) * n_rows)
        h = (_rms(x_ref[rows, :]) * (1.0 + mod_ref[1:2, :]) + mod_ref[0:1, :]).astype(BF16)

        def proj(w_ref, off, width, h=h):
            return _dot(h, w_ref[:, off:off + width])

        def put16(off, val, rows=rows):
            a16_ref[rows, off:off + val.shape[1]] = val.astype(BF16)

        def put32(off, val, rows=rows):
            a32_ref[rows, off:off + val.shape[1]] = val

        if rotary:
            tg_ref, tm_ref = rest[:2]
            rope_g = lambda v, rows=rows: _rope(v, tg_ref[0, rows, :], tg_ref[1, rows, :], tg_ref[2, rows, :],
                                                GQA_HEAD_DIM // 4)
            rope_m = lambda v, rows=rows: _rope(v, tm_ref[0, rows, :], tm_ref[1, rows, :], tm_ref[2, rows, :],
                                                MLA_ROPE // 4)
        else:
            rope_g = rope_m = lambda v: v

        def pair_norm_rope(xp, gain, rope_g=rope_g):
            x2 = xp * xp
            s_lo = jnp.sum(jnp.where(lo, x2, 0.0), axis=-1, keepdims=True)
            s_hi = jnp.sum(jnp.where(lo, 0.0, x2), axis=-1, keepdims=True)
            r = jnp.where(lo, lax.rsqrt(s_lo * (1.0 / GQA_HEAD_DIM) + EPS),
                          lax.rsqrt(s_hi * (1.0 / GQA_HEAD_DIM) + EPS))
            return rope_g(xp * r * gain)

        cq = (_rms(proj(w_a_ref, OFF_CQ, MLA_Q_RANK)) * g_mq_ref[...]).astype(BF16)
        ckv_kpe = proj(w_a_ref, OFF_CKV, 2 * LANES)
        ckv = (_rms(ckv_kpe[:, :LANES]) * g_mkv_ref[...]).astype(BF16)
        gq = proj(w_b_ref, OFF_GQ, GQA_HEADS * GQA_HEAD_DIM)
        gkv = proj(w_b_ref, OFF_GK, 2 * LANES)

        a = proj(w_a_ref, OFF_A, 2 * GROUP_W)
        put32(A32_Y, a[:, :GROUP_W] * _sigmoid(a[:, GROUP_W:]))
        rqk = proj(w_b_ref, OFF_RQ, 2 * LANES)
        put32(A32_RQ, rqk[:, :LANES])
        put32(A32_RK, rqk[:, LANES:] * RET_QK ** -0.5)
        put32(A32_RV, proj(w_b_ref, OFF_RV, RET_HEADS * RET_V))
        put32(A32_GF, _silu(proj(w_b_ref, OFF_GF, GROUP_W)))
        put32(A32_GB, _silu(proj(w_b_ref, OFF_GB, GROUP_W)))

        q = _dot(cq, w_uq_ref[...])
        for hd in range(MLA_HEADS):
            put16(A16_MQ + hd * LANES, rope_m(q[:, hd * LANES:(hd + 1) * LANES]) * q_scale)
        kv = _dot(ckv, w_ukv_ref[...])
        kpe = rope_m(pltpu.roll(ckv_kpe[:, LANES:], MLA_NOPE, 1))
        for hd in range(MLA_HEADS):
            put16(A16_MK + hd * LANES, kv[:, hd * LANES:(hd + 1) * LANES] + kpe)
            put16(A16_MV + hd * LANES,
                  jnp.where(lo, kv[:, (MLA_HEADS + hd) * LANES:(MLA_HEADS + hd + 1) * LANES], 1.0))

        for pr in range(2):
            qp = pair_norm_rope(gq[:, pr * LANES:(pr + 1) * LANES], g_gq_ref[...])
            put16(A16_GQ + pr * LANES, qp * (GQA_HEAD_DIM ** -0.5 * LOG2_E))
        kp = pair_norm_rope(gkv[:, :LANES], g_gk_ref[...])
        kp_sw = pltpu.roll(kp, LANES // 2, 1)
        put16(A16_GK + 0 * LANES, jnp.where(lo, kp, 0.0))
        put16(A16_GK + 1 * LANES, jnp.where(lo, 0.0, kp_sw))
        put16(A16_GK + 2 * LANES, jnp.where(lo, kp_sw, 0.0))
        put16(A16_GK + 3 * LANES, jnp.where(lo, 0.0, kp))
        vp = gkv[:, LANES:]
        put16(A16_GV + 0 * LANES, jnp.where(lo, vp, 1.0))
        put16(A16_GV + 1 * LANES, jnp.where(lo, pltpu.roll(vp, LANES // 2, 1), 1.0))


def _in_proj(xseq, mods, weights, layer, tables, tm):
    b, n, d = xseq.shape
    per_batch = mods.shape[0] == b
    tok = lambda w: pl.BlockSpec((None, tm, w), lambda i, j: (i, j, 0))
    in_specs = [tok(d), pl.BlockSpec((None, N_MOD, d), lambda i, j: (i if per_batch else 0, 0, 0))]
    in_specs += [_resident(w, layer) for w in weights]
    args = [xseq, mods, *weights]
    if tables is not None:
        in_specs += [pl.BlockSpec((3, tm, LANES), lambda i, j: (0, j, 0))] * 2
        args += list(tables)
    return pl.pallas_call(
        functools.partial(_in_proj_kernel, rotary=tables is not None, sub_tiles=tm // 256),
        grid=(b, n // tm),
        in_specs=in_specs,
        out_specs=[tok(A16_W), tok(A32_W)],
        out_shape=[jax.ShapeDtypeStruct((b, n, A16_W), BF16), jax.ShapeDtypeStruct((b, n, A32_W), F32)],
        compiler_params=_cparams(2),
        name="in_proj",
    )(*args)


CONV_ROWS = 64
CONV_FIRST = CONV_PAD - CONV_WIDTH // 2
CONV_TAIL = ((CONV_FIRST + CONV_WIDTH - 1) // SUBLANES) * SUBLANES


def _conv_steps(sh_ref, dw_ref, b_ref, lg_ref, lb_ref, pw_ref, o_ref, length):
    def fill(before, body, after):
        sh_ref[0, 0:CONV_PAD, :] = before
        sh_ref[0, CONV_PAD:CONV_PAD + length, :] = body
        sh_ref[0, CONV_PAD + length:2 * CONV_PAD + length, :] = after

    def shift(base, n=CONV_ROWS):
        win = sh_ref[0, pl.ds(base, n + SUBLANES), :]
        for r in range(1, SUBLANES):
            sh_ref[r, pl.ds(base, n), :] = win[r:r + n, :]

    def shift_tail():
        shift(length, CONV_TAIL)

    def taps(base):
        groups = (CONV_ROWS // SUBLANES, SUBLANES, GROUP_W)
        acc = jnp.zeros(groups, F32) + b_ref[...]
        for k in range(CONV_WIDTH):
            off = CONV_FIRST + k
            slab = sh_ref[off % SUBLANES, pl.ds(base + (off // SUBLANES) * SUBLANES, CONV_ROWS), :]
            acc = acc + slab.reshape(groups) * dw_ref[k]
        sh_ref[0, pl.ds(base, CONV_ROWS), :] = acc.reshape(CONV_ROWS, GROUP_W)

    def finish(base):
        acc = sh_ref[0, pl.ds(base, NORM_ROWS), :]
        mu = jnp.mean(acc, axis=-1, keepdims=True)
        cen = acc - mu
        var = jnp.mean(cen * cen, axis=-1, keepdims=True)
        z = _silu(cen * lax.rsqrt(var + EPS) * lg_ref[...] + lb_ref[...])
        o_ref[pl.ds(base, NORM_ROWS), :] = _dot(z.astype(BF16), pw_ref[...]).astype(BF16)

    return fill, shift, shift_tail, taps, finish


def _conv_kernel(y_ref, dw_ref, b_ref, lg_ref, lb_ref, pw_ref, o_ref, sh_ref):
    length = y_ref.shape[0]
    fill, shift, shift_tail, taps, finish = _conv_steps(sh_ref, dw_ref, b_ref, lg_ref, lb_ref, pw_ref, o_ref, length)
    zeros = jnp.zeros((CONV_PAD, GROUP_W), F32)
    fill(zeros, y_ref[...], zeros)

    def loop(step, rows, **kw):
        def body(c, carry):
            step(pl.multiple_of(c * rows, rows))
            return carry
        lax.fori_loop(0, length // rows, body, 0, **kw)

    loop(shift, CONV_ROWS)
    shift_tail()
    loop(taps, CONV_ROWS)
    loop(finish, NORM_ROWS, unroll=min(length // NORM_ROWS, 4))


def _conv(a32, weights, layer):
    b, n, _ = a32.shape
    return pl.pallas_call(
        _conv_kernel,
        grid=(b,),
        in_specs=[pl.BlockSpec((None, n, GROUP_W), lambda i: (i, 0, A32_Y // GROUP_W))]
        + [_resident(w, layer) for w in weights],
        out_specs=pl.BlockSpec((None, n, GROUP_W), lambda i: (i, 0, 0)),
        out_shape=jax.ShapeDtypeStruct((b, n, GROUP_W), BF16),
        scratch_shapes=[pltpu.VMEM((SUBLANES, n + 2 * CONV_PAD, GROUP_W), F32)],
        compiler_params=_cparams(1),
        name="conv",
    )(a32, *weights)


def _normalized_pair(acc_even, acc_odd):
    half = LANES // 2
    lo = lax.broadcasted_iota(jnp.int32, (1, LANES), 1) < half
    return jnp.where(lo, acc_even / pltpu.roll(acc_even, half, 1), pltpu.roll(acc_odd, half, 1) / acc_odd)


def _attn_lat_kernel(q_ref, qn_ref, kl_ref, kc_ref, kln_ref, kcn_ref, vl_ref, vc_ref, *rest,
                     q_blocks, v_blocks, with_conv):
    if with_conv:
        y_ref, yb_ref, ya_next_ref, dw_ref, cb_ref, lg_ref, lb_ref, pw_ref, o_ref, ya_ref = rest[:10]
        s_ref, m_ref, mb_ref, acc_ref, sh_ref = rest[10:]
    else:
        o_ref, s_ref, m_ref, mb_ref, acc_ref = rest
    tq = q_ref.shape[0]
    tk = KEY_CHUNK
    last_tile = pl.program_id(1) == pl.num_programs(1) - 1
    chunks = [(kl_ref, kln_ref, vl_ref, c * tk) for c in range(kl_ref.shape[0] // tk)]
    chunks += [(kc_ref, kcn_ref, vc_ref, c * tk) for c in range(kc_ref.shape[0] // tk)]

    def scores_chunk(hd, ci, ref, next_tile=False):
        slot = hd % 2
        k_ref, kn_ref, _, r0 = chunks[ci]
        q = ref[:, q_blocks[hd] * LANES:(q_blocks[hd] + 1) * LANES]
        k = k_ref[r0:r0 + tk, hd * LANES:(hd + 1) * LANES]
        if next_tile:
            k = jnp.where(last_tile, kn_ref[r0:r0 + tk, hd * LANES:(hd + 1) * LANES], k)
        s = _dot_nt(q, k)
        s_ref[slot, :, ci * tk:(ci + 1) * tk] = s
        mx = s[:, 0:LANES]
        for i in range(1, tk // LANES):
            mx = jnp.maximum(mx, s[:, i * LANES:(i + 1) * LANES])
        m_ref[slot] = mx if ci == 0 else jnp.maximum(m_ref[slot], mx)

    def scores_finish(hd):
        slot = hd % 2
        mb_ref[slot] = jnp.broadcast_to(jnp.max(m_ref[slot], axis=-1, keepdims=True), (tq, LANES))

    def values_chunk(hd, ci):
        slot = hd % 2
        _, _, v_ref, r0 = chunks[ci]
        mb = mb_ref[slot]
        p = jnp.concatenate(
            [jnp.exp2(s_ref[slot, :, ci * tk + i * LANES:ci * tk + (i + 1) * LANES] - mb).astype(BF16)
             for i in range(tk // LANES)], axis=1)
        pv = _dot(p, v_ref[r0:r0 + tk, v_blocks[hd] * LANES:(v_blocks[hd] + 1) * LANES])
        acc_ref[slot] = pv if ci == 0 else acc_ref[slot] + pv

    n = len(chunks)

    def scores_ahead(unit, ci):
        if unit < 4:
            scores_chunk(unit, ci, q_ref)
        else:
            scores_chunk(unit - 4, ci, qn_ref, next_tile=True)

    @pl.when((pl.program_id(0) == 0) & (pl.program_id(1) == 0))
    def _():
        for ci in range(n):
            scores_chunk(0, ci, q_ref)
        scores_finish(0)
        scores_chunk(1, 0, q_ref)

    side_steps = []
    if with_conv:
        fill, shift, shift_tail, taps, finish = _conv_steps(sh_ref, dw_ref, cb_ref, lg_ref, lb_ref, pw_ref, ya_ref, tq)
        j, last = pl.program_id(1), pl.num_programs(1) - 1
        side_steps.append(lambda: fill(jnp.where(j > 0, yb_ref[...], 0.0), y_ref[...],
                                       jnp.where(j < last, ya_next_ref[...], 0.0)))
        side_steps += [functools.partial(shift, c * CONV_ROWS) for c in range(tq // CONV_ROWS)] + [shift_tail]
        side_steps += [functools.partial(taps, c * CONV_ROWS) for c in range(tq // CONV_ROWS)]
        side_steps += [functools.partial(finish, c * NORM_ROWS) for c in range(tq // NORM_ROWS)]
    done = 0

    for hd in range(4):
        for ci in range(n):
            values_chunk(hd, ci)
            if ci + 1 < n:
                scores_ahead(hd + 1, ci + 1)
            else:
                scores_ahead(hd + 2, 0)
            if ci + 2 == n:
                scores_finish((hd + 1) % 4)
            due = ((hd * n + ci + 1) * len(side_steps)) // (4 * n)
            for step in side_steps[done:due]:
                step()
            done = due
        if hd % 2 == 1:
            pr = hd // 2
            o_ref[:, pr * LANES:(pr + 1) * LANES] = _normalized_pair(acc_ref[0], acc_ref[1]).astype(BF16)


def _attention_lat(a16_lat, a16_ctx, q_off, q_width, k_off, v_off, v_width, q_blocks, v_blocks, name, conv=None):
    b, n_lat, _ = a16_lat.shape
    n_ctx = a16_ctx.shape[1]
    tq = LAT_TILE
    n_tiles = n_lat // tq
    kw = 4 * LANES
    qc, kc, vc = q_off // q_width, k_off // kw, v_off // v_width
    next_row = lambda i: jnp.minimum(i + 1, b - 1)
    wraps = lambda j: j == n_tiles - 1
    in_specs = [pl.BlockSpec((None, tq, q_width), lambda i, j: (i, j, qc)),
                pl.BlockSpec((None, tq, q_width),
                             lambda i, j: (jnp.where(wraps(j), next_row(i), i), jnp.where(wraps(j), 0, j + 1), qc)),
                pl.BlockSpec((None, n_lat, kw), lambda i, j: (i, 0, kc)),
                pl.BlockSpec((None, n_ctx, kw), lambda i, j: (i, 0, kc)),
                pl.BlockSpec((None, n_lat, kw), lambda i, j: (next_row(i), 0, kc)),
                pl.BlockSpec((None, n_ctx, kw), lambda i, j: (next_row(i), 0, kc)),
                pl.BlockSpec((None, n_lat, v_width), lambda i, j: (i, 0, vc)),
                pl.BlockSpec((None, n_ctx, v_width), lambda i, j: (i, 0, vc))]
    args = [a16_lat, a16_lat, a16_lat, a16_ctx, a16_lat, a16_ctx, a16_lat, a16_ctx]
    tile = lambda w: pl.BlockSpec((None, tq, w), lambda i, j: (i, j, 0))
    out_specs = [tile(2 * LANES)]
    out_shape = [jax.ShapeDtypeStruct((b, n_lat, 2 * LANES), BF16)]
    scratch = [pltpu.VMEM((2, tq, n_lat + n_ctx), F32), pltpu.VMEM((2, tq, LANES), F32),
               pltpu.VMEM((2, tq, LANES), F32), pltpu.VMEM((2, tq, LANES), F32)]
    if conv is not None:
        a32_lat, weights, layer = conv
        yc = A32_Y // GROUP_W
        per_tile = tq // CONV_PAD
        halo = lambda index: pl.BlockSpec((None, CONV_PAD, GROUP_W), lambda i, j: (i, index(j), yc))
        in_specs += [pl.BlockSpec((None, tq, GROUP_W), lambda i, j: (i, j, yc)),
                     halo(lambda j: jnp.maximum(j * per_tile - 1, 0)),
                     halo(lambda j: jnp.minimum((j + 1) * per_tile, n_lat // CONV_PAD - 1))]
        in_specs += [_resident(w, layer) for w in weights]
        args += [a32_lat, a32_lat, a32_lat, *weights]
        out_specs.append(tile(GROUP_W))
        out_shape.append(jax.ShapeDtypeStruct((b, n_lat, GROUP_W), BF16))
        scratch.append(pltpu.VMEM((SUBLANES, tq + 2 * CONV_PAD, GROUP_W), F32))
    return pl.pallas_call(
        functools.partial(_attn_lat_kernel, q_blocks=q_blocks, v_blocks=v_blocks, with_conv=conv is not None),
        grid=(b, n_tiles),
        in_specs=in_specs,
        out_specs=out_specs,
        out_shape=out_shape,
        scratch_shapes=scratch,
        compiler_params=_cparams(2),
        name=name,
    )(*args)


def _attn_ctx_kernel(q_ref, k_ref, v_ref, o_ref, *, q_blocks, v_blocks):
    for pr in range(2):
        accs = []
        for half in range(2):
            hd = 2 * pr + half
            q = q_ref[:, q_blocks[hd] * LANES:(q_blocks[hd] + 1) * LANES]
            s = _dot_nt(q, k_ref[:, hd * LANES:(hd + 1) * LANES])
            p = jnp.exp2(s - jnp.max(s, axis=-1, keepdims=True))
            accs.append(_dot(p.astype(BF16), v_ref[:, v_blocks[hd] * LANES:(v_blocks[hd] + 1) * LANES]))
        o_ref[:, pr * LANES:(pr + 1) * LANES] = _normalized_pair(*accs).astype(BF16)


def _attention_ctx(a16_ctx, q_off, q_width, k_off, v_off, v_width, q_blocks, v_blocks, name):
    b, n_ctx, _ = a16_ctx.shape
    kw = 4 * LANES
    qc, kc, vc = q_off // q_width, k_off // kw, v_off // v_width
    return pl.pallas_call(
        functools.partial(_attn_ctx_kernel, q_blocks=q_blocks, v_blocks=v_blocks),
        grid=(b,),
        in_specs=[pl.BlockSpec((None, n_ctx, q_width), lambda i: (i, 0, qc)),
                  pl.BlockSpec((None, n_ctx, kw), lambda i: (i, 0, kc)),
                  pl.BlockSpec((None, n_ctx, v_width), lambda i: (i, 0, vc))],
        out_specs=pl.BlockSpec((None, n_ctx, 2 * LANES), lambda i: (i, 0, 0)),
        out_shape=jax.ShapeDtypeStruct((b, n_ctx, 2 * LANES), BF16),
        compiler_params=_cparams(1),
        name=name,
    )(a16_ctx, a16_ctx, a16_ctx)


def _log_sigmoid(x):
    return jnp.minimum(x, 0.0) - jnp.log(1.0 + jnp.exp(-jnp.abs(x)))


def _ret_kernel(ql_ref, kl_ref, vl_ref, gfl_ref, gbl_ref, qc_ref, kc_ref, vc_ref, gfc_ref, gbc_ref,
                dq_ref, dv_ref, dc_ref, ng_ref, ol_ref, oc_ref, acc_ref):
    c = RET_CHUNK
    n_lat, n_ctx = ql_ref.shape[0], qc_ref.shape[0]
    qk_w = RET_HEADS * RET_QK
    v_w = RET_HEADS * RET_V

    row_i = lax.broadcasted_iota(jnp.int32, (c, 1), 0).astype(F32)
    qk_head = lax.broadcasted_iota(jnp.int32, (1, qk_w), 1) // RET_QK
    v_head = lax.broadcasted_iota(jnp.int32, (1, v_w), 1) // RET_V
    qk_masks = [qk_head == hd for hd in range(RET_HEADS)]
    v_masks = [v_head == hd for hd in range(RET_HEADS)]
    state_rows = lax.broadcasted_iota(jnp.int32, (qk_w, 1), 0) // RET_QK
    bd_mask = (state_rows == v_head).astype(F32)
    ones_rows = lax.broadcasted_iota(jnp.int32, (v_w, 1), 0) // RET_V
    head_ones = (ones_rows == v_head).astype(BF16)
    ci = lax.broadcasted_iota(jnp.int32, (c, RET_HEADS * c), 0).astype(F32)
    cj = (lax.broadcasted_iota(jnp.int32, (c, RET_HEADS * c), 1) % c).astype(F32)

    def make_step(d):
        lg_q = _log_sigmoid(dq_ref[d:d + 1, :])
        lg_v = _log_sigmoid(dv_ref[d:d + 1, :])
        lg_c = _log_sigmoid(dc_ref[d:d + 1, :])
        if d == 0:
            diff = ci - cj
            q_dec = jnp.exp(lg_q * (row_i + 1.0))
            k_dec = jnp.exp(lg_q * (c - 1.0 - row_i))
        else:
            diff = cj - ci
            q_dec = jnp.exp(lg_q * (c - row_i))
            k_dec = jnp.exp(lg_q * row_i)
        decay = jnp.where(diff >= 0.0, jnp.exp(lg_c * jnp.maximum(diff, 0.0)), 0.0)
        chunk_dec = jnp.exp(lg_v * float(c))

        def step(refs, r0, acc_r0, state):
            q_ref, k_ref, v_ref = refs
            q = q_ref[pl.ds(r0, c), :]
            k = k_ref[pl.ds(r0, c), :]
            v = v_ref[pl.ds(r0, c), :]
            kb, vb = k.astype(BF16), v.astype(BF16)
            k_bd = jnp.concatenate([jnp.where(m, kb, 0) for m in qk_masks], axis=0)
            scores = _dot_nt(q.astype(BF16), k_bd) * decay
            v_bd = jnp.concatenate([jnp.where(m, vb, 0) for m in v_masks], axis=0)
            inner = _dot(scores.astype(BF16), v_bd)
            cross = _dot((q * q_dec).astype(BF16), state.astype(BF16))
            kv = _dot_tn((k * k_dec).astype(BF16), vb)
            new_state = state * chunk_dec + kv * bd_mask
            acc_ref[d, pl.ds(acc_r0, c), :] = inner + cross
            return new_state

        return step

    fwd, bwd = make_step(0), make_step(1)

    def scan(n, acc_base, fwd_refs, bwd_refs, states):
        def body(i, st):
            rf = pl.multiple_of(i * c, c)
            rb = pl.multiple_of((n - 1 - i) * c, c)
            return fwd(fwd_refs, rf, acc_base + rf, st[0]), bwd(bwd_refs, rb, acc_base + rb, st[1])
        return lax.fori_loop(0, n, body, states, unroll=min(n, 4))

    zero = jnp.zeros((qk_w, v_w), F32)
    states = scan(n_ctx // c, n_lat, (qc_ref, kc_ref, vc_ref), (qc_ref, kc_ref, vc_ref), (zero, zero))
    scan(n_lat // c, 0, (ql_ref, kl_ref, vl_ref), (ql_ref, kl_ref, vl_ref), states)

    def readout(o_ref, gate_refs, acc_base):
        rows = NORM_ROWS

        def body(i, carry):
            r0 = pl.multiple_of(i * rows, rows)
            y = None
            for d in range(2):
                o = acc_ref[d, pl.ds(acc_base + r0, rows), :]
                o2 = o * o
                o2_hi = o2.astype(BF16)
                o2_lo = (o2 - o2_hi.astype(F32)).astype(BF16)
                ss = _dot(o2_hi, head_ones) + _dot(o2_lo, head_ones)
                yd = o * lax.rsqrt(ss * (1.0 / RET_V) + EPS) * ng_ref[d:d + 1, :] * gate_refs[d][pl.ds(r0, rows), :]
                y = yd if y is None else y + yd
            o_ref[pl.ds(r0, rows), :] = y.astype(BF16)
            return carry

        n = o_ref.shape[0] // rows
        lax.fori_loop(0, n, body, 0, unroll=min(n, 4))

    readout(ol_ref, (gfl_ref, gbl_ref), 0)
    readout(oc_ref, (gfc_ref, gbc_ref), n_lat)


def _retention(a32_lat, a32_ctx, smalls, layer):
    b, n_lat, _ = a32_lat.shape
    n_ctx = a32_ctx.shape[1]

    def seq_specs(n):
        spec = lambda off, w: pl.BlockSpec((None, n, w), lambda i: (i, 0, off // w))
        return [spec(A32_RQ, 128), spec(A32_RK, 128), spec(A32_RV, 256), spec(A32_GF, 256), spec(A32_GB, 256)]

    out = lambda n: pl.BlockSpec((None, n, 256), lambda i: (i, 0, 0))
    return pl.pallas_call(
        _ret_kernel,
        grid=(b,),
        in_specs=seq_specs(n_lat) + seq_specs(n_ctx) + [_resident(a, layer) for a in smalls],
        out_specs=[out(n_lat), out(n_ctx)],
        out_shape=[jax.ShapeDtypeStruct((b, n_lat, 256), BF16), jax.ShapeDtypeStruct((b, n_ctx, 256), BF16)],
        scratch_shapes=[pltpu.VMEM((2, n_lat + n_ctx, 256), F32)],
        compiler_params=_cparams(1),
        name="retention",
    )(*([a32_lat] * 5), *([a32_ctx] * 5), *smalls)


def _out_mlp_kernel(x_ref, ya_ref, yb_ref, yc_ref, yd_ref, mod_ref, wo_ref, w1_ref, w2_ref, *rest, ff_chunk):
    o_ref = rest[-1]
    x = x_ref[...]
    gw = GROUP_W
    y = (_dot(ya_ref[...], wo_ref[0 * gw:1 * gw, :]) + _dot(yb_ref[...], wo_ref[1 * gw:2 * gw, :])
         + _dot(yc_ref[...], wo_ref[2 * gw:3 * gw, :]) + _dot(yd_ref[...], wo_ref[3 * gw:4 * gw, :]))
    x1 = x + mod_ref[2:3, :] * y
    h = (_rms(x1) * (1.0 + mod_ref[4:5, :]) + mod_ref[3:4, :]).astype(BF16)
    acc = jnp.zeros_like(x1)
    for j in range(w1_ref.shape[1] // ff_chunk):
        a = jnp.maximum(_dot(h, w1_ref[:, j * ff_chunk:(j + 1) * ff_chunk]), 0.0)
        acc = acc + _dot((a * a).astype(BF16), w2_ref[j * ff_chunk:(j + 1) * ff_chunk, :])
    x2 = x1 + mod_ref[5:6, :] * acc
    o_ref[...] = x2 if len(rest) == 1 else _rms(x2) * rest[0][...]


def _out_mlp(xseq, ys, mods, weights, layer, tm, in_place, final_gain=None):
    b, n, d = xseq.shape
    per_batch = mods.shape[0] == b
    tok = lambda w: pl.BlockSpec((None, tm, w), lambda i, j: (i, j, 0))
    in_specs = [tok(d)] + [tok(GROUP_W)] * 4
    in_specs += [pl.BlockSpec((None, N_MOD, d), lambda i, j: (i if per_batch else 0, 0, 0))]
    in_specs += [_resident(w, layer) for w in weights]
    args = [xseq, *ys, mods, *weights]
    if final_gain is not None:
        in_specs.append(_resident(final_gain))
        args.append(final_gain)
    return pl.pallas_call(
        functools.partial(_out_mlp_kernel, ff_chunk=1024),
        grid=(b, n // tm),
        in_specs=in_specs,
        out_specs=tok(d),
        out_shape=jax.ShapeDtypeStruct((b, n, d), F32),
        input_output_aliases={0: 0} if in_place else {},
        compiler_params=_cparams(2),
        name="out_mlp",
    )(*args)


def _rope_tables(n_lat):
    n = np.arange(n_lat)
    row, col = (n // GRID_W).astype(np.float64), (n % GRID_W).astype(np.float64)

    def table(groups):
        cos = np.ones((n_lat, LANES)); up = np.zeros((n_lat, LANES)); dn = np.zeros((n_lat, LANES))
        for rot_start, rot_dim in groups:
            half = rot_dim // 2
            q = half // 2
            freqs = ROPE_BASE ** (-np.arange(q, dtype=np.float64) / q)
            for axis, pos in enumerate((row, col)):
                ang = pos[:, None] * freqs[None, :]
                base = rot_start + axis * half
                cos[:, base:base + q] = np.cos(ang)
                cos[:, base + q:base + 2 * q] = np.cos(ang)
                up[:, base:base + q] = -np.sin(ang)
                dn[:, base + q:base + 2 * q] = np.sin(ang)
        return jnp.asarray(np.stack([cos, up, dn]).astype(np.float32))

    return table([(0, GQA_HEAD_DIM), (GQA_HEAD_DIM, GQA_HEAD_DIM)]), table([(MLA_NOPE, MLA_ROPE)])


def _layout_w_in(w_in):
    split = OFF_KPE + MLA_ROPE
    w_a = jnp.pad(w_in[..., :split].astype(BF16), ((0, 0), (0, 0), (0, N_IN_A - split)))
    w_b = w_in[..., split:].astype(BF16)
    assert w_b.shape[-1] == N_IN_B
    return w_a, w_b


def _layout_w_uq(w_uq):
    lead = w_uq.shape[:-1]
    w = w_uq.reshape(lead + (MLA_HEADS, MLA_NOPE + MLA_ROPE))
    w = jnp.pad(w, ((0, 0),) * (w.ndim - 1) + ((0, LANES - MLA_NOPE - MLA_ROPE),))
    return w.reshape(lead + (MLA_HEADS * LANES,)).astype(BF16)


def _layout_w_ukv(w_ukv):
    lead = w_ukv.shape[:-1]
    w = w_ukv.reshape(lead + (MLA_HEADS, MLA_NOPE + MLA_V))
    pad = lambda a: jnp.pad(a, ((0, 0),) * (a.ndim - 1) + ((0, LANES - a.shape[-1]),)).reshape(
        lead + (MLA_HEADS * LANES,))
    return jnp.concatenate([pad(w[..., :MLA_NOPE]), pad(w[..., MLA_NOPE:])], axis=-1).astype(BF16)


def kernel(x, c, ctx, c_ctx, w_mod, b_mod, w_in, w_out, conv_dw, conv_b, conv_ln_g, conv_ln_b, conv_pw, mla_q_g, mla_kv_g, mla_uq, mla_ukv, gqa_q_g, gqa_k_g, ret_decay, ret_norm_g, mlp_w1, mlp_w2, final_g):
    b, n_lat, d = x.shape
    n_ctx = ctx.shape[1]
    depth = w_mod.shape[0]
    assert n_lat % LAT_TILE == 0 and n_lat % PROJ_TILE == 0 and n_lat % KEY_CHUNK == 0 and n_ctx % KEY_CHUNK == 0
    assert n_ctx % NORM_ROWS == 0 and n_lat % NORM_ROWS == 0 and NORM_ROWS % (2 * RET_CHUNK) == 0

    rows = ((b + 1 + SUBLANES - 1) // SUBLANES) * SUBLANES
    cond = jnp.zeros((rows, d), F32).at[:b].set(c).at[b].set(c_ctx)
    mod_all = _adaln(cond, w_mod, b_mod).reshape(depth, rows, N_MOD, d)
    tables = _rope_tables(n_lat)
    mla = (A16_MQ, 4 * LANES, A16_MK, A16_MV, 4 * LANES, (0, 1, 2, 3), (0, 1, 2, 3))
    gqa = (A16_GQ, 2 * LANES, A16_GK, A16_GV, 2 * LANES, (0, 0, 1, 1), (0, 0, 1, 1))

    row = lambda a: a[:, None, :]
    w_proj = (*_layout_w_in(w_in), _layout_w_uq(mla_uq), _layout_w_ukv(mla_ukv), row(mla_q_g), row(mla_kv_g),
              row(jnp.tile(gqa_q_g, (1, 2))), row(jnp.tile(gqa_k_g, (1, 2))))
    w_conv = (jnp.broadcast_to(conv_dw[:, :, None, :], conv_dw.shape[:2] + (SUBLANES, GROUP_W)), row(conv_b), row(conv_ln_g), row(conv_ln_b), conv_pw.astype(BF16))
    dec = ret_decay.astype(F32)
    w_ret = (jnp.repeat(dec, RET_QK, axis=2), jnp.repeat(dec, RET_V, axis=2), jnp.repeat(dec, RET_CHUNK, axis=2),
             ret_norm_g.reshape(depth, 2, RET_HEADS * RET_V))
    w_mlp = (w_out.astype(BF16), mlp_w1.astype(BF16), mlp_w2.astype(BF16))

    cx = ctx
    for layer in range(depth):
        last = layer == depth - 1
        mods_lat, mods_ctx = mod_all[layer, :b], mod_all[layer, b:b + 1]
        a16_lat, a32_lat = _in_proj(x, mods_lat, w_proj, layer, tables, PROJ_TILE)
        a16_ctx, a32_ctx = _in_proj(cx, mods_ctx, w_proj, layer, None, n_ctx)
        ya_ctx = _conv(a32_ctx, w_conv, layer)
        yb_lat, = _attention_lat(a16_lat, a16_ctx, *mla, "mla_attn")
        yc_lat, ya_lat = _attention_lat(a16_lat, a16_ctx, *gqa, "gqa_attn", conv=(a32_lat, w_conv, layer))
        yd_lat, yd_ctx = _retention(a32_lat, a32_ctx, w_ret, layer)
        x = _out_mlp(x, (ya_lat, yb_lat, yc_lat, yd_lat), mods_lat, w_mlp, layer, LAT_TILE, layer > 0,
                     final_gain=final_g[None, :] if last else None)
        if not last:
            yb_ctx = _attention_ctx(a16_ctx, *mla, "mla_attn_ctx")
            yc_ctx = _attention_ctx(a16_ctx, *gqa, "gqa_attn_ctx")
            cx = _out_mlp(cx, (ya_ctx, yb_ctx, yc_ctx, yd_ctx), mods_ctx, w_mlp, layer, n_ctx, layer > 0)
    return x
```

```python
import functools

import numpy as np
import jax
import jax.numpy as jnp
from jax import lax
from jax.experimental import pallas as pl
from jax.experimental.pallas import tpu as pltpu

F32 = jnp.float32
BF16 = jnp.bfloat16

GRID_W = 64
N_MOD = 6
EPS = 1e-6
ROPE_BASE = 10000.0
GROUP_W = 256
CONV_WIDTH = 31
CONV_PAD = 16
MLA_HEADS = 4
MLA_NOPE = 64
MLA_ROPE = 32
MLA_V = 64
MLA_Q_RANK = 256
MLA_KV_RANK = 128
GQA_HEADS = 4
GQA_KV_HEADS = 2
GQA_HEAD_DIM = 64
RET_HEADS = 4
RET_QK = 32
RET_V = 64
RET_CHUNK = 128
LANES = 128
SUBLANES = 8
LOG2_E = 1.4426950408889634
LAT_TILE = 512
PROJ_TILE = 1024
KEY_CHUNK = 256
NORM_ROWS = 256
VMEM_LIMIT = 56 * 1024 * 1024

OFF_A, OFF_CQ, OFF_CKV, OFF_KPE, N_IN_A = 0, 512, 768, 896, 1024
OFF_GQ, OFF_GK, OFF_RQ, OFF_RV, OFF_GF, OFF_GB, N_IN_B = 0, 256, 512, 768, 1024, 1280, 1536
A16_MQ, A16_MK, A16_MV, A16_GK, A16_GQ, A16_GV, A16_W = 0, 512, 1024, 1536, 2048, 2304, 2560
A32_Y, A32_RQ, A32_RK, A32_RV, A32_GF, A32_GB, A32_W = 0, 256, 384, 512, 768, 1024, 1280


def _cparams(n_axes):
    return pltpu.CompilerParams(dimension_semantics=("arbitrary",) * n_axes,
                                vmem_limit_bytes=VMEM_LIMIT)


def _resident(a, layer=None):
    if layer is None:
        return pl.BlockSpec(a.shape, lambda *_: (0,) * a.ndim, pipeline_mode=pl.Buffered(1))
    return pl.BlockSpec((None,) + a.shape[1:], lambda *_: (layer,) + (0,) * (a.ndim - 1),
                        pipeline_mode=pl.Buffered(1))


def _rms(x):
    return x * lax.rsqrt(jnp.mean(x * x, axis=-1, keepdims=True) + EPS)


def _sigmoid(x):
    return 1.0 / (1.0 + jnp.exp(-x))


def _silu(x):
    return x * _sigmoid(x)


def _dot(a, b):
    return jnp.dot(a, b, preferred_element_type=F32)


def _dot_nt(a, b):
    return lax.dot_general(a, b, (((1,), (1,)), ((), ())), preferred_element_type=F32)


def _dot_tn(a, b):
    return lax.dot_general(a, b, (((0,), (0,)), ((), ())), preferred_element_type=F32)


def _rope(x, cos, sin_up, sin_dn, shift):
    n = x.shape[-1]
    return x * cos + pltpu.roll(x, n - shift, 1) * sin_up + pltpu.roll(x, shift, 1) * sin_dn


def _adaln_kernel(cond_ref, w_ref, b_ref, o_ref):
    cond = cond_ref[...]
    o_ref[...] = _dot(_silu(cond).astype(BF16), w_ref[...].astype(BF16)) + b_ref[...]


def _adaln(cond, w_mod, b_mod):
    depth, d, n = w_mod.shape
    r = cond.shape[0]
    tn = 1536
    return pl.pallas_call(
        _adaln_kernel,
        grid=(depth, n // tn),
        in_specs=[pl.BlockSpec((r, d), lambda l, j: (0, 0)),
                  pl.BlockSpec((None, d, tn), lambda l, j: (l, 0, j)),
                  pl.BlockSpec((None, 1, tn), lambda l, j: (l, 0, j))],
        out_specs=pl.BlockSpec((None, r, tn), lambda l, j: (l, 0, j)),
        out_shape=jax.ShapeDtypeStruct((depth, r, n), F32),
        compiler_params=_cparams(2),
        name="adaln",
    )(cond, w_mod, b_mod.reshape(depth, 1, n))


def _in_proj_kernel(x_ref, mod_ref, w_a_ref, w_b_ref, w_uq_ref, w_ukv_ref, g_mq_ref, g_mkv_ref, g_gq_ref, g_gk_ref,
                    *rest, rotary, sub_tiles):
    a16_ref, a32_ref = rest[-2:]
    lo = lax.broadcasted_iota(jnp.int32, (1, LANES), 1) < (LANES // 2)
    q_scale = (MLA_NOPE + MLA_ROPE) ** -0.5 * LOG2_E
    n_rows = x_ref.shape[0] // sub_tiles

    for st in range(sub_tiles):
        rows = slice(st * n_rows, (st + 1) * n_rows)
        h = (_rms(x_ref[rows, :]) * (1.0 + mod_ref[1:2, :]) + mod_ref[0:1, :]).astype(BF16)

        def proj(w_ref, off, width, h=h):
            return _dot(h, w_ref[:, off:off + width])

        def put16(off, val, rows=rows):
            a16_ref[rows, off:off + val.shape[1]] = val.astype(BF16)

        def put32(off, val, rows=rows):
            a32_ref[rows, off:off + val.shape[1]] = val

        if rotary:
            tg_ref, tm_ref = rest[:2]
            rope_g = lambda v, rows=rows: _rope(v, tg_ref[0, rows, :], tg_ref[1, rows, :], tg_ref[2, rows, :],
                                                GQA_HEAD_DIM // 4)
            rope_m = lambda v, rows=rows: _rope(v, tm_ref[0, rows, :], tm_ref[1, rows, :], tm_ref[2, rows, :],
                                                MLA_ROPE // 4)
        else:
            rope_g = rope_m = lambda v: v

        def pair_norm_rope(xp, gain, rope_g=rope_g):
            x2 = xp * xp
            s_lo = jnp.sum(jnp.where(lo, x2, 0.0), axis=-1, keepdims=True)
            s_hi = jnp.sum(jnp.where(lo, 0.0, x2), axis=-1, keepdims=True)
            r = jnp.where(lo, lax.rsqrt(s_lo * (1.0 / GQA_HEAD_DIM) + EPS),
                          lax.rsqrt(s_hi * (1.0 / GQA_HEAD_DIM) + EPS))
            return rope_g(xp * r * gain)

        cq = (_rms(proj(w_a_ref, OFF_CQ, MLA_Q_RANK)) * g_mq_ref[...]).astype(BF16)
        ckv_kpe = proj(w_a_ref, OFF_CKV, 2 * LANES)
        ckv = (_rms(ckv_kpe[:, :LANES]) * g_mkv_ref[...]).astype(BF16)
        gq = proj(w_b_ref, OFF_GQ, GQA_HEADS * GQA_HEAD_DIM)
        gkv = proj(w_b_ref, OFF_GK, 2 * LANES)

        a = proj(w_a_ref, OFF_A, 2 * GROUP_W)
        put32(A32_Y, a[:, :GROUP_W] * _sigmoid(a[:, GROUP_W:]))
        rqk = proj(w_b_ref, OFF_RQ, 2 * LANES)
        put32(A32_RQ, rqk[:, :LANES])
        put32(A32_RK, rqk[:, LANES:] * RET_QK ** -0.5)
        put32(A32_RV, proj(w_b_ref, OFF_RV, RET_HEADS * RET_V))
        put32(A32_GF, _silu(proj(w_b_ref, OFF_GF, GROUP_W)))
        put32(A32_GB, _silu(proj(w_b_ref, OFF_GB, GROUP_W)))

        q = _dot(cq, w_uq_ref[...])
        for hd in range(MLA_HEADS):
            put16(A16_MQ + hd * LANES, rope_m(q[:, hd * LANES:(hd + 1) * LANES]) * q_scale)
        kv = _dot(ckv, w_ukv_ref[...])
        kpe = rope_m(ckv_kpe[:, LANES:])
        for hd in range(MLA_HEADS):
            put16(A16_MK + hd * LANES, kv[:, hd * LANES:(hd + 1) * LANES] + kpe)
            put16(A16_MV + hd * LANES,
                  jnp.where(lo, kv[:, (MLA_HEADS + hd) * LANES:(MLA_HEADS + hd + 1) * LANES], 1.0))

        for pr in range(2):
            qp = pair_norm_rope(gq[:, pr * LANES:(pr + 1) * LANES], g_gq_ref[...])
            put16(A16_GQ + pr * LANES, qp * (GQA_HEAD_DIM ** -0.5 * LOG2_E))
        kp = pair_norm_rope(gkv[:, :LANES], g_gk_ref[...])
        kp_sw = pltpu.roll(kp, LANES // 2, 1)
        put16(A16_GK + 0 * LANES, jnp.where(lo, kp, 0.0))
        put16(A16_GK + 1 * LANES, jnp.where(lo, 0.0, kp_sw))
        put16(A16_GK + 2 * LANES, jnp.where(lo, kp_sw, 0.0))
        put16(A16_GK + 3 * LANES, jnp.where(lo, 0.0, kp))
        vp = gkv[:, LANES:]
        put16(A16_GV + 0 * LANES, jnp.where(lo, vp, 1.0))
        put16(A16_GV + 1 * LANES, jnp.where(lo, pltpu.roll(vp, LANES // 2, 1), 1.0))


def _in_proj(xseq, mods, weights, layer, tables, tm):
    b, n, d = xseq.shape
    per_batch = mods.shape[0] == b
    tok = lambda w: pl.BlockSpec((None, tm, w), lambda i, j: (i, j, 0))
    in_specs = [tok(d), pl.BlockSpec((None, N_MOD, d), lambda i, j: (i if per_batch else 0, 0, 0))]
    in_specs += [_resident(w, layer) for w in weights]
    args = [xseq, mods, *weights]
    if tables is not None:
        in_specs += [pl.BlockSpec((3, tm, LANES), lambda i, j: (0, j, 0))] * 2
        args += list(tables)
    return pl.pallas_call(
        functools.partial(_in_proj_kernel, rotary=tables is not None, sub_tiles=tm // 256),
        grid=(b, n // tm),
        in_specs=in_specs,
        out_specs=[tok(A16_W), tok(A32_W)],
        out_shape=[jax.ShapeDtypeStruct((b, n, A16_W), BF16), jax.ShapeDtypeStruct((b, n, A32_W), F32)],
        compiler_params=_cparams(2),
        name="in_proj",
    )(*args)


CONV_ROWS = 64
CONV_FIRST = CONV_PAD - CONV_WIDTH // 2
CONV_TAIL = ((CONV_FIRST + CONV_WIDTH - 1) // SUBLANES) * SUBLANES


def _conv_steps(sh_ref, dw_ref, b_ref, lg_ref, lb_ref, pw_ref, o_ref, length):
    def fill(before, body, after):
        sh_ref[0, 0:CONV_PAD, :] = before
        sh_ref[0, CONV_PAD:CONV_PAD + length, :] = body
        sh_ref[0, CONV_PAD + length:2 * CONV_PAD + length, :] = after

    def shift(base, n=CONV_ROWS):
        win = sh_ref[0, pl.ds(base, n + SUBLANES), :]
        for r in range(1, SUBLANES):
            sh_ref[r, pl.ds(base, n), :] = win[r:r + n, :]

    def shift_tail():
        shift(length, CONV_TAIL)

    def taps(base):
        groups = (CONV_ROWS // SUBLANES, SUBLANES, GROUP_W)
        acc = jnp.zeros(groups, F32) + b_ref[...]
        for k in range(CONV_WIDTH):
            off = CONV_FIRST + k
            slab = sh_ref[off % SUBLANES, pl.ds(base + (off // SUBLANES) * SUBLANES, CONV_ROWS), :]
            acc = acc + slab.reshape(groups) * dw_ref[k]
        sh_ref[0, pl.ds(base, CONV_ROWS), :] = acc.reshape(CONV_ROWS, GROUP_W)

    def finish(base):
        acc = sh_ref[0, pl.ds(base, NORM_ROWS), :]
        mu = jnp.mean(acc, axis=-1, keepdims=True)
        cen = acc - mu
        var = jnp.mean(cen * cen, axis=-1, keepdims=True)
        z = _silu(cen * lax.rsqrt(var + EPS) * lg_ref[...] + lb_ref[...])
        o_ref[pl.ds(base, NORM_ROWS), :] = _dot(z.astype(BF16), pw_ref[...]).astype(BF16)

    return fill, shift, shift_tail, taps, finish


def _conv_kernel(y_ref, dw_ref, b_ref, lg_ref, lb_ref, pw_ref, o_ref, sh_ref):
    length = y_ref.shape[0]
    fill, shift, shift_tail, taps, finish = _conv_steps(sh_ref, dw_ref, b_ref, lg_ref, lb_ref, pw_ref, o_ref, length)
    zeros = jnp.zeros((CONV_PAD, GROUP_W), F32)
    fill(zeros, y_ref[...], zeros)

    def loop(step, rows, **kw):
        def body(c, carry):
            step(pl.multiple_of(c * rows, rows))
            return carry
        lax.fori_loop(0, length // rows, body, 0, **kw)

    loop(shift, CONV_ROWS)
    shift_tail()
    loop(taps, CONV_ROWS)
    loop(finish, NORM_ROWS, unroll=min(length // NORM_ROWS, 4))


def _conv(a32, weights, layer):
    b, n, _ = a32.shape
    return pl.pallas_call(
        _conv_kernel,
        grid=(b,),
        in_specs=[pl.BlockSpec((None, n, GROUP_W), lambda i: (i, 0, A32_Y // GROUP_W))]
        + [_resident(w, layer) for w in weights],
        out_specs=pl.BlockSpec((None, n, GROUP_W), lambda i: (i, 0, 0)),
        out_shape=jax.ShapeDtypeStruct((b, n, GROUP_W), BF16),
        scratch_shapes=[pltpu.VMEM((SUBLANES, n + 2 * CONV_PAD, GROUP_W), F32)],
        compiler_params=_cparams(1),
        name="conv",
    )(a32, *weights)


def _normalized_pair(acc_even, acc_odd):
    half = LANES // 2
    lo = lax.broadcasted_iota(jnp.int32, (1, LANES), 1) < half
    return jnp.where(lo, acc_even / pltpu.roll(acc_even, half, 1), pltpu.roll(acc_odd, half, 1) / acc_odd)


def _attn_lat_kernel(q_ref, qn_ref, kl_ref, kc_ref, kln_ref, kcn_ref, vl_ref, vc_ref, *rest,
                     q_blocks, v_blocks, with_conv):
    if with_conv:
        y_ref, yb_ref, ya_next_ref, dw_ref, cb_ref, lg_ref, lb_ref, pw_ref, o_ref, ya_ref = rest[:10]
        s_ref, m_ref, mb_ref, acc_ref, sh_ref = rest[10:]
    else:
        o_ref, s_ref, m_ref, mb_ref, acc_ref = rest
    tq = q_ref.shape[0]
    tk = KEY_CHUNK
    last_tile = pl.program_id(1) == pl.num_programs(1) - 1
    chunks = [(kl_ref, kln_ref, vl_ref, c * tk) for c in range(kl_ref.shape[0] // tk)]
    chunks += [(kc_ref, kcn_ref, vc_ref, c * tk) for c in range(kc_ref.shape[0] // tk)]

    def scores_chunk(hd, ci, ref, next_tile=False):
        slot = hd % 2
        k_ref, kn_ref, _, r0 = chunks[ci]
        q = ref[:, q_blocks[hd] * LANES:(q_blocks[hd] + 1) * LANES]
        k = k_ref[r0:r0 + tk, hd * LANES:(hd + 1) * LANES]
        if next_tile:
            k = jnp.where(last_tile, kn_ref[r0:r0 + tk, hd * LANES:(hd + 1) * LANES], k)
        s = _dot_nt(q, k)
        s_ref[slot, :, ci * tk:(ci + 1) * tk] = s
        mx = s[:, 0:LANES]
        for i in range(1, tk // LANES):
            mx = jnp.maximum(mx, s[:, i * LANES:(i + 1) * LANES])
        m_ref[slot] = mx if ci == 0 else jnp.maximum(m_ref[slot], mx)

    def scores_finish(hd):
        slot = hd % 2
        mb_ref[slot] = jnp.broadcast_to(jnp.max(m_ref[slot], axis=-1, keepdims=True), (tq, LANES))

    def values_chunk(hd, ci):
        slot = hd % 2
        _, _, v_ref, r0 = chunks[ci]
        mb = mb_ref[slot]
        p = jnp.concatenate(
            [jnp.exp2(s_ref[slot, :, ci * tk + i * LANES:ci * tk + (i + 1) * LANES] - mb).astype(BF16)
             for i in range(tk // LANES)], axis=1)
        pv = _dot(p, v_ref[r0:r0 + tk, v_blocks[hd] * LANES:(v_blocks[hd] + 1) * LANES])
        acc_ref[slot] = pv if ci == 0 else acc_ref[slot] + pv

    n = len(chunks)

    def scores_ahead(unit, ci):
        if unit < 4:
            scores_chunk(unit, ci, q_ref)
        else:
            scores_chunk(unit - 4, ci, qn_ref, next_tile=True)

    @pl.when((pl.program_id(0) == 0) & (pl.program_id(1) == 0))
    def _():
        for ci in range(n):
            scores_chunk(0, ci, q_ref)
        scores_finish(0)
        scores_chunk(1, 0, q_ref)

    side_steps = []
    if with_conv:
        fill, shift, shift_tail, taps, finish = _conv_steps(sh_ref, dw_ref, cb_ref, lg_ref, lb_ref, pw_ref, ya_ref, tq)
        j, last = pl.program_id(1), pl.num_programs(1) - 1
        side_steps.append(lambda: fill(jnp.where(j > 0, yb_ref[...], 0.0), y_ref[...],
                                       jnp.where(j < last, ya_next_ref[...], 0.0)))
        side_steps += [functools.partial(shift, c * CONV_ROWS) for c in range(tq // CONV_ROWS)] + [shift_tail]
        side_steps += [functools.partial(taps, c * CONV_ROWS) for c in range(tq // CONV_ROWS)]
        side_steps += [functools.partial(finish, c * NORM_ROWS) for c in range(tq // NORM_ROWS)]
    done = 0

    for hd in range(4):
        for ci in range(n):
            values_chunk(hd, ci)
            if ci + 1 < n:
                scores_ahead(hd + 1, ci + 1)
            else:
                scores_ahead(hd + 2, 0)
            if ci + 2 == n:
                scores_finish((hd + 1) % 4)
            due = ((hd * n + ci + 1) * len(side_steps)) // (4 * n)
            for step in side_steps[done:due]:
                step()
            done = due
        if hd % 2 == 1:
            pr = hd // 2
            o_ref[:, pr * LANES:(pr + 1) * LANES] = _normalized_pair(acc_ref[0], acc_ref[1]).astype(BF16)


def _attention_lat(a16_lat, a16_ctx, q_off, q_width, k_off, v_off, v_width, q_blocks, v_blocks, name, conv=None):
    b, n_lat, _ = a16_lat.shape
    n_ctx = a16_ctx.shape[1]
    tq = LAT_TILE
    n_tiles = n_lat // tq
    kw = 4 * LANES
    qc, kc, vc = q_off // q_width, k_off // kw, v_off // v_width
    next_row = lambda i: jnp.minimum(i + 1, b - 1)
    wraps = lambda j: j == n_tiles - 1
    in_specs = [pl.BlockSpec((None, tq, q_width), lambda i, j: (i, j, qc)),
                pl.BlockSpec((None, tq, q_width),
                             lambda i, j: (jnp.where(wraps(j), next_row(i), i), jnp.where(wraps(j), 0, j + 1), qc)),
                pl.BlockSpec((None, n_lat, kw), lambda i, j: (i, 0, kc)),
                pl.BlockSpec((None, n_ctx, kw), lambda i, j: (i, 0, kc)),
                pl.BlockSpec((None, n_lat, kw), lambda i, j: (next_row(i), 0, kc)),
                pl.BlockSpec((None, n_ctx, kw), lambda i, j: (next_row(i), 0, kc)),
                pl.BlockSpec((None, n_lat, v_width), lambda i, j: (i, 0, vc)),
                pl.BlockSpec((None, n_ctx, v_width), lambda i, j: (i, 0, vc))]
    args = [a16_lat, a16_lat, a16_lat, a16_ctx, a16_lat, a16_ctx, a16_lat, a16_ctx]
    tile = lambda w: pl.BlockSpec((None, tq, w), lambda i, j: (i, j, 0))
    out_specs = [tile(2 * LANES)]
    out_shape = [jax.ShapeDtypeStruct((b, n_lat, 2 * LANES), BF16)]
    scratch = [pltpu.VMEM((2, tq, n_lat + n_ctx), F32), pltpu.VMEM((2, tq, LANES), F32),
               pltpu.VMEM((2, tq, LANES), F32), pltpu.VMEM((2, tq, LANES), F32)]
    if conv is not None:
        a32_lat, weights, layer = conv
        yc = A32_Y // GROUP_W
        per_tile = tq // CONV_PAD
        halo = lambda index: pl.BlockSpec((None, CONV_PAD, GROUP_W), lambda i, j: (i, index(j), yc))
        in_specs += [pl.BlockSpec((None, tq, GROUP_W), lambda i, j: (i, j, yc)),
                     halo(lambda j: jnp.maximum(j * per_tile - 1, 0)),
                     halo(lambda j: jnp.minimum((j + 1) * per_tile, n_lat // CONV_PAD - 1))]
        in_specs += [_resident(w, layer) for w in weights]
        args += [a32_lat, a32_lat, a32_lat, *weights]
        out_specs.append(tile(GROUP_W))
        out_shape.append(jax.ShapeDtypeStruct((b, n_lat, GROUP_W), BF16))
        scratch.append(pltpu.VMEM((SUBLANES, tq + 2 * CONV_PAD, GROUP_W), F32))
    return pl.pallas_call(
        functools.partial(_attn_lat_kernel, q_blocks=q_blocks, v_blocks=v_blocks, with_conv=conv is not None),
        grid=(b, n_tiles),
        in_specs=in_specs,
        out_specs=out_specs,
        out_shape=out_shape,
        scratch_shapes=scratch,
        compiler_params=_cparams(2),
        name=name,
    )(*args)


def _attn_ctx_kernel(q_ref, k_ref, v_ref, o_ref, *, q_blocks, v_blocks):
    for pr in range(2):
        accs = []
        for half in range(2):
            hd = 2 * pr + half
            q = q_ref[:, q_blocks[hd] * LANES:(q_blocks[hd] + 1) * LANES]
            s = _dot_nt(q, k_ref[:, hd * LANES:(hd + 1) * LANES])
            p = jnp.exp2(s - jnp.max(s, axis=-1, keepdims=True))
            accs.append(_dot(p.astype(BF16), v_ref[:, v_blocks[hd] * LANES:(v_blocks[hd] + 1) * LANES]))
        o_ref[:, pr * LANES:(pr + 1) * LANES] = _normalized_pair(*accs).astype(BF16)


def _attention_ctx(a16_ctx, q_off, q_width, k_off, v_off, v_width, q_blocks, v_blocks, name):
    b, n_ctx, _ = a16_ctx.shape
    kw = 4 * LANES
    qc, kc, vc = q_off // q_width, k_off // kw, v_off // v_width
    return pl.pallas_call(
        functools.partial(_attn_ctx_kernel, q_blocks=q_blocks, v_blocks=v_blocks),
        grid=(b,),
        in_specs=[pl.BlockSpec((None, n_ctx, q_width), lambda i: (i, 0, qc)),
                  pl.BlockSpec((None, n_ctx, kw), lambda i: (i, 0, kc)),
                  pl.BlockSpec((None, n_ctx, v_width), lambda i: (i, 0, vc))],
        out_specs=pl.BlockSpec((None, n_ctx, 2 * LANES), lambda i: (i, 0, 0)),
        out_shape=jax.ShapeDtypeStruct((b, n_ctx, 2 * LANES), BF16),
        compiler_params=_cparams(1),
        name=name,
    )(a16_ctx, a16_ctx, a16_ctx)


def _log_sigmoid(x):
    return jnp.minimum(x, 0.0) - jnp.log(1.0 + jnp.exp(-jnp.abs(x)))


def _ret_kernel(ql_ref, kl_ref, vl_ref, gfl_ref, gbl_ref, qc_ref, kc_ref, vc_ref, gfc_ref, gbc_ref,
                dq_ref, dv_ref, dc_ref, ng_ref, ol_ref, oc_ref, acc_ref):
    c = RET_CHUNK
    n_lat, n_ctx = ql_ref.shape[0], qc_ref.shape[0]
    qk_w = RET_HEADS * RET_QK
    v_w = RET_HEADS * RET_V

    row_i = lax.broadcasted_iota(jnp.int32, (c, 1), 0).astype(F32)
    qk_head = lax.broadcasted_iota(jnp.int32, (1, qk_w), 1) // RET_QK
    v_head = lax.broadcasted_iota(jnp.int32, (1, v_w), 1) // RET_V
    qk_masks = [qk_head == hd for hd in range(RET_HEADS)]
    v_masks = [v_head == hd for hd in range(RET_HEADS)]
    state_rows = lax.broadcasted_iota(jnp.int32, (qk_w, 1), 0) // RET_QK
    bd_mask = (state_rows == v_head).astype(F32)
    ones_rows = lax.broadcasted_iota(jnp.int32, (v_w, 1), 0) // RET_V
    head_mean = jnp.where(ones_rows == v_head, 1.0 / RET_V, 0.0).astype(BF16)
    ci = lax.broadcasted_iota(jnp.int32, (c, RET_HEADS * c), 0).astype(F32)
    cj = (lax.broadcasted_iota(jnp.int32, (c, RET_HEADS * c), 1) % c).astype(F32)

    def make_step(d):
        lg_q = _log_sigmoid(dq_ref[d:d + 1, :])
        lg_v = _log_sigmoid(dv_ref[d:d + 1, :])
        lg_c = _log_sigmoid(dc_ref[d:d + 1, :])
        if d == 0:
            diff = ci - cj
            q_dec = jnp.exp(lg_q * (row_i + 1.0))
            k_dec = jnp.exp(lg_q * (c - 1.0 - row_i))
        else:
            diff = cj - ci
            q_dec = jnp.exp(lg_q * (c - row_i))
            k_dec = jnp.exp(lg_q * row_i)
        decay = jnp.where(diff >= 0.0, jnp.exp(lg_c * jnp.maximum(diff, 0.0)), 0.0)
        chunk_dec = jnp.exp(lg_v * float(c))

        def step(refs, r0, acc_r0, state):
            q_ref, k_ref, v_ref = refs
            q = q_ref[pl.ds(r0, c), :]
            k = k_ref[pl.ds(r0, c), :]
            v = v_ref[pl.ds(r0, c), :]
            kb, vb = k.astype(BF16), v.astype(BF16)
            k_bd = jnp.concatenate([jnp.where(m, kb, 0) for m in qk_masks], axis=0)
            scores = _dot_nt(q.astype(BF16), k_bd) * decay
            v_bd = jnp.concatenate([jnp.where(m, vb, 0) for m in v_masks], axis=0)
            inner = _dot(scores.astype(BF16), v_bd)
            cross = _dot((q * q_dec).astype(BF16), state.astype(BF16))
            kv = _dot_tn((k * k_dec).astype(BF16), vb)
            new_state = state * chunk_dec + kv * bd_mask
            acc_ref[d, pl.ds(acc_r0, c), :] = inner + cross
            return new_state

        return step

    fwd, bwd = make_step(0), make_step(1)

    def scan(n, acc_base, fwd_refs, bwd_refs, states):
        def body(i, st):
            rf = pl.multiple_of(i * c, c)
            rb = pl.multiple_of((n - 1 - i) * c, c)
            return fwd(fwd_refs, rf, acc_base + rf, st[0]), bwd(bwd_refs, rb, acc_base + rb, st[1])
        return lax.fori_loop(0, n, body, states, unroll=min(n, 8))

    zero = jnp.zeros((qk_w, v_w), F32)
    states = scan(n_ctx // c, n_lat, (qc_ref, kc_ref, vc_ref), (qc_ref, kc_ref, vc_ref), (zero, zero))
    scan(n_lat // c, 0, (ql_ref, kl_ref, vl_ref), (ql_ref, kl_ref, vl_ref), states)

    def readout(o_ref, gate_refs, acc_base):
        rows = NORM_ROWS

        def body(i, carry):
            r0 = pl.multiple_of(i * rows, rows)
            y = None
            for d in range(2):
                o = acc_ref[d, pl.ds(acc_base + r0, rows), :]
                o2 = o * o
                o2_hi = o2.astype(BF16)
                o2_lo = (o2 - o2_hi.astype(F32)).astype(BF16)
                ms = _dot(o2_hi, head_mean) + _dot(o2_lo, head_mean)
                yd = o * lax.rsqrt(ms + EPS) * ng_ref[d:d + 1, :] * gate_refs[d][pl.ds(r0, rows), :]
                y = yd if y is None else y + yd
            o_ref[pl.ds(r0, rows), :] = y.astype(BF16)
            return carry

        n = o_ref.shape[0] // rows
        lax.fori_loop(0, n, body, 0, unroll=min(n, 4))

    readout(ol_ref, (gfl_ref, gbl_ref), 0)
    readout(oc_ref, (gfc_ref, gbc_ref), n_lat)


def _retention(a32_lat, a32_ctx, smalls, layer):
    b, n_lat, _ = a32_lat.shape
    n_ctx = a32_ctx.shape[1]

    def seq_specs(n):
        spec = lambda off, w: pl.BlockSpec((None, n, w), lambda i: (i, 0, off // w))
        return [spec(A32_RQ, 128), spec(A32_RK, 128), spec(A32_RV, 256), spec(A32_GF, 256), spec(A32_GB, 256)]

    out = lambda n: pl.BlockSpec((None, n, 256), lambda i: (i, 0, 0))
    return pl.pallas_call(
        _ret_kernel,
        grid=(b,),
        in_specs=seq_specs(n_lat) + seq_specs(n_ctx) + [_resident(a, layer) for a in smalls],
        out_specs=[out(n_lat), out(n_ctx)],
        out_shape=[jax.ShapeDtypeStruct((b, n_lat, 256), BF16), jax.ShapeDtypeStruct((b, n_ctx, 256), BF16)],
        scratch_shapes=[pltpu.VMEM((2, n_lat + n_ctx, 256), F32)],
        compiler_params=_cparams(1),
        name="retention",
    )(*([a32_lat] * 5), *([a32_ctx] * 5), *smalls)


def _out_mlp_kernel(x_ref, ya_ref, yb_ref, yc_ref, yd_ref, mod_ref, wo_ref, w1_ref, w2_ref, *rest, ff_chunk):
    o_ref = rest[-1]
    x = x_ref[...]
    gw = GROUP_W
    y = (_dot(ya_ref[...], wo_ref[0 * gw:1 * gw, :]) + _dot(yb_ref[...], wo_ref[1 * gw:2 * gw, :])
         + _dot(yc_ref[...], wo_ref[2 * gw:3 * gw, :]) + _dot(yd_ref[...], wo_ref[3 * gw:4 * gw, :]))
    x1 = x + mod_ref[2:3, :] * y
    h = (_rms(x1) * (1.0 + mod_ref[4:5, :]) + mod_ref[3:4, :]).astype(BF16)
    acc = jnp.zeros_like(x1)
    for j in range(w1_ref.shape[1] // ff_chunk):
        a = jnp.maximum(_dot(h, w1_ref[:, j * ff_chunk:(j + 1) * ff_chunk]), 0.0)
        acc = acc + _dot((a * a).astype(BF16), w2_ref[j * ff_chunk:(j + 1) * ff_chunk, :])
    x2 = x1 + mod_ref[5:6, :] * acc
    o_ref[...] = x2 if len(rest) == 1 else _rms(x2) * rest[0][...]


def _out_mlp(xseq, ys, mods, weights, layer, tm, in_place, final_gain=None):
    b, n, d = xseq.shape
    per_batch = mods.shape[0] == b
    tok = lambda w: pl.BlockSpec((None, tm, w), lambda i, j: (i, j, 0))
    in_specs = [tok(d)] + [tok(GROUP_W)] * 4
    in_specs += [pl.BlockSpec((None, N_MOD, d), lambda i, j: (i if per_batch else 0, 0, 0))]
    in_specs += [_resident(w, layer) for w in weights]
    args = [xseq, *ys, mods, *weights]
    if final_gain is not None:
        in_specs.append(_resident(final_gain))
        args.append(final_gain)
    return pl.pallas_call(
        functools.partial(_out_mlp_kernel, ff_chunk=1024),
        grid=(b, n // tm),
        in_specs=in_specs,
        out_specs=tok(d),
        out_shape=jax.ShapeDtypeStruct((b, n, d), F32),
        input_output_aliases={0: 0} if in_place else {},
        compiler_params=_cparams(2),
        name="out_mlp",
    )(*args)


def _rope_tables(n_lat):
    n = np.arange(n_lat)
    row, col = (n // GRID_W).astype(np.float64), (n % GRID_W).astype(np.float64)

    def table(groups):
        cos = np.ones((n_lat, LANES)); up = np.zeros((n_lat, LANES)); dn = np.zeros((n_lat, LANES))
        for rot_start, rot_dim in groups:
            half = rot_dim // 2
            q = half // 2
            freqs = ROPE_BASE ** (-np.arange(q, dtype=np.float64) / q)
            for axis, pos in enumerate((row, col)):
                ang = pos[:, None] * freqs[None, :]
                base = rot_start + axis * half
                cos[:, base:base + q] = np.cos(ang)
                cos[:, base + q:base + 2 * q] = np.cos(ang)
                up[:, base:base + q] = -np.sin(ang)
                dn[:, base + q:base + 2 * q] = np.sin(ang)
        return jnp.asarray(np.stack([cos, up, dn]).astype(np.float32))

    return table([(0, GQA_HEAD_DIM), (GQA_HEAD_DIM, GQA_HEAD_DIM)]), table([(0, MLA_ROPE)])


def _layout_w_in(w_in):
    split = OFF_KPE + MLA_ROPE
    w_a = jnp.pad(w_in[..., :split].astype(BF16), ((0, 0), (0, 0), (0, N_IN_A - split)))
    w_b = w_in[..., split:].astype(BF16)
    assert w_b.shape[-1] == N_IN_B
    return w_a, w_b


def _layout_w_uq(w_uq):
    lead = w_uq.shape[:-1]
    w = w_uq.reshape(lead + (MLA_HEADS, MLA_NOPE + MLA_ROPE))
    w = jnp.concatenate([w[..., MLA_NOPE:], w[..., :MLA_NOPE]], axis=-1)
    w = jnp.pad(w, ((0, 0),) * (w.ndim - 1) + ((0, LANES - MLA_NOPE - MLA_ROPE),))
    return w.reshape(lead + (MLA_HEADS * LANES,)).astype(BF16)


def _layout_w_ukv(w_ukv):
    lead = w_ukv.shape[:-1]
    w = w_ukv.reshape(lead + (MLA_HEADS, MLA_NOPE + MLA_V))
    pad = lambda a, before: jnp.pad(a, ((0, 0),) * (a.ndim - 1) + ((before, LANES - before - a.shape[-1]),)).reshape(
        lead + (MLA_HEADS * LANES,))
    return jnp.concatenate([pad(w[..., :MLA_NOPE], MLA_ROPE), pad(w[..., MLA_NOPE:], 0)], axis=-1).astype(BF16)


def kernel(x, c, ctx, c_ctx, w_mod, b_mod, w_in, w_out, conv_dw, conv_b, conv_ln_g, conv_ln_b, conv_pw, mla_q_g, mla_kv_g, mla_uq, mla_ukv, gqa_q_g, gqa_k_g, ret_decay, ret_norm_g, mlp_w1, mlp_w2, final_g):
    b, n_lat, d = x.shape
    n_ctx = ctx.shape[1]
    depth = w_mod.shape[0]
    assert n_lat % LAT_TILE == 0 and n_lat % PROJ_TILE == 0 and n_lat % KEY_CHUNK == 0 and n_ctx % KEY_CHUNK == 0
    assert n_ctx % NORM_ROWS == 0 and n_lat % NORM_ROWS == 0 and NORM_ROWS % (2 * RET_CHUNK) == 0

    rows = ((b + 1 + SUBLANES - 1) // SUBLANES) * SUBLANES
    cond = jnp.zeros((rows, d), F32).at[:b].set(c).at[b].set(c_ctx)
    mod_all = _adaln(cond, w_mod, b_mod).reshape(depth, rows, N_MOD, d)
    tables = _rope_tables(n_lat)
    mla = (A16_MQ, 4 * LANES, A16_MK, A16_MV, 4 * LANES, (0, 1, 2, 3), (0, 1, 2, 3))
    gqa = (A16_GQ, 2 * LANES, A16_GK, A16_GV, 2 * LANES, (0, 0, 1, 1), (0, 0, 1, 1))

    row = lambda a: a[:, None, :]
    w_proj = (*_layout_w_in(w_in), _layout_w_uq(mla_uq), _layout_w_ukv(mla_ukv), row(mla_q_g), row(mla_kv_g),
              row(jnp.tile(gqa_q_g, (1, 2))), row(jnp.tile(gqa_k_g, (1, 2))))
    w_conv = (jnp.broadcast_to(conv_dw[:, :, None, :], conv_dw.shape[:2] + (SUBLANES, GROUP_W)), row(conv_b), row(conv_ln_g), row(conv_ln_b), conv_pw.astype(BF16))
    dec = ret_decay.astype(F32)
    w_ret = (jnp.repeat(dec, RET_QK, axis=2), jnp.repeat(dec, RET_V, axis=2), jnp.repeat(dec, RET_CHUNK, axis=2),
             ret_norm_g.reshape(depth, 2, RET_HEADS * RET_V))
    w_mlp = (w_out.astype(BF16), mlp_w1.astype(BF16), mlp_w2.astype(BF16))

    cx = ctx
    for layer in range(depth):
        last = layer == depth - 1
        mods_lat, mods_ctx = mod_all[layer, :b], mod_all[layer, b:b + 1]
        a16_lat, a32_lat = _in_proj(x, mods_lat, w_proj, layer, tables, PROJ_TILE)
        a16_ctx, a32_ctx = _in_proj(cx, mods_ctx, w_proj, layer, None, n_ctx)
        ya_ctx = _conv(a32_ctx, w_conv, layer)
        yb_lat, = _attention_lat(a16_lat, a16_ctx, *mla, "mla_attn")
        yc_lat, ya_lat = _attention_lat(a16_lat, a16_ctx, *gqa, "gqa_attn", conv=(a32_lat, w_conv, layer))
        yd_lat, yd_ctx = _retention(a32_lat, a32_ctx, w_ret, layer)
        x = _out_mlp(x, (ya_lat, yb_lat, yc_lat, yd_lat), mods_lat, w_mlp, layer, LAT_TILE, layer > 0,
                     final_gain=final_g[None, :] if last else None)
        if not last:
            yb_ctx = _attention_ctx(a16_ctx, *mla, "mla_attn_ctx")
            yc_ctx = _attention_ctx(a16_ctx, *gqa, "gqa_attn_ctx")
            cx = _out_mlp(cx, (ya_ctx, yb_ctx, yc_ctx, yd_ctx), mods_ctx, w_mlp, layer, n_ctx, layer > 0)
    return x
```

```python
import functools

import numpy as np
import jax
import jax.numpy as jnp
from jax import lax
from jax.experimental import pallas as pl
from jax.experimental.pallas import tpu as pltpu

F32 = jnp.float32
BF16 = jnp.bfloat16

GRID_W = 64
N_MOD = 6
EPS = 1e-6
ROPE_BASE = 10000.0
GROUP_W = 256
CONV_WIDTH = 31
CONV_PAD = 16
MLA_HEADS = 4
MLA_NOPE = 64
MLA_ROPE = 32
MLA_V = 64
MLA_Q_RANK = 256
MLA_KV_RANK = 128
GQA_HEADS = 4
GQA_KV_HEADS = 2
GQA_HEAD_DIM = 64
RET_HEADS = 4
RET_QK = 32
RET_V = 64
RET_CHUNK = 128
LANES = 128
SUBLANES = 8
LOG2_E = 1.4426950408889634
LAT_TILE = 512
PROJ_TILE = 1024
KEY_CHUNK = 256
NORM_ROWS = 256
VMEM_LIMIT = 56 * 1024 * 1024

OFF_A, OFF_CQ, OFF_CKV, OFF_KPE, N_IN_A = 0, 512, 768, 896, 1024
OFF_GQ, OFF_GK, OFF_RQ, OFF_RV, OFF_GF, OFF_GB, N_IN_B = 0, 256, 512, 768, 1024, 1280, 1536
A16_MQ, A16_MK, A16_MV, A16_GK, A16_GQ, A16_GV, A16_W = 0, 512, 1024, 1536, 2048, 2304, 2560
A32_Y, A32_RQ, A32_RK, A32_RV, A32_GF, A32_GB, A32_W = 0, 256, 384, 512, 768, 1024, 1280


def _cparams(n_axes):
    return pltpu.CompilerParams(dimension_semantics=("arbitrary",) * n_axes,
                                vmem_limit_bytes=VMEM_LIMIT)


def _resident(a, layer=None):
    if layer is None:
        return pl.BlockSpec(a.shape, lambda *_: (0,) * a.ndim, pipeline_mode=pl.Buffered(1))
    return pl.BlockSpec((None,) + a.shape[1:], lambda *_: (layer,) + (0,) * (a.ndim - 1),
                        pipeline_mode=pl.Buffered(1))


def _rms(x):
    return x * lax.rsqrt(jnp.mean(x * x, axis=-1, keepdims=True) + EPS)


def _sigmoid(x):
    return 1.0 / (1.0 + jnp.exp(-x))


def _silu(x):
    return x * _sigmoid(x)


def _dot(a, b):
    return jnp.dot(a, b, preferred_element_type=F32)


def _dot_nt(a, b):
    return lax.dot_general(a, b, (((1,), (1,)), ((), ())), preferred_element_type=F32)


def _dot_tn(a, b):
    return lax.dot_general(a, b, (((0,), (0,)), ((), ())), preferred_element_type=F32)


def _rope(x, cos, sin_up, sin_dn, shift):
    n = x.shape[-1]
    return x * cos + pltpu.roll(x, n - shift, 1) * sin_up + pltpu.roll(x, shift, 1) * sin_dn


def _adaln_kernel(cond_ref, w_ref, b_ref, o_ref):
    cond = cond_ref[...]
    o_ref[...] = _dot(_silu(cond).astype(BF16), w_ref[...].astype(BF16)) + b_ref[...]


def _adaln(cond, w_mod, b_mod):
    depth, d, n = w_mod.shape
    r = cond.shape[0]
    tn = 1536
    return pl.pallas_call(
        _adaln_kernel,
        grid=(depth, n // tn),
        in_specs=[pl.BlockSpec((r, d), lambda l, j: (0, 0)),
                  pl.BlockSpec((None, d, tn), lambda l, j: (l, 0, j)),
                  pl.BlockSpec((None, 1, tn), lambda l, j: (l, 0, j))],
        out_specs=pl.BlockSpec((None, r, tn), lambda l, j: (l, 0, j)),
        out_shape=jax.ShapeDtypeStruct((depth, r, n), F32),
        compiler_params=_cparams(2),
        name="adaln",
    )(cond, w_mod, b_mod.reshape(depth, 1, n))


def _in_proj_kernel(x_ref, mod_ref, w_a_ref, w_b_ref, w_uq_ref, w_ukv_ref, g_mq_ref, g_mkv_ref, g_gq_ref, g_gk_ref,
                    *rest, rotary, sub_tiles):
    a16_ref, a32_ref = rest[-2:]
    lo = lax.broadcasted_iota(jnp.int32, (1, LANES), 1) < (LANES // 2)
    q_scale = (MLA_NOPE + MLA_ROPE) ** -0.5 * LOG2_E
    n_rows = x_ref.shape[0] // sub_tiles

    for st in range(sub_tiles):
        rows = slice(st * n_rows, (st + 1) * n_rows)
        h = (_rms(x_ref[rows, :]) * (1.0 + mod_ref[1:2, :]) + mod_ref[0:1, :]).astype(BF16)

        def proj(w_ref, off, width, h=h):
            return _dot(h, w_ref[:, off:off + width])

        def put16(off, val, rows=rows):
            a16_ref[rows, off:off + val.shape[1]] = val.astype(BF16)

        def put32(off, val, rows=rows):
            a32_ref[rows, off:off + val.shape[1]] = val

        if rotary:
            tg_ref, tm_ref = rest[:2]
            rope_g = lambda v, rows=rows: _rope(v, tg_ref[0, rows, :], tg_ref[1, rows, :], tg_ref[2, rows, :],
                                                GQA_HEAD_DIM // 4)
            rope_m = lambda v, rows=rows: _rope(v, tm_ref[0, rows, :], tm_ref[1, rows, :], tm_ref[2, rows, :],
                                                MLA_ROPE // 4)
        else:
            rope_g = rope_m = lambda v: v

        def pair_norm_rope(xp, gain, rope_g=rope_g):
            x2 = xp * xp
            s_lo = jnp.sum(jnp.where(lo, x2, 0.0), axis=-1, keepdims=True)
            s_hi = jnp.sum(jnp.where(lo, 0.0, x2), axis=-1, keepdims=True)
            r = jnp.where(lo, lax.rsqrt(s_lo * (1.0 / GQA_HEAD_DIM) + EPS),
                          lax.rsqrt(s_hi * (1.0 / GQA_HEAD_DIM) + EPS))
            return rope_g(xp * r * gain)

        cq = (_rms(proj(w_a_ref, OFF_CQ, MLA_Q_RANK)) * g_mq_ref[...]).astype(BF16)
        ckv_kpe = proj(w_a_ref, OFF_CKV, 2 * LANES)
        ckv = (_rms(ckv_kpe[:, :LANES]) * g_mkv_ref[...]).astype(BF16)
        gq = proj(w_b_ref, OFF_GQ, GQA_HEADS * GQA_HEAD_DIM)
        gkv = proj(w_b_ref, OFF_GK, 2 * LANES)

        a = proj(w_a_ref, OFF_A, 2 * GROUP_W)
        put32(A32_Y, a[:, :GROUP_W] * _sigmoid(a[:, GROUP_W:]))
        rqk = proj(w_b_ref, OFF_RQ, 2 * LANES)
        put32(A32_RQ, rqk[:, :LANES])
        put32(A32_RK, rqk[:, LANES:] * RET_QK ** -0.5)
        put32(A32_RV, proj(w_b_ref, OFF_RV, RET_HEADS * RET_V))
        put32(A32_GF, _silu(proj(w_b_ref, OFF_GF, GROUP_W)))
        put32(A32_GB, _silu(proj(w_b_ref, OFF_GB, GROUP_W)))

        q = _dot(cq, w_uq_ref[...])
        for hd in range(MLA_HEADS):
            put16(A16_MQ + hd * LANES, rope_m(q[:, hd * LANES:(hd + 1) * LANES]) * q_scale)
        kv = _dot(ckv, w_ukv_ref[...])
        kpe = rope_m(ckv_kpe[:, LANES:])
        for hd in range(MLA_HEADS):
            put16(A16_MK + hd * LANES, kv[:, hd * LANES:(hd + 1) * LANES] + kpe)
            put16(A16_MV + hd * LANES,
                  jnp.where(lo, kv[:, (MLA_HEADS + hd) * LANES:(MLA_HEADS + hd + 1) * LANES], 1.0))

        for pr in range(2):
            qp = pair_norm_rope(gq[:, pr * LANES:(pr + 1) * LANES], g_gq_ref[...])
            put16(A16_GQ + pr * LANES, qp * (GQA_HEAD_DIM ** -0.5 * LOG2_E))
        kp = pair_norm_rope(gkv[:, :LANES], g_gk_ref[...])
        kp_sw = pltpu.roll(kp, LANES // 2, 1)
        put16(A16_GK + 0 * LANES, jnp.where(lo, kp, 0.0))
        put16(A16_GK + 1 * LANES, jnp.where(lo, 0.0, kp_sw))
        put16(A16_GK + 2 * LANES, jnp.where(lo, kp_sw, 0.0))
        put16(A16_GK + 3 * LANES, jnp.where(lo, 0.0, kp))
        vp = gkv[:, LANES:]
        put16(A16_GV + 0 * LANES, jnp.where(lo, vp, 1.0))
        put16(A16_GV + 1 * LANES, jnp.where(lo, pltpu.roll(vp, LANES // 2, 1), 1.0))


def _in_proj(xseq, mods, weights, layer, tables, tm):
    b, n, d = xseq.shape
    per_batch = mods.shape[0] == b
    tok = lambda w: pl.BlockSpec((None, tm, w), lambda i, j: (i, j, 0))
    in_specs = [tok(d), pl.BlockSpec((None, N_MOD, d), lambda i, j: (i if per_batch else 0, 0, 0))]
    in_specs += [_resident(w, layer) for w in weights]
    args = [xseq, mods, *weights]
    if tables is not None:
        in_specs += [pl.BlockSpec((3, tm, LANES), lambda i, j: (0, j, 0))] * 2
        args += list(tables)
    return pl.pallas_call(
        functools.partial(_in_proj_kernel, rotary=tables is not None, sub_tiles=tm // 256),
        grid=(b, n // tm),
        in_specs=in_specs,
        out_specs=[tok(A16_W), tok(A32_W)],
        out_shape=[jax.ShapeDtypeStruct((b, n, A16_W), BF16), jax.ShapeDtypeStruct((b, n, A32_W), F32)],
        compiler_params=_cparams(2),
        name="in_proj",
    )(*args)


CONV_ROWS = 64
CONV_FIRST = CONV_PAD - CONV_WIDTH // 2
CONV_TAIL = ((CONV_FIRST + CONV_WIDTH - 1) // SUBLANES) * SUBLANES


def _conv_steps(sh_ref, dw_ref, b_ref, lg_ref, lb_ref, pw_ref, o_ref, length):
    def fill(before, body, after):
        sh_ref[0, 0:CONV_PAD, :] = before
        sh_ref[0, CONV_PAD:CONV_PAD + length, :] = body
        sh_ref[0, CONV_PAD + length:2 * CONV_PAD + length, :] = after

    def shift(base, n=CONV_ROWS):
        win = sh_ref[0, pl.ds(base, n + SUBLANES), :]
        for r in range(1, SUBLANES):
            sh_ref[r, pl.ds(base, n), :] = win[r:r + n, :]

    def shift_tail():
        shift(length, CONV_TAIL)

    def taps(base):
        groups = (CONV_ROWS // SUBLANES, SUBLANES, GROUP_W)
        acc = jnp.zeros(groups, F32) + b_ref[...]
        for k in range(CONV_WIDTH):
            off = CONV_FIRST + k
            slab = sh_ref[off % SUBLANES, pl.ds(base + (off // SUBLANES) * SUBLANES, CONV_ROWS), :]
            acc = acc + slab.reshape(groups) * dw_ref[k]
        sh_ref[0, pl.ds(base, CONV_ROWS), :] = acc.reshape(CONV_ROWS, GROUP_W)

    def finish(base):
        acc = sh_ref[0, pl.ds(base, NORM_ROWS), :]
        mu = jnp.mean(acc, axis=-1, keepdims=True)
        cen = acc - mu
        var = jnp.mean(cen * cen, axis=-1, keepdims=True)
        z = _silu(cen * lax.rsqrt(var + EPS) * lg_ref[...] + lb_ref[...])
        o_ref[pl.ds(base, NORM_ROWS), :] = _dot(z.astype(BF16), pw_ref[...]).astype(BF16)

    return fill, shift, shift_tail, taps, finish


def _conv_sequence(y_ref, dw_ref, b_ref, lg_ref, lb_ref, pw_ref, o_ref, sh_ref):
    length = y_ref.shape[0]
    fill, shift, shift_tail, taps, finish = _conv_steps(sh_ref, dw_ref, b_ref, lg_ref, lb_ref, pw_ref, o_ref, length)
    zeros = jnp.zeros((CONV_PAD, GROUP_W), F32)
    fill(zeros, y_ref[...], zeros)

    def loop(step, rows, **kw):
        def body(c, carry):
            step(pl.multiple_of(c * rows, rows))
            return carry
        lax.fori_loop(0, length // rows, body, 0, **kw)

    loop(shift, CONV_ROWS)
    shift_tail()
    loop(taps, CONV_ROWS)
    loop(finish, NORM_ROWS, unroll=min(length // NORM_ROWS, 4))


def _normalized_pair(acc_even, acc_odd):
    half = LANES // 2
    lo = lax.broadcasted_iota(jnp.int32, (1, LANES), 1) < half
    return jnp.where(lo, acc_even / pltpu.roll(acc_even, half, 1), pltpu.roll(acc_odd, half, 1) / acc_odd)


def _attn_lat_kernel(q_ref, qn_ref, kl_ref, kc_ref, kln_ref, kcn_ref, vl_ref, vc_ref, *rest,
                     q_blocks, v_blocks, with_conv):
    if with_conv:
        y_ref, yb_ref, ya_next_ref, dw_ref, cb_ref, lg_ref, lb_ref, pw_ref, o_ref, ya_ref = rest[:10]
        s_ref, m_ref, mb_ref, acc_ref, sh_ref = rest[10:]
    else:
        o_ref, s_ref, m_ref, mb_ref, acc_ref = rest
    tq = q_ref.shape[0]
    tk = KEY_CHUNK
    last_tile = pl.program_id(1) == pl.num_programs(1) - 1
    chunks = [(kl_ref, kln_ref, vl_ref, c * tk) for c in range(kl_ref.shape[0] // tk)]
    chunks += [(kc_ref, kcn_ref, vc_ref, c * tk) for c in range(kc_ref.shape[0] // tk)]

    def scores_chunk(hd, ci, ref, next_tile=False):
        slot = hd % 2
        k_ref, kn_ref, _, r0 = chunks[ci]
        q = ref[:, q_blocks[hd] * LANES:(q_blocks[hd] + 1) * LANES]
        k = k_ref[r0:r0 + tk, hd * LANES:(hd + 1) * LANES]
        if next_tile:
            k = jnp.where(last_tile, kn_ref[r0:r0 + tk, hd * LANES:(hd + 1) * LANES], k)
        s = _dot_nt(q, k)
        s_ref[slot, :, ci * tk:(ci + 1) * tk] = s
        mx = s[:, 0:LANES]
        for i in range(1, tk // LANES):
            mx = jnp.maximum(mx, s[:, i * LANES:(i + 1) * LANES])
        m_ref[slot] = mx if ci == 0 else jnp.maximum(m_ref[slot], mx)

    def scores_finish(hd):
        slot = hd % 2
        mb_ref[slot] = jnp.broadcast_to(jnp.max(m_ref[slot], axis=-1, keepdims=True), (tq, LANES))

    def values_chunk(hd, ci):
        slot = hd % 2
        _, _, v_ref, r0 = chunks[ci]
        mb = mb_ref[slot]
        p = jnp.concatenate(
            [jnp.exp2(s_ref[slot, :, ci * tk + i * LANES:ci * tk + (i + 1) * LANES] - mb).astype(BF16)
             for i in range(tk // LANES)], axis=1)
        pv = _dot(p, v_ref[r0:r0 + tk, v_blocks[hd] * LANES:(v_blocks[hd] + 1) * LANES])
        acc_ref[slot] = pv if ci == 0 else acc_ref[slot] + pv

    n = len(chunks)

    def scores_ahead(unit, ci):
        if unit < 4:
            scores_chunk(unit, ci, q_ref)
        else:
            scores_chunk(unit - 4, ci, qn_ref, next_tile=True)

    @pl.when((pl.program_id(0) == 0) & (pl.program_id(1) == 0))
    def _():
        for ci in range(n):
            scores_chunk(0, ci, q_ref)
        scores_finish(0)
        scores_chunk(1, 0, q_ref)

    side_steps = []
    if with_conv:
        fill, shift, shift_tail, taps, finish = _conv_steps(sh_ref, dw_ref, cb_ref, lg_ref, lb_ref, pw_ref, ya_ref, tq)
        j, last = pl.program_id(1), pl.num_programs(1) - 1
        side_steps.append(lambda: fill(jnp.where(j > 0, yb_ref[...], 0.0), y_ref[...],
                                       jnp.where(j < last, ya_next_ref[...], 0.0)))
        side_steps += [functools.partial(shift, c * CONV_ROWS) for c in range(tq // CONV_ROWS)] + [shift_tail]
        side_steps += [functools.partial(taps, c * CONV_ROWS) for c in range(tq // CONV_ROWS)]
        side_steps += [functools.partial(finish, c * NORM_ROWS) for c in range(tq // NORM_ROWS)]
    done = 0

    for hd in range(4):
        for ci in range(n):
            values_chunk(hd, ci)
            if ci + 1 < n:
                scores_ahead(hd + 1, ci + 1)
            else:
                scores_ahead(hd + 2, 0)
            if ci + 2 == n:
                scores_finish((hd + 1) % 4)
            due = ((hd * n + ci + 1) * len(side_steps)) // (4 * n)
            for step in side_steps[done:due]:
                step()
            done = due
        if hd % 2 == 1:
            pr = hd // 2
            o_ref[:, pr * LANES:(pr + 1) * LANES] = _normalized_pair(acc_ref[0], acc_ref[1]).astype(BF16)


def _attention_lat(a16_lat, a16_ctx, q_off, q_width, k_off, v_off, v_width, q_blocks, v_blocks, name, conv=None):
    b, n_lat, _ = a16_lat.shape
    n_ctx = a16_ctx.shape[1]
    tq = LAT_TILE
    n_tiles = n_lat // tq
    kw = 4 * LANES
    qc, kc, vc = q_off // q_width, k_off // kw, v_off // v_width
    next_row = lambda i: jnp.minimum(i + 1, b - 1)
    wraps = lambda j: j == n_tiles - 1
    in_specs = [pl.BlockSpec((None, tq, q_width), lambda i, j: (i, j, qc)),
                pl.BlockSpec((None, tq, q_width),
                             lambda i, j: (jnp.where(wraps(j), next_row(i), i), jnp.where(wraps(j), 0, j + 1), qc)),
                pl.BlockSpec((None, n_lat, kw), lambda i, j: (i, 0, kc)),
                pl.BlockSpec((None, n_ctx, kw), lambda i, j: (i, 0, kc)),
                pl.BlockSpec((None, n_lat, kw), lambda i, j: (next_row(i), 0, kc)),
                pl.BlockSpec((None, n_ctx, kw), lambda i, j: (next_row(i), 0, kc)),
                pl.BlockSpec((None, n_lat, v_width), lambda i, j: (i, 0, vc)),
                pl.BlockSpec((None, n_ctx, v_width), lambda i, j: (i, 0, vc))]
    args = [a16_lat, a16_lat, a16_lat, a16_ctx, a16_lat, a16_ctx, a16_lat, a16_ctx]
    tile = lambda w: pl.BlockSpec((None, tq, w), lambda i, j: (i, j, 0))
    out_specs = [tile(2 * LANES)]
    out_shape = [jax.ShapeDtypeStruct((b, n_lat, 2 * LANES), BF16)]
    scratch = [pltpu.VMEM((2, tq, n_lat + n_ctx), F32), pltpu.VMEM((2, tq, LANES), F32),
               pltpu.VMEM((2, tq, LANES), F32), pltpu.VMEM((2, tq, LANES), F32)]
    if conv is not None:
        a32_lat, weights, layer = conv
        yc = A32_Y // GROUP_W
        per_tile = tq // CONV_PAD
        halo = lambda index: pl.BlockSpec((None, CONV_PAD, GROUP_W), lambda i, j: (i, index(j), yc))
        in_specs += [pl.BlockSpec((None, tq, GROUP_W), lambda i, j: (i, j, yc)),
                     halo(lambda j: jnp.maximum(j * per_tile - 1, 0)),
                     halo(lambda j: jnp.minimum((j + 1) * per_tile, n_lat // CONV_PAD - 1))]
        in_specs += [_resident(w, layer) for w in weights]
        args += [a32_lat, a32_lat, a32_lat, *weights]
        out_specs.append(tile(GROUP_W))
        out_shape.append(jax.ShapeDtypeStruct((b, n_lat, GROUP_W), BF16))
        scratch.append(pltpu.VMEM((SUBLANES, tq + 2 * CONV_PAD, GROUP_W), F32))
    return pl.pallas_call(
        functools.partial(_attn_lat_kernel, q_blocks=q_blocks, v_blocks=v_blocks, with_conv=conv is not None),
        grid=(b, n_tiles),
        in_specs=in_specs,
        out_specs=out_specs,
        out_shape=out_shape,
        scratch_shapes=scratch,
        compiler_params=_cparams(2),
        name=name,
    )(*args)


def _ctx_attention(q_ref, k_ref, v_ref, o_ref, q_blocks, v_blocks):
    for pr in range(2):
        accs = []
        for half in range(2):
            hd = 2 * pr + half
            q = q_ref[:, q_blocks[hd] * LANES:(q_blocks[hd] + 1) * LANES]
            s = _dot_nt(q, k_ref[:, hd * LANES:(hd + 1) * LANES])
            p = jnp.exp2(s - jnp.max(s, axis=-1, keepdims=True))
            accs.append(_dot(p.astype(BF16), v_ref[:, v_blocks[hd] * LANES:(v_blocks[hd] + 1) * LANES]))
        o_ref[:, pr * LANES:(pr + 1) * LANES] = _normalized_pair(*accs).astype(BF16)


def _ctx_mixers_kernel(mq_ref, mk_ref, mv_ref, gq_ref, gk_ref, gv_ref, y_ref, dw_ref, b_ref, lg_ref, lb_ref, pw_ref,
                       ob_ref, oc_ref, oa_ref, sh_ref, *, mla_blocks, gqa_blocks):
    _ctx_attention(mq_ref, mk_ref, mv_ref, ob_ref, *mla_blocks)
    _ctx_attention(gq_ref, gk_ref, gv_ref, oc_ref, *gqa_blocks)
    _conv_sequence(y_ref, dw_ref, b_ref, lg_ref, lb_ref, pw_ref, oa_ref, sh_ref)


def _ctx_mixers(a16_ctx, a32_ctx, mla, gqa, conv_weights, layer):
    b, n_ctx, _ = a16_ctx.shape
    kw = 4 * LANES

    def qkv_specs(q_off, q_width, k_off, v_off, v_width, *_):
        spec = lambda off, w: pl.BlockSpec((None, n_ctx, w), lambda i: (i, 0, off // w))
        return [spec(q_off, q_width), spec(k_off, kw), spec(v_off, v_width)]

    out = pl.BlockSpec((None, n_ctx, GROUP_W), lambda i: (i, 0, 0))
    yb, yc, ya = pl.pallas_call(
        functools.partial(_ctx_mixers_kernel, mla_blocks=mla[-2:], gqa_blocks=gqa[-2:]),
        grid=(b,),
        in_specs=qkv_specs(*mla) + qkv_specs(*gqa)
        + [pl.BlockSpec((None, n_ctx, GROUP_W), lambda i: (i, 0, A32_Y // GROUP_W))]
        + [_resident(w, layer) for w in conv_weights],
        out_specs=[out] * 3,
        out_shape=[jax.ShapeDtypeStruct((b, n_ctx, GROUP_W), BF16)] * 3,
        scratch_shapes=[pltpu.VMEM((SUBLANES, n_ctx + 2 * CONV_PAD, GROUP_W), F32)],
        compiler_params=_cparams(1),
        name="ctx_mixers",
    )(*([a16_ctx] * 6), a32_ctx, *conv_weights)
    return ya, yb, yc


def _log_sigmoid(x):
    return jnp.minimum(x, 0.0) - jnp.log(1.0 + jnp.exp(-jnp.abs(x)))


def _ret_kernel(ql_ref, kl_ref, vl_ref, gfl_ref, gbl_ref, qc_ref, kc_ref, vc_ref, gfc_ref, gbc_ref,
                dq_ref, dv_ref, dc_ref, ng_ref, ol_ref, oc_ref, acc_ref):
    c = RET_CHUNK
    n_lat, n_ctx = ql_ref.shape[0], qc_ref.shape[0]
    qk_w = RET_HEADS * RET_QK
    v_w = RET_HEADS * RET_V

    row_i = lax.broadcasted_iota(jnp.int32, (c, 1), 0).astype(F32)
    qk_head = lax.broadcasted_iota(jnp.int32, (1, qk_w), 1) // RET_QK
    v_head = lax.broadcasted_iota(jnp.int32, (1, v_w), 1) // RET_V
    qk_masks = [qk_head == hd for hd in range(RET_HEADS)]
    v_masks = [v_head == hd for hd in range(RET_HEADS)]
    state_rows = lax.broadcasted_iota(jnp.int32, (qk_w, 1), 0) // RET_QK
    bd_mask = (state_rows == v_head).astype(F32)
    ones_rows = lax.broadcasted_iota(jnp.int32, (v_w, 1), 0) // RET_V
    head_mean = jnp.where(ones_rows == v_head, 1.0 / RET_V, 0.0).astype(BF16)
    ci = lax.broadcasted_iota(jnp.int32, (c, RET_HEADS * c), 0).astype(F32)
    cj = (lax.broadcasted_iota(jnp.int32, (c, RET_HEADS * c), 1) % c).astype(F32)

    def make_step(d):
        lg_q = _log_sigmoid(dq_ref[d:d + 1, :])
        lg_v = _log_sigmoid(dv_ref[d:d + 1, :])
        lg_c = _log_sigmoid(dc_ref[d:d + 1, :])
        if d == 0:
            diff = ci - cj
            q_dec = jnp.exp(lg_q * (row_i + 1.0))
            k_dec = jnp.exp(lg_q * (c - 1.0 - row_i))
        else:
            diff = cj - ci
            q_dec = jnp.exp(lg_q * (c - row_i))
            k_dec = jnp.exp(lg_q * row_i)
        decay = jnp.where(diff >= 0.0, jnp.exp(lg_c * jnp.maximum(diff, 0.0)), 0.0)
        chunk_dec = jnp.exp(lg_v * float(c))

        def step(refs, r0, acc_r0, state):
            q_ref, k_ref, v_ref = refs
            q = q_ref[pl.ds(r0, c), :]
            k = k_ref[pl.ds(r0, c), :]
            v = v_ref[pl.ds(r0, c), :]
            kb, vb = k.astype(BF16), v.astype(BF16)
            k_bd = jnp.concatenate([jnp.where(m, kb, 0) for m in qk_masks], axis=0)
            scores = _dot_nt(q.astype(BF16), k_bd) * decay
            v_bd = jnp.concatenate([jnp.where(m, vb, 0) for m in v_masks], axis=0)
            inner = _dot(scores.astype(BF16), v_bd)
            cross = _dot((q * q_dec).astype(BF16), state.astype(BF16))
            kv = _dot_tn((k * k_dec).astype(BF16), vb)
            new_state = state * chunk_dec + kv * bd_mask
            acc_ref[d, pl.ds(acc_r0, c), :] = inner + cross
            return new_state

        return step

    fwd, bwd = make_step(0), make_step(1)

    def scan(n, acc_base, fwd_refs, bwd_refs, states):
        def body(i, st):
            rf = pl.multiple_of(i * c, c)
            rb = pl.multiple_of((n - 1 - i) * c, c)
            return fwd(fwd_refs, rf, acc_base + rf, st[0]), bwd(bwd_refs, rb, acc_base + rb, st[1])
        return lax.fori_loop(0, n, body, states, unroll=min(n, 8))

    zero = jnp.zeros((qk_w, v_w), F32)
    states = scan(n_ctx // c, n_lat, (qc_ref, kc_ref, vc_ref), (qc_ref, kc_ref, vc_ref), (zero, zero))
    scan(n_lat // c, 0, (ql_ref, kl_ref, vl_ref), (ql_ref, kl_ref, vl_ref), states)

    def readout(o_ref, gate_refs, acc_base):
        rows = NORM_ROWS

        def body(i, carry):
            r0 = pl.multiple_of(i * rows, rows)
            y = None
            for d in range(2):
                o = acc_ref[d, pl.ds(acc_base + r0, rows), :]
                o2 = o * o
                o2_hi = o2.astype(BF16)
                o2_lo = (o2 - o2_hi.astype(F32)).astype(BF16)
                ms = _dot(o2_hi, head_mean) + _dot(o2_lo, head_mean)
                yd = o * lax.rsqrt(ms + EPS) * ng_ref[d:d + 1, :] * gate_refs[d][pl.ds(r0, rows), :]
                y = yd if y is None else y + yd
            o_ref[pl.ds(r0, rows), :] = y.astype(BF16)
            return carry

        n = o_ref.shape[0] // rows
        lax.fori_loop(0, n, body, 0, unroll=min(n, 4))

    readout(ol_ref, (gfl_ref, gbl_ref), 0)
    readout(oc_ref, (gfc_ref, gbc_ref), n_lat)


def _retention(a32_lat, a32_ctx, smalls, layer):
    b, n_lat, _ = a32_lat.shape
    n_ctx = a32_ctx.shape[1]

    def seq_specs(n):
        spec = lambda off, w: pl.BlockSpec((None, n, w), lambda i: (i, 0, off // w))
        return [spec(A32_RQ, 128), spec(A32_RK, 128), spec(A32_RV, 256), spec(A32_GF, 256), spec(A32_GB, 256)]

    out = lambda n: pl.BlockSpec((None, n, 256), lambda i: (i, 0, 0))
    return pl.pallas_call(
        _ret_kernel,
        grid=(b,),
        in_specs=seq_specs(n_lat) + seq_specs(n_ctx) + [_resident(a, layer) for a in smalls],
        out_specs=[out(n_lat), out(n_ctx)],
        out_shape=[jax.ShapeDtypeStruct((b, n_lat, 256), BF16), jax.ShapeDtypeStruct((b, n_ctx, 256), BF16)],
        scratch_shapes=[pltpu.VMEM((2, n_lat + n_ctx, 256), F32)],
        compiler_params=_cparams(1),
        name="retention",
    )(*([a32_lat] * 5), *([a32_ctx] * 5), *smalls)


def _out_mlp_kernel(x_ref, ya_ref, yb_ref, yc_ref, yd_ref, mod_ref, wo_ref, w1_ref, w2_ref, *rest, ff_chunk):
    o_ref = rest[-1]
    x = x_ref[...]
    gw = GROUP_W
    y = (_dot(ya_ref[...], wo_ref[0 * gw:1 * gw, :]) + _dot(yb_ref[...], wo_ref[1 * gw:2 * gw, :])
         + _dot(yc_ref[...], wo_ref[2 * gw:3 * gw, :]) + _dot(yd_ref[...], wo_ref[3 * gw:4 * gw, :]))
    x1 = x + mod_ref[2:3, :] * y
    h = (_rms(x1) * (1.0 + mod_ref[4:5, :]) + mod_ref[3:4, :]).astype(BF16)
    acc = jnp.zeros_like(x1)
    for j in range(w1_ref.shape[1] // ff_chunk):
        a = jnp.maximum(_dot(h, w1_ref[:, j * ff_chunk:(j + 1) * ff_chunk]), 0.0)
        acc = acc + _dot((a * a).astype(BF16), w2_ref[j * ff_chunk:(j + 1) * ff_chunk, :])
    x2 = x1 + mod_ref[5:6, :] * acc
    o_ref[...] = x2 if len(rest) == 1 else _rms(x2) * rest[0][...]


def _out_mlp(xseq, ys, mods, weights, layer, tm, in_place, final_gain=None):
    b, n, d = xseq.shape
    per_batch = mods.shape[0] == b
    tok = lambda w: pl.BlockSpec((None, tm, w), lambda i, j: (i, j, 0))
    in_specs = [tok(d)] + [tok(GROUP_W)] * 4
    in_specs += [pl.BlockSpec((None, N_MOD, d), lambda i, j: (i if per_batch else 0, 0, 0))]
    in_specs += [_resident(w, layer) for w in weights]
    args = [xseq, *ys, mods, *weights]
    if final_gain is not None:
        in_specs.append(_resident(final_gain))
        args.append(final_gain)
    return pl.pallas_call(
        functools.partial(_out_mlp_kernel, ff_chunk=1024),
        grid=(b, n // tm),
        in_specs=in_specs,
        out_specs=tok(d),
        out_shape=jax.ShapeDtypeStruct((b, n, d), F32),
        input_output_aliases={0: 0} if in_place else {},
        compiler_params=_cparams(2),
        name="out_mlp",
    )(*args)


def _rope_tables(n_lat):
    n = np.arange(n_lat)
    row, col = (n // GRID_W).astype(np.float64), (n % GRID_W).astype(np.float64)

    def table(groups):
        cos = np.ones((n_lat, LANES)); up = np.zeros((n_lat, LANES)); dn = np.zeros((n_lat, LANES))
        for rot_start, rot_dim in groups:
            half = rot_dim // 2
            q = half // 2
            freqs = ROPE_BASE ** (-np.arange(q, dtype=np.float64) / q)
            for axis, pos in enumerate((row, col)):
                ang = pos[:, None] * freqs[None, :]
                base = rot_start + axis * half
                cos[:, base:base + q] = np.cos(ang)
                cos[:, base + q:base + 2 * q] = np.cos(ang)
                up[:, base:base + q] = -np.sin(ang)
                dn[:, base + q:base + 2 * q] = np.sin(ang)
        return jnp.asarray(np.stack([cos, up, dn]).astype(np.float32))

    return table([(0, GQA_HEAD_DIM), (GQA_HEAD_DIM, GQA_HEAD_DIM)]), table([(0, MLA_ROPE)])


def _layout_w_in(w_in):
    split = OFF_KPE + MLA_ROPE
    w_a = jnp.pad(w_in[..., :split].astype(BF16), ((0, 0), (0, 0), (0, N_IN_A - split)))
    w_b = w_in[..., split:].astype(BF16)
    assert w_b.shape[-1] == N_IN_B
    return w_a, w_b


def _layout_w_uq(w_uq):
    lead = w_uq.shape[:-1]
    w = w_uq.reshape(lead + (MLA_HEADS, MLA_NOPE + MLA_ROPE))
    w = jnp.concatenate([w[..., MLA_NOPE:], w[..., :MLA_NOPE]], axis=-1)
    w = jnp.pad(w, ((0, 0),) * (w.ndim - 1) + ((0, LANES - MLA_NOPE - MLA_ROPE),))
    return w.reshape(lead + (MLA_HEADS * LANES,)).astype(BF16)


def _layout_w_ukv(w_ukv):
    lead = w_ukv.shape[:-1]
    w = w_ukv.reshape(lead + (MLA_HEADS, MLA_NOPE + MLA_V))
    pad = lambda a, before: jnp.pad(a, ((0, 0),) * (a.ndim - 1) + ((before, LANES - before - a.shape[-1]),)).reshape(
        lead + (MLA_HEADS * LANES,))
    return jnp.concatenate([pad(w[..., :MLA_NOPE], MLA_ROPE), pad(w[..., MLA_NOPE:], 0)], axis=-1).astype(BF16)


def kernel(x, c, ctx, c_ctx, w_mod, b_mod, w_in, w_out, conv_dw, conv_b, conv_ln_g, conv_ln_b, conv_pw, mla_q_g, mla_kv_g, mla_uq, mla_ukv, gqa_q_g, gqa_k_g, ret_decay, ret_norm_g, mlp_w1, mlp_w2, final_g):
    b, n_lat, d = x.shape
    n_ctx = ctx.shape[1]
    depth = w_mod.shape[0]
    assert n_lat % LAT_TILE == 0 and n_lat % PROJ_TILE == 0 and n_lat % KEY_CHUNK == 0 and n_ctx % KEY_CHUNK == 0
    assert n_ctx % NORM_ROWS == 0 and n_lat % NORM_ROWS == 0 and NORM_ROWS % (2 * RET_CHUNK) == 0

    rows = ((b + 1 + SUBLANES - 1) // SUBLANES) * SUBLANES
    cond = jnp.zeros((rows, d), F32).at[:b].set(c).at[b].set(c_ctx)
    mod_all = _adaln(cond, w_mod, b_mod).reshape(depth, rows, N_MOD, d)
    tables = _rope_tables(n_lat)
    mla = (A16_MQ, 4 * LANES, A16_MK, A16_MV, 4 * LANES, (0, 1, 2, 3), (0, 1, 2, 3))
    gqa = (A16_GQ, 2 * LANES, A16_GK, A16_GV, 2 * LANES, (0, 0, 1, 1), (0, 0, 1, 1))

    row = lambda a: a[:, None, :]
    w_proj = (*_layout_w_in(w_in), _layout_w_uq(mla_uq), _layout_w_ukv(mla_ukv), row(mla_q_g), row(mla_kv_g),
              row(jnp.tile(gqa_q_g, (1, 2))), row(jnp.tile(gqa_k_g, (1, 2))))
    w_conv = (jnp.broadcast_to(conv_dw[:, :, None, :], conv_dw.shape[:2] + (SUBLANES, GROUP_W)), row(conv_b), row(conv_ln_g), row(conv_ln_b), conv_pw.astype(BF16))
    dec = ret_decay.astype(F32)
    w_ret = (jnp.repeat(dec, RET_QK, axis=2), jnp.repeat(dec, RET_V, axis=2), jnp.repeat(dec, RET_CHUNK, axis=2),
             ret_norm_g.reshape(depth, 2, RET_HEADS * RET_V))
    w_mlp = (w_out.astype(BF16), mlp_w1.astype(BF16), mlp_w2.astype(BF16))

    cx = ctx
    for layer in range(depth):
        last = layer == depth - 1
        mods_lat, mods_ctx = mod_all[layer, :b], mod_all[layer, b:b + 1]
        a16_lat, a32_lat = _in_proj(x, mods_lat, w_proj, layer, tables, PROJ_TILE)
        a16_ctx, a32_ctx = _in_proj(cx, mods_ctx, w_proj, layer, None, n_ctx)
        yb_lat, = _attention_lat(a16_lat, a16_ctx, *mla, "mla_attn")
        yc_lat, ya_lat = _attention_lat(a16_lat, a16_ctx, *gqa, "gqa_attn", conv=(a32_lat, w_conv, layer))
        yd_lat, yd_ctx = _retention(a32_lat, a32_ctx, w_ret, layer)
        x = _out_mlp(x, (ya_lat, yb_lat, yc_lat, yd_lat), mods_lat, w_mlp, layer, LAT_TILE, layer > 0,
                     final_gain=final_g[None, :] if last else None)
        if not last:
            ya_ctx, yb_ctx, yc_ctx = _ctx_mixers(a16_ctx, a32_ctx, mla, gqa, w_conv, layer)
            cx = _out_mlp(cx, (ya_ctx, yb_ctx, yc_ctx, yd_ctx), mods_ctx, w_mlp, layer, n_ctx, layer > 0)
    return x
```

```python
import functools

import numpy as np
import jax
import jax.numpy as jnp
from jax import lax
from jax.experimental import pallas as pl
from jax.experimental.pallas import tpu as pltpu

F32 = jnp.float32
BF16 = jnp.bfloat16

GRID_W = 64
N_MOD = 6
EPS = 1e-6
ROPE_BASE = 10000.0
GROUP_W = 256
CONV_WIDTH = 31
CONV_PAD = 16
MLA_HEADS = 4
MLA_NOPE = 64
MLA_ROPE = 32
MLA_V = 64
MLA_Q_RANK = 256
MLA_KV_RANK = 128
GQA_HEADS = 4
GQA_KV_HEADS = 2
GQA_HEAD_DIM = 64
RET_HEADS = 4
RET_QK = 32
RET_V = 64
RET_CHUNK = 128
LANES = 128
SUBLANES = 8
LOG2_E = 1.4426950408889634
LAT_TILE = 512
PROJ_TILE = 1024
KEY_CHUNK = 256
NORM_ROWS = 256
VMEM_LIMIT = 56 * 1024 * 1024

OFF_A, OFF_CQ, OFF_CKV, OFF_KPE, N_IN_A = 0, 512, 768, 896, 1024
OFF_GQ, OFF_GK, OFF_RQ, OFF_RV, OFF_GF, OFF_GB, N_IN_B = 0, 256, 512, 768, 1024, 1280, 1536
A16_MQ, A16_MK, A16_MV, A16_GK, A16_GQ, A16_GV, A16_W = 0, 512, 1024, 1536, 2048, 2304, 2560
A32_Y, A32_RQ, A32_RK, A32_RV, A32_GF, A32_GB, A32_W = 0, 256, 384, 512, 768, 1024, 1280


def _cparams(n_axes):
    return pltpu.CompilerParams(dimension_semantics=("arbitrary",) * n_axes,
                                vmem_limit_bytes=VMEM_LIMIT)


def _resident(a, layer=None):
    if layer is None:
        return pl.BlockSpec(a.shape, lambda *_: (0,) * a.ndim, pipeline_mode=pl.Buffered(1))
    return pl.BlockSpec((None,) + a.shape[1:], lambda *_: (layer,) + (0,) * (a.ndim - 1),
                        pipeline_mode=pl.Buffered(1))


def _rms(x):
    return x * lax.rsqrt(jnp.mean(x * x, axis=-1, keepdims=True) + EPS)


def _sigmoid(x):
    return 1.0 / (1.0 + jnp.exp(-x))


def _silu(x):
    return x * _sigmoid(x)


def _dot(a, b):
    return jnp.dot(a, b, preferred_element_type=F32)


def _dot_nt(a, b):
    return lax.dot_general(a, b, (((1,), (1,)), ((), ())), preferred_element_type=F32)


def _dot_tn(a, b):
    return lax.dot_general(a, b, (((0,), (0,)), ((), ())), preferred_element_type=F32)


def _rope(x, cos, sin_up, sin_dn, shift):
    n = x.shape[-1]
    return x * cos + pltpu.roll(x, n - shift, 1) * sin_up + pltpu.roll(x, shift, 1) * sin_dn


def _adaln_kernel(cond_ref, w_ref, b_ref, o_ref):
    cond = cond_ref[...]
    o_ref[...] = _dot(_silu(cond).astype(BF16), w_ref[...].astype(BF16)) + b_ref[...]


def _adaln(cond, w_mod, b_mod):
    depth, d, n = w_mod.shape
    r = cond.shape[0]
    tn = 1536
    return pl.pallas_call(
        _adaln_kernel,
        grid=(depth, n // tn),
        in_specs=[pl.BlockSpec((r, d), lambda l, j: (0, 0)),
                  pl.BlockSpec((None, d, tn), lambda l, j: (l, 0, j)),
                  pl.BlockSpec((None, 1, tn), lambda l, j: (l, 0, j))],
        out_specs=pl.BlockSpec((None, r, tn), lambda l, j: (l, 0, j)),
        out_shape=jax.ShapeDtypeStruct((depth, r, n), F32),
        compiler_params=_cparams(2),
        name="adaln",
    )(cond, w_mod, b_mod.reshape(depth, 1, n))


def _in_proj_kernel(x_ref, mod_ref, w_a_ref, w_b_ref, w_uq_ref, w_ukv_ref, g_mq_ref, g_mkv_ref, g_gq_ref, g_gk_ref,
                    *rest, rotary, sub_tiles):
    a16_ref, a32_ref = rest[-2:]
    lo = lax.broadcasted_iota(jnp.int32, (1, LANES), 1) < (LANES // 2)
    q_scale = (MLA_NOPE + MLA_ROPE) ** -0.5 * LOG2_E
    n_rows = x_ref.shape[0] // sub_tiles

    for st in range(sub_tiles):
        rows = slice(st * n_rows, (st + 1) * n_rows)
        h = (_rms(x_ref[rows, :]) * (1.0 + mod_ref[1:2, :]) + mod_ref[0:1, :]).astype(BF16)

        def proj(w_ref, off, width, h=h):
            return _dot(h, w_ref[:, off:off + width])

        def put16(off, val, rows=rows):
            a16_ref[rows, off:off + val.shape[1]] = val.astype(BF16)

        def put32(off, val, rows=rows):
            a32_ref[rows, off:off + val.shape[1]] = val

        if rotary:
            tg_ref, tm_ref = rest[:2]
            pos = pl.ds(pl.multiple_of(pl.program_id(1) * x_ref.shape[0] + st * n_rows, n_rows), n_rows)
            rope_g = lambda v, pos=pos: _rope(v, tg_ref[0, pos, :], tg_ref[1, pos, :], tg_ref[2, pos, :],
                                              GQA_HEAD_DIM // 4)
            rope_m = lambda v, pos=pos: _rope(v, tm_ref[0, pos, :], tm_ref[1, pos, :], tm_ref[2, pos, :],
                                              MLA_ROPE // 4)
        else:
            rope_g = rope_m = lambda v: v

        def pair_norm_rope(xp, gain, rope_g=rope_g):
            x2 = xp * xp
            s_lo = jnp.sum(jnp.where(lo, x2, 0.0), axis=-1, keepdims=True)
            s_hi = jnp.sum(jnp.where(lo, 0.0, x2), axis=-1, keepdims=True)
            r = jnp.where(lo, lax.rsqrt(s_lo * (1.0 / GQA_HEAD_DIM) + EPS),
                          lax.rsqrt(s_hi * (1.0 / GQA_HEAD_DIM) + EPS))
            return rope_g(xp * r * gain)

        cq = (_rms(proj(w_a_ref, OFF_CQ, MLA_Q_RANK)) * g_mq_ref[...]).astype(BF16)
        ckv_kpe = proj(w_a_ref, OFF_CKV, 2 * LANES)
        ckv = (_rms(ckv_kpe[:, :LANES]) * g_mkv_ref[...]).astype(BF16)
        gq = proj(w_b_ref, OFF_GQ, GQA_HEADS * GQA_HEAD_DIM)
        gkv = proj(w_b_ref, OFF_GK, 2 * LANES)

        a = proj(w_a_ref, OFF_A, 2 * GROUP_W)
        put32(A32_Y, a[:, :GROUP_W] * _sigmoid(a[:, GROUP_W:]))
        rqk = proj(w_b_ref, OFF_RQ, 2 * LANES)
        put32(A32_RQ, rqk[:, :LANES])
        put32(A32_RK, rqk[:, LANES:] * RET_QK ** -0.5)
        put32(A32_RV, proj(w_b_ref, OFF_RV, RET_HEADS * RET_V))
        put32(A32_GF, _silu(proj(w_b_ref, OFF_GF, GROUP_W)))
        put32(A32_GB, _silu(proj(w_b_ref, OFF_GB, GROUP_W)))

        q = _dot(cq, w_uq_ref[...])
        for hd in range(MLA_HEADS):
            put16(A16_MQ + hd * LANES, rope_m(q[:, hd * LANES:(hd + 1) * LANES]) * q_scale)
        kv = _dot(ckv, w_ukv_ref[...])
        kpe = rope_m(ckv_kpe[:, LANES:])
        for hd in range(MLA_HEADS):
            put16(A16_MK + hd * LANES, kv[:, hd * LANES:(hd + 1) * LANES] + kpe)
            put16(A16_MV + hd * LANES,
                  jnp.where(lo, kv[:, (MLA_HEADS + hd) * LANES:(MLA_HEADS + hd + 1) * LANES], 1.0))

        for pr in range(2):
            qp = pair_norm_rope(gq[:, pr * LANES:(pr + 1) * LANES], g_gq_ref[...])
            put16(A16_GQ + pr * LANES, qp * (GQA_HEAD_DIM ** -0.5 * LOG2_E))
        kp = pair_norm_rope(gkv[:, :LANES], g_gk_ref[...])
        kp_sw = pltpu.roll(kp, LANES // 2, 1)
        put16(A16_GK + 0 * LANES, jnp.where(lo, kp, 0.0))
        put16(A16_GK + 1 * LANES, jnp.where(lo, 0.0, kp_sw))
        put16(A16_GK + 2 * LANES, jnp.where(lo, kp_sw, 0.0))
        put16(A16_GK + 3 * LANES, jnp.where(lo, 0.0, kp))
        vp = gkv[:, LANES:]
        put16(A16_GV + 0 * LANES, jnp.where(lo, vp, 1.0))
        put16(A16_GV + 1 * LANES, jnp.where(lo, pltpu.roll(vp, LANES // 2, 1), 1.0))


def _in_proj(xseq, mods, weights, layer, tables, tm):
    b, n, d = xseq.shape
    per_batch = mods.shape[0] == b
    tok = lambda w: pl.BlockSpec((None, tm, w), lambda i, j: (i, j, 0))
    in_specs = [tok(d), pl.BlockSpec((None, N_MOD, d), lambda i, j: (i if per_batch else 0, 0, 0))]
    in_specs += [_resident(w, layer) for w in weights]
    args = [xseq, mods, *weights]
    if tables is not None:
        in_specs += [_resident(t) for t in tables]
        args += list(tables)
    return pl.pallas_call(
        functools.partial(_in_proj_kernel, rotary=tables is not None, sub_tiles=tm // 256),
        grid=(b, n // tm),
        in_specs=in_specs,
        out_specs=[tok(A16_W), tok(A32_W)],
        out_shape=[jax.ShapeDtypeStruct((b, n, A16_W), BF16), jax.ShapeDtypeStruct((b, n, A32_W), F32)],
        compiler_params=_cparams(2),
        name="in_proj",
    )(*args)


CONV_ROWS = 64
CONV_FIRST = CONV_PAD - CONV_WIDTH // 2
CONV_TAIL = ((CONV_FIRST + CONV_WIDTH - 1) // SUBLANES) * SUBLANES


def _conv_steps(sh_ref, dw_ref, b_ref, lg_ref, lb_ref, pw_ref, o_ref, length):
    def fill(before, body, after):
        sh_ref[0, 0:CONV_PAD, :] = before
        sh_ref[0, CONV_PAD:CONV_PAD + length, :] = body
        sh_ref[0, CONV_PAD + length:2 * CONV_PAD + length, :] = after

    def shift(base, n=CONV_ROWS):
        win = sh_ref[0, pl.ds(base, n + SUBLANES), :]
        for r in range(1, SUBLANES):
            sh_ref[r, pl.ds(base, n), :] = win[r:r + n, :]

    def shift_tail():
        shift(length, CONV_TAIL)

    def taps(base):
        groups = (CONV_ROWS // SUBLANES, SUBLANES, GROUP_W)
        acc = jnp.zeros(groups, F32) + b_ref[...]
        for k in range(CONV_WIDTH):
            off = CONV_FIRST + k
            slab = sh_ref[off % SUBLANES, pl.ds(base + (off // SUBLANES) * SUBLANES, CONV_ROWS), :]
            acc = acc + slab.reshape(groups) * dw_ref[k]
        sh_ref[0, pl.ds(base, CONV_ROWS), :] = acc.reshape(CONV_ROWS, GROUP_W)

    def finish(base):
        acc = sh_ref[0, pl.ds(base, NORM_ROWS), :]
        mu = jnp.mean(acc, axis=-1, keepdims=True)
        cen = acc - mu
        var = jnp.mean(cen * cen, axis=-1, keepdims=True)
        z = _silu(cen * lax.rsqrt(var + EPS) * lg_ref[...] + lb_ref[...])
        o_ref[pl.ds(base, NORM_ROWS), :] = _dot(z.astype(BF16), pw_ref[...]).astype(BF16)

    return fill, shift, shift_tail, taps, finish


def _conv_sequence(y_ref, dw_ref, b_ref, lg_ref, lb_ref, pw_ref, o_ref, sh_ref):
    length = y_ref.shape[0]
    fill, shift, shift_tail, taps, finish = _conv_steps(sh_ref, dw_ref, b_ref, lg_ref, lb_ref, pw_ref, o_ref, length)
    zeros = jnp.zeros((CONV_PAD, GROUP_W), F32)
    fill(zeros, y_ref[...], zeros)

    def loop(step, rows, **kw):
        def body(c, carry):
            step(pl.multiple_of(c * rows, rows))
            return carry
        lax.fori_loop(0, length // rows, body, 0, **kw)

    loop(shift, CONV_ROWS)
    shift_tail()
    loop(taps, CONV_ROWS)
    loop(finish, NORM_ROWS, unroll=min(length // NORM_ROWS, 4))


def _normalized_pair(acc_even, acc_odd):
    half = LANES // 2
    lo = lax.broadcasted_iota(jnp.int32, (1, LANES), 1) < half
    return jnp.where(lo, acc_even / pltpu.roll(acc_even, half, 1), pltpu.roll(acc_odd, half, 1) / acc_odd)


def _attn_lat_kernel(q_ref, qn_ref, kl_ref, kc_ref, kln_ref, kcn_ref, vl_ref, vc_ref, *rest,
                     q_blocks, v_blocks, with_conv):
    if with_conv:
        y_ref, yb_ref, ya_next_ref, dw_ref, cb_ref, lg_ref, lb_ref, pw_ref, o_ref, ya_ref = rest[:10]
        s_ref, m_ref, mb_ref, acc_ref, sh_ref = rest[10:]
    else:
        o_ref, s_ref, m_ref, mb_ref, acc_ref = rest
    tq = q_ref.shape[0]
    tk = KEY_CHUNK
    last_tile = pl.program_id(1) == pl.num_programs(1) - 1
    chunks = [(kl_ref, kln_ref, vl_ref, c * tk) for c in range(kl_ref.shape[0] // tk)]
    chunks += [(kc_ref, kcn_ref, vc_ref, c * tk) for c in range(kc_ref.shape[0] // tk)]

    def scores_chunk(hd, ci, ref, next_tile=False):
        slot = hd % 2
        k_ref, kn_ref, _, r0 = chunks[ci]
        q = ref[:, q_blocks[hd] * LANES:(q_blocks[hd] + 1) * LANES]
        k = k_ref[r0:r0 + tk, hd * LANES:(hd + 1) * LANES]
        if next_tile:
            k = jnp.where(last_tile, kn_ref[r0:r0 + tk, hd * LANES:(hd + 1) * LANES], k)
        s = _dot_nt(q, k)
        s_ref[slot, :, ci * tk:(ci + 1) * tk] = s
        mx = s[:, 0:LANES]
        for i in range(1, tk // LANES):
            mx = jnp.maximum(mx, s[:, i * LANES:(i + 1) * LANES])
        m_ref[slot] = mx if ci == 0 else jnp.maximum(m_ref[slot], mx)

    def scores_finish(hd):
        slot = hd % 2
        mb_ref[slot] = jnp.broadcast_to(jnp.max(m_ref[slot], axis=-1, keepdims=True), (tq, LANES))

    def values_chunk(hd, ci):
        slot = hd % 2
        _, _, v_ref, r0 = chunks[ci]
        mb = mb_ref[slot]
        p = jnp.concatenate(
            [jnp.exp2(s_ref[slot, :, ci * tk + i * LANES:ci * tk + (i + 1) * LANES] - mb).astype(BF16)
             for i in range(tk // LANES)], axis=1)
        pv = _dot(p, v_ref[r0:r0 + tk, v_blocks[hd] * LANES:(v_blocks[hd] + 1) * LANES])
        acc_ref[slot] = pv if ci == 0 else acc_ref[slot] + pv

    n = len(chunks)

    def scores_ahead(unit, ci):
        if unit < 4:
            scores_chunk(unit, ci, q_ref)
        else:
            scores_chunk(unit - 4, ci, qn_ref, next_tile=True)

    @pl.when((pl.program_id(0) == 0) & (pl.program_id(1) == 0))
    def _():
        for ci in range(n):
            scores_chunk(0, ci, q_ref)
        scores_finish(0)
        scores_chunk(1, 0, q_ref)

    side_steps = []
    if with_conv:
        fill, shift, shift_tail, taps, finish = _conv_steps(sh_ref, dw_ref, cb_ref, lg_ref, lb_ref, pw_ref, ya_ref, tq)
        j, last = pl.program_id(1), pl.num_programs(1) - 1
        side_steps.append(lambda: fill(jnp.where(j > 0, yb_ref[...], 0.0), y_ref[...],
                                       jnp.where(j < last, ya_next_ref[...], 0.0)))
        side_steps += [functools.partial(shift, c * CONV_ROWS) for c in range(tq // CONV_ROWS)] + [shift_tail]
        side_steps += [functools.partial(taps, c * CONV_ROWS) for c in range(tq // CONV_ROWS)]
        side_steps += [functools.partial(finish, c * NORM_ROWS) for c in range(tq // NORM_ROWS)]
    done = 0

    for hd in range(4):
        for ci in range(n):
            values_chunk(hd, ci)
            if ci + 1 < n:
                scores_ahead(hd + 1, ci + 1)
            else:
                scores_ahead(hd + 2, 0)
            if ci + 2 == n:
                scores_finish((hd + 1) % 4)
            due = ((hd * n + ci + 1) * len(side_steps)) // (4 * n)
            for step in side_steps[done:due]:
                step()
            done = due
        if hd % 2 == 1:
            pr = hd // 2
            o_ref[:, pr * LANES:(pr + 1) * LANES] = _normalized_pair(acc_ref[0], acc_ref[1]).astype(BF16)


def _attention_lat(a16_lat, a16_ctx, q_off, q_width, k_off, v_off, v_width, q_blocks, v_blocks, name, conv=None):
    b, n_lat, _ = a16_lat.shape
    n_ctx = a16_ctx.shape[1]
    tq = LAT_TILE
    n_tiles = n_lat // tq
    kw = 4 * LANES
    qc, kc, vc = q_off // q_width, k_off // kw, v_off // v_width
    next_row = lambda i: jnp.minimum(i + 1, b - 1)
    wraps = lambda j: j == n_tiles - 1
    in_specs = [pl.BlockSpec((None, tq, q_width), lambda i, j: (i, j, qc)),
                pl.BlockSpec((None, tq, q_width),
                             lambda i, j: (jnp.where(wraps(j), next_row(i), i), jnp.where(wraps(j), 0, j + 1), qc)),
                pl.BlockSpec((None, n_lat, kw), lambda i, j: (i, 0, kc)),
                pl.BlockSpec((None, n_ctx, kw), lambda i, j: (i, 0, kc)),
                pl.BlockSpec((None, n_lat, kw), lambda i, j: (next_row(i), 0, kc)),
                pl.BlockSpec((None, n_ctx, kw), lambda i, j: (next_row(i), 0, kc)),
                pl.BlockSpec((None, n_lat, v_width), lambda i, j: (i, 0, vc)),
                pl.BlockSpec((None, n_ctx, v_width), lambda i, j: (i, 0, vc))]
    args = [a16_lat, a16_lat, a16_lat, a16_ctx, a16_lat, a16_ctx, a16_lat, a16_ctx]
    tile = lambda w: pl.BlockSpec((None, tq, w), lambda i, j: (i, j, 0))
    out_specs = [tile(2 * LANES)]
    out_shape = [jax.ShapeDtypeStruct((b, n_lat, 2 * LANES), BF16)]
    scratch = [pltpu.VMEM((2, tq, n_lat + n_ctx), F32), pltpu.VMEM((2, tq, LANES), F32),
               pltpu.VMEM((2, tq, LANES), F32), pltpu.VMEM((2, tq, LANES), F32)]
    if conv is not None:
        a32_lat, weights, layer = conv
        yc = A32_Y // GROUP_W
        per_tile = tq // CONV_PAD
        halo = lambda index: pl.BlockSpec((None, CONV_PAD, GROUP_W), lambda i, j: (i, index(j), yc))
        in_specs += [pl.BlockSpec((None, tq, GROUP_W), lambda i, j: (i, j, yc)),
                     halo(lambda j: jnp.maximum(j * per_tile - 1, 0)),
                     halo(lambda j: jnp.minimum((j + 1) * per_tile, n_lat // CONV_PAD - 1))]
        in_specs += [_resident(w, layer) for w in weights]
        args += [a32_lat, a32_lat, a32_lat, *weights]
        out_specs.append(tile(GROUP_W))
        out_shape.append(jax.ShapeDtypeStruct((b, n_lat, GROUP_W), BF16))
        scratch.append(pltpu.VMEM((SUBLANES, tq + 2 * CONV_PAD, GROUP_W), F32))
    return pl.pallas_call(
        functools.partial(_attn_lat_kernel, q_blocks=q_blocks, v_blocks=v_blocks, with_conv=conv is not None),
        grid=(b, n_tiles),
        in_specs=in_specs,
        out_specs=out_specs,
        out_shape=out_shape,
        scratch_shapes=scratch,
        compiler_params=_cparams(2),
        name=name,
    )(*args)


def _ctx_attention(q_ref, k_ref, v_ref, o_ref, q_blocks, v_blocks):
    for pr in range(2):
        accs = []
        for half in range(2):
            hd = 2 * pr + half
            q = q_ref[:, q_blocks[hd] * LANES:(q_blocks[hd] + 1) * LANES]
            s = _dot_nt(q, k_ref[:, hd * LANES:(hd + 1) * LANES])
            p = jnp.exp2(s - jnp.max(s, axis=-1, keepdims=True))
            accs.append(_dot(p.astype(BF16), v_ref[:, v_blocks[hd] * LANES:(v_blocks[hd] + 1) * LANES]))
        o_ref[:, pr * LANES:(pr + 1) * LANES] = _normalized_pair(*accs).astype(BF16)


def _ctx_mixers_kernel(mq_ref, mk_ref, mv_ref, gq_ref, gk_ref, gv_ref, y_ref, dw_ref, b_ref, lg_ref, lb_ref, pw_ref,
                       ob_ref, oc_ref, oa_ref, sh_ref, *, mla_blocks, gqa_blocks):
    _ctx_attention(mq_ref, mk_ref, mv_ref, ob_ref, *mla_blocks)
    _ctx_attention(gq_ref, gk_ref, gv_ref, oc_ref, *gqa_blocks)
    _conv_sequence(y_ref, dw_ref, b_ref, lg_ref, lb_ref, pw_ref, oa_ref, sh_ref)


def _ctx_mixers(a16_ctx, a32_ctx, mla, gqa, conv_weights, layer):
    b, n_ctx, _ = a16_ctx.shape
    kw = 4 * LANES

    def qkv_specs(q_off, q_width, k_off, v_off, v_width, *_):
        spec = lambda off, w: pl.BlockSpec((None, n_ctx, w), lambda i: (i, 0, off // w))
        return [spec(q_off, q_width), spec(k_off, kw), spec(v_off, v_width)]

    out = pl.BlockSpec((None, n_ctx, GROUP_W), lambda i: (i, 0, 0))
    yb, yc, ya = pl.pallas_call(
        functools.partial(_ctx_mixers_kernel, mla_blocks=mla[-2:], gqa_blocks=gqa[-2:]),
        grid=(b,),
        in_specs=qkv_specs(*mla) + qkv_specs(*gqa)
        + [pl.BlockSpec((None, n_ctx, GROUP_W), lambda i: (i, 0, A32_Y // GROUP_W))]
        + [_resident(w, layer) for w in conv_weights],
        out_specs=[out] * 3,
        out_shape=[jax.ShapeDtypeStruct((b, n_ctx, GROUP_W), BF16)] * 3,
        scratch_shapes=[pltpu.VMEM((SUBLANES, n_ctx + 2 * CONV_PAD, GROUP_W), F32)],
        compiler_params=_cparams(1),
        name="ctx_mixers",
    )(*([a16_ctx] * 6), a32_ctx, *conv_weights)
    return ya, yb, yc


def _log_sigmoid(x):
    return jnp.minimum(x, 0.0) - jnp.log(1.0 + jnp.exp(-jnp.abs(x)))


def _ret_kernel(ql_ref, kl_ref, vl_ref, gfl_ref, gbl_ref, qc_ref, kc_ref, vc_ref, gfc_ref, gbc_ref,
                dq_ref, dv_ref, dc_ref, ng_ref, ol_ref, oc_ref, acc_ref):
    c = RET_CHUNK
    n_lat, n_ctx = ql_ref.shape[0], qc_ref.shape[0]
    qk_w = RET_HEADS * RET_QK
    v_w = RET_HEADS * RET_V

    row_i = lax.broadcasted_iota(jnp.int32, (c, 1), 0).astype(F32)
    qk_head = lax.broadcasted_iota(jnp.int32, (1, qk_w), 1) // RET_QK
    v_head = lax.broadcasted_iota(jnp.int32, (1, v_w), 1) // RET_V
    qk_masks = [qk_head == hd for hd in range(RET_HEADS)]
    v_masks = [v_head == hd for hd in range(RET_HEADS)]
    state_rows = lax.broadcasted_iota(jnp.int32, (qk_w, 1), 0) // RET_QK
    bd_mask = (state_rows == v_head).astype(F32)
    ones_rows = lax.broadcasted_iota(jnp.int32, (v_w, 1), 0) // RET_V
    head_mean = jnp.where(ones_rows == v_head, 1.0 / RET_V, 0.0).astype(BF16)
    ci = lax.broadcasted_iota(jnp.int32, (c, RET_HEADS * c), 0).astype(F32)
    cj = (lax.broadcasted_iota(jnp.int32, (c, RET_HEADS * c), 1) % c).astype(F32)

    def make_step(d):
        lg_q = _log_sigmoid(dq_ref[d:d + 1, :])
        lg_v = _log_sigmoid(dv_ref[d:d + 1, :])
        lg_c = _log_sigmoid(dc_ref[d:d + 1, :])
        if d == 0:
            diff = ci - cj
            q_dec = jnp.exp(lg_q * (row_i + 1.0))
            k_dec = jnp.exp(lg_q * (c - 1.0 - row_i))
        else:
            diff = cj - ci
            q_dec = jnp.exp(lg_q * (c - row_i))
            k_dec = jnp.exp(lg_q * row_i)
        decay = jnp.where(diff >= 0.0, jnp.exp(lg_c * jnp.maximum(diff, 0.0)), 0.0)
        chunk_dec = jnp.exp(lg_v * float(c))

        def step(refs, r0, acc_r0, state):
            q_ref, k_ref, v_ref = refs
            q = q_ref[pl.ds(r0, c), :]
            k = k_ref[pl.ds(r0, c), :]
            v = v_ref[pl.ds(r0, c), :]
            kb, vb = k.astype(BF16), v.astype(BF16)
            k_bd = jnp.concatenate([jnp.where(m, kb, 0) for m in qk_masks], axis=0)
            scores = _dot_nt(q.astype(BF16), k_bd) * decay
            v_bd = jnp.concatenate([jnp.where(m, vb, 0) for m in v_masks], axis=0)
            inner = _dot(scores.astype(BF16), v_bd)
            cross = _dot((q * q_dec).astype(BF16), state.astype(BF16))
            kv = _dot_tn((k * k_dec).astype(BF16), vb)
            new_state = state * chunk_dec + kv * bd_mask
            acc_ref[d, pl.ds(acc_r0, c), :] = inner + cross
            return new_state

        return step

    fwd, bwd = make_step(0), make_step(1)

    def scan(n, acc_base, fwd_refs, bwd_refs, states):
        def body(i, st):
            rf = pl.multiple_of(i * c, c)
            rb = pl.multiple_of((n - 1 - i) * c, c)
            return fwd(fwd_refs, rf, acc_base + rf, st[0]), bwd(bwd_refs, rb, acc_base + rb, st[1])
        return lax.fori_loop(0, n, body, states, unroll=min(n, 8))

    zero = jnp.zeros((qk_w, v_w), F32)
    states = scan(n_ctx // c, n_lat, (qc_ref, kc_ref, vc_ref), (qc_ref, kc_ref, vc_ref), (zero, zero))
    scan(n_lat // c, 0, (ql_ref, kl_ref, vl_ref), (ql_ref, kl_ref, vl_ref), states)

    def readout(o_ref, gate_refs, acc_base):
        rows = NORM_ROWS

        def body(i, carry):
            r0 = pl.multiple_of(i * rows, rows)
            y = None
            for d in range(2):
                o = acc_ref[d, pl.ds(acc_base + r0, rows), :]
                o2 = o * o
                o2_hi = o2.astype(BF16)
                o2_lo = (o2 - o2_hi.astype(F32)).astype(BF16)
                ms = _dot(o2_hi, head_mean) + _dot(o2_lo, head_mean)
                yd = o * lax.rsqrt(ms + EPS) * ng_ref[d:d + 1, :] * gate_refs[d][pl.ds(r0, rows), :]
                y = yd if y is None else y + yd
            o_ref[pl.ds(r0, rows), :] = y.astype(BF16)
            return carry

        n = o_ref.shape[0] // rows
        lax.fori_loop(0, n, body, 0, unroll=min(n, 4))

    readout(ol_ref, (gfl_ref, gbl_ref), 0)
    readout(oc_ref, (gfc_ref, gbc_ref), n_lat)


def _retention(a32_lat, a32_ctx, smalls, layer):
    b, n_lat, _ = a32_lat.shape
    n_ctx = a32_ctx.shape[1]

    def seq_specs(n):
        spec = lambda off, w: pl.BlockSpec((None, n, w), lambda i: (i, 0, off // w))
        return [spec(A32_RQ, 128), spec(A32_RK, 128), spec(A32_RV, 256), spec(A32_GF, 256), spec(A32_GB, 256)]

    out = lambda n: pl.BlockSpec((None, n, 256), lambda i: (i, 0, 0))
    return pl.pallas_call(
        _ret_kernel,
        grid=(b,),
        in_specs=seq_specs(n_lat) + seq_specs(n_ctx) + [_resident(a, layer) for a in smalls],
        out_specs=[out(n_lat), out(n_ctx)],
        out_shape=[jax.ShapeDtypeStruct((b, n_lat, 256), BF16), jax.ShapeDtypeStruct((b, n_ctx, 256), BF16)],
        scratch_shapes=[pltpu.VMEM((2, n_lat + n_ctx, 256), F32)],
        compiler_params=_cparams(1),
        name="retention",
    )(*([a32_lat] * 5), *([a32_ctx] * 5), *smalls)


def _out_mlp_kernel(x_ref, ya_ref, yb_ref, yc_ref, yd_ref, mod_ref, wo_ref, w1_ref, w2_ref, *rest, ff_chunk):
    o_ref = rest[-1]
    x = x_ref[...]
    gw = GROUP_W
    y = (_dot(ya_ref[...], wo_ref[0 * gw:1 * gw, :]) + _dot(yb_ref[...], wo_ref[1 * gw:2 * gw, :])
         + _dot(yc_ref[...], wo_ref[2 * gw:3 * gw, :]) + _dot(yd_ref[...], wo_ref[3 * gw:4 * gw, :]))
    x1 = x + mod_ref[2:3, :] * y
    h = (_rms(x1) * (1.0 + mod_ref[4:5, :]) + mod_ref[3:4, :]).astype(BF16)
    acc = jnp.zeros_like(x1)
    for j in range(w1_ref.shape[1] // ff_chunk):
        a = jnp.maximum(_dot(h, w1_ref[:, j * ff_chunk:(j + 1) * ff_chunk]), 0.0)
        acc = acc + _dot((a * a).astype(BF16), w2_ref[j * ff_chunk:(j + 1) * ff_chunk, :])
    x2 = x1 + mod_ref[5:6, :] * acc
    o_ref[...] = x2 if len(rest) == 1 else _rms(x2) * rest[0][...]


def _out_mlp(xseq, ys, mods, weights, layer, tm, in_place, final_gain=None):
    b, n, d = xseq.shape
    per_batch = mods.shape[0] == b
    tok = lambda w: pl.BlockSpec((None, tm, w), lambda i, j: (i, j, 0))
    in_specs = [tok(d)] + [tok(GROUP_W)] * 4
    in_specs += [pl.BlockSpec((None, N_MOD, d), lambda i, j: (i if per_batch else 0, 0, 0))]
    in_specs += [_resident(w, layer) for w in weights]
    args = [xseq, *ys, mods, *weights]
    if final_gain is not None:
        in_specs.append(_resident(final_gain))
        args.append(final_gain)
    return pl.pallas_call(
        functools.partial(_out_mlp_kernel, ff_chunk=1024),
        grid=(b, n // tm),
        in_specs=in_specs,
        out_specs=tok(d),
        out_shape=jax.ShapeDtypeStruct((b, n, d), F32),
        input_output_aliases={0: 0} if in_place else {},
        compiler_params=_cparams(2),
        name="out_mlp",
    )(*args)


def _rope_tables(n_lat):
    n = np.arange(n_lat)
    row, col = (n // GRID_W).astype(np.float64), (n % GRID_W).astype(np.float64)

    def table(groups):
        cos = np.ones((n_lat, LANES)); up = np.zeros((n_lat, LANES)); dn = np.zeros((n_lat, LANES))
        for rot_start, rot_dim in groups:
            half = rot_dim // 2
            q = half // 2
            freqs = ROPE_BASE ** (-np.arange(q, dtype=np.float64) / q)
            for axis, pos in enumerate((row, col)):
                ang = pos[:, None] * freqs[None, :]
                base = rot_start + axis * half
                cos[:, base:base + q] = np.cos(ang)
                cos[:, base + q:base + 2 * q] = np.cos(ang)
                up[:, base:base + q] = -np.sin(ang)
                dn[:, base + q:base + 2 * q] = np.sin(ang)
        return jnp.asarray(np.stack([cos, up, dn]).astype(np.float32))

    return table([(0, GQA_HEAD_DIM), (GQA_HEAD_DIM, GQA_HEAD_DIM)]), table([(0, MLA_ROPE)])


def _layout_w_in(w_in):
    split = OFF_KPE + MLA_ROPE
    w_a = jnp.pad(w_in[..., :split].astype(BF16), ((0, 0), (0, 0), (0, N_IN_A - split)))
    w_b = w_in[..., split:].astype(BF16)
    assert w_b.shape[-1] == N_IN_B
    return w_a, w_b


def _layout_w_uq(w_uq):
    lead = w_uq.shape[:-1]
    w = w_uq.reshape(lead + (MLA_HEADS, MLA_NOPE + MLA_ROPE))
    w = jnp.concatenate([w[..., MLA_NOPE:], w[..., :MLA_NOPE]], axis=-1)
    w = jnp.pad(w, ((0, 0),) * (w.ndim - 1) + ((0, LANES - MLA_NOPE - MLA_ROPE),))
    return w.reshape(lead + (MLA_HEADS * LANES,)).astype(BF16)


def _layout_w_ukv(w_ukv):
    lead = w_ukv.shape[:-1]
    w = w_ukv.reshape(lead + (MLA_HEADS, MLA_NOPE + MLA_V))
    pad = lambda a, before: jnp.pad(a, ((0, 0),) * (a.ndim - 1) + ((before, LANES - before - a.shape[-1]),)).reshape(
        lead + (MLA_HEADS * LANES,))
    return jnp.concatenate([pad(w[..., :MLA_NOPE], MLA_ROPE), pad(w[..., MLA_NOPE:], 0)], axis=-1).astype(BF16)


def kernel(x, c, ctx, c_ctx, w_mod, b_mod, w_in, w_out, conv_dw, conv_b, conv_ln_g, conv_ln_b, conv_pw, mla_q_g, mla_kv_g, mla_uq, mla_ukv, gqa_q_g, gqa_k_g, ret_decay, ret_norm_g, mlp_w1, mlp_w2, final_g):
    b, n_lat, d = x.shape
    n_ctx = ctx.shape[1]
    depth = w_mod.shape[0]
    assert n_lat % LAT_TILE == 0 and n_lat % PROJ_TILE == 0 and n_lat % KEY_CHUNK == 0 and n_ctx % KEY_CHUNK == 0
    assert n_ctx % NORM_ROWS == 0 and n_lat % NORM_ROWS == 0 and NORM_ROWS % (2 * RET_CHUNK) == 0

    rows = ((b + 1 + SUBLANES - 1) // SUBLANES) * SUBLANES
    cond = jnp.zeros((rows, d), F32).at[:b].set(c).at[b].set(c_ctx)
    mod_all = _adaln(cond, w_mod, b_mod).reshape(depth, rows, N_MOD, d)
    tables = _rope_tables(n_lat)
    mla = (A16_MQ, 4 * LANES, A16_MK, A16_MV, 4 * LANES, (0, 1, 2, 3), (0, 1, 2, 3))
    gqa = (A16_GQ, 2 * LANES, A16_GK, A16_GV, 2 * LANES, (0, 0, 1, 1), (0, 0, 1, 1))

    row = lambda a: a[:, None, :]
    w_proj = (*_layout_w_in(w_in), _layout_w_uq(mla_uq), _layout_w_ukv(mla_ukv), row(mla_q_g), row(mla_kv_g),
              row(jnp.tile(gqa_q_g, (1, 2))), row(jnp.tile(gqa_k_g, (1, 2))))
    w_conv = (jnp.broadcast_to(conv_dw[:, :, None, :], conv_dw.shape[:2] + (SUBLANES, GROUP_W)), row(conv_b), row(conv_ln_g), row(conv_ln_b), conv_pw.astype(BF16))
    dec = ret_decay.astype(F32)
    w_ret = (jnp.repeat(dec, RET_QK, axis=2), jnp.repeat(dec, RET_V, axis=2), jnp.repeat(dec, RET_CHUNK, axis=2),
             ret_norm_g.reshape(depth, 2, RET_HEADS * RET_V))
    w_mlp = (w_out.astype(BF16), mlp_w1.astype(BF16), mlp_w2.astype(BF16))

    cx = ctx
    for layer in range(depth):
        last = layer == depth - 1
        mods_lat, mods_ctx = mod_all[layer, :b], mod_all[layer, b:b + 1]
        a16_lat, a32_lat = _in_proj(x, mods_lat, w_proj, layer, tables, PROJ_TILE)
        a16_ctx, a32_ctx = _in_proj(cx, mods_ctx, w_proj, layer, None, n_ctx)
        yb_lat, = _attention_lat(a16_lat, a16_ctx, *mla, "mla_attn")
        yc_lat, ya_lat = _attention_lat(a16_lat, a16_ctx, *gqa, "gqa_attn", conv=(a32_lat, w_conv, layer))
        yd_lat, yd_ctx = _retention(a32_lat, a32_ctx, w_ret, layer)
        x = _out_mlp(x, (ya_lat, yb_lat, yc_lat, yd_lat), mods_lat, w_mlp, layer, LAT_TILE, layer > 0,
                     final_gain=final_g[None, :] if last else None)
        if not last:
            ya_ctx, yb_ctx, yc_ctx = _ctx_mixers(a16_ctx, a32_ctx, mla, gqa, w_conv, layer)
            cx = _out_mlp(cx, (ya_ctx, yb_ctx, yc_ctx, yd_ctx), mods_ctx, w_mlp, layer, n_ctx, layer > 0)
    return x
```

```python
import functools

import numpy as np
import jax
import jax.numpy as jnp
from jax import lax
from jax.experimental import pallas as pl
from jax.experimental.pallas import tpu as pltpu

F32 = jnp.float32
BF16 = jnp.bfloat16

GRID_W = 64
N_MOD = 6
EPS = 1e-6
ROPE_BASE = 10000.0
GROUP_W = 256
CONV_WIDTH = 31
CONV_PAD = 16
MLA_HEADS = 4
MLA_NOPE = 64
MLA_ROPE = 32
MLA_V = 64
MLA_Q_RANK = 256
MLA_KV_RANK = 128
GQA_HEADS = 4
GQA_KV_HEADS = 2
GQA_HEAD_DIM = 64
RET_HEADS = 4
RET_QK = 32
RET_V = 64
RET_CHUNK = 128
LANES = 128
SUBLANES = 8
LOG2_E = 1.4426950408889634
LAT_TILE = 512
PROJ_TILE = 1024
KEY_CHUNK = 256
NORM_ROWS = 256
VMEM_LIMIT = 56 * 1024 * 1024

OFF_A, OFF_CQ, OFF_CKV, OFF_KPE, N_IN_A = 0, 512, 768, 896, 1024
OFF_GQ, OFF_GK, OFF_RQ, OFF_RV, OFF_GF, OFF_GB, N_IN_B = 0, 256, 512, 768, 1024, 1280, 1536
A16_MQ, A16_MK, A16_MV, A16_GK, A16_GQ, A16_GV, A16_W = 0, 512, 1024, 1536, 2048, 2304, 2560
A32_Y, A32_RQ, A32_RK, A32_RV, A32_GF, A32_GB, A32_W = 0, 256, 384, 512, 768, 1024, 1280


def _cparams(n_axes):
    return pltpu.CompilerParams(dimension_semantics=("arbitrary",) * n_axes,
                                vmem_limit_bytes=VMEM_LIMIT)


def _resident(a, layer=None):
    if layer is None:
        return pl.BlockSpec(a.shape, lambda *_: (0,) * a.ndim, pipeline_mode=pl.Buffered(1))
    return pl.BlockSpec((None,) + a.shape[1:], lambda *_: (layer,) + (0,) * (a.ndim - 1),
                        pipeline_mode=pl.Buffered(1))


def _rms(x):
    return x * lax.rsqrt(jnp.mean(x * x, axis=-1, keepdims=True) + EPS)


def _sigmoid(x):
    return 1.0 / (1.0 + jnp.exp(-x))


def _silu(x):
    return x * _sigmoid(x)


def _dot(a, b):
    return jnp.dot(a, b, preferred_element_type=F32)


def _dot_nt(a, b):
    return lax.dot_general(a, b, (((1,), (1,)), ((), ())), preferred_element_type=F32)


def _dot_tn(a, b):
    return lax.dot_general(a, b, (((0,), (0,)), ((), ())), preferred_element_type=F32)


def _rope(x, cos, sin_up, sin_dn, shift):
    n = x.shape[-1]
    return x * cos + pltpu.roll(x, n - shift, 1) * sin_up + pltpu.roll(x, shift, 1) * sin_dn


def _adaln_kernel(cond_ref, w_ref, b_ref, o_ref):
    cond = cond_ref[...]
    o_ref[...] = _dot(_silu(cond).astype(BF16), w_ref[...].astype(BF16)) + b_ref[...]


def _adaln(cond, w_mod, b_mod):
    depth, d, n = w_mod.shape
    r = cond.shape[0]
    tn = 1536
    return pl.pallas_call(
        _adaln_kernel,
        grid=(depth, n // tn),
        in_specs=[pl.BlockSpec((r, d), lambda l, j: (0, 0)),
                  pl.BlockSpec((None, d, tn), lambda l, j: (l, 0, j)),
                  pl.BlockSpec((None, 1, tn), lambda l, j: (l, 0, j))],
        out_specs=pl.BlockSpec((None, r, tn), lambda l, j: (l, 0, j)),
        out_shape=jax.ShapeDtypeStruct((depth, r, n), F32),
        compiler_params=_cparams(2),
        name="adaln",
    )(cond, w_mod, b_mod.reshape(depth, 1, n))


def _in_proj_kernel(x_ref, mod_ref, w_a_ref, w_b_ref, w_uq_ref, w_ukv_ref, g_mq_ref, g_mkv_ref, g_gq_ref, g_gk_ref,
                    *rest, rotary, sub_tiles):
    a16_ref, a32_ref = rest[-2:]
    lo = lax.broadcasted_iota(jnp.int32, (1, LANES), 1) < (LANES // 2)
    q_scale = (MLA_NOPE + MLA_ROPE) ** -0.5 * LOG2_E
    n_rows = x_ref.shape[0] // sub_tiles

    for st in range(sub_tiles):
        rows = slice(st * n_rows, (st + 1) * n_rows)
        h = (_rms(x_ref[rows, :]) * (1.0 + mod_ref[1:2, :]) + mod_ref[0:1, :]).astype(BF16)

        def proj(w_ref, off, width, h=h):
            return _dot(h, w_ref[:, off:off + width])

        def put16(off, val, rows=rows):
            a16_ref[rows, off:off + val.shape[1]] = val.astype(BF16)

        def put32(off, val, rows=rows):
            a32_ref[rows, off:off + val.shape[1]] = val

        if rotary:
            tg_ref, tm_ref = rest[:2]
            rope_g = lambda v, rows=rows: _rope(v, tg_ref[0, rows, :], tg_ref[1, rows, :], tg_ref[2, rows, :],
                                                GQA_HEAD_DIM // 4)
            rope_m = lambda v, rows=rows: _rope(v, tm_ref[0, rows, :], tm_ref[1, rows, :], tm_ref[2, rows, :],
                                                MLA_ROPE // 4)
        else:
            rope_g = rope_m = lambda v: v

        def pair_norm_rope(xp, gain, rope_g=rope_g):
            x2 = xp * xp
            s_lo = jnp.sum(jnp.where(lo, x2, 0.0), axis=-1, keepdims=True)
            s_hi = jnp.sum(jnp.where(lo, 0.0, x2), axis=-1, keepdims=True)
            r = jnp.where(lo, lax.rsqrt(s_lo * (1.0 / GQA_HEAD_DIM) + EPS),
                          lax.rsqrt(s_hi * (1.0 / GQA_HEAD_DIM) + EPS))
            return rope_g(xp * r * gain)

        cq = (_rms(proj(w_a_ref, OFF_CQ, MLA_Q_RANK)) * g_mq_ref[...]).astype(BF16)
        ckv_kpe = proj(w_a_ref, OFF_CKV, 2 * LANES)
        ckv = (_rms(ckv_kpe[:, :LANES]) * g_mkv_ref[...]).astype(BF16)
        gq = proj(w_b_ref, OFF_GQ, GQA_HEADS * GQA_HEAD_DIM)
        gkv = proj(w_b_ref, OFF_GK, 2 * LANES)

        a = proj(w_a_ref, OFF_A, 2 * GROUP_W)
        put32(A32_Y, a[:, :GROUP_W] * _sigmoid(a[:, GROUP_W:]))
        rqk = proj(w_b_ref, OFF_RQ, 2 * LANES)
        put32(A32_RQ, rqk[:, :LANES])
        put32(A32_RK, rqk[:, LANES:] * RET_QK ** -0.5)
        put32(A32_RV, proj(w_b_ref, OFF_RV, RET_HEADS * RET_V))
        put32(A32_GF, _silu(proj(w_b_ref, OFF_GF, GROUP_W)))
        put32(A32_GB, _silu(proj(w_b_ref, OFF_GB, GROUP_W)))

        q = _dot(cq, w_uq_ref[...])
        for hd in range(MLA_HEADS):
            put16(A16_MQ + hd * LANES, rope_m(q[:, hd * LANES:(hd + 1) * LANES]) * q_scale)
        kv = _dot(ckv, w_ukv_ref[...])
        kpe = rope_m(ckv_kpe[:, LANES:])
        for hd in range(MLA_HEADS):
            put16(A16_MK + hd * LANES, kv[:, hd * LANES:(hd + 1) * LANES] + kpe)
            put16(A16_MV + hd * LANES,
                  jnp.where(lo, kv[:, (MLA_HEADS + hd) * LANES:(MLA_HEADS + hd + 1) * LANES], 1.0))

        for pr in range(2):
            qp = pair_norm_rope(gq[:, pr * LANES:(pr + 1) * LANES], g_gq_ref[...])
            put16(A16_GQ + pr * LANES, qp * (GQA_HEAD_DIM ** -0.5 * LOG2_E))
        kp = pair_norm_rope(gkv[:, :LANES], g_gk_ref[...])
        kp_sw = pltpu.roll(kp, LANES // 2, 1)
        put16(A16_GK + 0 * LANES, jnp.where(lo, kp, 0.0))
        put16(A16_GK + 1 * LANES, jnp.where(lo, 0.0, kp_sw))
        put16(A16_GK + 2 * LANES, jnp.where(lo, kp_sw, 0.0))
        put16(A16_GK + 3 * LANES, jnp.where(lo, 0.0, kp))
        vp = gkv[:, LANES:]
        put16(A16_GV + 0 * LANES, jnp.where(lo, vp, 1.0))
        put16(A16_GV + 1 * LANES, jnp.where(lo, pltpu.roll(vp, LANES // 2, 1), 1.0))


def _in_proj(xseq, mods, weights, layer, tables, tm):
    b, n, d = xseq.shape
    per_batch = mods.shape[0] == b
    tok = lambda w: pl.BlockSpec((None, tm, w), lambda i, j: (i, j, 0))
    in_specs = [tok(d), pl.BlockSpec((None, N_MOD, d), lambda i, j: (i if per_batch else 0, 0, 0))]
    in_specs += [_resident(w, layer) for w in weights]
    args = [xseq, mods, *weights]
    if tables is not None:
        in_specs += [pl.BlockSpec((3, tm, LANES), lambda i, j: (0, j, 0))] * 2
        args += list(tables)
    return pl.pallas_call(
        functools.partial(_in_proj_kernel, rotary=tables is not None, sub_tiles=tm // 256),
        grid=(b, n // tm),
        in_specs=in_specs,
        out_specs=[tok(A16_W), tok(A32_W)],
        out_shape=[jax.ShapeDtypeStruct((b, n, A16_W), BF16), jax.ShapeDtypeStruct((b, n, A32_W), F32)],
        compiler_params=_cparams(2),
        name="in_proj",
    )(*args)


CONV_ROWS = 64
CONV_FIRST = CONV_PAD - CONV_WIDTH // 2
CONV_TAIL = ((CONV_FIRST + CONV_WIDTH - 1) // SUBLANES) * SUBLANES


def _conv_steps(sh_ref, dw_ref, b_ref, lg_ref, lb_ref, pw_ref, o_ref, length):
    def fill(before, body, after):
        sh_ref[0, 0:CONV_PAD, :] = before
        sh_ref[0, CONV_PAD:CONV_PAD + length, :] = body
        sh_ref[0, CONV_PAD + length:2 * CONV_PAD + length, :] = after

    def shift(base, n=CONV_ROWS):
        win = sh_ref[0, pl.ds(base, n + SUBLANES), :]
        for r in range(1, SUBLANES):
            sh_ref[r, pl.ds(base, n), :] = win[r:r + n, :]

    def shift_tail():
        shift(length, CONV_TAIL)

    def taps(base):
        groups = (CONV_ROWS // SUBLANES, SUBLANES, GROUP_W)
        acc = jnp.zeros(groups, F32) + b_ref[...]
        for k in range(CONV_WIDTH):
            off = CONV_FIRST + k
            slab = sh_ref[off % SUBLANES, pl.ds(base + (off // SUBLANES) * SUBLANES, CONV_ROWS), :]
            acc = acc + slab.reshape(groups) * dw_ref[k]
        sh_ref[0, pl.ds(base, CONV_ROWS), :] = acc.reshape(CONV_ROWS, GROUP_W)

    def finish(base):
        acc = sh_ref[0, pl.ds(base, NORM_ROWS), :]
        mu = jnp.mean(acc, axis=-1, keepdims=True)
        cen = acc - mu
        var = jnp.mean(cen * cen, axis=-1, keepdims=True)
        z = _silu(cen * lax.rsqrt(var + EPS) * lg_ref[...] + lb_ref[...])
        o_ref[pl.ds(base, NORM_ROWS), :] = _dot(z.astype(BF16), pw_ref[...]).astype(BF16)

    return fill, shift, shift_tail, taps, finish


def _conv_sequence(y_ref, dw_ref, b_ref, lg_ref, lb_ref, pw_ref, o_ref, sh_ref):
    length = y_ref.shape[0]
    fill, shift, shift_tail, taps, finish = _conv_steps(sh_ref, dw_ref, b_ref, lg_ref, lb_ref, pw_ref, o_ref, length)
    zeros = jnp.zeros((CONV_PAD, GROUP_W), F32)
    fill(zeros, y_ref[...], zeros)

    def loop(step, rows, **kw):
        def body(c, carry):
            step(pl.multiple_of(c * rows, rows))
            return carry
        lax.fori_loop(0, length // rows, body, 0, **kw)

    loop(shift, CONV_ROWS)
    shift_tail()
    loop(taps, CONV_ROWS)
    loop(finish, NORM_ROWS, unroll=min(length // NORM_ROWS, 4))


def _normalized_pair(acc_even, acc_odd):
    half = LANES // 2
    lo = lax.broadcasted_iota(jnp.int32, (1, LANES), 1) < half
    return jnp.where(lo, acc_even / pltpu.roll(acc_even, half, 1), pltpu.roll(acc_odd, half, 1) / acc_odd)


def _attn_lat_kernel(q_ref, qn_ref, kl_ref, kc_ref, kln_ref, kcn_ref, vl_ref, vc_ref, *rest,
                     q_blocks, v_blocks, with_conv):
    if with_conv:
        y_ref, yb_ref, ya_next_ref, dw_ref, cb_ref, lg_ref, lb_ref, pw_ref, o_ref, ya_ref = rest[:10]
        s_ref, m_ref, mb_ref, acc_ref, sh_ref = rest[10:]
    else:
        o_ref, s_ref, m_ref, mb_ref, acc_ref = rest
    tq = q_ref.shape[0]
    tk = KEY_CHUNK
    last_tile = pl.program_id(1) == pl.num_programs(1) - 1
    chunks = [(kl_ref, kln_ref, vl_ref, c * tk) for c in range(kl_ref.shape[0] // tk)]
    chunks += [(kc_ref, kcn_ref, vc_ref, c * tk) for c in range(kc_ref.shape[0] // tk)]

    def scores_chunk(hd, ci, ref, next_tile=False):
        slot = hd % 2
        k_ref, kn_ref, _, r0 = chunks[ci]
        q = ref[:, q_blocks[hd] * LANES:(q_blocks[hd] + 1) * LANES]
        k = k_ref[r0:r0 + tk, hd * LANES:(hd + 1) * LANES]
        if next_tile:
            k = jnp.where(last_tile, kn_ref[r0:r0 + tk, hd * LANES:(hd + 1) * LANES], k)
        s = _dot_nt(q, k)
        s_ref[slot, :, ci * tk:(ci + 1) * tk] = s
        mx = s[:, 0:LANES]
        for i in range(1, tk // LANES):
            mx = jnp.maximum(mx, s[:, i * LANES:(i + 1) * LANES])
        m_ref[slot] = mx if ci == 0 else jnp.maximum(m_ref[slot], mx)

    def scores_finish(hd):
        slot = hd % 2
        mb_ref[slot] = jnp.broadcast_to(jnp.max(m_ref[slot], axis=-1, keepdims=True), (tq, LANES))

    def values_chunk(hd, ci):
        slot = hd % 2
        _, _, v_ref, r0 = chunks[ci]
        mb = mb_ref[slot]
        p = jnp.concatenate(
            [jnp.exp2(s_ref[slot, :, ci * tk + i * LANES:ci * tk + (i + 1) * LANES] - mb).astype(BF16)
             for i in range(tk // LANES)], axis=1)
        pv = _dot(p, v_ref[r0:r0 + tk, v_blocks[hd] * LANES:(v_blocks[hd] + 1) * LANES])
        acc_ref[slot] = pv if ci == 0 else acc_ref[slot] + pv

    n = len(chunks)

    def scores_ahead(unit, ci):
        if unit < 4:
            scores_chunk(unit, ci, q_ref)
        else:
            scores_chunk(unit - 4, ci, qn_ref, next_tile=True)

    @pl.when((pl.program_id(0) == 0) & (pl.program_id(1) == 0))
    def _():
        for ci in range(n):
            scores_chunk(0, ci, q_ref)
        scores_finish(0)
        scores_chunk(1, 0, q_ref)

    side_steps = []
    if with_conv:
        fill, shift, shift_tail, taps, finish = _conv_steps(sh_ref, dw_ref, cb_ref, lg_ref, lb_ref, pw_ref, ya_ref, tq)
        j, last = pl.program_id(1), pl.num_programs(1) - 1
        side_steps.append(lambda: fill(jnp.where(j > 0, yb_ref[...], 0.0), y_ref[...],
                                       jnp.where(j < last, ya_next_ref[...], 0.0)))
        side_steps += [functools.partial(shift, c * CONV_ROWS) for c in range(tq // CONV_ROWS)] + [shift_tail]
        side_steps += [functools.partial(taps, c * CONV_ROWS) for c in range(tq // CONV_ROWS)]
        side_steps += [functools.partial(finish, c * NORM_ROWS) for c in range(tq // NORM_ROWS)]
    done = 0

    for hd in range(4):
        for ci in range(n):
            values_chunk(hd, ci)
            if ci + 1 < n:
                scores_ahead(hd + 1, ci + 1)
            else:
                scores_ahead(hd + 2, 0)
            if ci + 2 == n:
                scores_finish((hd + 1) % 4)
            due = ((hd * n + ci + 1) * len(side_steps)) // (4 * n)
            for step in side_steps[done:due]:
                step()
            done = due
        if hd % 2 == 1:
            pr = hd // 2
            o_ref[:, pr * LANES:(pr + 1) * LANES] = _normalized_pair(acc_ref[0], acc_ref[1]).astype(BF16)


def _attention_lat(a16_lat, a16_ctx, q_off, q_width, k_off, v_off, v_width, q_blocks, v_blocks, name, conv=None):
    b, n_lat, _ = a16_lat.shape
    n_ctx = a16_ctx.shape[1]
    tq = LAT_TILE
    n_tiles = n_lat // tq
    kw = 4 * LANES
    qc, kc, vc = q_off // q_width, k_off // kw, v_off // v_width
    next_row = lambda i: jnp.minimum(i + 1, b - 1)
    wraps = lambda j: j == n_tiles - 1
    in_specs = [pl.BlockSpec((None, tq, q_width), lambda i, j: (i, j, qc)),
                pl.BlockSpec((None, tq, q_width),
                             lambda i, j: (jnp.where(wraps(j), next_row(i), i), jnp.where(wraps(j), 0, j + 1), qc)),
                pl.BlockSpec((None, n_lat, kw), lambda i, j: (i, 0, kc)),
                pl.BlockSpec((None, n_ctx, kw), lambda i, j: (i, 0, kc)),
                pl.BlockSpec((None, n_lat, kw), lambda i, j: (next_row(i), 0, kc)),
                pl.BlockSpec((None, n_ctx, kw), lambda i, j: (next_row(i), 0, kc)),
                pl.BlockSpec((None, n_lat, v_width), lambda i, j: (i, 0, vc)),
                pl.BlockSpec((None, n_ctx, v_width), lambda i, j: (i, 0, vc))]
    args = [a16_lat, a16_lat, a16_lat, a16_ctx, a16_lat, a16_ctx, a16_lat, a16_ctx]
    tile = lambda w: pl.BlockSpec((None, tq, w), lambda i, j: (i, j, 0))
    out_specs = [tile(2 * LANES)]
    out_shape = [jax.ShapeDtypeStruct((b, n_lat, 2 * LANES), BF16)]
    scratch = [pltpu.VMEM((2, tq, n_lat + n_ctx), F32), pltpu.VMEM((2, tq, LANES), F32),
               pltpu.VMEM((2, tq, LANES), F32), pltpu.VMEM((2, tq, LANES), F32)]
    if conv is not None:
        a32_lat, weights, layer = conv
        yc = A32_Y // GROUP_W
        per_tile = tq // CONV_PAD
        halo = lambda index: pl.BlockSpec((None, CONV_PAD, GROUP_W), lambda i, j: (i, index(j), yc))
        in_specs += [pl.BlockSpec((None, tq, GROUP_W), lambda i, j: (i, j, yc)),
                     halo(lambda j: jnp.maximum(j * per_tile - 1, 0)),
                     halo(lambda j: jnp.minimum((j + 1) * per_tile, n_lat // CONV_PAD - 1))]
        in_specs += [_resident(w, layer) for w in weights]
        args += [a32_lat, a32_lat, a32_lat, *weights]
        out_specs.append(tile(GROUP_W))
        out_shape.append(jax.ShapeDtypeStruct((b, n_lat, GROUP_W), BF16))
        scratch.append(pltpu.VMEM((SUBLANES, tq + 2 * CONV_PAD, GROUP_W), F32))
    return pl.pallas_call(
        functools.partial(_attn_lat_kernel, q_blocks=q_blocks, v_blocks=v_blocks, with_conv=conv is not None),
        grid=(b, n_tiles),
        in_specs=in_specs,
        out_specs=out_specs,
        out_shape=out_shape,
        scratch_shapes=scratch,
        compiler_params=_cparams(2),
        name=name,
    )(*args)


def _ctx_attention(q_ref, k_ref, v_ref, o_ref, q_blocks, v_blocks):
    for pr in range(2):
        accs = []
        for half in range(2):
            hd = 2 * pr + half
            q = q_ref[:, q_blocks[hd] * LANES:(q_blocks[hd] + 1) * LANES]
            s = _dot_nt(q, k_ref[:, hd * LANES:(hd + 1) * LANES])
            p = jnp.exp2(s - jnp.max(s, axis=-1, keepdims=True))
            accs.append(_dot(p.astype(BF16), v_ref[:, v_blocks[hd] * LANES:(v_blocks[hd] + 1) * LANES]))
        o_ref[:, pr * LANES:(pr + 1) * LANES] = _normalized_pair(*accs).astype(BF16)


def _ctx_mixers_kernel(mq_ref, mk_ref, mv_ref, gq_ref, gk_ref, gv_ref, y_ref, dw_ref, b_ref, lg_ref, lb_ref, pw_ref,
                       ob_ref, oc_ref, oa_ref, sh_ref, *, mla_blocks, gqa_blocks):
    _ctx_attention(mq_ref, mk_ref, mv_ref, ob_ref, *mla_blocks)
    _ctx_attention(gq_ref, gk_ref, gv_ref, oc_ref, *gqa_blocks)
    _conv_sequence(y_ref, dw_ref, b_ref, lg_ref, lb_ref, pw_ref, oa_ref, sh_ref)


def _ctx_mixers(a16_ctx, a32_ctx, mla, gqa, conv_weights, layer):
    b, n_ctx, _ = a16_ctx.shape
    kw = 4 * LANES

    def qkv_specs(q_off, q_width, k_off, v_off, v_width, *_):
        spec = lambda off, w: pl.BlockSpec((None, n_ctx, w), lambda i: (i, 0, off // w))
        return [spec(q_off, q_width), spec(k_off, kw), spec(v_off, v_width)]

    out = pl.BlockSpec((None, n_ctx, GROUP_W), lambda i: (i, 0, 0))
    yb, yc, ya = pl.pallas_call(
        functools.partial(_ctx_mixers_kernel, mla_blocks=mla[-2:], gqa_blocks=gqa[-2:]),
        grid=(b,),
        in_specs=qkv_specs(*mla) + qkv_specs(*gqa)
        + [pl.BlockSpec((None, n_ctx, GROUP_W), lambda i: (i, 0, A32_Y // GROUP_W))]
        + [_resident(w, layer) for w in conv_weights],
        out_specs=[out] * 3,
        out_shape=[jax.ShapeDtypeStruct((b, n_ctx, GROUP_W), BF16)] * 3,
        scratch_shapes=[pltpu.VMEM((SUBLANES, n_ctx + 2 * CONV_PAD, GROUP_W), F32)],
        compiler_params=_cparams(1),
        name="ctx_mixers",
    )(*([a16_ctx] * 6), a32_ctx, *conv_weights)
    return ya, yb, yc


def _log_sigmoid(x):
    return jnp.minimum(x, 0.0) - jnp.log(1.0 + jnp.exp(-jnp.abs(x)))


def _ret_kernel(ql_ref, kl_ref, vl_ref, gfl_ref, gbl_ref, qc_ref, kc_ref, vc_ref, gfc_ref, gbc_ref,
                dq_ref, dv_ref, dc_ref, ng_ref, ol_ref, oc_ref, acc_ref):
    c = RET_CHUNK
    n_lat, n_ctx = ql_ref.shape[0], qc_ref.shape[0]
    qk_w = RET_HEADS * RET_QK
    v_w = RET_HEADS * RET_V

    row_i = lax.broadcasted_iota(jnp.int32, (c, 1), 0).astype(F32)
    qk_head = lax.broadcasted_iota(jnp.int32, (1, qk_w), 1) // RET_QK
    v_head = lax.broadcasted_iota(jnp.int32, (1, v_w), 1) // RET_V
    qk_masks = [qk_head == hd for hd in range(RET_HEADS)]
    v_masks = [v_head == hd for hd in range(RET_HEADS)]
    state_rows = lax.broadcasted_iota(jnp.int32, (qk_w, 1), 0) // RET_QK
    bd_mask = (state_rows == v_head).astype(F32)
    ones_rows = lax.broadcasted_iota(jnp.int32, (v_w, 1), 0) // RET_V
    head_mean = jnp.where(ones_rows == v_head, 1.0 / RET_V, 0.0).astype(BF16)
    ci = lax.broadcasted_iota(jnp.int32, (c, RET_HEADS * c), 0).astype(F32)
    cj = (lax.broadcasted_iota(jnp.int32, (c, RET_HEADS * c), 1) % c).astype(F32)

    def make_step(d):
        lg_q = _log_sigmoid(dq_ref[d:d + 1, :])
        lg_v = _log_sigmoid(dv_ref[d:d + 1, :])
        lg_c = _log_sigmoid(dc_ref[d:d + 1, :])
        if d == 0:
            diff = ci - cj
            q_dec = jnp.exp(lg_q * (row_i + 1.0))
            k_dec = jnp.exp(lg_q * (c - 1.0 - row_i))
        else:
            diff = cj - ci
            q_dec = jnp.exp(lg_q * (c - row_i))
            k_dec = jnp.exp(lg_q * row_i)
        decay = jnp.where(diff >= 0.0, jnp.exp(lg_c * jnp.maximum(diff, 0.0)), 0.0)
        chunk_dec = jnp.exp(lg_v * float(c))

        def step(refs, r0, acc_r0, state):
            q_ref, k_ref, v_ref = refs
            q = q_ref[pl.ds(r0, c), :]
            k = k_ref[pl.ds(r0, c), :]
            v = v_ref[pl.ds(r0, c), :]
            kb, vb = k.astype(BF16), v.astype(BF16)
            k_bd = jnp.concatenate([jnp.where(m, kb, 0) for m in qk_masks], axis=0)
            scores = _dot_nt(q.astype(BF16), k_bd) * decay
            v_bd = jnp.concatenate([jnp.where(m, vb, 0) for m in v_masks], axis=0)
            inner = _dot(scores.astype(BF16), v_bd)
            cross = _dot((q * q_dec).astype(BF16), state.astype(BF16))
            kv = _dot_tn((k * k_dec).astype(BF16), vb)
            new_state = state * chunk_dec + kv * bd_mask
            acc_ref[d, pl.ds(acc_r0, c), :] = inner + cross
            return new_state

        return step

    fwd, bwd = make_step(0), make_step(1)

    def scan(n, acc_base, fwd_refs, bwd_refs, states):
        def body(i, st):
            rf = pl.multiple_of(i * c, c)
            rb = pl.multiple_of((n - 1 - i) * c, c)
            return fwd(fwd_refs, rf, acc_base + rf, st[0]), bwd(bwd_refs, rb, acc_base + rb, st[1])
        return lax.fori_loop(0, n, body, states, unroll=min(n, 8))

    zero = jnp.zeros((qk_w, v_w), F32)
    states = scan(n_ctx // c, n_lat, (qc_ref, kc_ref, vc_ref), (qc_ref, kc_ref, vc_ref), (zero, zero))
    scan(n_lat // c, 0, (ql_ref, kl_ref, vl_ref), (ql_ref, kl_ref, vl_ref), states)

    def readout(o_ref, gate_refs, acc_base):
        rows = NORM_ROWS

        def body(i, carry):
            r0 = pl.multiple_of(i * rows, rows)
            y = None
            for d in range(2):
                o = acc_ref[d, pl.ds(acc_base + r0, rows), :]
                o2 = o * o
                o2_hi = o2.astype(BF16)
                o2_lo = (o2 - o2_hi.astype(F32)).astype(BF16)
                ms = _dot(o2_hi, head_mean) + _dot(o2_lo, head_mean)
                yd = o * lax.rsqrt(ms + EPS) * ng_ref[d:d + 1, :] * gate_refs[d][pl.ds(r0, rows), :]
                y = yd if y is None else y + yd
            o_ref[pl.ds(r0, rows), :] = y.astype(BF16)
            return carry

        n = o_ref.shape[0] // rows
        lax.fori_loop(0, n, body, 0, unroll=min(n, 4))

    readout(ol_ref, (gfl_ref, gbl_ref), 0)
    readout(oc_ref, (gfc_ref, gbc_ref), n_lat)


def _retention(a32_lat, a32_ctx, smalls, layer):
    b, n_lat, _ = a32_lat.shape
    n_ctx = a32_ctx.shape[1]

    def seq_specs(n):
        spec = lambda off, w: pl.BlockSpec((None, n, w), lambda i: (i, 0, off // w))
        return [spec(A32_RQ, 128), spec(A32_RK, 128), spec(A32_RV, 256), spec(A32_GF, 256), spec(A32_GB, 256)]

    out = lambda n: pl.BlockSpec((None, n, 256), lambda i: (i, 0, 0))
    return pl.pallas_call(
        _ret_kernel,
        grid=(b,),
        in_specs=seq_specs(n_lat) + seq_specs(n_ctx) + [_resident(a, layer) for a in smalls],
        out_specs=[out(n_lat), out(n_ctx)],
        out_shape=[jax.ShapeDtypeStruct((b, n_lat, 256), BF16), jax.ShapeDtypeStruct((b, n_ctx, 256), BF16)],
        scratch_shapes=[pltpu.VMEM((2, n_lat + n_ctx, 256), F32)],
        compiler_params=_cparams(1),
        name="retention",
    )(*([a32_lat] * 5), *([a32_ctx] * 5), *smalls)


def _out_mlp_kernel(x_ref, ya_ref, yb_ref, yc_ref, yd_ref, mod_ref, wo_ref, w1_ref, w2_ref, *rest, ff_chunk):
    o_ref = rest[-1]
    x = x_ref[...]
    y = _dot(jnp.concatenate([ya_ref[...], yb_ref[...], yc_ref[...], yd_ref[...]], axis=1), wo_ref[...])
    x1 = x + mod_ref[2:3, :] * y
    h = (_rms(x1) * (1.0 + mod_ref[4:5, :]) + mod_ref[3:4, :]).astype(BF16)
    acc = jnp.zeros_like(x1)
    for j in range(w1_ref.shape[1] // ff_chunk):
        a = jnp.maximum(_dot(h, w1_ref[:, j * ff_chunk:(j + 1) * ff_chunk]), 0.0)
        acc = acc + _dot((a * a).astype(BF16), w2_ref[j * ff_chunk:(j + 1) * ff_chunk, :])
    x2 = x1 + mod_ref[5:6, :] * acc
    o_ref[...] = x2 if len(rest) == 1 else _rms(x2) * rest[0][...]


def _out_mlp(xseq, ys, mods, weights, layer, tm, in_place, final_gain=None):
    b, n, d = xseq.shape
    per_batch = mods.shape[0] == b
    tok = lambda w: pl.BlockSpec((None, tm, w), lambda i, j: (i, j, 0))
    in_specs = [tok(d)] + [tok(GROUP_W)] * 4
    in_specs += [pl.BlockSpec((None, N_MOD, d), lambda i, j: (i if per_batch else 0, 0, 0))]
    in_specs += [_resident(w, layer) for w in weights]
    args = [xseq, *ys, mods, *weights]
    if final_gain is not None:
        in_specs.append(_resident(final_gain))
        args.append(final_gain)
    return pl.pallas_call(
        functools.partial(_out_mlp_kernel, ff_chunk=1024),
        grid=(b, n // tm),
        in_specs=in_specs,
        out_specs=tok(d),
        out_shape=jax.ShapeDtypeStruct((b, n, d), F32),
        input_output_aliases={0: 0} if in_place else {},
        compiler_params=_cparams(2),
        name="out_mlp",
    )(*args)


def _rope_tables(n_lat):
    n = np.arange(n_lat)
    row, col = (n // GRID_W).astype(np.float64), (n % GRID_W).astype(np.float64)

    def table(groups):
        cos = np.ones((n_lat, LANES)); up = np.zeros((n_lat, LANES)); dn = np.zeros((n_lat, LANES))
        for rot_start, rot_dim in groups:
            half = rot_dim // 2
            q = half // 2
            freqs = ROPE_BASE ** (-np.arange(q, dtype=np.float64) / q)
            for axis, pos in enumerate((row, col)):
                ang = pos[:, None] * freqs[None, :]
                base = rot_start + axis * half
                cos[:, base:base + q] = np.cos(ang)
                cos[:, base + q:base + 2 * q] = np.cos(ang)
                up[:, base:base + q] = -np.sin(ang)
                dn[:, base + q:base + 2 * q] = np.sin(ang)
        return jnp.asarray(np.stack([cos, up, dn]).astype(np.float32))

    return table([(0, GQA_HEAD_DIM), (GQA_HEAD_DIM, GQA_HEAD_DIM)]), table([(0, MLA_ROPE)])


def _layout_w_in(w_in):
    split = OFF_KPE + MLA_ROPE
    w_a = jnp.pad(w_in[..., :split].astype(BF16), ((0, 0), (0, 0), (0, N_IN_A - split)))
    w_b = w_in[..., split:].astype(BF16)
    assert w_b.shape[-1] == N_IN_B
    return w_a, w_b


def _layout_w_uq(w_uq):
    lead = w_uq.shape[:-1]
    w = w_uq.reshape(lead + (MLA_HEADS, MLA_NOPE + MLA_ROPE))
    w = jnp.concatenate([w[..., MLA_NOPE:], w[..., :MLA_NOPE]], axis=-1)
    w = jnp.pad(w, ((0, 0),) * (w.ndim - 1) + ((0, LANES - MLA_NOPE - MLA_ROPE),))
    return w.reshape(lead + (MLA_HEADS * LANES,)).astype(BF16)


def _layout_w_ukv(w_ukv):
    lead = w_ukv.shape[:-1]
    w = w_ukv.reshape(lead + (MLA_HEADS, MLA_NOPE + MLA_V))
    pad = lambda a, before: jnp.pad(a, ((0, 0),) * (a.ndim - 1) + ((before, LANES - before - a.shape[-1]),)).reshape(
        lead + (MLA_HEADS * LANES,))
    return jnp.concatenate([pad(w[..., :MLA_NOPE], MLA_ROPE), pad(w[..., MLA_NOPE:], 0)], axis=-1).astype(BF16)


def kernel(x, c, ctx, c_ctx, w_mod, b_mod, w_in, w_out, conv_dw, conv_b, conv_ln_g, conv_ln_b, conv_pw, mla_q_g, mla_kv_g, mla_uq, mla_ukv, gqa_q_g, gqa_k_g, ret_decay, ret_norm_g, mlp_w1, mlp_w2, final_g):
    b, n_lat, d = x.shape
    n_ctx = ctx.shape[1]
    depth = w_mod.shape[0]
    assert n_lat % LAT_TILE == 0 and n_lat % PROJ_TILE == 0 and n_lat % KEY_CHUNK == 0 and n_ctx % KEY_CHUNK == 0
    assert n_ctx % NORM_ROWS == 0 and n_lat % NORM_ROWS == 0 and NORM_ROWS % (2 * RET_CHUNK) == 0

    rows = ((b + 1 + SUBLANES - 1) // SUBLANES) * SUBLANES
    cond = jnp.zeros((rows, d), F32).at[:b].set(c).at[b].set(c_ctx)
    mod_all = _adaln(cond, w_mod, b_mod).reshape(depth, rows, N_MOD, d)
    tables = _rope_tables(n_lat)
    mla = (A16_MQ, 4 * LANES, A16_MK, A16_MV, 4 * LANES, (0, 1, 2, 3), (0, 1, 2, 3))
    gqa = (A16_GQ, 2 * LANES, A16_GK, A16_GV, 2 * LANES, (0, 0, 1, 1), (0, 0, 1, 1))

    row = lambda a: a[:, None, :]
    w_proj = (*_layout_w_in(w_in), _layout_w_uq(mla_uq), _layout_w_ukv(mla_ukv), row(mla_q_g), row(mla_kv_g),
              row(jnp.tile(gqa_q_g, (1, 2))), row(jnp.tile(gqa_k_g, (1, 2))))
    w_conv = (jnp.broadcast_to(conv_dw[:, :, None, :], conv_dw.shape[:2] + (SUBLANES, GROUP_W)), row(conv_b), row(conv_ln_g), row(conv_ln_b), conv_pw.astype(BF16))
    dec = ret_decay.astype(F32)
    w_ret = (jnp.repeat(dec, RET_QK, axis=2), jnp.repeat(dec, RET_V, axis=2), jnp.repeat(dec, RET_CHUNK, axis=2),
             ret_norm_g.reshape(depth, 2, RET_HEADS * RET_V))
    w_mlp = (w_out.astype(BF16), mlp_w1.astype(BF16), mlp_w2.astype(BF16))

    cx = ctx
    for layer in range(depth):
        last = layer == depth - 1
        mods_lat, mods_ctx = mod_all[layer, :b], mod_all[layer, b:b + 1]
        a16_lat, a32_lat = _in_proj(x, mods_lat, w_proj, layer, tables, PROJ_TILE)
        a16_ctx, a32_ctx = _in_proj(cx, mods_ctx, w_proj, layer, None, n_ctx)
        yb_lat, = _attention_lat(a16_lat, a16_ctx, *mla, "mla_attn")
        yc_lat, ya_lat = _attention_lat(a16_lat, a16_ctx, *gqa, "gqa_attn", conv=(a32_lat, w_conv, layer))
        yd_lat, yd_ctx = _retention(a32_lat, a32_ctx, w_ret, layer)
        x = _out_mlp(x, (ya_lat, yb_lat, yc_lat, yd_lat), mods_lat, w_mlp, layer, LAT_TILE, layer > 0,
                     final_gain=final_g[None, :] if last else None)
        if not last:
            ya_ctx, yb_ctx, yc_ctx = _ctx_mixers(a16_ctx, a32_ctx, mla, gqa, w_conv, layer)
            cx = _out_mlp(cx, (ya_ctx, yb_ctx, yc_ctx, yd_ctx), mods_ctx, w_mlp, layer, n_ctx, layer > 0)
    return x
```
